```python
import math
import jax
import jax.numpy as jnp
from jax import lax
import numpy as np

D_MODEL = 1024
BATCH = 4
SEQ = 4096
DEPTH = 2

N_A_LAYERS = DEPTH // 2
N_B_LAYERS = DEPTH - N_A_LAYERS

RWKV_HEAD = 64
RWKV_HEADS = D_MODEL // RWKV_HEAD
DECAY_LORA = 64
AAA_LORA = 64
GATE_LORA = 128
RWKV_GN_EPS = 64e-5

DIFF_QK_DIM = 64
DIFF_V_DIM = 2 * DIFF_QK_DIM
DIFF_HEADS = D_MODEL // DIFF_V_DIM
Q_BLOCK = 128
SUBLN_EPS = 1e-5

N_GROUPS = 4
EXPERTS_PER_GROUP = 8
N_EXPERTS = N_GROUPS * EXPERTS_PER_GROUP
TOP_K = 2
EXPERT_FF = 512
MOE_BLOCK = 128

RMS_EPS = 1e-6

kernel_name = 'hybrid_rwkv7_diffattn_hmoe_yoco'


def rmsnorm(x, g, eps=RMS_EPS):
    xf = x.astype(jnp.float32)
    y = xf * lax.rsqrt(jnp.mean(xf * xf, axis=-1, keepdims=True) + eps)
    return (y * g.astype(jnp.float32)).astype(x.dtype)


def modulate(h, shift, scale):
    return h * (1 + scale[:, None, :]) + shift[:, None, :]


def rwkv7_time_mix(h, mu, w_rkv, w0, w1, w2, a0, a1, a2, g1, g2, k_k, k_a, r_k, gn_g, gn_b, w_o):
    B, T, D = h.shape
    H, N = RWKV_HEADS, RWKV_HEAD
    xx = jnp.pad(h, ((0, 0), (1, 0), (0, 0)))[:, :-1] - h
    xs = h[None] + xx[None] * mu[:, None, None, :]
    r, k, v = jnp.einsum('ibtd,ide->ibte', xs[:3], w_rkv)
    xw, xa, xg = xs[3], xs[4], xs[5]
    w_log = -jax.nn.softplus(-(w0 + jnp.tanh(xw @ w1) @ w2)) - 0.5
    decay = jnp.exp(-jnp.exp(w_log.astype(jnp.float32)))
    a = jax.nn.sigmoid(a0 + (xa @ a1) @ a2)
    g = jax.nn.sigmoid(xg @ g1) @ g2
    heads = lambda t: t.reshape(B, T, H, N).astype(jnp.float32)
    kk = heads(k * k_k)
    kk = kk / jnp.maximum(jnp.linalg.norm(kk, axis=-1, keepdims=True), 1e-12)
    k = k * (1 + (a - 1) * k_a)
    r_h, k_h, v_h, a_h = heads(r), heads(k), heads(v), heads(a)
    w_h = decay.reshape(B, T, H, N)
    a_vec = -kk
    b_vec = kk * a_h

    def step(S, inp):
        r_t, w_t, k_t, v_t, a_t, b_t = inp
        sa = jnp.einsum('bhij,bhj->bhi', S, a_t)
        S = S * w_t[:, :, None, :] + sa[..., None] * b_t[:, :, None, :] + v_t[..., None] * k_t[:, :, None, :]
        return S, jnp.einsum('bhij,bhj->bhi', S, r_t)

    tm = lambda t: jnp.moveaxis(t, 1, 0)
    S0 = jnp.zeros((B, H, N, N), jnp.float32)
    _, y = lax.scan(step, S0, (tm(r_h), tm(w_h), tm(k_h), tm(v_h), tm(a_vec), tm(b_vec)))
    y = jnp.moveaxis(y, 0, 1)
    mean = jnp.mean(y, axis=-1, keepdims=True)
    var = jnp.mean(jnp.square(y - mean), axis=-1, keepdims=True)
    y = ((y - mean) * lax.rsqrt(var + RWKV_GN_EPS)).reshape(B, T, D) * gn_g + gn_b
    bonus = jnp.sum(r_h * k_h * r_k, axis=-1, keepdims=True) * v_h
    y = y + bonus.reshape(B, T, D)
    return (y.astype(h.dtype) * g) @ w_o


def diff_attention(h, k_sh, v_sh, w_q, lq1, lk1, lq2, lk2, subln_g, w_o, lambda_init):
    B, T, D = h.shape
    H = DIFF_HEADS
    q = (h @ w_q).reshape(B, T, H, 2, DIFF_QK_DIM)
    lam = (jnp.exp(jnp.sum(lq1 * lk1).astype(jnp.float32))
           - jnp.exp(jnp.sum(lq2 * lk2).astype(jnp.float32)) + lambda_init)
    nb = T // Q_BLOCK
    qb = jnp.moveaxis(q.reshape(B, nb, Q_BLOCK, H, 2, DIFF_QK_DIM), 1, 0)
    k_pos = jnp.arange(T)
    scale = DIFF_QK_DIM ** -0.5

    def attend(args):
        q_i, i = args
        s = jnp.einsum('bqhcd,bkhcd->bhcqk', q_i, k_sh).astype(jnp.float32) * scale
        q_pos = i * Q_BLOCK + jnp.arange(Q_BLOCK)
        mask = k_pos[None, :] <= q_pos[:, None]
        p = jax.nn.softmax(jnp.where(mask, s, -jnp.inf), axis=-1)
        attn = p[:, :, 0] - lam * p[:, :, 1]
        return jnp.einsum('bhqk,bkhe->bqhe', attn.astype(v_sh.dtype), v_sh)

    o = lax.map(attend, (qb, jnp.arange(nb)))
    o = jnp.moveaxis(o, 0, 1).reshape(B, T, H, DIFF_V_DIM)
    o = rmsnorm(o, subln_g, eps=SUBLN_EPS) * (1 - lambda_init)
    return o.reshape(B, T, D) @ w_o


def hier_moe(h, w_rg, w_re, w_gate, w_up, w_down):
    B, T, D = h.shape
    n_tok = B * T
    x = h.reshape(n_tok, D)
    grp_logits = (x @ w_rg).astype(jnp.float32)
    grp_prob = jax.nn.softmax(grp_logits, axis=-1)
    grp_idx = jnp.argmax(grp_logits, axis=-1).astype(jnp.int32)
    grp_gate = jnp.max(grp_prob, axis=-1, keepdims=True)
    exp_logits = (x @ w_re).astype(jnp.float32).reshape(n_tok, N_GROUPS, EXPERTS_PER_GROUP)
    exp_logits = exp_logits[jnp.arange(n_tok), grp_idx]
    top_val, top_idx = lax.top_k(exp_logits, TOP_K)
    gate = grp_gate * jax.nn.softmax(top_val, axis=-1)
    eid = grp_idx[:, None] * EXPERTS_PER_GROUP + top_idx.astype(jnp.int32)
    n_asg = n_tok * TOP_K
    flat_e = eid.reshape(-1)
    flat_tok = jnp.repeat(jnp.arange(n_tok, dtype=jnp.int32), TOP_K)
    flat_w = gate.reshape(-1)
    order = jnp.argsort(flat_e)
    se = flat_e[order]
    counts = jax.ops.segment_sum(jnp.ones_like(flat_e), flat_e, num_segments=N_EXPERTS)
    padded = (counts + MOE_BLOCK - 1) // MOE_BLOCK * MOE_BLOCK
    pad_end = jnp.cumsum(padded)
    pad_start = pad_end - padded
    start = jnp.cumsum(counts) - counts
    dest = pad_start[se] + jnp.arange(n_asg, dtype=jnp.int32) - start[se]
    n_rows = (n_asg + N_EXPERTS * MOE_BLOCK + MOE_BLOCK - 1) // MOE_BLOCK * MOE_BLOCK
    n_blocks = n_rows // MOE_BLOCK
    row_tok = jnp.zeros((n_rows,), jnp.int32).at[dest].set(flat_tok[order])
    row_w = jnp.zeros((n_rows,), jnp.float32).at[dest].set(flat_w[order])
    blk_start = jnp.arange(n_blocks, dtype=jnp.int32) * MOE_BLOCK
    blk_e = jnp.minimum(jnp.searchsorted(pad_end, blk_start, side='right'), N_EXPERTS - 1)
    xb = x[row_tok].reshape(n_blocks, MOE_BLOCK, D)

    def expert_block(args):
        xi, e = args
        hdn = jax.nn.silu(xi @ w_gate[e]) * (xi @ w_up[e])
        return hdn @ w_down[e]

    yb = lax.map(expert_block, (xb, blk_e)).reshape(n_rows, D)
    out = jnp.zeros((n_tok, D), jnp.float32).at[row_tok].add(yb.astype(jnp.float32) * row_w[:, None])
    return out.astype(h.dtype).reshape(B, T, D)


def setup_inputs(seed: int = 0) -> dict:
    key = jax.random.key(seed)
    ks = iter(jax.random.split(key, 64))
    D = D_MODEL
    NA, NB, L = N_A_LAYERS, N_B_LAYERS, DEPTH

    def nrm(shape, scale):
        return jax.random.normal(next(ks), shape, jnp.float32) * scale

    def gain(shape):
        return 1.0 + nrm(shape, 0.02)

    def unif(shape, lo, hi):
        return jax.random.uniform(next(ks), shape, jnp.float32, lo, hi)

    return {
        'x': nrm((BATCH, SEQ, D), 1.0),
        'c': nrm((BATCH, D), 1.0),
        'ada_w': nrm((L, D, 6 * D), 0.5 * D ** -0.5),
        'ada_b': nrm((L, 6 * D), 0.02),
        'norm_mix_g': gain((L, D)),
        'norm_ffn_g': gain((L, D)),
        'rw_mu': unif((NA, 6, D), 0.0, 1.0),
        'rw_w_rkv': nrm((NA, 3, D, D), D ** -0.5),
        'rw_w0': unif((NA, D), -6.0, -1.0),
        'rw_w1': nrm((NA, D, DECAY_LORA), D ** -0.5),
        'rw_w2': nrm((NA, DECAY_LORA, D), 0.1 * DECAY_LORA ** -0.5),
        'rw_a0': nrm((NA, D), 0.1),
        'rw_a1': nrm((NA, D, AAA_LORA), D ** -0.5),
        'rw_a2': nrm((NA, AAA_LORA, D), 0.1 * AAA_LORA ** -0.5),
        'rw_g1': nrm((NA, D, GATE_LORA), D ** -0.5),
        'rw_g2': nrm((NA, GATE_LORA, D), GATE_LORA ** -0.5),
        'rw_k_k': 0.85 + nrm((NA, D), 0.05),
        'rw_k_a': 1.0 + nrm((NA, D), 0.05),
        'rw_r_k': nrm((NA, RWKV_HEADS, RWKV_HEAD), 0.1),
        'rw_gn_g': gain((NA, D)),
        'rw_gn_b': nrm((NA, D), 0.02),
        'rw_w_o': nrm((NA, D, D), D ** -0.5),
        'ada_kv_w': nrm((D, 2 * D), 0.5 * D ** -0.5),
        'ada_kv_b': nrm((2 * D,), 0.02),
        'norm_kv_g': gain((D,)),
        'w_kv': nrm((D, 2 * D), D ** -0.5),
        'df_w_q': nrm((NB, D, D), D ** -0.5),
        'df_lq1': nrm((NB, DIFF_QK_DIM), 0.1),
        'df_lk1': nrm((NB, DIFF_QK_DIM), 0.1),
        'df_lq2': nrm((NB, DIFF_QK_DIM), 0.1),
        'df_lk2': nrm((NB, DIFF_QK_DIM), 0.1),
        'df_subln_g': gain((NB, DIFF_V_DIM)),
        'df_w_o': nrm((NB, D, D), D ** -0.5),
        'moe_w_rg': nrm((L, D, N_GROUPS), D ** -0.5),
        'moe_w_re': nrm((L, D, N_EXPERTS), D ** -0.5),
        'moe_w_gate': nrm((L, N_EXPERTS, D, EXPERT_FF), D ** -0.5),
        'moe_w_up': nrm((L, N_EXPERTS, D, EXPERT_FF), D ** -0.5),
        'moe_w_down': nrm((L, N_EXPERTS, EXPERT_FF, D), EXPERT_FF ** -0.5),
        'final_g': gain((D,)),
    }


def reference(x, c, ada_w, ada_b, norm_mix_g, norm_ffn_g,
              rw_mu, rw_w_rkv, rw_w0, rw_w1, rw_w2, rw_a0, rw_a1, rw_a2, rw_g1, rw_g2,
              rw_k_k, rw_k_a, rw_r_k, rw_gn_g, rw_gn_b, rw_w_o,
              ada_kv_w, ada_kv_b, norm_kv_g, w_kv,
              df_w_q, df_lq1, df_lk1, df_lq2, df_lk2, df_subln_g, df_w_o,
              moe_w_rg, moe_w_re, moe_w_gate, moe_w_up, moe_w_down, final_g):
    B, T, D = x.shape
    c_act = jax.nn.silu(c)
    k_sh = None
    v_sh = None
    for l in range(DEPTH):
        mod = c_act @ ada_w[l] + ada_b[l]
        sh_m, sc_m, g_m, sh_f, sc_f, g_f = jnp.split(mod, 6, axis=-1)
        if l < N_A_LAYERS:
            i = l
            h = modulate(rmsnorm(x, norm_mix_g[l]), sh_m, sc_m)
            y = rwkv7_time_mix(h, rw_mu[i], rw_w_rkv[i], rw_w0[i], rw_w1[i], rw_w2[i],
                               rw_a0[i], rw_a1[i], rw_a2[i], rw_g1[i], rw_g2[i],
                               rw_k_k[i], rw_k_a[i], rw_r_k[i], rw_gn_g[i], rw_gn_b[i], rw_w_o[i])
        else:
            if l == N_A_LAYERS:
                sh_kv, sc_kv = jnp.split(c_act @ ada_kv_w + ada_kv_b, 2, axis=-1)
                hk = modulate(rmsnorm(x, norm_kv_g), sh_kv, sc_kv)
                kv = hk @ w_kv
                k_sh = kv[..., :D].reshape(B, T, DIFF_HEADS, 2, DIFF_QK_DIM)
                v_sh = kv[..., D:].reshape(B, T, DIFF_HEADS, DIFF_V_DIM)
            j = l - N_A_LAYERS
            h = modulate(rmsnorm(x, norm_mix_g[l]), sh_m, sc_m)
            lambda_init = 0.8 - 0.6 * math.exp(-0.3 * l)
            y = diff_attention(h, k_sh, v_sh, df_w_q[j], df_lq1[j], df_lk1[j], df_lq2[j], df_lk2[j],
                               df_subln_g[j], df_w_o[j], lambda_init)
        x = x + g_m[:, None, :] * y
        h = modulate(rmsnorm(x, norm_ffn_g[l]), sh_f, sc_f)
        x = x + g_f[:, None, :] * hier_moe(h, moe_w_rg[l], moe_w_re[l], moe_w_gate[l],
                                           moe_w_up[l], moe_w_down[l])
    return rmsnorm(x, final_g)
```

```python
import functools
import math

import jax
import jax.numpy as jnp
from jax import lax
from jax.experimental import pallas as pl
from jax.experimental.pallas import tpu as pltpu

F32 = jnp.float32
BF16 = jnp.bfloat16

D_MODEL = 1024
DEPTH = 2
N_A_LAYERS = DEPTH // 2
RWKV_HEAD = 64
RWKV_HEADS = D_MODEL // RWKV_HEAD
RWKV_GN_EPS = 64e-5
DIFF_QK_DIM = 64
DIFF_V_DIM = 2 * DIFF_QK_DIM
DIFF_HEADS = D_MODEL // DIFF_V_DIM
SUBLN_EPS = 1e-5
N_GROUPS = 4
EXPERTS_PER_GROUP = 8
N_EXPERTS = N_GROUPS * EXPERTS_PER_GROUP
EXPERT_FF = 512
RMS_EPS = 1e-6

LANES = 128
SUBLANES = 8
VMEM_LIMIT_BYTES = 56 * 1024 * 1024

SCAN_CHUNK = 64
PAIR = 2 * RWKV_HEAD
PROJ_TM = 256
DENSE_TM = 512
ATT_TQ = 256
ATT_TK = 256
MOE_TM = 128
ROUTE_TM = 512
RANK_TB = 512
ROW_TB = 256


def _cparams(*sem):
    return pltpu.CompilerParams(dimension_semantics=sem, vmem_limit_bytes=VMEM_LIMIT_BYTES)


def _dot(a, b):
    return jnp.dot(a.astype(BF16), b.astype(BF16), preferred_element_type=F32)


def _dot_nt(a, b):
    return lax.dot_general(a.astype(BF16), b.astype(BF16), (((1,), (1,)), ((), ())), preferred_element_type=F32)


def _split(x):
    hi = x.astype(BF16)
    lo = (x - hi.astype(F32)).astype(BF16)
    return hi, lo


def _dot3(a, b):
    ah, al = _split(a)
    bh, bl = _split(b)
    d = functools.partial(jnp.dot, preferred_element_type=F32)
    return d(ah, bh) + d(ah, bl) + d(al, bh)


def _dot_hl(a, b_exact):
    ah, al = _split(a)
    d = functools.partial(jnp.dot, preferred_element_type=F32)
    return d(ah, b_exact) + d(al, b_exact)


def _rms_mod(x, g, shift, scale):
    ms = jnp.mean(x * x, axis=-1, keepdims=True)
    return (x * lax.rsqrt(ms + RMS_EPS) * g) * (1.0 + scale) + shift


def _ada_kernel(c_ref, w_ref, b_ref, o_ref):
    c = c_ref[...]
    ca = c * jax.nn.sigmoid(c)
    o_ref[...] = _dot3(ca, w_ref[...]) + b_ref[...]


def _ada(c_pad, w, b, tn):
    L, D, N = w.shape
    return pl.pallas_call(
        _ada_kernel,
        out_shape=jax.ShapeDtypeStruct((L, SUBLANES, N), F32),
        grid=(L, N // tn),
        in_specs=[
            pl.BlockSpec((SUBLANES, D), lambda l, j: (0, 0)),
            pl.BlockSpec((None, D, tn), lambda l, j: (l, 0, j)),
            pl.BlockSpec((None, 1, tn), lambda l, j: (l, 0, j)),
        ],
        out_specs=pl.BlockSpec((None, SUBLANES, tn), lambda l, j: (l, 0, j)),
        compiler_params=_cparams("arbitrary", "arbitrary"),
        name="ada_mod",
    )(c_pad, w, b.reshape(L, 1, N))


def _rwkv_proj_kernel(x_ref, xp_ref, mod_ref, g_ref, mu_ref, wrkv_ref, w1_ref, w2_ref, a1_ref, a2_ref,
                      g1_ref, g2_ref, vec_ref, r_ref, k_ref, v_ref, lw_ref, kk_ref, al_ref, gate_ref):
    i = pl.program_id(1)
    g = g_ref[...]
    shift, scale = mod_ref[0, 0:1, :], mod_ref[0, 1:2, :]
    h = _rms_mod(x_ref[0], g, shift, scale)
    hp = _rms_mod(xp_ref[0, SUBLANES - 1:SUBLANES, :], g, shift, scale)
    hp = jnp.where(i == 0, 0.0, hp)
    row = lax.broadcasted_iota(jnp.int32, h.shape, 0)
    h_prev = jnp.where(row == 0, hp, pltpu.roll(h, 1, axis=0))
    xx = h_prev - h
    mu = mu_ref[...]
    xs = [(h + xx * mu[j:j + 1, :]).astype(BF16) for j in range(6)]
    w0, a0, k_k, k_a = (vec_ref[j:j + 1, :] for j in range(4))
    d = functools.partial(jnp.dot, preferred_element_type=F32)
    r = d(xs[0], wrkv_ref[0])
    k = d(xs[1], wrkv_ref[1])
    v = d(xs[2], wrkv_ref[2])
    z = w0 + _dot(jnp.tanh(d(xs[3], w1_ref[...])), w2_ref[...])
    lw = (-math.exp(-0.5)) * jax.nn.sigmoid(z)
    a = jax.nn.sigmoid(a0 + _dot(d(xs[4], a1_ref[...]), a2_ref[...]))
    gate = _dot(jax.nn.sigmoid(d(xs[5], g1_ref[...])), g2_ref[...])
    r_ref[0] = r
    k_ref[0] = k * (1.0 + (a - 1.0) * k_a)
    v_ref[0] = v
    lw_ref[0] = lw
    kk_ref[0] = k * k_k
    al_ref[0] = a
    gate_ref[0] = gate


def _rwkv_proj(x, mod2, g, mu, wrkv, w1, w2, a1, a2, g1, g2, vec):
    B, T, D = x.shape
    tm = PROJ_TM
    const2 = lambda b, i: (0, 0)
    const3 = lambda b, i: (0, 0, 0)
    act = pl.BlockSpec((1, tm, D), lambda b, i: (b, i, 0))
    n_sub = tm // SUBLANES
    return pl.pallas_call(
        _rwkv_proj_kernel,
        out_shape=[jax.ShapeDtypeStruct((B, T, D), F32)] * 7,
        grid=(B, T // tm),
        in_specs=[
            act,
            pl.BlockSpec((1, SUBLANES, D), lambda b, i: (b, jnp.maximum(i * n_sub - 1, 0), 0)),
            pl.BlockSpec((1, 2, D), lambda b, i: (b, 0, 0)),
            pl.BlockSpec((1, D), const2),
            pl.BlockSpec((6, D), const2),
            pl.BlockSpec((3, D, D), const3),
            pl.BlockSpec(w1.shape, const2), pl.BlockSpec(w2.shape, const2),
            pl.BlockSpec(a1.shape, const2), pl.BlockSpec(a2.shape, const2),
            pl.BlockSpec(g1.shape, const2), pl.BlockSpec(g2.shape, const2),
            pl.BlockSpec((4, D), const2),
        ],
        out_specs=[act] * 7,
        compiler_params=_cparams("arbitrary", "arbitrary"),
        name="rwkv_proj",
    )(x, x, mod2, g, mu, wrkv, w1, w2, a1, a2, g1, g2, vec)


def _rwkv_scan_kernel(r_ref, k_ref, v_ref, lw_ref, kk_ref, al_ref, pv_ref, y_ref, h_ref):
    C = SCAN_CHUNK
    P2 = 2 * C

    @pl.when(pl.program_id(1) == 0)
    def _():
        h_ref[...] = jnp.zeros_like(h_ref)

    lane = lax.broadcasted_iota(jnp.int32, (1, PAIR), 1)
    m_left = (lane < RWKV_HEAD).astype(F32)
    m_right = 1.0 - m_left
    ri = lax.broadcasted_iota(jnp.int32, (P2, P2), 0)
    ci = lax.broadcasted_iota(jnp.int32, (P2, P2), 1)
    same = (ri >= C) == (ci >= C)
    strict = same & (ri > ci)
    incl = same & (ri >= ci)
    eye = ri == ci
    block_ones = same.astype(BF16)
    tri = (lax.broadcasted_iota(jnp.int32, (C, C), 0) >= lax.broadcasted_iota(jnp.int32, (C, C), 1)).astype(BF16)

    def stack(x):
        return jnp.concatenate([x * m_left, x * m_right], axis=0)

    inv_n = 1.0 / RWKV_HEAD
    for p in range(RWKV_HEADS // 2):
        sl = slice(p * PAIR, (p + 1) * PAIR)
        r, k, v = r_ref[0, :, sl], k_ref[0, :, sl], v_ref[0, :, sl]
        lw, kkr, al = lw_ref[0, :, sl], kk_ref[0, :, sl], al_ref[0, :, sl]
        r_k, gn_g, gn_b = pv_ref[0:1, sl], pv_ref[1:2, sl], pv_ref[2:3, sl]

        ss = _dot_hl(kkr * kkr, block_ones)
        kk = kkr * lax.rsqrt(jnp.maximum(ss, 1e-24))
        b_vec = kk * al
        l_hi, l_lo = _split(lw)
        l_lo2 = (lw - l_hi.astype(F32) - l_lo.astype(F32)).astype(BF16)
        dd = functools.partial(jnp.dot, preferred_element_type=F32)
        L = dd(tri, l_hi) + dd(tri, l_lo) + dd(tri, l_lo2)
        LC = L[C - 1:C, :]
        e_pos = jnp.exp(L)
        e_neg = jnp.exp(-L)
        e_end = jnp.exp(LC - L)
        At = stack(-kk * jnp.exp(L - lw))
        Rt = stack(r * e_pos)
        Bt = stack(b_vec * e_neg)
        Kt = stack(k * e_neg)
        Bh = stack(b_vec * e_end)
        Kh = stack(k * e_end)
        Vs = stack(v)

        G = _dot_nt(jnp.concatenate([At, Rt], axis=0), jnp.concatenate([Bt, Kt], axis=0))
        A_ab = jnp.where(strict, G[:P2, :P2], 0.0)
        A_ak = jnp.where(strict, G[:P2, P2:], 0.0)
        A_rb = jnp.where(incl, G[P2:, :P2], 0.0)
        A_rk = jnp.where(incl, G[P2:, P2:], 0.0)

        Z = jnp.concatenate([At, _dot(A_ak, Vs)], axis=1)
        Ai = A_ab.astype(BF16)
        n_dbl = int(math.log2(C))
        for s in range(n_dbl):
            Z = Z + _dot(Ai, Z)
            if s + 1 < n_dbl:
                Ai = dd(Ai, Ai).astype(BF16)
        Pm, Qm = Z[:, :PAIR], Z[:, PAIR:]
        rhs = jnp.concatenate([Z, jnp.concatenate([jnp.zeros_like(Vs), Vs], axis=1)], axis=0).astype(BF16)
        o6 = dd(jnp.concatenate([A_rb, A_rk], axis=1).astype(BF16), rhs)
        Rp = Rt + o6[:, :PAIR]
        Y0 = o6[:, PAIR:]
        o7 = dd(jnp.concatenate([Bh, Kh], axis=0).T.astype(BF16), rhs)
        Mbd = o7[:, :PAIR] + jnp.where(eye, jnp.exp(LC), 0.0)
        Gbd = o7[:, PAIR:]

        Hbd = h_ref[p]
        Hb = Hbd.astype(BF16)
        Y = dd(Rp.astype(BF16), Hb) + Y0
        h_ref[p] = dd(Mbd.astype(BF16), Hb) + Gbd
        y = Y[:C] + Y[C:]

        mean = _dot_hl(y, block_ones) * inv_n
        yc = y - mean
        var = _dot_hl(yc * yc, block_ones) * inv_n
        bonus = _dot_hl(r * k * r_k, block_ones) * v
        y_ref[0, :, sl] = yc * lax.rsqrt(var + RWKV_GN_EPS) * gn_g + gn_b + bonus


def _rwkv_scan(r, k, v, lw, kk, al, pvec):
    B, T, D = r.shape
    C = SCAN_CHUNK
    act = pl.BlockSpec((1, C, D), lambda b, c: (b, c, 0))
    return pl.pallas_call(
        _rwkv_scan_kernel,
        out_shape=jax.ShapeDtypeStruct((B, T, D), F32),
        grid=(B, T // C),
        in_specs=[act] * 6 + [pl.BlockSpec((3, D), lambda b, c: (0, 0))],
        out_specs=act,
        scratch_shapes=[pltpu.VMEM((RWKV_HEADS // 2, PAIR, PAIR), F32)],
        compiler_params=_cparams("arbitrary", "arbitrary"),
        name="rwkv_scan",
    )(r, k, v, lw, kk, al, pvec)


def _proj_res_kernel(*refs, has_gate):
    if has_gate:
        y_ref, g_ref, x_ref, gm_ref, w_ref, o_ref = refs
        y = y_ref[0] * g_ref[0]
    else:
        y_ref, x_ref, gm_ref, w_ref, o_ref = refs
        y = y_ref[0]
    o_ref[0] = x_ref[0] + gm_ref[0] * jnp.dot(y.astype(BF16), w_ref[...], preferred_element_type=F32)


def _proj_res(y, g, x, gm, w):
    B, T, D = x.shape
    tm = DENSE_TM
    act = pl.BlockSpec((1, tm, D), lambda b, i: (b, i, 0))
    ins = [y] + ([g] if g is not None else []) + [x, gm, w]
    specs = [act] * (len(ins) - 2) + [pl.BlockSpec((1, 1, D), lambda b, i: (b, 0, 0)),
                                     pl.BlockSpec((D, D), lambda b, i: (0, 0))]
    return pl.pallas_call(
        functools.partial(_proj_res_kernel, has_gate=g is not None),
        out_shape=jax.ShapeDtypeStruct((B, T, D), F32),
        grid=(B, T // tm),
        in_specs=specs,
        out_specs=act,
        compiler_params=_cparams("arbitrary", "arbitrary"),
        name="proj_res",
    )(*ins)


def _norm_mm_kernel(x_ref, mod_ref, g_ref, w_ref, o_ref):
    h = _rms_mod(x_ref[0], g_ref[...], mod_ref[0, 0:1, :], mod_ref[0, 1:2, :])
    o_ref[0] = jnp.dot(h.astype(BF16), w_ref[...], preferred_element_type=F32).astype(o_ref.dtype)


def _norm_mm(x, mod2, g, w, out_dtype):
    B, T, D = x.shape
    N = w.shape[1]
    tm = DENSE_TM
    return pl.pallas_call(
        _norm_mm_kernel,
        out_shape=jax.ShapeDtypeStruct((B, T, N), out_dtype),
        grid=(B, T // tm),
        in_specs=[
            pl.BlockSpec((1, tm, D), lambda b, i: (b, i, 0)),
            pl.BlockSpec((1, 2, D), lambda b, i: (b, 0, 0)),
            pl.BlockSpec((1, D), lambda b, i: (0, 0)),
            pl.BlockSpec((D, N), lambda b, i: (0, 0)),
        ],
        out_specs=pl.BlockSpec((1, tm, N), lambda b, i: (b, i, 0)),
        compiler_params=_cparams("arbitrary", "arbitrary"),
        name="norm_mm",
    )(x, mod2, g, w)


def _diff_attn_kernel(q_ref, k_ref, v_ref, lam_ref, sg_ref, o_ref, *, lambda_init):
    tq, tk = ATT_TQ, ATT_TK
    qi = pl.program_id(2)
    lane = lax.broadcasted_iota(jnp.int32, (1, DIFF_V_DIM), 1)
    m_left = (lane < DIFF_QK_DIM).astype(F32)
    q = q_ref[0].astype(F32) * (DIFF_QK_DIM ** -0.5)
    qs = jnp.concatenate([q * m_left, q * (1.0 - m_left)], axis=0).astype(BF16)
    q_pos = qi * tq + lax.broadcasted_iota(jnp.int32, (2 * tq, tk), 0) % tq
    k_off = lax.broadcasted_iota(jnp.int32, (2 * tq, tk), 1)
    ones_col = jnp.ones((tk, DIFF_V_DIM), BF16)

    def body(j, carry):
        m, acc = carry
        kb = k_ref[0, pl.ds(j * tk, tk), :]
        vb = v_ref[0, pl.ds(j * tk, tk), :]
        s = lax.dot_general(qs, kb, (((1,), (1,)), ((), ())), preferred_element_type=F32)
        s = jnp.where(j * tk + k_off <= q_pos, s, -jnp.inf)
        m_new = jnp.maximum(m, jnp.max(s, axis=-1, keepdims=True))
        alpha = jnp.exp(m - m_new)
        p = jnp.exp(s - m_new).astype(BF16)
        pv = jnp.dot(p, jnp.concatenate([vb, ones_col], axis=1), preferred_element_type=F32)
        return m_new, acc * alpha + pv

    m0 = jnp.full((2 * tq, 1), -jnp.inf, F32)
    acc0 = jnp.zeros((2 * tq, 2 * DIFF_V_DIM), F32)
    n_kv = (qi * tq + tq + tk - 1) // tk
    _, acc = lax.fori_loop(0, n_kv, body, (m0, acc0))
    o = acc[:, :DIFF_V_DIM] / acc[:, DIFF_V_DIM:DIFF_V_DIM + 1]
    lv = lam_ref[...]
    lam = (jnp.exp(jnp.sum(lv[0:1] * lv[1:2], axis=-1, keepdims=True))
           - jnp.exp(jnp.sum(lv[2:3] * lv[3:4], axis=-1, keepdims=True)) + lambda_init)
    o = o[:tq] - lam * o[tq:]
    ms = jnp.mean(o * o, axis=-1, keepdims=True)
    o_ref[0] = (o * lax.rsqrt(ms + SUBLN_EPS) * sg_ref[...] * (1.0 - lambda_init)).astype(o_ref.dtype)


def _diff_attn(q, kv, lam_vecs, subln_g, lambda_init):
    B, T, D = q.shape
    H = DIFF_HEADS
    tq = ATT_TQ
    return pl.pallas_call(
        functools.partial(_diff_attn_kernel, lambda_init=lambda_init),
        out_shape=jax.ShapeDtypeStruct((B, T, D), BF16),
        grid=(B, H, T // tq),
        in_specs=[
            pl.BlockSpec((1, tq, DIFF_V_DIM), lambda b, h, i: (b, i, h)),
            pl.BlockSpec((1, T, DIFF_V_DIM), lambda b, h, i: (b, 0, h)),
            pl.BlockSpec((1, T, DIFF_V_DIM), lambda b, h, i: (b, 0, H + h)),
            pl.BlockSpec((4, DIFF_QK_DIM), lambda b, h, i: (0, 0)),
            pl.BlockSpec((1, DIFF_V_DIM), lambda b, h, i: (0, 0)),
        ],
        out_specs=pl.BlockSpec((1, tq, DIFF_V_DIM), lambda b, h, i: (b, i, h)),
        compiler_params=_cparams("arbitrary", "arbitrary", "arbitrary"),
        name="diff_attn",
    )(q, kv, kv, lam_vecs, subln_g)


def _router_kernel(x_ref, mod_ref, g_ref, w_ref, info_ref):
    h = _rms_mod(x_ref[...], g_ref[...], mod_ref[0, 0:1, :], mod_ref[0, 1:2, :])
    logit = _dot3(h, w_ref[...])
    lane_i = lax.broadcasted_iota(jnp.int32, logit.shape, 1)
    lane = lane_i.astype(F32)
    neg = -jnp.inf
    big = float(LANES)
    is_grp = lane_i < N_GROUPS
    gl = jnp.where(is_grp, logit, neg)
    gmax = jnp.max(gl, axis=-1, keepdims=True)
    gidx = jnp.min(jnp.where(gl == gmax, lane, big), axis=-1, keepdims=True)
    grp_gate = 1.0 / jnp.sum(jnp.where(is_grp, jnp.exp(logit - gmax), 0.0), axis=-1, keepdims=True)
    lo = N_GROUPS + gidx * EXPERTS_PER_GROUP
    in_grp = (lane >= lo) & (lane < lo + EXPERTS_PER_GROUP)
    el = jnp.where(in_grp, logit, neg)
    t1 = jnp.max(el, axis=-1, keepdims=True)
    i1 = jnp.min(jnp.where(el == t1, lane, big), axis=-1, keepdims=True)
    el2 = jnp.where(lane == i1, neg, el)
    t2 = jnp.max(el2, axis=-1, keepdims=True)
    i2 = jnp.min(jnp.where(el2 == t2, lane, big), axis=-1, keepdims=True)
    e21 = jnp.exp(t2 - t1)
    p1 = 1.0 / (1.0 + e21)
    w1 = grp_gate * p1
    w2 = grp_gate * (e21 * p1)
    e1 = i1 - N_GROUPS
    e2 = i2 - N_GROUPS
    info_ref[...] = jnp.where(lane_i == 0, e1, jnp.where(lane_i == 1, e2, jnp.where(lane_i == 2, w1, jnp.where(lane_i == 3, w2, 0.0))))


def _router(x2, mod2, g, w_cat, T):
    n_tok, D = x2.shape
    tm = ROUTE_TM
    per_b = T // tm
    return pl.pallas_call(
        _router_kernel,
        out_shape=jax.ShapeDtypeStruct((n_tok, LANES), F32),
        grid=(n_tok // tm,),
        in_specs=[
            pl.BlockSpec((tm, D), lambda i: (i, 0)),
            pl.BlockSpec((1, 2, D), lambda i: (i // per_b, 0, 0)),
            pl.BlockSpec((1, D), lambda i: (0, 0)),
            pl.BlockSpec((D, LANES), lambda i: (0, 0)),
        ],
        out_specs=pl.BlockSpec((tm, LANES), lambda i: (i, 0)),
        compiler_params=_cparams("arbitrary"),
        name="moe_router",
    )(x2, mod2, g, w_cat)


def _rank_kernel(info_ref, dest_ref, meta_ref, cnt_ref, start_ref):
    ps = pl.program_id(0)
    i = pl.program_id(1)
    tb = info_ref.shape[0]
    lane = lax.broadcasted_iota(jnp.int32, (tb, LANES), 1)
    info = info_ref[...]
    e0 = info[:, 0:1].astype(jnp.int32)
    e1 = info[:, 1:2].astype(jnp.int32)
    o0 = (lane == e0).astype(F32)
    o1 = (lane == e1).astype(F32)
    both = o0 + o1

    @pl.when((ps == 0) & (i == 0))
    def _():
        cnt_ref[...] = jnp.zeros_like(cnt_ref)

    @pl.when(ps == 0)
    def _():
        cnt_ref[...] += jnp.sum(both, axis=0, keepdims=True)
        dest_ref[...] = jnp.zeros_like(dest_ref)
        meta_ref[...] = jnp.zeros_like(meta_ref)

    @pl.when((ps == 1) & (i == 0))
    def _():
        cnt = cnt_ref[...]
        padded = jnp.floor((cnt + (MOE_TM - 1)) * (1.0 / MOE_TM)) * MOE_TM
        r = lax.broadcasted_iota(jnp.int32, (LANES, LANES), 0)
        c = lax.broadcasted_iota(jnp.int32, (LANES, LANES), 1)
        upper_strict = (r < c).astype(BF16)
        start = _dot_hl(jnp.broadcast_to(padded, (SUBLANES, LANES)), upper_strict)[0:1]
        start_ref[...] = start
        meta_ref[...] = jnp.broadcast_to(start + padded, (SUBLANES, LANES))

    @pl.when(ps == 1)
    def _():
        r = lax.broadcasted_iota(jnp.int32, (tb, tb), 0)
        c = lax.broadcasted_iota(jnp.int32, (tb, tb), 1)
        lower_strict = (r > c).astype(BF16)
        before = jnp.dot(lower_strict, both.astype(BF16), preferred_element_type=F32) + start_ref[...]
        d0 = jnp.sum(o0 * before, axis=-1, keepdims=True)
        d1 = jnp.sum(o1 * before, axis=-1, keepdims=True)
        dest_ref[...] = jnp.where(lane == 0, d0, jnp.where(lane == 1, d1, 0.0)).astype(jnp.int32)
        start_ref[...] += jnp.sum(both, axis=0, keepdims=True)


def _rank(info):
    n_tok = info.shape[0]
    tb = RANK_TB
    return pl.pallas_call(
        _rank_kernel,
        out_shape=[jax.ShapeDtypeStruct((n_tok, LANES), jnp.int32), jax.ShapeDtypeStruct((SUBLANES, LANES), F32)],
        grid=(2, n_tok // tb),
        in_specs=[pl.BlockSpec((tb, LANES), lambda p, i: (i, 0))],
        out_specs=[pl.BlockSpec((tb, LANES), lambda p, i: (i * p, 0)), pl.BlockSpec((SUBLANES, LANES), lambda p, i: (0, 0))],
        scratch_shapes=[pltpu.VMEM((1, LANES), F32), pltpu.VMEM((1, LANES), F32)],
        compiler_params=_cparams("arbitrary", "arbitrary"),
        name="moe_rank",
    )(info)


def _row_copy(src_ref, s, dst_ref, d, sem):
    return pltpu.make_async_copy(src_ref.at[pl.ds(s, 1)], dst_ref.at[pl.ds(d, 1)], sem)


def _dispatch_kernel(dest_ref, x_ref, mod_ref, g_ref, xs_in_ref, xs_ref, h_ref, sem):
    del xs_in_ref
    tb = h_ref.shape[0]
    h_ref[...] = _rms_mod(x_ref[...], g_ref[...], mod_ref[0, 0:1, :], mod_ref[0, 1:2, :])

    def start(j, c):
        _row_copy(h_ref, j, xs_ref, dest_ref[0, 0, 2 * j], sem).start()
        _row_copy(h_ref, j, xs_ref, dest_ref[0, 0, 2 * j + 1], sem).start()
        return c

    lax.fori_loop(0, tb, start, 0, unroll=8)

    def wait(j, c):
        _row_copy(h_ref, 0, xs_ref, 0, sem).wait()
        return c

    lax.fori_loop(0, 2 * tb, wait, 0, unroll=8)


def _dispatch(dest3, x2, mod2, g, xs_zero, T):
    n_tok, D = x2.shape
    tb = ROW_TB
    per_b = T // tb
    return pl.pallas_call(
        _dispatch_kernel,
        out_shape=jax.ShapeDtypeStruct(xs_zero.shape, F32),
        grid=(n_tok // tb,),
        in_specs=[
            pl.BlockSpec((1, 1, 2 * tb), lambda i: (i, 0, 0), memory_space=pltpu.SMEM),
            pl.BlockSpec((tb, D), lambda i: (i, 0)),
            pl.BlockSpec((1, 2, D), lambda i: (i // per_b, 0, 0)),
            pl.BlockSpec((1, D), lambda i: (0, 0)),
            pl.BlockSpec(memory_space=pl.ANY),
        ],
        out_specs=pl.BlockSpec(memory_space=pl.ANY),
        scratch_shapes=[pltpu.VMEM((tb, D), F32), pltpu.SemaphoreType.DMA(())],
        input_output_aliases={4: 0},
        compiler_params=_cparams("arbitrary"),
        name="moe_dispatch",
    )(dest3, x2, mod2, g, xs_zero)


def _expert_kernel(be_ref, nb_ref, xs_ref, wg_ref, wu_ref, wd_ref, ys_ref, wgb, wub, wdb):
    i = pl.program_id(0)
    changed = (i == 0) | (be_ref[i] != be_ref[jnp.maximum(i - 1, 0)])

    @pl.when(changed)
    def _():
        wgb[...] = wg_ref[0].astype(BF16)
        wub[...] = wu_ref[0].astype(BF16)
        wdb[...] = wd_ref[0].astype(BF16)

    @pl.when(i < nb_ref[0])
    def _():
        x = xs_ref[...].astype(BF16)
        a = jnp.dot(x, wgb[...], preferred_element_type=F32)
        u = jnp.dot(x, wub[...], preferred_element_type=F32)
        hdn = (a * jax.nn.sigmoid(a)) * u
        ys_ref[...] = jnp.dot(hdn.astype(BF16), wdb[...], preferred_element_type=F32)

    @pl.when(i >= nb_ref[0])
    def _():
        ys_ref[...] = jnp.zeros_like(ys_ref)


def _experts(blk_e, n_used, xs, w_gate, w_up, w_down, layer):
    n_rows, D = xs.shape
    tm = MOE_TM
    FF = EXPERT_FF
    grid_spec = pltpu.PrefetchScalarGridSpec(
        num_scalar_prefetch=2,
        grid=(n_rows // tm,),
        in_specs=[
            pl.BlockSpec((tm, D), lambda i, be, nb: (i, 0)),
            pl.BlockSpec((None, 1, D, FF), lambda i, be, nb: (layer, be[i], 0, 0)),
            pl.BlockSpec((None, 1, D, FF), lambda i, be, nb: (layer, be[i], 0, 0)),
            pl.BlockSpec((None, 1, FF, D), lambda i, be, nb: (layer, be[i], 0, 0)),
        ],
        out_specs=pl.BlockSpec((tm, D), lambda i, be, nb: (i, 0)),
        scratch_shapes=[pltpu.VMEM((D, FF), BF16), pltpu.VMEM((D, FF), BF16), pltpu.VMEM((FF, D), BF16)],
    )
    return pl.pallas_call(
        _expert_kernel,
        out_shape=jax.ShapeDtypeStruct((n_rows, D), F32),
        grid_spec=grid_spec,
        compiler_params=_cparams("arbitrary"),
        name="moe_experts",
    )(blk_e, n_used, xs, w_gate, w_up, w_down)


def _combine_kernel(dest_ref, info_ref, x_ref, gf_ref, fg_ref, ys_ref, o_ref, y_ref, sem, *, final_norm):
    tb = x_ref.shape[0]

    def start(j, c):
        _row_copy(ys_ref, dest_ref[0, 0, 2 * j], y_ref.at[0], j, sem).start()
        _row_copy(ys_ref, dest_ref[0, 0, 2 * j + 1], y_ref.at[1], j, sem).start()
        return c

    lax.fori_loop(0, tb, start, 0, unroll=8)

    def wait(j, c):
        _row_copy(ys_ref, 0, y_ref.at[0], 0, sem).wait()
        return c

    lax.fori_loop(0, 2 * tb, wait, 0, unroll=8)
    info = info_ref[...]
    moe = info[:, 2:3] * y_ref[0] + info[:, 3:4] * y_ref[1]
    out = x_ref[...] + gf_ref[0] * moe
    if final_norm:
        ms = jnp.mean(out * out, axis=-1, keepdims=True)
        out = out * lax.rsqrt(ms + RMS_EPS) * fg_ref[...]
    o_ref[...] = out


def _combine(dest3, info, x2, gf, final_g, ys, T, final_norm):
    n_tok, D = x2.shape
    tb = ROW_TB
    per_b = T // tb
    return pl.pallas_call(
        functools.partial(_combine_kernel, final_norm=final_norm),
        out_shape=jax.ShapeDtypeStruct((n_tok, D), F32),
        grid=(n_tok // tb,),
        in_specs=[
            pl.BlockSpec((1, 1, 2 * tb), lambda i: (i, 0, 0), memory_space=pltpu.SMEM),
            pl.BlockSpec((tb, LANES), lambda i: (i, 0)),
            pl.BlockSpec((tb, D), lambda i: (i, 0)),
            pl.BlockSpec((1, 1, D), lambda i: (i // per_b, 0, 0)),
            pl.BlockSpec((1, D), lambda i: (0, 0)),
            pl.BlockSpec(memory_space=pl.ANY),
        ],
        out_specs=pl.BlockSpec((tb, D), lambda i: (i, 0)),
        scratch_shapes=[pltpu.VMEM((2, tb, D), F32), pltpu.SemaphoreType.DMA(())],
        compiler_params=_cparams("arbitrary"),
        name="moe_combine",
    )(dest3, info, x2, gf, final_g, ys)


def _moe_layer(x, mod_f, gf, norm_g, w_rg, w_re, w_gate, w_up, w_down, layer, final_g, final_norm):
    B, T, D = x.shape
    n_tok = B * T
    x2 = x.reshape(n_tok, D)
    w_cat = jnp.concatenate([w_rg, w_re, jnp.zeros((D, LANES - N_GROUPS - N_EXPERTS), F32)], axis=1)
    info = _router(x2, mod_f, norm_g, w_cat, T)
    dest, meta = _rank(info)
    pad_end = meta[0, :N_EXPERTS].astype(jnp.int32)
    n_rows = -(-(2 * n_tok + N_EXPERTS * MOE_TM) // MOE_TM) * MOE_TM
    n_blocks = n_rows // MOE_TM
    blk_start = jnp.arange(n_blocks, dtype=jnp.int32) * MOE_TM
    blk_e = jnp.minimum(jnp.sum(pad_end[None, :] <= blk_start[:, None], axis=1), N_EXPERTS - 1).astype(jnp.int32)
    n_used = (pad_end[N_EXPERTS - 1:] // MOE_TM).astype(jnp.int32)
    dest3 = dest[:, :2].reshape(n_tok // ROW_TB, 1, 2 * ROW_TB)
    xs = _dispatch(dest3, x2, mod_f, norm_g, jnp.zeros((n_rows, D), F32), T)
    ys = _experts(blk_e, n_used, xs, w_gate, w_up, w_down, layer)
    out = _combine(dest3, info, x2, gf, final_g, ys, T, final_norm)
    return out.reshape(B, T, D)


def kernel(x, c, ada_w, ada_b, norm_mix_g, norm_ffn_g, rw_mu, rw_w_rkv, rw_w0, rw_w1, rw_w2, rw_a0, rw_a1, rw_a2,
           rw_g1, rw_g2, rw_k_k, rw_k_a, rw_r_k, rw_gn_g, rw_gn_b, rw_w_o, ada_kv_w, ada_kv_b, norm_kv_g, w_kv,
           df_w_q, df_lq1, df_lk1, df_lq2, df_lk2, df_subln_g, df_w_o, moe_w_rg, moe_w_re, moe_w_gate, moe_w_up,
           moe_w_down, final_g):
    B, T, D = x.shape
    c_pad = jnp.zeros((SUBLANES, D), F32).at[:B].set(c)
    mod = _ada(c_pad, ada_w, ada_b, 6 * D // 4)[:, :B]
    mod_kv = _ada(c_pad, ada_kv_w[None], ada_kv_b[None], D)[0, :B]
    bf = lambda w: w.astype(BF16)
    row = lambda v: v.reshape(1, -1)

    for l in range(DEPTH):
        sh_m, sc_m, g_m, sh_f, sc_f, g_f = jnp.split(mod[l], 6, axis=-1)
        mod_m = jnp.stack([sh_m, sc_m], axis=1)
        mod_f = jnp.stack([sh_f, sc_f], axis=1)
        if l < N_A_LAYERS:
            i = l
            vec = jnp.stack([rw_w0[i], rw_a0[i], rw_k_k[i], rw_k_a[i]], axis=0)
            r, k, v, lw, kk, al, gate = _rwkv_proj(
                x, mod_m, row(norm_mix_g[l]), rw_mu[i], bf(rw_w_rkv[i]), bf(rw_w1[i]), bf(rw_w2[i]),
                bf(rw_a1[i]), bf(rw_a2[i]), bf(rw_g1[i]), bf(rw_g2[i]), vec)
            pvec = jnp.stack([rw_r_k[i].reshape(-1), rw_gn_g[i], rw_gn_b[i]], axis=0)
            y = _rwkv_scan(r, k, v, lw, kk, al, pvec)
            x = _proj_res(y, gate, x, g_m[:, None, :], bf(rw_w_o[i]))
        else:
            j = l - N_A_LAYERS
            if l == N_A_LAYERS:
                sh_kv, sc_kv = jnp.split(mod_kv, 2, axis=-1)
                kv = _norm_mm(x, jnp.stack([sh_kv, sc_kv], axis=1), row(norm_kv_g), bf(w_kv), BF16)
            q = _norm_mm(x, mod_m, row(norm_mix_g[l]), bf(df_w_q[j]), BF16)
            lambda_init = 0.8 - 0.6 * math.exp(-0.3 * l)
            lam_vecs = jnp.stack([df_lq1[j], df_lk1[j], df_lq2[j], df_lk2[j]], axis=0)
            o = _diff_attn(q, kv, lam_vecs, row(df_subln_g[j]), lambda_init)
            x = _proj_res(o, None, x, g_m[:, None, :], bf(df_w_o[j]))
        x = _moe_layer(x, mod_f, g_f[:, None, :], row(norm_ffn_g[l]), moe_w_rg[l], moe_w_re[l], moe_w_gate,
                       moe_w_up, moe_w_down, l, row(final_g), final_norm=(l == DEPTH - 1))
    return x
```

```python
import functools
import math

import jax
import jax.numpy as jnp
from jax import lax
from jax.experimental import pallas as pl
from jax.experimental.pallas import tpu as pltpu

F32 = jnp.float32
BF16 = jnp.bfloat16

D_MODEL = 1024
DEPTH = 2
N_A_LAYERS = DEPTH // 2
RWKV_HEAD = 64
RWKV_HEADS = D_MODEL // RWKV_HEAD
RWKV_GN_EPS = 64e-5
DIFF_QK_DIM = 64
DIFF_V_DIM = 2 * DIFF_QK_DIM
DIFF_HEADS = D_MODEL // DIFF_V_DIM
SUBLN_EPS = 1e-5
N_GROUPS = 4
EXPERTS_PER_GROUP = 8
N_EXPERTS = N_GROUPS * EXPERTS_PER_GROUP
EXPERT_FF = 512
RMS_EPS = 1e-6

LANES = 128
SUBLANES = 8
VMEM_LIMIT_BYTES = 56 * 1024 * 1024

SCAN_CHUNK = 64
PAIR = 2 * RWKV_HEAD
PROJ_TM = 256
DENSE_TM = 512
ATT_TQ = 256
ATT_HB = 4
MOE_TM = 128
ROUTE_TM = 512
RANK_TB = 512
ROW_TB = 256


def _cparams(*sem):
    return pltpu.CompilerParams(dimension_semantics=sem, vmem_limit_bytes=VMEM_LIMIT_BYTES)


def _dot(a, b):
    return jnp.dot(a.astype(BF16), b.astype(BF16), preferred_element_type=F32)


def _dot_nt(a, b):
    return lax.dot_general(a.astype(BF16), b.astype(BF16), (((1,), (1,)), ((), ())), preferred_element_type=F32)


def _split(x):
    hi = x.astype(BF16)
    lo = (x - hi.astype(F32)).astype(BF16)
    return hi, lo


def _dot3(a, b):
    ah, al = _split(a)
    bh, bl = _split(b)
    d = functools.partial(jnp.dot, preferred_element_type=F32)
    return d(ah, bh) + d(ah, bl) + d(al, bh)


def _dot_hl(a, b_exact):
    ah, al = _split(a)
    d = functools.partial(jnp.dot, preferred_element_type=F32)
    return d(ah, b_exact) + d(al, b_exact)


def _rms_mod(x, g, shift, scale):
    ms = jnp.mean(x * x, axis=-1, keepdims=True)
    return (x * lax.rsqrt(ms + RMS_EPS) * g) * (1.0 + scale) + shift


def _ada_kernel(c_ref, w_ref, b_ref, o_ref):
    c = c_ref[...]
    ca = c * jax.nn.sigmoid(c)
    o_ref[...] = _dot3(ca, w_ref[...]) + b_ref[...]


def _ada(c_pad, w, b, tn):
    L, D, N = w.shape
    return pl.pallas_call(
        _ada_kernel,
        out_shape=jax.ShapeDtypeStruct((L, SUBLANES, N), F32),
        grid=(L, N // tn),
        in_specs=[
            pl.BlockSpec((SUBLANES, D), lambda l, j: (0, 0)),
            pl.BlockSpec((None, D, tn), lambda l, j: (l, 0, j)),
            pl.BlockSpec((None, 1, tn), lambda l, j: (l, 0, j)),
        ],
        out_specs=pl.BlockSpec((None, SUBLANES, tn), lambda l, j: (l, 0, j)),
        compiler_params=_cparams("arbitrary", "arbitrary"),
        name="ada_mod",
    )(c_pad, w, b.reshape(L, 1, N))


def _rwkv_proj_kernel(x_ref, xp_ref, mod_ref, g_ref, mu_ref, wrkv_ref, w1_ref, w2_ref, a1_ref, a2_ref,
                      g1_ref, g2_ref, vec_ref, r_ref, k_ref, v_ref, lw_ref, kk_ref, al_ref, gate_ref):
    i = pl.program_id(1)
    g = g_ref[...]
    shift, scale = mod_ref[0, 0:1, :], mod_ref[0, 1:2, :]
    h = _rms_mod(x_ref[0], g, shift, scale)
    hp = _rms_mod(xp_ref[0, SUBLANES - 1:SUBLANES, :], g, shift, scale)
    hp = jnp.where(i == 0, 0.0, hp)
    row = lax.broadcasted_iota(jnp.int32, h.shape, 0)
    h_prev = jnp.where(row == 0, hp, pltpu.roll(h, 1, axis=0))
    xx = h_prev - h
    mu = mu_ref[...]
    xs = [(h + xx * mu[j:j + 1, :]).astype(BF16) for j in range(6)]
    w0, a0, k_k, k_a = (vec_ref[j:j + 1, :] for j in range(4))
    d = functools.partial(jnp.dot, preferred_element_type=F32)
    r = d(xs[0], wrkv_ref[0])
    k = d(xs[1], wrkv_ref[1])
    v = d(xs[2], wrkv_ref[2])
    z = w0 + _dot(jnp.tanh(d(xs[3], w1_ref[...])), w2_ref[...])
    lw = (-math.exp(-0.5)) * jax.nn.sigmoid(z)
    a = jax.nn.sigmoid(a0 + _dot(d(xs[4], a1_ref[...]), a2_ref[...]))
    gate = _dot(jax.nn.sigmoid(d(xs[5], g1_ref[...])), g2_ref[...])
    r_ref[0] = r
    k_ref[0] = k * (1.0 + (a - 1.0) * k_a)
    v_ref[0] = v
    lw_ref[0] = lw
    kk_ref[0] = k * k_k
    al_ref[0] = a
    gate_ref[0] = gate


def _rwkv_proj(x, mod2, g, mu, wrkv, w1, w2, a1, a2, g1, g2, vec):
    B, T, D = x.shape
    tm = PROJ_TM
    const2 = lambda b, i: (0, 0)
    const3 = lambda b, i: (0, 0, 0)
    act = pl.BlockSpec((1, tm, D), lambda b, i: (b, i, 0))
    n_sub = tm // SUBLANES
    return pl.pallas_call(
        _rwkv_proj_kernel,
        out_shape=[jax.ShapeDtypeStruct((B, T, D), F32)] * 7,
        grid=(B, T // tm),
        in_specs=[
            act,
            pl.BlockSpec((1, SUBLANES, D), lambda b, i: (b, jnp.maximum(i * n_sub - 1, 0), 0)),
            pl.BlockSpec((1, 2, D), lambda b, i: (b, 0, 0)),
            pl.BlockSpec((1, D), const2),
            pl.BlockSpec((6, D), const2),
            pl.BlockSpec((3, D, D), const3),
            pl.BlockSpec(w1.shape, const2), pl.BlockSpec(w2.shape, const2),
            pl.BlockSpec(a1.shape, const2), pl.BlockSpec(a2.shape, const2),
            pl.BlockSpec(g1.shape, const2), pl.BlockSpec(g2.shape, const2),
            pl.BlockSpec((4, D), const2),
        ],
        out_specs=[act] * 7,
        compiler_params=_cparams("arbitrary", "arbitrary"),
        name="rwkv_proj",
    )(x, x, mod2, g, mu, wrkv, w1, w2, a1, a2, g1, g2, vec)


def _rwkv_scan_kernel(r_ref, k_ref, v_ref, lw_ref, kk_ref, al_ref, pv_ref, y_ref, h_ref):
    C = SCAN_CHUNK
    P2 = 2 * C

    @pl.when(pl.program_id(1) == 0)
    def _():
        h_ref[...] = jnp.zeros_like(h_ref)

    lane = lax.broadcasted_iota(jnp.int32, (1, PAIR), 1)
    m_left = (lane < RWKV_HEAD).astype(F32)
    m_right = 1.0 - m_left
    ri = lax.broadcasted_iota(jnp.int32, (P2, P2), 0)
    ci = lax.broadcasted_iota(jnp.int32, (P2, P2), 1)
    same = (ri >= C) == (ci >= C)
    strict = same & (ri > ci)
    incl = same & (ri >= ci)
    eye = ri == ci
    block_ones = same.astype(BF16)
    tri = (lax.broadcasted_iota(jnp.int32, (C, C), 0) >= lax.broadcasted_iota(jnp.int32, (C, C), 1)).astype(BF16)

    def stack(x):
        return jnp.concatenate([x * m_left, x * m_right], axis=0)

    def head_sums(x):
        hi, lo = _split(x)
        s = dd(jnp.concatenate([hi, lo], axis=0), block_ones)
        return s[:x.shape[0]] + s[x.shape[0]:]

    inv_n = 1.0 / RWKV_HEAD
    dd = functools.partial(jnp.dot, preferred_element_type=F32)
    pairs = range(RWKV_HEADS // 2)
    sls = [slice(p * PAIR, (p + 1) * PAIR) for p in pairs]
    kkr = [kk_ref[0, :, sl] for sl in sls]
    ss = [head_sums(x * x) for x in kkr]
    lws = [lw_ref[0, :, sl] for sl in sls]
    Ls = []
    for lw in lws:
        l_hi, l_lo = _split(lw)
        cs = dd(tri, jnp.concatenate([l_hi, l_lo], axis=1))
        Ls.append(cs[:, :PAIR] + cs[:, PAIR:])
    lhs_g, rhs_g, bk_hat, vs, at32, rt32, dec_end = [], [], [], [], [], [], []
    for p in pairs:
        L, lw = Ls[p], lws[p]
        kk = kkr[p] * lax.rsqrt(jnp.maximum(ss[p], 1e-24))
        b_vec = kk * al_ref[0, :, sls[p]]
        k = k_ref[0, :, sls[p]]
        LC = L[C - 1:C, :]
        e_neg = jnp.exp(-L)
        e_end = jnp.exp(LC - L)
        At = stack(-kk * jnp.exp(L - lw))
        Rt = stack(r_ref[0, :, sls[p]] * jnp.exp(L))
        at32.append(At)
        rt32.append(Rt)
        lhs_g.append(jnp.concatenate([At, Rt], axis=0).astype(BF16))
        rhs_g.append(jnp.concatenate([stack(b_vec * e_neg), stack(k * e_neg)], axis=0).astype(BF16))
        bk_hat.append(jnp.concatenate([stack(b_vec * e_end), stack(k * e_end)], axis=0))
        vs.append(stack(v_ref[0, :, sls[p]]).astype(BF16))
        dec_end.append(jnp.exp(LC))
    G = [lax.dot_general(lhs_g[p], rhs_g[p], (((1,), (1,)), ((), ())), preferred_element_type=F32) for p in pairs]
    A_ab = [jnp.where(strict, G[p][:P2, :P2], 0.0).astype(BF16) for p in pairs]
    A_ak = [jnp.where(strict, G[p][:P2, P2:], 0.0).astype(BF16) for p in pairs]
    A_r = [jnp.concatenate([jnp.where(incl, G[p][P2:, :P2], 0.0), jnp.where(incl, G[p][P2:, P2:], 0.0)],
                           axis=1).astype(BF16) for p in pairs]
    W = [dd(A_ak[p], vs[p]) for p in pairs]
    Z = [jnp.concatenate([at32[p], W[p]], axis=1) for p in pairs]
    Ai = A_ab
    n_dbl = int(math.log2(C))
    for s in range(n_dbl):
        Z = [Z[p] + dd(Ai[p], Z[p].astype(BF16)) for p in pairs]
        if s + 1 < n_dbl:
            Ai = [dd(Ai[p], Ai[p]).astype(BF16) for p in pairs]
    rhs = [jnp.concatenate([Z[p].astype(BF16), jnp.concatenate([jnp.zeros_like(vs[p]), vs[p]], axis=1)], axis=0)
           for p in pairs]
    o6 = [dd(A_r[p], rhs[p]) for p in pairs]
    o7 = [dd(bk_hat[p].T.astype(BF16), rhs[p]) for p in pairs]
    Hb = [h_ref[p].astype(BF16) for p in pairs]
    Y = [dd((rt32[p] + o6[p][:, :PAIR]).astype(BF16), Hb[p]) + o6[p][:, PAIR:] for p in pairs]
    for p in pairs:
        Mbd = o7[p][:, :PAIR] + jnp.where(eye, dec_end[p], 0.0)
        h_ref[p] = dd(Mbd.astype(BF16), Hb[p]) + o7[p][:, PAIR:]
    ys = [Y[p][:C] + Y[p][C:] for p in pairs]
    rk = [r_ref[0, :, sl] * k_ref[0, :, sl] * pv_ref[0:1, sl] for sl in sls]
    st1 = [head_sums(jnp.concatenate([ys[p], rk[p]], axis=0)) for p in pairs]
    yc = [ys[p] - st1[p][:C] * inv_n for p in pairs]
    var = [head_sums(yc[p] * yc[p]) * inv_n for p in pairs]
    for p in pairs:
        sl = sls[p]
        bonus = st1[p][C:] * v_ref[0, :, sl]
        y_ref[0, :, sl] = yc[p] * lax.rsqrt(var[p] + RWKV_GN_EPS) * pv_ref[1:2, sl] + pv_ref[2:3, sl] + bonus


def _rwkv_scan(r, k, v, lw, kk, al, pvec):
    B, T, D = r.shape
    C = SCAN_CHUNK
    act = pl.BlockSpec((1, C, D), lambda b, c: (b, c, 0))
    return pl.pallas_call(
        _rwkv_scan_kernel,
        out_shape=jax.ShapeDtypeStruct((B, T, D), F32),
        grid=(B, T // C),
        in_specs=[act] * 6 + [pl.BlockSpec((3, D), lambda b, c: (0, 0))],
        out_specs=act,
        scratch_shapes=[pltpu.VMEM((RWKV_HEADS // 2, PAIR, PAIR), F32)],
        compiler_params=_cparams("arbitrary", "arbitrary"),
        name="rwkv_scan",
    )(r, k, v, lw, kk, al, pvec)


def _proj_res_kernel(*refs, has_gate):
    if has_gate:
        y_ref, g_ref, x_ref, gm_ref, w_ref, o_ref = refs
        y = y_ref[0] * g_ref[0]
    else:
        y_ref, x_ref, gm_ref, w_ref, o_ref = refs
        y = y_ref[0]
    o_ref[0] = x_ref[0] + gm_ref[0] * jnp.dot(y.astype(BF16), w_ref[...], preferred_element_type=F32)


def _proj_res(y, g, x, gm, w):
    B, T, D = x.shape
    tm = DENSE_TM
    act = pl.BlockSpec((1, tm, D), lambda b, i: (b, i, 0))
    ins = [y] + ([g] if g is not None else []) + [x, gm, w]
    specs = [act] * (len(ins) - 2) + [pl.BlockSpec((1, 1, D), lambda b, i: (b, 0, 0)),
                                     pl.BlockSpec((D, D), lambda b, i: (0, 0))]
    return pl.pallas_call(
        functools.partial(_proj_res_kernel, has_gate=g is not None),
        out_shape=jax.ShapeDtypeStruct((B, T, D), F32),
        grid=(B, T // tm),
        in_specs=specs,
        out_specs=act,
        compiler_params=_cparams("arbitrary", "arbitrary"),
        name="proj_res",
    )(*ins)


def _norm_mm_kernel(x_ref, mod_ref, g_ref, w_ref, o_ref):
    h = _rms_mod(x_ref[0], g_ref[...], mod_ref[0, 0:1, :], mod_ref[0, 1:2, :])
    o_ref[0] = jnp.dot(h.astype(BF16), w_ref[...], preferred_element_type=F32).astype(o_ref.dtype)


def _norm_mm(x, mod2, g, w, out_dtype):
    B, T, D = x.shape
    N = w.shape[1]
    tm = DENSE_TM
    return pl.pallas_call(
        _norm_mm_kernel,
        out_shape=jax.ShapeDtypeStruct((B, T, N), out_dtype),
        grid=(B, T // tm),
        in_specs=[
            pl.BlockSpec((1, tm, D), lambda b, i: (b, i, 0)),
            pl.BlockSpec((1, 2, D), lambda b, i: (b, 0, 0)),
            pl.BlockSpec((1, D), lambda b, i: (0, 0)),
            pl.BlockSpec((D, N), lambda b, i: (0, 0)),
        ],
        out_specs=pl.BlockSpec((1, tm, N), lambda b, i: (b, i, 0)),
        compiler_params=_cparams("arbitrary", "arbitrary"),
        name="norm_mm",
    )(x, mod2, g, w)


def _diff_attn_kernel(q_ref, k_ref, v_ref, lam_ref, sg_ref, o_ref, m_ref, acc_ref, *, lambda_init):
    tq, HB, dv = ATT_TQ, ATT_HB, DIFF_V_DIM
    qi = pl.program_id(2)
    heads = range(HB)
    hs = [slice(h * dv, (h + 1) * dv) for h in heads]
    lane = lax.broadcasted_iota(jnp.int32, (1, dv), 1)
    m_left = (lane < DIFF_QK_DIM).astype(F32)
    qs = []
    for h in heads:
        q = q_ref[0, :, hs[h]].astype(F32) * (DIFF_QK_DIM ** -0.5)
        qs.append(jnp.concatenate([q * m_left, q * (1.0 - m_left)], axis=0).astype(BF16))
    ones_col = jnp.ones((tq, dv), BF16)
    m_ref[...] = jnp.full(m_ref.shape, -jnp.inf, F32)
    acc_ref[...] = jnp.zeros(acc_ref.shape, F32)
    causal = (lax.broadcasted_iota(jnp.int32, (2 * tq, tq), 1)
              <= lax.broadcasted_iota(jnp.int32, (2 * tq, tq), 0) % tq)

    def block(j, masked):
        rows = pl.ds(pl.multiple_of(j * tq, tq), tq)

        def scores(h):
            return lax.dot_general(qs[h], k_ref[0, rows, hs[h]], (((1,), (1,)), ((), ())),
                                   preferred_element_type=F32)

        def softmax(h, s):
            if masked:
                s = jnp.where(causal, s, -jnp.inf)
            m_old = m_ref[h]
            m_new = jnp.maximum(m_old, jnp.max(s, axis=-1, keepdims=True))
            m_ref[h] = m_new
            p = jnp.exp(s - jnp.concatenate([m_new, m_new], axis=1)).astype(BF16)
            return p, jnp.exp(m_old - m_new)

        def accumulate(h, p, alpha):
            pv = jnp.dot(p, jnp.concatenate([v_ref[0, rows, hs[h]], ones_col], axis=1), preferred_element_type=F32)
            acc_ref[h] = acc_ref[h] * jnp.concatenate([alpha, alpha], axis=1) + pv

        s_next = scores(0)
        for h in heads:
            s_cur = s_next
            if h + 1 < HB:
                s_next = scores(h + 1)
            accumulate(h, *softmax(h, s_cur))

    def body(j, c):
        block(j, False)
        return c

    lax.fori_loop(0, qi, body, 0)
    block(qi, True)
    lv = lam_ref[...]
    lam = (jnp.exp(jnp.sum(lv[0:1] * lv[1:2], axis=-1, keepdims=True))
           - jnp.exp(jnp.sum(lv[2:3] * lv[3:4], axis=-1, keepdims=True)) + lambda_init)
    for h in heads:
        acc = acc_ref[h]
        o = acc[:, :dv] / acc[:, dv:dv + 1]
        o = o[:tq] - lam * o[tq:]
        ms = jnp.mean(o * o, axis=-1, keepdims=True)
        o_ref[0, :, hs[h]] = (o * lax.rsqrt(ms + SUBLN_EPS) * sg_ref[...] * (1.0 - lambda_init)).astype(o_ref.dtype)


def _diff_attn(q, kv, lam_vecs, subln_g, lambda_init):
    B, T, D = q.shape
    tq, HB = ATT_TQ, ATT_HB
    n_hb = DIFF_HEADS // HB
    w = HB * DIFF_V_DIM
    return pl.pallas_call(
        functools.partial(_diff_attn_kernel, lambda_init=lambda_init),
        out_shape=jax.ShapeDtypeStruct((B, T, D), BF16),
        grid=(B, n_hb, T // tq),
        in_specs=[
            pl.BlockSpec((1, tq, w), lambda b, h, i: (b, i, h)),
            pl.BlockSpec((1, T, w), lambda b, h, i: (b, 0, h)),
            pl.BlockSpec((1, T, w), lambda b, h, i: (b, 0, n_hb + h)),
            pl.BlockSpec((4, DIFF_QK_DIM), lambda b, h, i: (0, 0)),
            pl.BlockSpec((1, DIFF_V_DIM), lambda b, h, i: (0, 0)),
        ],
        out_specs=pl.BlockSpec((1, tq, w), lambda b, h, i: (b, i, h)),
        scratch_shapes=[pltpu.VMEM((HB, 2 * tq, LANES), F32), pltpu.VMEM((HB, 2 * tq, 2 * DIFF_V_DIM), F32)],
        compiler_params=_cparams("arbitrary", "arbitrary", "arbitrary"),
        name="diff_attn",
    )(q, kv, kv, lam_vecs, subln_g)


def _router_kernel(x_ref, mod_ref, g_ref, w_ref, info_ref):
    h = _rms_mod(x_ref[...], g_ref[...], mod_ref[0, 0:1, :], mod_ref[0, 1:2, :])
    logit = _dot3(h, w_ref[...])
    lane_i = lax.broadcasted_iota(jnp.int32, logit.shape, 1)
    lane = lane_i.astype(F32)
    neg = -jnp.inf
    big = float(LANES)
    is_grp = lane_i < N_GROUPS
    gl = jnp.where(is_grp, logit, neg)
    gmax = jnp.max(gl, axis=-1, keepdims=True)
    gidx = jnp.min(jnp.where(gl == gmax, lane, big), axis=-1, keepdims=True)
    grp_gate = 1.0 / jnp.sum(jnp.where(is_grp, jnp.exp(logit - gmax), 0.0), axis=-1, keepdims=True)
    lo = N_GROUPS + gidx * EXPERTS_PER_GROUP
    in_grp = (lane >= lo) & (lane < lo + EXPERTS_PER_GROUP)
    el = jnp.where(in_grp, logit, neg)
    t1 = jnp.max(el, axis=-1, keepdims=True)
    i1 = jnp.min(jnp.where(el == t1, lane, big), axis=-1, keepdims=True)
    el2 = jnp.where(lane == i1, neg, el)
    t2 = jnp.max(el2, axis=-1, keepdims=True)
    i2 = jnp.min(jnp.where(el2 == t2, lane, big), axis=-1, keepdims=True)
    e21 = jnp.exp(t2 - t1)
    p1 = 1.0 / (1.0 + e21)
    w1 = grp_gate * p1
    w2 = grp_gate * (e21 * p1)
    e1 = i1 - N_GROUPS
    e2 = i2 - N_GROUPS
    info_ref[...] = jnp.where(lane_i == 0, e1, jnp.where(lane_i == 1, e2, jnp.where(lane_i == 2, w1, jnp.where(lane_i == 3, w2, 0.0))))


def _router(x2, mod2, g, w_cat, T):
    n_tok, D = x2.shape
    tm = ROUTE_TM
    per_b = T // tm
    return pl.pallas_call(
        _router_kernel,
        out_shape=jax.ShapeDtypeStruct((n_tok, LANES), F32),
        grid=(n_tok // tm,),
        in_specs=[
            pl.BlockSpec((tm, D), lambda i: (i, 0)),
            pl.BlockSpec((1, 2, D), lambda i: (i // per_b, 0, 0)),
            pl.BlockSpec((1, D), lambda i: (0, 0)),
            pl.BlockSpec((D, LANES), lambda i: (0, 0)),
        ],
        out_specs=pl.BlockSpec((tm, LANES), lambda i: (i, 0)),
        compiler_params=_cparams("arbitrary"),
        name="moe_router",
    )(x2, mod2, g, w_cat)


def _rank_kernel(info_ref, dest_ref, meta_ref, cnt_ref, start_ref):
    ps = pl.program_id(0)
    i = pl.program_id(1)
    tb = info_ref.shape[0]
    lane = lax.broadcasted_iota(jnp.int32, (tb, LANES), 1)
    info = info_ref[...]
    e0 = info[:, 0:1].astype(jnp.int32)
    e1 = info[:, 1:2].astype(jnp.int32)
    o0 = (lane == e0).astype(F32)
    o1 = (lane == e1).astype(F32)
    both = o0 + o1

    @pl.when((ps == 0) & (i == 0))
    def _():
        cnt_ref[...] = jnp.zeros_like(cnt_ref)

    @pl.when(ps == 0)
    def _():
        cnt_ref[...] += jnp.sum(both, axis=0, keepdims=True)
        dest_ref[...] = jnp.zeros_like(dest_ref)
        meta_ref[...] = jnp.zeros_like(meta_ref)

    @pl.when((ps == 1) & (i == 0))
    def _():
        cnt = cnt_ref[...]
        padded = jnp.floor((cnt + (MOE_TM - 1)) * (1.0 / MOE_TM)) * MOE_TM
        r = lax.broadcasted_iota(jnp.int32, (LANES, LANES), 0)
        c = lax.broadcasted_iota(jnp.int32, (LANES, LANES), 1)
        upper_strict = (r < c).astype(BF16)
        start = _dot_hl(jnp.broadcast_to(padded, (SUBLANES, LANES)), upper_strict)[0:1]
        start_ref[...] = start
        meta_ref[...] = jnp.broadcast_to(start + padded, (SUBLANES, LANES))

    @pl.when(ps == 1)
    def _():
        r = lax.broadcasted_iota(jnp.int32, (tb, tb), 0)
        c = lax.broadcasted_iota(jnp.int32, (tb, tb), 1)
        lower_strict = (r > c).astype(BF16)
        before = jnp.dot(lower_strict, both.astype(BF16), preferred_element_type=F32) + start_ref[...]
        d0 = jnp.sum(o0 * before, axis=-1, keepdims=True)
        d1 = jnp.sum(o1 * before, axis=-1, keepdims=True)
        dest_ref[...] = jnp.where(lane == 0, d0, jnp.where(lane == 1, d1, 0.0)).astype(jnp.int32)
        start_ref[...] += jnp.sum(both, axis=0, keepdims=True)


def _rank(info):
    n_tok = info.shape[0]
    tb = RANK_TB
    return pl.pallas_call(
        _rank_kernel,
        out_shape=[jax.ShapeDtypeStruct((n_tok, LANES), jnp.int32), jax.ShapeDtypeStruct((SUBLANES, LANES), F32)],
        grid=(2, n_tok // tb),
        in_specs=[pl.BlockSpec((tb, LANES), lambda p, i: (i, 0))],
        out_specs=[pl.BlockSpec((tb, LANES), lambda p, i: (i * p, 0)), pl.BlockSpec((SUBLANES, LANES), lambda p, i: (0, 0))],
        scratch_shapes=[pltpu.VMEM((1, LANES), F32), pltpu.VMEM((1, LANES), F32)],
        compiler_params=_cparams("arbitrary", "arbitrary"),
        name="moe_rank",
    )(info)


def _row_copy(src_ref, s, dst_ref, d, sem):
    return pltpu.make_async_copy(src_ref.at[pl.ds(s, 1)], dst_ref.at[pl.ds(d, 1)], sem)


def _dispatch_kernel(dest_ref, x_ref, mod_ref, g_ref, xs_in_ref, xs_ref, h_ref, sem):
    del xs_in_ref
    tb = h_ref.shape[0]
    h_ref[...] = _rms_mod(x_ref[...], g_ref[...], mod_ref[0, 0:1, :], mod_ref[0, 1:2, :])

    def start(j, c):
        _row_copy(h_ref, j, xs_ref, dest_ref[0, 0, 2 * j], sem).start()
        _row_copy(h_ref, j, xs_ref, dest_ref[0, 0, 2 * j + 1], sem).start()
        return c

    lax.fori_loop(0, tb, start, 0, unroll=8)

    def wait(j, c):
        _row_copy(h_ref, 0, xs_ref, 0, sem).wait()
        return c

    lax.fori_loop(0, 2 * tb, wait, 0, unroll=8)


def _dispatch(dest3, x2, mod2, g, xs_zero, T):
    n_tok, D = x2.shape
    tb = ROW_TB
    per_b = T // tb
    return pl.pallas_call(
        _dispatch_kernel,
        out_shape=jax.ShapeDtypeStruct(xs_zero.shape, F32),
        grid=(n_tok // tb,),
        in_specs=[
            pl.BlockSpec((1, 1, 2 * tb), lambda i: (i, 0, 0), memory_space=pltpu.SMEM),
            pl.BlockSpec((tb, D), lambda i: (i, 0)),
            pl.BlockSpec((1, 2, D), lambda i: (i // per_b, 0, 0)),
            pl.BlockSpec((1, D), lambda i: (0, 0)),
            pl.BlockSpec(memory_space=pl.ANY),
        ],
        out_specs=pl.BlockSpec(memory_space=pl.ANY),
        scratch_shapes=[pltpu.VMEM((tb, D), F32), pltpu.SemaphoreType.DMA(())],
        input_output_aliases={4: 0},
        compiler_params=_cparams("arbitrary"),
        name="moe_dispatch",
    )(dest3, x2, mod2, g, xs_zero)


def _expert_kernel(be_ref, nb_ref, xs_ref, wg_ref, wu_ref, wd_ref, ys_ref, wgb, wub, wdb):
    i = pl.program_id(0)
    changed = (i == 0) | (be_ref[i] != be_ref[jnp.maximum(i - 1, 0)])

    @pl.when(changed)
    def _():
        wgb[...] = wg_ref[0].astype(BF16)
        wub[...] = wu_ref[0].astype(BF16)
        wdb[...] = wd_ref[0].astype(BF16)

    @pl.when(i < nb_ref[0])
    def _():
        x = xs_ref[...].astype(BF16)
        a = jnp.dot(x, wgb[...], preferred_element_type=F32)
        u = jnp.dot(x, wub[...], preferred_element_type=F32)
        hdn = (a * jax.nn.sigmoid(a)) * u
        ys_ref[...] = jnp.dot(hdn.astype(BF16), wdb[...], preferred_element_type=F32)

    @pl.when(i >= nb_ref[0])
    def _():
        ys_ref[...] = jnp.zeros_like(ys_ref)


def _experts(blk_e, n_used, xs, w_gate, w_up, w_down, layer):
    n_rows, D = xs.shape
    tm = MOE_TM
    FF = EXPERT_FF
    grid_spec = pltpu.PrefetchScalarGridSpec(
        num_scalar_prefetch=2,
        grid=(n_rows // tm,),
        in_specs=[
            pl.BlockSpec((tm, D), lambda i, be, nb: (i, 0)),
            pl.BlockSpec((None, 1, D, FF), lambda i, be, nb: (layer, be[i], 0, 0)),
            pl.BlockSpec((None, 1, D, FF), lambda i, be, nb: (layer, be[i], 0, 0)),
            pl.BlockSpec((None, 1, FF, D), lambda i, be, nb: (layer, be[i], 0, 0)),
        ],
        out_specs=pl.BlockSpec((tm, D), lambda i, be, nb: (i, 0)),
        scratch_shapes=[pltpu.VMEM((D, FF), BF16), pltpu.VMEM((D, FF), BF16), pltpu.VMEM((FF, D), BF16)],
    )
    return pl.pallas_call(
        _expert_kernel,
        out_shape=jax.ShapeDtypeStruct((n_rows, D), F32),
        grid_spec=grid_spec,
        compiler_params=_cparams("arbitrary"),
        name="moe_experts",
    )(blk_e, n_used, xs, w_gate, w_up, w_down)


def _combine_kernel(dest_ref, info_ref, x_ref, gf_ref, fg_ref, ys_ref, o_ref, y_ref, sem, *, final_norm):
    tb = x_ref.shape[0]

    def start(j, c):
        _row_copy(ys_ref, dest_ref[0, 0, 2 * j], y_ref.at[0], j, sem).start()
        _row_copy(ys_ref, dest_ref[0, 0, 2 * j + 1], y_ref.at[1], j, sem).start()
        return c

    lax.fori_loop(0, tb, start, 0, unroll=8)

    def wait(j, c):
        _row_copy(ys_ref, 0, y_ref.at[0], 0, sem).wait()
        return c

    lax.fori_loop(0, 2 * tb, wait, 0, unroll=8)
    info = info_ref[...]
    moe = info[:, 2:3] * y_ref[0] + info[:, 3:4] * y_ref[1]
    out = x_ref[...] + gf_ref[0] * moe
    if final_norm:
        ms = jnp.mean(out * out, axis=-1, keepdims=True)
        out = out * lax.rsqrt(ms + RMS_EPS) * fg_ref[...]
    o_ref[...] = out


def _combine(dest3, info, x2, gf, final_g, ys, T, final_norm):
    n_tok, D = x2.shape
    tb = ROW_TB
    per_b = T // tb
    return pl.pallas_call(
        functools.partial(_combine_kernel, final_norm=final_norm),
        out_shape=jax.ShapeDtypeStruct((n_tok, D), F32),
        grid=(n_tok // tb,),
        in_specs=[
            pl.BlockSpec((1, 1, 2 * tb), lambda i: (i, 0, 0), memory_space=pltpu.SMEM),
            pl.BlockSpec((tb, LANES), lambda i: (i, 0)),
            pl.BlockSpec((tb, D), lambda i: (i, 0)),
            pl.BlockSpec((1, 1, D), lambda i: (i // per_b, 0, 0)),
            pl.BlockSpec((1, D), lambda i: (0, 0)),
            pl.BlockSpec(memory_space=pl.ANY),
        ],
        out_specs=pl.BlockSpec((tb, D), lambda i: (i, 0)),
        scratch_shapes=[pltpu.VMEM((2, tb, D), F32), pltpu.SemaphoreType.DMA(())],
        compiler_params=_cparams("arbitrary"),
        name="moe_combine",
    )(dest3, info, x2, gf, final_g, ys)


def _moe_layer(x, mod_f, gf, norm_g, w_rg, w_re, w_gate, w_up, w_down, layer, final_g, final_norm):
    B, T, D = x.shape
    n_tok = B * T
    x2 = x.reshape(n_tok, D)
    w_cat = jnp.concatenate([w_rg, w_re, jnp.zeros((D, LANES - N_GROUPS - N_EXPERTS), F32)], axis=1)
    info = _router(x2, mod_f, norm_g, w_cat, T)
    dest, meta = _rank(info)
    pad_end = meta[0, :N_EXPERTS].astype(jnp.int32)
    n_rows = -(-(2 * n_tok + N_EXPERTS * MOE_TM) // MOE_TM) * MOE_TM
    n_blocks = n_rows // MOE_TM
    blk_start = jnp.arange(n_blocks, dtype=jnp.int32) * MOE_TM
    blk_e = jnp.minimum(jnp.sum(pad_end[None, :] <= blk_start[:, None], axis=1), N_EXPERTS - 1).astype(jnp.int32)
    n_used = (pad_end[N_EXPERTS - 1:] // MOE_TM).astype(jnp.int32)
    dest3 = dest[:, :2].reshape(n_tok // ROW_TB, 1, 2 * ROW_TB)
    xs = _dispatch(dest3, x2, mod_f, norm_g, jnp.zeros((n_rows, D), F32), T)
    ys = _experts(blk_e, n_used, xs, w_gate, w_up, w_down, layer)
    out = _combine(dest3, info, x2, gf, final_g, ys, T, final_norm)
    return out.reshape(B, T, D)


def kernel(x, c, ada_w, ada_b, norm_mix_g, norm_ffn_g, rw_mu, rw_w_rkv, rw_w0, rw_w1, rw_w2, rw_a0, rw_a1, rw_a2,
           rw_g1, rw_g2, rw_k_k, rw_k_a, rw_r_k, rw_gn_g, rw_gn_b, rw_w_o, ada_kv_w, ada_kv_b, norm_kv_g, w_kv,
           df_w_q, df_lq1, df_lk1, df_lq2, df_lk2, df_subln_g, df_w_o, moe_w_rg, moe_w_re, moe_w_gate, moe_w_up,
           moe_w_down, final_g):
    B, T, D = x.shape
    c_pad = jnp.zeros((SUBLANES, D), F32).at[:B].set(c)
    mod = _ada(c_pad, ada_w, ada_b, 6 * D // 4)[:, :B]
    mod_kv = _ada(c_pad, ada_kv_w[None], ada_kv_b[None], D)[0, :B]
    bf = lambda w: w.astype(BF16)
    row = lambda v: v.reshape(1, -1)

    for l in range(DEPTH):
        sh_m, sc_m, g_m, sh_f, sc_f, g_f = jnp.split(mod[l], 6, axis=-1)
        mod_m = jnp.stack([sh_m, sc_m], axis=1)
        mod_f = jnp.stack([sh_f, sc_f], axis=1)
        if l < N_A_LAYERS:
            i = l
            vec = jnp.stack([rw_w0[i], rw_a0[i], rw_k_k[i], rw_k_a[i]], axis=0)
            r, k, v, lw, kk, al, gate = _rwkv_proj(
                x, mod_m, row(norm_mix_g[l]), rw_mu[i], bf(rw_w_rkv[i]), bf(rw_w1[i]), bf(rw_w2[i]),
                bf(rw_a1[i]), bf(rw_a2[i]), bf(rw_g1[i]), bf(rw_g2[i]), vec)
            pvec = jnp.stack([rw_r_k[i].reshape(-1), rw_gn_g[i], rw_gn_b[i]], axis=0)
            y = _rwkv_scan(r, k, v, lw, kk, al, pvec)
            x = _proj_res(y, gate, x, g_m[:, None, :], bf(rw_w_o[i]))
        else:
            j = l - N_A_LAYERS
            if l == N_A_LAYERS:
                sh_kv, sc_kv = jnp.split(mod_kv, 2, axis=-1)
                kv = _norm_mm(x, jnp.stack([sh_kv, sc_kv], axis=1), row(norm_kv_g), bf(w_kv), BF16)
            q = _norm_mm(x, mod_m, row(norm_mix_g[l]), bf(df_w_q[j]), BF16)
            lambda_init = 0.8 - 0.6 * math.exp(-0.3 * l)
            lam_vecs = jnp.stack([df_lq1[j], df_lk1[j], df_lq2[j], df_lk2[j]], axis=0)
            o = _diff_attn(q, kv, lam_vecs, row(df_subln_g[j]), lambda_init)
            x = _proj_res(o, None, x, g_m[:, None, :], bf(df_w_o[j]))
        x = _moe_layer(x, mod_f, g_f[:, None, :], row(norm_ffn_g[l]), moe_w_rg[l], moe_w_re[l], moe_w_gate,
                       moe_w_up, moe_w_down, l, row(final_g), final_norm=(l == DEPTH - 1))
    return x
```

```python
import functools
import math

import jax
import jax.numpy as jnp
from jax import lax
from jax.experimental import pallas as pl
from jax.experimental.pallas import tpu as pltpu

F32 = jnp.float32
BF16 = jnp.bfloat16

D_MODEL = 1024
DEPTH = 2
N_A_LAYERS = DEPTH // 2
RWKV_HEAD = 64
RWKV_HEADS = D_MODEL // RWKV_HEAD
RWKV_GN_EPS = 64e-5
DIFF_QK_DIM = 64
DIFF_V_DIM = 2 * DIFF_QK_DIM
DIFF_HEADS = D_MODEL // DIFF_V_DIM
SUBLN_EPS = 1e-5
N_GROUPS = 4
EXPERTS_PER_GROUP = 8
N_EXPERTS = N_GROUPS * EXPERTS_PER_GROUP
EXPERT_FF = 512
RMS_EPS = 1e-6

LANES = 128
SUBLANES = 8
VMEM_LIMIT_BYTES = 56 * 1024 * 1024

SCAN_CHUNK = 64
PAIR = 2 * RWKV_HEAD
PROJ_TM = 256
DENSE_TM = 512
ATT_TQ = 512
ATT_HB = 2
MOE_TM = 256
ROUTE_TM = 512
RANK_TB = 512
ROW_TB = 256


def _cparams(*sem):
    return pltpu.CompilerParams(dimension_semantics=sem, vmem_limit_bytes=VMEM_LIMIT_BYTES)


def _dot(a, b):
    return jnp.dot(a.astype(BF16), b.astype(BF16), preferred_element_type=F32)


def _dot_nt(a, b):
    return lax.dot_general(a.astype(BF16), b.astype(BF16), (((1,), (1,)), ((), ())), preferred_element_type=F32)


def _split(x):
    hi = x.astype(BF16)
    lo = (x - hi.astype(F32)).astype(BF16)
    return hi, lo


def _dot3(a, b):
    ah, al = _split(a)
    bh, bl = _split(b)
    d = functools.partial(jnp.dot, preferred_element_type=F32)
    return d(ah, bh) + d(ah, bl) + d(al, bh)


def _dot_hl(a, b_exact):
    ah, al = _split(a)
    d = functools.partial(jnp.dot, preferred_element_type=F32)
    return d(ah, b_exact) + d(al, b_exact)


def _rms_mod(x, g, shift, scale):
    ms = jnp.mean(x * x, axis=-1, keepdims=True)
    return (x * lax.rsqrt(ms + RMS_EPS) * g) * (1.0 + scale) + shift


def _ada_kernel(c_ref, w_ref, b_ref, o_ref):
    c = c_ref[...]
    ca = c * jax.nn.sigmoid(c)
    o_ref[...] = _dot3(ca, w_ref[...]) + b_ref[...]


def _ada(c_pad, w, b, tn):
    L, D, N = w.shape
    return pl.pallas_call(
        _ada_kernel,
        out_shape=jax.ShapeDtypeStruct((L, SUBLANES, N), F32),
        grid=(L, N // tn),
        in_specs=[
            pl.BlockSpec((SUBLANES, D), lambda l, j: (0, 0)),
            pl.BlockSpec((None, D, tn), lambda l, j: (l, 0, j)),
            pl.BlockSpec((None, 1, tn), lambda l, j: (l, 0, j)),
        ],
        out_specs=pl.BlockSpec((None, SUBLANES, tn), lambda l, j: (l, 0, j)),
        compiler_params=_cparams("arbitrary", "arbitrary"),
        name="ada_mod",
    )(c_pad, w, b.reshape(L, 1, N))


def _rwkv_proj_kernel(x_ref, xp_ref, mod_ref, g_ref, mu_ref, wrkv_ref, w1_ref, w2_ref, a1_ref, a2_ref,
                      g1_ref, g2_ref, vec_ref, r_ref, k_ref, v_ref, lw_ref, kk_ref, al_ref, gate_ref):
    i = pl.program_id(1)
    g = g_ref[...]
    shift, scale = mod_ref[0, 0:1, :], mod_ref[0, 1:2, :]
    h = _rms_mod(x_ref[0], g, shift, scale)
    hp = _rms_mod(xp_ref[0, SUBLANES - 1:SUBLANES, :], g, shift, scale)
    hp = jnp.where(i == 0, 0.0, hp)
    row = lax.broadcasted_iota(jnp.int32, h.shape, 0)
    h_prev = jnp.where(row == 0, hp, pltpu.roll(h, 1, axis=0))
    xx = h_prev - h
    mu = mu_ref[...]
    xs = [(h + xx * mu[j:j + 1, :]).astype(BF16) for j in range(6)]
    w0, a0, k_k, k_a = (vec_ref[j:j + 1, :] for j in range(4))
    d = functools.partial(jnp.dot, preferred_element_type=F32)
    r = d(xs[0], wrkv_ref[0])
    k = d(xs[1], wrkv_ref[1])
    v = d(xs[2], wrkv_ref[2])
    z = w0 + _dot(jnp.tanh(d(xs[3], w1_ref[...])), w2_ref[...])
    lw = (-math.exp(-0.5)) * jax.nn.sigmoid(z)
    a = jax.nn.sigmoid(a0 + _dot(d(xs[4], a1_ref[...]), a2_ref[...]))
    gate = _dot(jax.nn.sigmoid(d(xs[5], g1_ref[...])), g2_ref[...])
    r_ref[0] = r
    k_ref[0] = k * (1.0 + (a - 1.0) * k_a)
    v_ref[0] = v
    lw_ref[0] = lw
    kk_ref[0] = k * k_k
    al_ref[0] = a
    gate_ref[0] = gate


def _rwkv_proj(x, mod2, g, mu, wrkv, w1, w2, a1, a2, g1, g2, vec):
    B, T, D = x.shape
    tm = PROJ_TM
    const2 = lambda b, i: (0, 0)
    const3 = lambda b, i: (0, 0, 0)
    act = pl.BlockSpec((1, tm, D), lambda b, i: (b, i, 0))
    n_sub = tm // SUBLANES
    return pl.pallas_call(
        _rwkv_proj_kernel,
        out_shape=[jax.ShapeDtypeStruct((B, T, D), F32)] * 7,
        grid=(B, T // tm),
        in_specs=[
            act,
            pl.BlockSpec((1, SUBLANES, D), lambda b, i: (b, jnp.maximum(i * n_sub - 1, 0), 0)),
            pl.BlockSpec((1, 2, D), lambda b, i: (b, 0, 0)),
            pl.BlockSpec((1, D), const2),
            pl.BlockSpec((6, D), const2),
            pl.BlockSpec((3, D, D), const3),
            pl.BlockSpec(w1.shape, const2), pl.BlockSpec(w2.shape, const2),
            pl.BlockSpec(a1.shape, const2), pl.BlockSpec(a2.shape, const2),
            pl.BlockSpec(g1.shape, const2), pl.BlockSpec(g2.shape, const2),
            pl.BlockSpec((4, D), const2),
        ],
        out_specs=[act] * 7,
        compiler_params=_cparams("arbitrary", "arbitrary"),
        name="rwkv_proj",
    )(x, x, mod2, g, mu, wrkv, w1, w2, a1, a2, g1, g2, vec)


def _rwkv_scan_kernel(r_ref, k_ref, v_ref, lw_ref, kk_ref, al_ref, pv_ref, y_ref, h_ref):
    C = SCAN_CHUNK
    P2 = 2 * C

    @pl.when(pl.program_id(1) == 0)
    def _():
        h_ref[...] = jnp.zeros_like(h_ref)

    lane = lax.broadcasted_iota(jnp.int32, (1, PAIR), 1)
    m_left = (lane < RWKV_HEAD).astype(F32)
    m_right = 1.0 - m_left
    ri = lax.broadcasted_iota(jnp.int32, (P2, P2), 0)
    ci = lax.broadcasted_iota(jnp.int32, (P2, P2), 1)
    same = (ri >= C) == (ci >= C)
    strict = same & (ri > ci)
    incl = same & (ri >= ci)
    eye = ri == ci
    block_ones = same.astype(BF16)
    tri = (lax.broadcasted_iota(jnp.int32, (C, C), 0) >= lax.broadcasted_iota(jnp.int32, (C, C), 1)).astype(BF16)

    def stack(x):
        return jnp.concatenate([x * m_left, x * m_right], axis=0)

    def head_sums(x):
        hi, lo = _split(x)
        s = dd(jnp.concatenate([hi, lo], axis=0), block_ones)
        return s[:x.shape[0]] + s[x.shape[0]:]

    inv_n = 1.0 / RWKV_HEAD
    dd = functools.partial(jnp.dot, preferred_element_type=F32)
    pairs = range(RWKV_HEADS // 2)
    sls = [slice(p * PAIR, (p + 1) * PAIR) for p in pairs]
    kkr = [kk_ref[0, :, sl] for sl in sls]
    ss = [head_sums(x * x) for x in kkr]
    lws = [lw_ref[0, :, sl] for sl in sls]
    Ls = []
    for lw in lws:
        l_hi, l_lo = _split(lw)
        cs = dd(tri, jnp.concatenate([l_hi, l_lo], axis=1))
        Ls.append(cs[:, :PAIR] + cs[:, PAIR:])
    lhs_g, rhs_g, bk_hat, vs, at32, rt32, dec_end = [], [], [], [], [], [], []
    for p in pairs:
        L, lw = Ls[p], lws[p]
        kk = kkr[p] * lax.rsqrt(jnp.maximum(ss[p], 1e-24))
        b_vec = kk * al_ref[0, :, sls[p]]
        k = k_ref[0, :, sls[p]]
        LC = L[C - 1:C, :]
        e_neg = jnp.exp(-L)
        e_end = jnp.exp(LC - L)
        At = stack(-kk * jnp.exp(L - lw))
        Rt = stack(r_ref[0, :, sls[p]] * jnp.exp(L))
        at32.append(At)
        rt32.append(Rt)
        lhs_g.append(jnp.concatenate([At, Rt], axis=0).astype(BF16))
        rhs_g.append(jnp.concatenate([stack(b_vec * e_neg), stack(k * e_neg)], axis=0).astype(BF16))
        bk_hat.append(jnp.concatenate([stack(b_vec * e_end), stack(k * e_end)], axis=0))
        vs.append(stack(v_ref[0, :, sls[p]]).astype(BF16))
        dec_end.append(jnp.exp(LC))
    G = [lax.dot_general(lhs_g[p], rhs_g[p], (((1,), (1,)), ((), ())), preferred_element_type=F32) for p in pairs]
    A_ab = [jnp.where(strict, G[p][:P2, :P2], 0.0).astype(BF16) for p in pairs]
    A_ak = [jnp.where(strict, G[p][:P2, P2:], 0.0).astype(BF16) for p in pairs]
    A_r = [jnp.concatenate([jnp.where(incl, G[p][P2:, :P2], 0.0), jnp.where(incl, G[p][P2:, P2:], 0.0)],
                           axis=1).astype(BF16) for p in pairs]
    W = [dd(A_ak[p], vs[p]) for p in pairs]
    Z = [jnp.concatenate([at32[p], W[p]], axis=1) for p in pairs]
    Ai = A_ab
    n_dbl = int(math.log2(C))
    for s in range(n_dbl):
        Z = [Z[p] + dd(Ai[p], Z[p].astype(BF16)) for p in pairs]
        if s + 1 < n_dbl:
            Ai = [dd(Ai[p], Ai[p]).astype(BF16) for p in pairs]
    rhs = [jnp.concatenate([Z[p].astype(BF16), jnp.concatenate([jnp.zeros_like(vs[p]), vs[p]], axis=1)], axis=0)
           for p in pairs]
    o6 = [dd(A_r[p], rhs[p]) for p in pairs]
    o7 = [dd(bk_hat[p].T.astype(BF16), rhs[p]) for p in pairs]
    Hb = [h_ref[p].astype(BF16) for p in pairs]
    Y = [dd((rt32[p] + o6[p][:, :PAIR]).astype(BF16), Hb[p]) + o6[p][:, PAIR:] for p in pairs]
    for p in pairs:
        Mbd = o7[p][:, :PAIR] + jnp.where(eye, dec_end[p], 0.0)
        h_ref[p] = dd(Mbd.astype(BF16), Hb[p]) + o7[p][:, PAIR:]
    ys = [Y[p][:C] + Y[p][C:] for p in pairs]
    rk = [r_ref[0, :, sl] * k_ref[0, :, sl] * pv_ref[0:1, sl] for sl in sls]
    st1 = [head_sums(jnp.concatenate([ys[p], rk[p]], axis=0)) for p in pairs]
    yc = [ys[p] - st1[p][:C] * inv_n for p in pairs]
    var = [head_sums(yc[p] * yc[p]) * inv_n for p in pairs]
    for p in pairs:
        sl = sls[p]
        bonus = st1[p][C:] * v_ref[0, :, sl]
        y_ref[0, :, sl] = yc[p] * lax.rsqrt(var[p] + RWKV_GN_EPS) * pv_ref[1:2, sl] + pv_ref[2:3, sl] + bonus


def _rwkv_scan(r, k, v, lw, kk, al, pvec):
    B, T, D = r.shape
    C = SCAN_CHUNK
    act = pl.BlockSpec((1, C, D), lambda b, c: (b, c, 0))
    return pl.pallas_call(
        _rwkv_scan_kernel,
        out_shape=jax.ShapeDtypeStruct((B, T, D), F32),
        grid=(B, T // C),
        in_specs=[act] * 6 + [pl.BlockSpec((3, D), lambda b, c: (0, 0))],
        out_specs=act,
        scratch_shapes=[pltpu.VMEM((RWKV_HEADS // 2, PAIR, PAIR), F32)],
        compiler_params=_cparams("arbitrary", "arbitrary"),
        name="rwkv_scan",
    )(r, k, v, lw, kk, al, pvec)


def _proj_res_kernel(*refs, has_gate):
    if has_gate:
        y_ref, g_ref, x_ref, gm_ref, w_ref, o_ref = refs
        y = y_ref[0] * g_ref[0]
    else:
        y_ref, x_ref, gm_ref, w_ref, o_ref = refs
        y = y_ref[0]
    o_ref[0] = x_ref[0] + gm_ref[0] * jnp.dot(y.astype(BF16), w_ref[...], preferred_element_type=F32)


def _proj_res(y, g, x, gm, w):
    B, T, D = x.shape
    tm = DENSE_TM
    act = pl.BlockSpec((1, tm, D), lambda b, i: (b, i, 0))
    ins = [y] + ([g] if g is not None else []) + [x, gm, w]
    specs = [act] * (len(ins) - 2) + [pl.BlockSpec((1, 1, D), lambda b, i: (b, 0, 0)),
                                     pl.BlockSpec((D, D), lambda b, i: (0, 0))]
    return pl.pallas_call(
        functools.partial(_proj_res_kernel, has_gate=g is not None),
        out_shape=jax.ShapeDtypeStruct((B, T, D), F32),
        grid=(B, T // tm),
        in_specs=specs,
        out_specs=act,
        compiler_params=_cparams("arbitrary", "arbitrary"),
        name="proj_res",
    )(*ins)


def _norm_mm_kernel(x_ref, mod_ref, g_ref, w_ref, o_ref):
    h = _rms_mod(x_ref[0], g_ref[...], mod_ref[0, 0:1, :], mod_ref[0, 1:2, :])
    o_ref[0] = jnp.dot(h.astype(BF16), w_ref[...], preferred_element_type=F32).astype(o_ref.dtype)


def _norm_mm(x, mod2, g, w, out_dtype):
    B, T, D = x.shape
    N = w.shape[1]
    tm = DENSE_TM
    return pl.pallas_call(
        _norm_mm_kernel,
        out_shape=jax.ShapeDtypeStruct((B, T, N), out_dtype),
        grid=(B, T // tm),
        in_specs=[
            pl.BlockSpec((1, tm, D), lambda b, i: (b, i, 0)),
            pl.BlockSpec((1, 2, D), lambda b, i: (b, 0, 0)),
            pl.BlockSpec((1, D), lambda b, i: (0, 0)),
            pl.BlockSpec((D, N), lambda b, i: (0, 0)),
        ],
        out_specs=pl.BlockSpec((1, tm, N), lambda b, i: (b, i, 0)),
        compiler_params=_cparams("arbitrary", "arbitrary"),
        name="norm_mm",
    )(x, mod2, g, w)


def _diff_attn_kernel(q_ref, k_ref, v_ref, lam_ref, sg_ref, o_ref, m_ref, acc_ref, *, lambda_init):
    tq, HB, dv = ATT_TQ, ATT_HB, DIFF_V_DIM
    qi = pl.program_id(2)
    heads = range(HB)
    hs = [slice(h * dv, (h + 1) * dv) for h in heads]
    lane = lax.broadcasted_iota(jnp.int32, (1, dv), 1)
    m_left = (lane < DIFF_QK_DIM).astype(F32)
    qs = []
    for h in heads:
        q = q_ref[0, :, hs[h]].astype(F32) * (DIFF_QK_DIM ** -0.5)
        qs.append(jnp.concatenate([q * m_left, q * (1.0 - m_left)], axis=0).astype(BF16))
    ones_col = jnp.ones((tq, dv), BF16)
    m_ref[...] = jnp.full(m_ref.shape, -jnp.inf, F32)
    acc_ref[...] = jnp.zeros(acc_ref.shape, F32)
    causal = (lax.broadcasted_iota(jnp.int32, (2 * tq, tq), 1)
              <= lax.broadcasted_iota(jnp.int32, (2 * tq, tq), 0) % tq)

    def block(j, masked):
        rows = pl.ds(pl.multiple_of(j * tq, tq), tq)

        def scores(h):
            return lax.dot_general(qs[h], k_ref[0, rows, hs[h]], (((1,), (1,)), ((), ())),
                                   preferred_element_type=F32)

        def softmax(h, s):
            if masked:
                s = jnp.where(causal, s, -jnp.inf)
            m_old = m_ref[h]
            m_new = jnp.maximum(m_old, jnp.max(s, axis=-1, keepdims=True))
            m_ref[h] = m_new
            p = jnp.exp(s - jnp.concatenate([m_new] * (tq // LANES), axis=1)).astype(BF16)
            return p, jnp.exp(m_old - m_new)

        def accumulate(h, p, alpha):
            pv = jnp.dot(p, jnp.concatenate([v_ref[0, rows, hs[h]], ones_col], axis=1), preferred_element_type=F32)
            acc_ref[h] = acc_ref[h] * jnp.concatenate([alpha, alpha], axis=1) + pv

        s_next = scores(0)
        for h in heads:
            s_cur = s_next
            if h + 1 < HB:
                s_next = scores(h + 1)
            accumulate(h, *softmax(h, s_cur))

    def body(j, c):
        block(j, False)
        return c

    lax.fori_loop(0, qi, body, 0)
    block(qi, True)
    lv = lam_ref[...]
    lam = (jnp.exp(jnp.sum(lv[0:1] * lv[1:2], axis=-1, keepdims=True))
           - jnp.exp(jnp.sum(lv[2:3] * lv[3:4], axis=-1, keepdims=True)) + lambda_init)
    for h in heads:
        acc = acc_ref[h]
        o = acc[:, :dv] / acc[:, dv:dv + 1]
        o = o[:tq] - lam * o[tq:]
        ms = jnp.mean(o * o, axis=-1, keepdims=True)
        o_ref[0, :, hs[h]] = (o * lax.rsqrt(ms + SUBLN_EPS) * sg_ref[...] * (1.0 - lambda_init)).astype(o_ref.dtype)


def _diff_attn(q, kv, lam_vecs, subln_g, lambda_init):
    B, T, D = q.shape
    tq, HB = ATT_TQ, ATT_HB
    n_hb = DIFF_HEADS // HB
    w = HB * DIFF_V_DIM
    return pl.pallas_call(
        functools.partial(_diff_attn_kernel, lambda_init=lambda_init),
        out_shape=jax.ShapeDtypeStruct((B, T, D), BF16),
        grid=(B, n_hb, T // tq),
        in_specs=[
            pl.BlockSpec((1, tq, w), lambda b, h, i: (b, i, h)),
            pl.BlockSpec((1, T, w), lambda b, h, i: (b, 0, h)),
            pl.BlockSpec((1, T, w), lambda b, h, i: (b, 0, n_hb + h)),
            pl.BlockSpec((4, DIFF_QK_DIM), lambda b, h, i: (0, 0)),
            pl.BlockSpec((1, DIFF_V_DIM), lambda b, h, i: (0, 0)),
        ],
        out_specs=pl.BlockSpec((1, tq, w), lambda b, h, i: (b, i, h)),
        scratch_shapes=[pltpu.VMEM((HB, 2 * tq, LANES), F32), pltpu.VMEM((HB, 2 * tq, 2 * DIFF_V_DIM), F32)],
        compiler_params=_cparams("arbitrary", "arbitrary", "arbitrary"),
        name="diff_attn",
    )(q, kv, kv, lam_vecs, subln_g)


def _router_kernel(x_ref, mod_ref, g_ref, w_ref, info_ref):
    h = _rms_mod(x_ref[...], g_ref[...], mod_ref[0, 0:1, :], mod_ref[0, 1:2, :])
    logit = _dot3(h, w_ref[...])
    lane_i = lax.broadcasted_iota(jnp.int32, logit.shape, 1)
    lane = lane_i.astype(F32)
    neg = -jnp.inf
    big = float(LANES)
    is_grp = lane_i < N_GROUPS
    gl = jnp.where(is_grp, logit, neg)
    gmax = jnp.max(gl, axis=-1, keepdims=True)
    gidx = jnp.min(jnp.where(gl == gmax, lane, big), axis=-1, keepdims=True)
    grp_gate = 1.0 / jnp.sum(jnp.where(is_grp, jnp.exp(logit - gmax), 0.0), axis=-1, keepdims=True)
    lo = N_GROUPS + gidx * EXPERTS_PER_GROUP
    in_grp = (lane >= lo) & (lane < lo + EXPERTS_PER_GROUP)
    el = jnp.where(in_grp, logit, neg)
    t1 = jnp.max(el, axis=-1, keepdims=True)
    i1 = jnp.min(jnp.where(el == t1, lane, big), axis=-1, keepdims=True)
    el2 = jnp.where(lane == i1, neg, el)
    t2 = jnp.max(el2, axis=-1, keepdims=True)
    i2 = jnp.min(jnp.where(el2 == t2, lane, big), axis=-1, keepdims=True)
    e21 = jnp.exp(t2 - t1)
    p1 = 1.0 / (1.0 + e21)
    w1 = grp_gate * p1
    w2 = grp_gate * (e21 * p1)
    e1 = i1 - N_GROUPS
    e2 = i2 - N_GROUPS
    info_ref[...] = jnp.where(lane_i == 0, e1, jnp.where(lane_i == 1, e2, jnp.where(lane_i == 2, w1, jnp.where(lane_i == 3, w2, 0.0))))


def _router(x2, mod2, g, w_cat, T):
    n_tok, D = x2.shape
    tm = ROUTE_TM
    per_b = T // tm
    return pl.pallas_call(
        _router_kernel,
        out_shape=jax.ShapeDtypeStruct((n_tok, LANES), F32),
        grid=(n_tok // tm,),
        in_specs=[
            pl.BlockSpec((tm, D), lambda i: (i, 0)),
            pl.BlockSpec((1, 2, D), lambda i: (i // per_b, 0, 0)),
            pl.BlockSpec((1, D), lambda i: (0, 0)),
            pl.BlockSpec((D, LANES), lambda i: (0, 0)),
        ],
        out_specs=pl.BlockSpec((tm, LANES), lambda i: (i, 0)),
        compiler_params=_cparams("arbitrary"),
        name="moe_router",
    )(x2, mod2, g, w_cat)


def _rank_kernel(info_ref, dest_ref, meta_ref, cnt_ref, start_ref):
    ps = pl.program_id(0)
    i = pl.program_id(1)
    tb = info_ref.shape[0]
    lane = lax.broadcasted_iota(jnp.int32, (tb, LANES), 1)
    info = info_ref[...]
    e0 = info[:, 0:1].astype(jnp.int32)
    e1 = info[:, 1:2].astype(jnp.int32)
    o0 = (lane == e0).astype(F32)
    o1 = (lane == e1).astype(F32)
    both = o0 + o1

    @pl.when((ps == 0) & (i == 0))
    def _():
        cnt_ref[...] = jnp.zeros_like(cnt_ref)

    @pl.when(ps == 0)
    def _():
        cnt_ref[...] += jnp.sum(both, axis=0, keepdims=True)
        dest_ref[...] = jnp.zeros_like(dest_ref)
        meta_ref[...] = jnp.zeros_like(meta_ref)

    @pl.when((ps == 1) & (i == 0))
    def _():
        cnt = cnt_ref[...]
        padded = jnp.floor((cnt + (MOE_TM - 1)) * (1.0 / MOE_TM)) * MOE_TM
        r = lax.broadcasted_iota(jnp.int32, (LANES, LANES), 0)
        c = lax.broadcasted_iota(jnp.int32, (LANES, LANES), 1)
        upper_strict = (r < c).astype(BF16)
        start = _dot_hl(jnp.broadcast_to(padded, (SUBLANES, LANES)), upper_strict)[0:1]
        start_ref[...] = start
        meta_ref[...] = jnp.broadcast_to(start + padded, (SUBLANES, LANES))

    @pl.when(ps == 1)
    def _():
        r = lax.broadcasted_iota(jnp.int32, (tb, tb), 0)
        c = lax.broadcasted_iota(jnp.int32, (tb, tb), 1)
        lower_strict = (r > c).astype(BF16)
        before = jnp.dot(lower_strict, both.astype(BF16), preferred_element_type=F32) + start_ref[...]
        d0 = jnp.sum(o0 * before, axis=-1, keepdims=True)
        d1 = jnp.sum(o1 * before, axis=-1, keepdims=True)
        dest_ref[...] = jnp.where(lane == 0, d0, jnp.where(lane == 1, d1, 0.0)).astype(jnp.int32)
        start_ref[...] += jnp.sum(both, axis=0, keepdims=True)


def _rank(info):
    n_tok = info.shape[0]
    tb = RANK_TB
    return pl.pallas_call(
        _rank_kernel,
        out_shape=[jax.ShapeDtypeStruct((n_tok, LANES), jnp.int32), jax.ShapeDtypeStruct((SUBLANES, LANES), F32)],
        grid=(2, n_tok // tb),
        in_specs=[pl.BlockSpec((tb, LANES), lambda p, i: (i, 0))],
        out_specs=[pl.BlockSpec((tb, LANES), lambda p, i: (i * p, 0)), pl.BlockSpec((SUBLANES, LANES), lambda p, i: (0, 0))],
        scratch_shapes=[pltpu.VMEM((1, LANES), F32), pltpu.VMEM((1, LANES), F32)],
        compiler_params=_cparams("arbitrary", "arbitrary"),
        name="moe_rank",
    )(info)


assert D_MODEL == SUBLANES * LANES


def _tile_rows_store(ref, x):
    n = x.shape[0]
    for s in range(SUBLANES):
        ref[pl.ds(s, n, stride=SUBLANES), :] = x[:, s * LANES:(s + 1) * LANES]


def _tile_rows_load(ref, n):
    return jnp.concatenate([ref[pl.ds(s, n, stride=SUBLANES), :] for s in range(SUBLANES)], axis=1)


def _row_copy(src_ref, s, dst_ref, d, sem):
    rows = lambda r: pl.ds(pl.multiple_of(r * SUBLANES, SUBLANES), SUBLANES)
    return pltpu.make_async_copy(src_ref.at[rows(s)], dst_ref.at[rows(d)], sem)


def _dispatch_kernel(dest_ref, x_ref, mod_ref, g_ref, xs_in_ref, xs_ref, h_ref, sem):
    del xs_in_ref
    tb = x_ref.shape[0]
    _tile_rows_store(h_ref, _rms_mod(x_ref[...], g_ref[...], mod_ref[0, 0:1, :], mod_ref[0, 1:2, :]))

    def start(j, c):
        _row_copy(h_ref, j, xs_ref, dest_ref[0, 0, 2 * j], sem).start()
        _row_copy(h_ref, j, xs_ref, dest_ref[0, 0, 2 * j + 1], sem).start()
        return c

    lax.fori_loop(0, tb, start, 0, unroll=8)

    def wait(j, c):
        _row_copy(h_ref, 0, xs_ref, 0, sem).wait()
        return c

    lax.fori_loop(0, 2 * tb, wait, 0, unroll=8)


def _dispatch(dest3, x2, mod2, g, xs_zero, T):
    n_tok, D = x2.shape
    tb = ROW_TB
    per_b = T // tb
    return pl.pallas_call(
        _dispatch_kernel,
        out_shape=jax.ShapeDtypeStruct(xs_zero.shape, F32),
        grid=(n_tok // tb,),
        in_specs=[
            pl.BlockSpec((1, 1, 2 * tb), lambda i: (i, 0, 0), memory_space=pltpu.SMEM),
            pl.BlockSpec((tb, D), lambda i: (i, 0)),
            pl.BlockSpec((1, 2, D), lambda i: (i // per_b, 0, 0)),
            pl.BlockSpec((1, D), lambda i: (0, 0)),
            pl.BlockSpec(memory_space=pl.ANY),
        ],
        out_specs=pl.BlockSpec(memory_space=pl.ANY),
        scratch_shapes=[pltpu.VMEM((tb * SUBLANES, LANES), F32), pltpu.SemaphoreType.DMA(())],
        input_output_aliases={4: 0},
        compiler_params=_cparams("arbitrary"),
        name="moe_dispatch",
    )(dest3, x2, mod2, g, xs_zero)


def _expert_kernel(be_ref, nb_ref, xs_ref, wg_ref, wu_ref, wd_ref, ys_ref, wgb, wub, wdb):
    i = pl.program_id(0)
    changed = (i == 0) | (be_ref[i] != be_ref[jnp.maximum(i - 1, 0)])

    @pl.when(changed)
    def _():
        wgb[...] = wg_ref[0].astype(BF16)
        wub[...] = wu_ref[0].astype(BF16)
        wdb[...] = wd_ref[0].astype(BF16)

    @pl.when(i < nb_ref[0])
    def _():
        x = _tile_rows_load(xs_ref, MOE_TM).astype(BF16)
        a = jnp.dot(x, wgb[...], preferred_element_type=F32)
        u = jnp.dot(x, wub[...], preferred_element_type=F32)
        hdn = (a * jax.nn.sigmoid(a)) * u
        _tile_rows_store(ys_ref, jnp.dot(hdn.astype(BF16), wdb[...], preferred_element_type=F32))

    @pl.when(i >= nb_ref[0])
    def _():
        ys_ref[...] = jnp.zeros_like(ys_ref)


def _experts(blk_e, n_used, xs, w_gate, w_up, w_down, layer):
    D = D_MODEL
    n_rows = xs.shape[0] // SUBLANES
    tm = MOE_TM
    FF = EXPERT_FF
    row_block = pl.BlockSpec((tm * SUBLANES, LANES), lambda i, be, nb: (i, 0))
    grid_spec = pltpu.PrefetchScalarGridSpec(
        num_scalar_prefetch=2,
        grid=(n_rows // tm,),
        in_specs=[
            row_block,
            pl.BlockSpec((None, 1, D, FF), lambda i, be, nb: (layer, be[i], 0, 0)),
            pl.BlockSpec((None, 1, D, FF), lambda i, be, nb: (layer, be[i], 0, 0)),
            pl.BlockSpec((None, 1, FF, D), lambda i, be, nb: (layer, be[i], 0, 0)),
        ],
        out_specs=row_block,
        scratch_shapes=[pltpu.VMEM((D, FF), BF16), pltpu.VMEM((D, FF), BF16), pltpu.VMEM((FF, D), BF16)],
    )
    return pl.pallas_call(
        _expert_kernel,
        out_shape=jax.ShapeDtypeStruct(xs.shape, F32),
        grid_spec=grid_spec,
        compiler_params=_cparams("arbitrary"),
        name="moe_experts",
    )(blk_e, n_used, xs, w_gate, w_up, w_down)


def _combine_kernel(dest_ref, info_ref, x_ref, gf_ref, fg_ref, ys_ref, o_ref, y_ref, sem, *, final_norm):
    tb = x_ref.shape[0]

    def start(j, c):
        _row_copy(ys_ref, dest_ref[0, 0, 2 * j], y_ref.at[0], j, sem).start()
        _row_copy(ys_ref, dest_ref[0, 0, 2 * j + 1], y_ref.at[1], j, sem).start()
        return c

    lax.fori_loop(0, tb, start, 0, unroll=8)

    def wait(j, c):
        _row_copy(ys_ref, 0, y_ref.at[0], 0, sem).wait()
        return c

    lax.fori_loop(0, 2 * tb, wait, 0, unroll=8)
    info = info_ref[...]
    moe = info[:, 2:3] * _tile_rows_load(y_ref.at[0], tb) + info[:, 3:4] * _tile_rows_load(y_ref.at[1], tb)
    out = x_ref[...] + gf_ref[0] * moe
    if final_norm:
        ms = jnp.mean(out * out, axis=-1, keepdims=True)
        out = out * lax.rsqrt(ms + RMS_EPS) * fg_ref[...]
    o_ref[...] = out


def _combine(dest3, info, x2, gf, final_g, ys, T, final_norm):
    n_tok, D = x2.shape
    tb = ROW_TB
    per_b = T // tb
    return pl.pallas_call(
        functools.partial(_combine_kernel, final_norm=final_norm),
        out_shape=jax.ShapeDtypeStruct((n_tok, D), F32),
        grid=(n_tok // tb,),
        in_specs=[
            pl.BlockSpec((1, 1, 2 * tb), lambda i: (i, 0, 0), memory_space=pltpu.SMEM),
            pl.BlockSpec((tb, LANES), lambda i: (i, 0)),
            pl.BlockSpec((tb, D), lambda i: (i, 0)),
            pl.BlockSpec((1, 1, D), lambda i: (i // per_b, 0, 0)),
            pl.BlockSpec((1, D), lambda i: (0, 0)),
            pl.BlockSpec(memory_space=pl.ANY),
        ],
        out_specs=pl.BlockSpec((tb, D), lambda i: (i, 0)),
        scratch_shapes=[pltpu.VMEM((2, tb * SUBLANES, LANES), F32), pltpu.SemaphoreType.DMA(())],
        compiler_params=_cparams("arbitrary"),
        name="moe_combine",
    )(dest3, info, x2, gf, final_g, ys)


def _moe_layer(x, mod_f, gf, norm_g, w_rg, w_re, w_gate, w_up, w_down, layer, final_g, final_norm):
    B, T, D = x.shape
    n_tok = B * T
    x2 = x.reshape(n_tok, D)
    w_cat = jnp.concatenate([w_rg, w_re, jnp.zeros((D, LANES - N_GROUPS - N_EXPERTS), F32)], axis=1)
    info = _router(x2, mod_f, norm_g, w_cat, T)
    dest, meta = _rank(info)
    pad_end = meta[0, :N_EXPERTS].astype(jnp.int32)
    n_rows = -(-(2 * n_tok + N_EXPERTS * MOE_TM) // MOE_TM) * MOE_TM
    n_blocks = n_rows // MOE_TM
    blk_start = jnp.arange(n_blocks, dtype=jnp.int32) * MOE_TM
    blk_e = jnp.minimum(jnp.sum(pad_end[None, :] <= blk_start[:, None], axis=1), N_EXPERTS - 1).astype(jnp.int32)
    n_used = (pad_end[N_EXPERTS - 1:] // MOE_TM).astype(jnp.int32)
    dest3 = dest[:, :2].reshape(n_tok // ROW_TB, 1, 2 * ROW_TB)
    xs = _dispatch(dest3, x2, mod_f, norm_g, jnp.zeros((n_rows * SUBLANES, LANES), F32), T)
    ys = _experts(blk_e, n_used, xs, w_gate, w_up, w_down, layer)
    out = _combine(dest3, info, x2, gf, final_g, ys, T, final_norm)
    return out.reshape(B, T, D)


def kernel(x, c, ada_w, ada_b, norm_mix_g, norm_ffn_g, rw_mu, rw_w_rkv, rw_w0, rw_w1, rw_w2, rw_a0, rw_a1, rw_a2,
           rw_g1, rw_g2, rw_k_k, rw_k_a, rw_r_k, rw_gn_g, rw_gn_b, rw_w_o, ada_kv_w, ada_kv_b, norm_kv_g, w_kv,
           df_w_q, df_lq1, df_lk1, df_lq2, df_lk2, df_subln_g, df_w_o, moe_w_rg, moe_w_re, moe_w_gate, moe_w_up,
           moe_w_down, final_g):
    B, T, D = x.shape
    c_pad = jnp.zeros((SUBLANES, D), F32).at[:B].set(c)
    mod = _ada(c_pad, ada_w, ada_b, 6 * D // 4)[:, :B]
    mod_kv = _ada(c_pad, ada_kv_w[None], ada_kv_b[None], D)[0, :B]
    bf = lambda w: w.astype(BF16)
    row = lambda v: v.reshape(1, -1)

    for l in range(DEPTH):
        sh_m, sc_m, g_m, sh_f, sc_f, g_f = jnp.split(mod[l], 6, axis=-1)
        mod_m = jnp.stack([sh_m, sc_m], axis=1)
        mod_f = jnp.stack([sh_f, sc_f], axis=1)
        if l < N_A_LAYERS:
            i = l
            vec = jnp.stack([rw_w0[i], rw_a0[i], rw_k_k[i], rw_k_a[i]], axis=0)
            r, k, v, lw, kk, al, gate = _rwkv_proj(
                x, mod_m, row(norm_mix_g[l]), rw_mu[i], bf(rw_w_rkv[i]), bf(rw_w1[i]), bf(rw_w2[i]),
                bf(rw_a1[i]), bf(rw_a2[i]), bf(rw_g1[i]), bf(rw_g2[i]), vec)
            pvec = jnp.stack([rw_r_k[i].reshape(-1), rw_gn_g[i], rw_gn_b[i]], axis=0)
            y = _rwkv_scan(r, k, v, lw, kk, al, pvec)
            x = _proj_res(y, gate, x, g_m[:, None, :], bf(rw_w_o[i]))
        else:
            j = l - N_A_LAYERS
            if l == N_A_LAYERS:
                sh_kv, sc_kv = jnp.split(mod_kv, 2, axis=-1)
                kv = _norm_mm(x, jnp.stack([sh_kv, sc_kv], axis=1), row(norm_kv_g), bf(w_kv), BF16)
            q = _norm_mm(x, mod_m, row(norm_mix_g[l]), bf(df_w_q[j]), BF16)
            lambda_init = 0.8 - 0.6 * math.exp(-0.3 * l)
            lam_vecs = jnp.stack([df_lq1[j], df_lk1[j], df_lq2[j], df_lk2[j]], axis=0)
            o = _diff_attn(q, kv, lam_vecs, row(df_subln_g[j]), lambda_init)
            x = _proj_res(o, None, x, g_m[:, None, :], bf(df_w_o[j]))
        x = _moe_layer(x, mod_f, g_f[:, None, :], row(norm_ffn_g[l]), moe_w_rg[l], moe_w_re[l], moe_w_gate,
                       moe_w_up, moe_w_down, l, row(final_g), final_norm=(l == DEPTH - 1))
    return x
```

```python
import functools
import math

import jax
import jax.numpy as jnp
from jax import lax
from jax.experimental import pallas as pl
from jax.experimental.pallas import tpu as pltpu

F32 = jnp.float32
BF16 = jnp.bfloat16

D_MODEL = 1024
DEPTH = 2
N_A_LAYERS = DEPTH // 2
RWKV_HEAD = 64
RWKV_HEADS = D_MODEL // RWKV_HEAD
RWKV_GN_EPS = 64e-5
DIFF_QK_DIM = 64
DIFF_V_DIM = 2 * DIFF_QK_DIM
DIFF_HEADS = D_MODEL // DIFF_V_DIM
SUBLN_EPS = 1e-5
N_GROUPS = 4
EXPERTS_PER_GROUP = 8
N_EXPERTS = N_GROUPS * EXPERTS_PER_GROUP
EXPERT_FF = 512
RMS_EPS = 1e-6

LANES = 128
SUBLANES = 8
VMEM_LIMIT_BYTES = 56 * 1024 * 1024

SCAN_CHUNK = 64
SCAN_CHUNKS_PER_STEP = 2
PAIR = 2 * RWKV_HEAD
PROJ_TM = 256
DENSE_TM = 512
ATT_TQ = 512
ATT_HB = 2
MOE_TM = 256
ROUTE_TM = 512
RANK_TB = 512
ROW_TB = 256


def _cparams(*sem):
    return pltpu.CompilerParams(dimension_semantics=sem, vmem_limit_bytes=VMEM_LIMIT_BYTES)


def _dot(a, b):
    return jnp.dot(a.astype(BF16), b.astype(BF16), preferred_element_type=F32)


def _dot_nt(a, b):
    return lax.dot_general(a.astype(BF16), b.astype(BF16), (((1,), (1,)), ((), ())), preferred_element_type=F32)


def _split(x):
    hi = x.astype(BF16)
    lo = (x - hi.astype(F32)).astype(BF16)
    return hi, lo


def _dot3(a, b):
    ah, al = _split(a)
    bh, bl = _split(b)
    d = functools.partial(jnp.dot, preferred_element_type=F32)
    return d(ah, bh) + d(ah, bl) + d(al, bh)


def _dot_hl(a, b_exact):
    ah, al = _split(a)
    d = functools.partial(jnp.dot, preferred_element_type=F32)
    return d(ah, b_exact) + d(al, b_exact)


def _rms_mod(x, g, shift, scale):
    ms = jnp.mean(x * x, axis=-1, keepdims=True)
    return (x * lax.rsqrt(ms + RMS_EPS) * g) * (1.0 + scale) + shift


def _ada_kernel(c_ref, w_ref, b_ref, o_ref):
    c = c_ref[...]
    ca = c * jax.nn.sigmoid(c)
    o_ref[...] = _dot3(ca, w_ref[...]) + b_ref[...]


def _ada(c_pad, w, b, tn):
    L, D, N = w.shape
    return pl.pallas_call(
        _ada_kernel,
        out_shape=jax.ShapeDtypeStruct((L, SUBLANES, N), F32),
        grid=(L, N // tn),
        in_specs=[
            pl.BlockSpec((SUBLANES, D), lambda l, j: (0, 0)),
            pl.BlockSpec((None, D, tn), lambda l, j: (l, 0, j)),
            pl.BlockSpec((None, 1, tn), lambda l, j: (l, 0, j)),
        ],
        out_specs=pl.BlockSpec((None, SUBLANES, tn), lambda l, j: (l, 0, j)),
        compiler_params=_cparams("arbitrary", "arbitrary"),
        name="ada_mod",
    )(c_pad, w, b.reshape(L, 1, N))


def _rwkv_proj_kernel(x_ref, xp_ref, mod_ref, g_ref, mu_ref, wrkv_ref, w1_ref, w2_ref, a1_ref, a2_ref,
                      g1_ref, g2_ref, vec_ref, r_ref, k_ref, v_ref, lw_ref, kk_ref, al_ref, gate_ref):
    i = pl.program_id(1)
    g = g_ref[...]
    shift, scale = mod_ref[0, 0:1, :], mod_ref[0, 1:2, :]
    h = _rms_mod(x_ref[0], g, shift, scale)
    hp = _rms_mod(xp_ref[0, SUBLANES - 1:SUBLANES, :], g, shift, scale)
    hp = jnp.where(i == 0, 0.0, hp)
    row = lax.broadcasted_iota(jnp.int32, h.shape, 0)
    h_prev = jnp.where(row == 0, hp, pltpu.roll(h, 1, axis=0))
    xx = h_prev - h
    mu = mu_ref[...]
    xs = [(h + xx * mu[j:j + 1, :]).astype(BF16) for j in range(6)]
    w0, a0, k_k, k_a = (vec_ref[j:j + 1, :] for j in range(4))
    d = functools.partial(jnp.dot, preferred_element_type=F32)
    r = d(xs[0], wrkv_ref[0])
    k = d(xs[1], wrkv_ref[1])
    v = d(xs[2], wrkv_ref[2])
    z = w0 + _dot(jnp.tanh(d(xs[3], w1_ref[...])), w2_ref[...])
    lw = (-math.exp(-0.5)) * jax.nn.sigmoid(z)
    a = jax.nn.sigmoid(a0 + _dot(d(xs[4], a1_ref[...]), a2_ref[...]))
    gate = _dot(jax.nn.sigmoid(d(xs[5], g1_ref[...])), g2_ref[...])
    r_ref[0] = r
    k_ref[0] = k * (1.0 + (a - 1.0) * k_a)
    v_ref[0] = v
    lw_ref[0] = lw
    kk_ref[0] = k * k_k
    al_ref[0] = a
    gate_ref[0] = gate


def _rwkv_proj(x, mod2, g, mu, wrkv, w1, w2, a1, a2, g1, g2, vec):
    B, T, D = x.shape
    tm = PROJ_TM
    const2 = lambda b, i: (0, 0)
    const3 = lambda b, i: (0, 0, 0)
    act = pl.BlockSpec((1, tm, D), lambda b, i: (b, i, 0))
    n_sub = tm // SUBLANES
    return pl.pallas_call(
        _rwkv_proj_kernel,
        out_shape=[jax.ShapeDtypeStruct((B, T, D), F32)] * 7,
        grid=(B, T // tm),
        in_specs=[
            act,
            pl.BlockSpec((1, SUBLANES, D), lambda b, i: (b, jnp.maximum(i * n_sub - 1, 0), 0)),
            pl.BlockSpec((1, 2, D), lambda b, i: (b, 0, 0)),
            pl.BlockSpec((1, D), const2),
            pl.BlockSpec((6, D), const2),
            pl.BlockSpec((3, D, D), const3),
            pl.BlockSpec(w1.shape, const2), pl.BlockSpec(w2.shape, const2),
            pl.BlockSpec(a1.shape, const2), pl.BlockSpec(a2.shape, const2),
            pl.BlockSpec(g1.shape, const2), pl.BlockSpec(g2.shape, const2),
            pl.BlockSpec((4, D), const2),
        ],
        out_specs=[act] * 7,
        compiler_params=_cparams("arbitrary", "arbitrary"),
        name="rwkv_proj",
    )(x, x, mod2, g, mu, wrkv, w1, w2, a1, a2, g1, g2, vec)


def _rwkv_scan_kernel(r_ref, k_ref, v_ref, lw_ref, kk_ref, al_ref, pv_ref, y_ref, h_ref):
    C = SCAN_CHUNK
    P2 = 2 * C

    @pl.when(pl.program_id(1) == 0)
    def _():
        h_ref[...] = jnp.zeros_like(h_ref)

    lane = lax.broadcasted_iota(jnp.int32, (1, PAIR), 1)
    m_left = (lane < RWKV_HEAD).astype(F32)
    m_right = 1.0 - m_left
    ri = lax.broadcasted_iota(jnp.int32, (P2, P2), 0)
    ci = lax.broadcasted_iota(jnp.int32, (P2, P2), 1)
    same = (ri >= C) == (ci >= C)
    strict = same & (ri > ci)
    incl = same & (ri >= ci)
    eye = ri == ci
    block_ones = same.astype(BF16)
    tri = (lax.broadcasted_iota(jnp.int32, (C, C), 0) >= lax.broadcasted_iota(jnp.int32, (C, C), 1)).astype(BF16)

    def stack(x):
        return jnp.concatenate([x * m_left, x * m_right], axis=0)

    def head_sums(x):
        s_left = jnp.sum(x * m_left, axis=-1, keepdims=True)
        s_right = jnp.sum(x * m_right, axis=-1, keepdims=True)
        return jnp.where(lane < RWKV_HEAD, s_left, s_right)

    inv_n = 1.0 / RWKV_HEAD
    dd = functools.partial(jnp.dot, preferred_element_type=F32)
    n_pairs = RWKV_HEADS // 2
    units = [(ch, p) for ch in range(SCAN_CHUNKS_PER_STEP) for p in range(n_pairs)]
    idx = [(slice(ch * C, (ch + 1) * C), slice(p * PAIR, (p + 1) * PAIR)) for ch, p in units]
    U = range(len(units))
    kkr = [kk_ref[0, rs, sl] for rs, sl in idx]
    ss = [head_sums(x * x) for x in kkr]
    lws = [lw_ref[0, rs, sl] for rs, sl in idx]
    Ls = []
    for lw in lws:
        l_hi, l_lo = _split(lw)
        cs = dd(tri, jnp.concatenate([l_hi, l_lo], axis=1))
        Ls.append(cs[:, :PAIR] + cs[:, PAIR:])
    lhs_g, rhs_g, bk_hat, vs, at32, rt32, dec_end = [], [], [], [], [], [], []
    for u in U:
        rs, sl = idx[u]
        L, lw = Ls[u], lws[u]
        kk = kkr[u] * lax.rsqrt(jnp.maximum(ss[u], 1e-24))
        b_vec = kk * al_ref[0, rs, sl]
        k = k_ref[0, rs, sl]
        LC = L[C - 1:C, :]
        e_neg = jnp.exp(-L)
        e_end = jnp.exp(LC - L)
        At = stack(-kk * jnp.exp(L - lw))
        Rt = stack(r_ref[0, rs, sl] * jnp.exp(L))
        at32.append(At)
        rt32.append(Rt)
        lhs_g.append(jnp.concatenate([At, Rt], axis=0).astype(BF16))
        rhs_g.append(jnp.concatenate([stack(b_vec * e_neg), stack(k * e_neg)], axis=0).astype(BF16))
        bk_hat.append(jnp.concatenate([stack(b_vec * e_end), stack(k * e_end)], axis=0))
        vs.append(stack(v_ref[0, rs, sl]).astype(BF16))
        dec_end.append(jnp.exp(LC))
    G = [lax.dot_general(lhs_g[u], rhs_g[u], (((1,), (1,)), ((), ())), preferred_element_type=F32) for u in U]
    A_ab = [jnp.where(strict, G[u][:P2, :P2], 0.0).astype(BF16) for u in U]
    A_ak = [jnp.where(strict, G[u][:P2, P2:], 0.0).astype(BF16) for u in U]
    A_r = [jnp.concatenate([jnp.where(incl, G[u][P2:, :P2], 0.0), jnp.where(incl, G[u][P2:, P2:], 0.0)],
                           axis=1).astype(BF16) for u in U]
    W = [dd(A_ak[u], vs[u]) for u in U]
    Z = [jnp.concatenate([at32[u], W[u]], axis=1) for u in U]
    Ai = A_ab
    n_dbl = int(math.log2(C))
    for s in range(n_dbl):
        Z = [Z[u] + dd(Ai[u], Z[u].astype(BF16)) for u in U]
        if s + 1 < n_dbl:
            Ai = [dd(Ai[u], Ai[u]).astype(BF16) for u in U]
    rhs = [jnp.concatenate([Z[u].astype(BF16), jnp.concatenate([jnp.zeros_like(vs[u]), vs[u]], axis=1)], axis=0)
           for u in U]
    o6 = [dd(A_r[u], rhs[u]) for u in U]
    o7 = [dd(bk_hat[u].T.astype(BF16), rhs[u]) for u in U]
    H = [h_ref[p] for p in range(n_pairs)]
    Y = [None] * len(units)
    for u in U:
        p = units[u][1]
        Hb = H[p].astype(BF16)
        Y[u] = dd((rt32[u] + o6[u][:, :PAIR]).astype(BF16), Hb) + o6[u][:, PAIR:]
        Mbd = o7[u][:, :PAIR] + jnp.where(eye, dec_end[u], 0.0)
        H[p] = dd(Mbd.astype(BF16), Hb) + o7[u][:, PAIR:]
    for p in range(n_pairs):
        h_ref[p] = H[p]
    ys = [Y[u][:C] + Y[u][C:] for u in U]
    rk = [r_ref[0, rs, sl] * k_ref[0, rs, sl] * pv_ref[0:1, sl] for rs, sl in idx]
    st1 = [head_sums(jnp.concatenate([ys[u], rk[u]], axis=0)) for u in U]
    yc = [ys[u] - st1[u][:C] * inv_n for u in U]
    var = [head_sums(yc[u] * yc[u]) * inv_n for u in U]
    for u in U:
        rs, sl = idx[u]
        bonus = st1[u][C:] * v_ref[0, rs, sl]
        y_ref[0, rs, sl] = yc[u] * lax.rsqrt(var[u] + RWKV_GN_EPS) * pv_ref[1:2, sl] + pv_ref[2:3, sl] + bonus


def _rwkv_scan(r, k, v, lw, kk, al, pvec):
    B, T, D = r.shape
    rows = SCAN_CHUNK * SCAN_CHUNKS_PER_STEP
    act = pl.BlockSpec((1, rows, D), lambda b, c: (b, c, 0))
    return pl.pallas_call(
        _rwkv_scan_kernel,
        out_shape=jax.ShapeDtypeStruct((B, T, D), F32),
        grid=(B, T // rows),
        in_specs=[act] * 6 + [pl.BlockSpec((3, D), lambda b, c: (0, 0))],
        out_specs=act,
        scratch_shapes=[pltpu.VMEM((RWKV_HEADS // 2, PAIR, PAIR), F32)],
        compiler_params=_cparams("arbitrary", "arbitrary"),
        name="rwkv_scan",
    )(r, k, v, lw, kk, al, pvec)


def _proj_res_kernel(*refs, has_gate):
    if has_gate:
        y_ref, g_ref, x_ref, gm_ref, w_ref, o_ref = refs
        y = y_ref[0] * g_ref[0]
    else:
        y_ref, x_ref, gm_ref, w_ref, o_ref = refs
        y = y_ref[0]
    o_ref[0] = x_ref[0] + gm_ref[0] * jnp.dot(y.astype(BF16), w_ref[...], preferred_element_type=F32)


def _proj_res(y, g, x, gm, w):
    B, T, D = x.shape
    tm = DENSE_TM
    act = pl.BlockSpec((1, tm, D), lambda b, i: (b, i, 0))
    ins = [y] + ([g] if g is not None else []) + [x, gm, w]
    specs = [act] * (len(ins) - 2) + [pl.BlockSpec((1, 1, D), lambda b, i: (b, 0, 0)),
                                     pl.BlockSpec((D, D), lambda b, i: (0, 0))]
    return pl.pallas_call(
        functools.partial(_proj_res_kernel, has_gate=g is not None),
        out_shape=jax.ShapeDtypeStruct((B, T, D), F32),
        grid=(B, T // tm),
        in_specs=specs,
        out_specs=act,
        compiler_params=_cparams("arbitrary", "arbitrary"),
        name="proj_res",
    )(*ins)


def _norm_mm_kernel(x_ref, mod_ref, g_ref, w_ref, o_ref):
    h = _rms_mod(x_ref[0], g_ref[...], mod_ref[0, 0:1, :], mod_ref[0, 1:2, :])
    o_ref[0] = jnp.dot(h.astype(BF16), w_ref[...], preferred_element_type=F32).astype(o_ref.dtype)


def _norm_mm(x, mod2, g, w, out_dtype):
    B, T, D = x.shape
    N = w.shape[1]
    tm = DENSE_TM
    return pl.pallas_call(
        _norm_mm_kernel,
        out_shape=jax.ShapeDtypeStruct((B, T, N), out_dtype),
        grid=(B, T // tm),
        in_specs=[
            pl.BlockSpec((1, tm, D), lambda b, i: (b, i, 0)),
            pl.BlockSpec((1, 2, D), lambda b, i: (b, 0, 0)),
            pl.BlockSpec((1, D), lambda b, i: (0, 0)),
            pl.BlockSpec((D, N), lambda b, i: (0, 0)),
        ],
        out_specs=pl.BlockSpec((1, tm, N), lambda b, i: (b, i, 0)),
        compiler_params=_cparams("arbitrary", "arbitrary"),
        name="norm_mm",
    )(x, mod2, g, w)


def _diff_attn_kernel(q_ref, k_ref, v_ref, lam_ref, sg_ref, o_ref, m_ref, acc_ref, *, lambda_init):
    tq, HB, dv = ATT_TQ, ATT_HB, DIFF_V_DIM
    qi = pl.program_id(2)
    heads = range(HB)
    hs = [slice(h * dv, (h + 1) * dv) for h in heads]
    lane = lax.broadcasted_iota(jnp.int32, (1, dv), 1)
    m_left = (lane < DIFF_QK_DIM).astype(F32)
    qs = []
    for h in heads:
        q = q_ref[0, :, hs[h]].astype(F32) * (DIFF_QK_DIM ** -0.5)
        qs.append(jnp.concatenate([q * m_left, q * (1.0 - m_left)], axis=0).astype(BF16))
    ones_col = jnp.ones((tq, dv), BF16)
    m_ref[...] = jnp.full(m_ref.shape, -jnp.inf, F32)
    acc_ref[...] = jnp.zeros(acc_ref.shape, F32)
    causal = (lax.broadcasted_iota(jnp.int32, (2 * tq, tq), 1)
              <= lax.broadcasted_iota(jnp.int32, (2 * tq, tq), 0) % tq)

    def block(j, masked):
        rows = pl.ds(pl.multiple_of(j * tq, tq), tq)

        def scores(h):
            return lax.dot_general(qs[h], k_ref[0, rows, hs[h]], (((1,), (1,)), ((), ())),
                                   preferred_element_type=F32)

        def softmax(h, s):
            if masked:
                s = jnp.where(causal, s, -jnp.inf)
            m_old = m_ref[h]
            m_new = jnp.maximum(m_old, jnp.max(s, axis=-1, keepdims=True))
            m_ref[h] = m_new
            p = jnp.exp(s - jnp.concatenate([m_new] * (tq // LANES), axis=1)).astype(BF16)
            return p, jnp.exp(m_old - m_new)

        def accumulate(h, p, alpha):
            pv = jnp.dot(p, jnp.concatenate([v_ref[0, rows, hs[h]], ones_col], axis=1), preferred_element_type=F32)
            acc_ref[h] = acc_ref[h] * jnp.concatenate([alpha, alpha], axis=1) + pv

        s_next = scores(0)
        for h in heads:
            s_cur = s_next
            if h + 1 < HB:
                s_next = scores(h + 1)
            accumulate(h, *softmax(h, s_cur))

    def body(j, c):
        block(j, False)
        return c

    lax.fori_loop(0, qi, body, 0)
    block(qi, True)
    lv = lam_ref[...]
    lam = (jnp.exp(jnp.sum(lv[0:1] * lv[1:2], axis=-1, keepdims=True))
           - jnp.exp(jnp.sum(lv[2:3] * lv[3:4], axis=-1, keepdims=True)) + lambda_init)
    for h in heads:
        acc = acc_ref[h]
        o = acc[:, :dv] / acc[:, dv:dv + 1]
        o = o[:tq] - lam * o[tq:]
        ms = jnp.mean(o * o, axis=-1, keepdims=True)
        o_ref[0, :, hs[h]] = (o * lax.rsqrt(ms + SUBLN_EPS) * sg_ref[...] * (1.0 - lambda_init)).astype(o_ref.dtype)


def _diff_attn(q, kv, lam_vecs, subln_g, lambda_init):
    B, T, D = q.shape
    tq, HB = ATT_TQ, ATT_HB
    n_hb = DIFF_HEADS // HB
    w = HB * DIFF_V_DIM
    return pl.pallas_call(
        functools.partial(_diff_attn_kernel, lambda_init=lambda_init),
        out_shape=jax.ShapeDtypeStruct((B, T, D), BF16),
        grid=(B, n_hb, T // tq),
        in_specs=[
            pl.BlockSpec((1, tq, w), lambda b, h, i: (b, i, h)),
            pl.BlockSpec((1, T, w), lambda b, h, i: (b, 0, h)),
            pl.BlockSpec((1, T, w), lambda b, h, i: (b, 0, n_hb + h)),
            pl.BlockSpec((4, DIFF_QK_DIM), lambda b, h, i: (0, 0)),
            pl.BlockSpec((1, DIFF_V_DIM), lambda b, h, i: (0, 0)),
        ],
        out_specs=pl.BlockSpec((1, tq, w), lambda b, h, i: (b, i, h)),
        scratch_shapes=[pltpu.VMEM((HB, 2 * tq, LANES), F32), pltpu.VMEM((HB, 2 * tq, 2 * DIFF_V_DIM), F32)],
        compiler_params=_cparams("arbitrary", "arbitrary", "arbitrary"),
        name="diff_attn",
    )(q, kv, kv, lam_vecs, subln_g)


def _router_kernel(x_ref, mod_ref, g_ref, w_ref, info_ref):
    h = _rms_mod(x_ref[...], g_ref[...], mod_ref[0, 0:1, :], mod_ref[0, 1:2, :])
    logit = _dot3(h, w_ref[...])
    lane_i = lax.broadcasted_iota(jnp.int32, logit.shape, 1)
    lane = lane_i.astype(F32)
    neg = -jnp.inf
    big = float(LANES)
    is_grp = lane_i < N_GROUPS
    gl = jnp.where(is_grp, logit, neg)
    gmax = jnp.max(gl, axis=-1, keepdims=True)
    gidx = jnp.min(jnp.where(gl == gmax, lane, big), axis=-1, keepdims=True)
    grp_gate = 1.0 / jnp.sum(jnp.where(is_grp, jnp.exp(logit - gmax), 0.0), axis=-1, keepdims=True)
    lo = N_GROUPS + gidx * EXPERTS_PER_GROUP
    in_grp = (lane >= lo) & (lane < lo + EXPERTS_PER_GROUP)
    el = jnp.where(in_grp, logit, neg)
    t1 = jnp.max(el, axis=-1, keepdims=True)
    i1 = jnp.min(jnp.where(el == t1, lane, big), axis=-1, keepdims=True)
    el2 = jnp.where(lane == i1, neg, el)
    t2 = jnp.max(el2, axis=-1, keepdims=True)
    i2 = jnp.min(jnp.where(el2 == t2, lane, big), axis=-1, keepdims=True)
    e21 = jnp.exp(t2 - t1)
    p1 = 1.0 / (1.0 + e21)
    w1 = grp_gate * p1
    w2 = grp_gate * (e21 * p1)
    e1 = i1 - N_GROUPS
    e2 = i2 - N_GROUPS
    info_ref[...] = jnp.where(lane_i == 0, e1, jnp.where(lane_i == 1, e2, jnp.where(lane_i == 2, w1, jnp.where(lane_i == 3, w2, 0.0))))


def _router(x2, mod2, g, w_cat, T):
    n_tok, D = x2.shape
    tm = ROUTE_TM
    per_b = T // tm
    return pl.pallas_call(
        _router_kernel,
        out_shape=jax.ShapeDtypeStruct((n_tok, LANES), F32),
        grid=(n_tok // tm,),
        in_specs=[
            pl.BlockSpec((tm, D), lambda i: (i, 0)),
            pl.BlockSpec((1, 2, D), lambda i: (i // per_b, 0, 0)),
            pl.BlockSpec((1, D), lambda i: (0, 0)),
            pl.BlockSpec((D, LANES), lambda i: (0, 0)),
        ],
        out_specs=pl.BlockSpec((tm, LANES), lambda i: (i, 0)),
        compiler_params=_cparams("arbitrary"),
        name="moe_router",
    )(x2, mod2, g, w_cat)


def _rank_kernel(info_ref, dest_ref, meta_ref, cnt_ref, start_ref):
    ps = pl.program_id(0)
    i = pl.program_id(1)
    tb = info_ref.shape[0]
    lane = lax.broadcasted_iota(jnp.int32, (tb, LANES), 1)
    info = info_ref[...]
    e0 = info[:, 0:1].astype(jnp.int32)
    e1 = info[:, 1:2].astype(jnp.int32)
    o0 = (lane == e0).astype(F32)
    o1 = (lane == e1).astype(F32)
    both = o0 + o1

    @pl.when((ps == 0) & (i == 0))
    def _():
        cnt_ref[...] = jnp.zeros_like(cnt_ref)

    @pl.when(ps == 0)
    def _():
        cnt_ref[...] += jnp.sum(both, axis=0, keepdims=True)
        dest_ref[...] = jnp.zeros_like(dest_ref)
        meta_ref[...] = jnp.zeros_like(meta_ref)

    @pl.when((ps == 1) & (i == 0))
    def _():
        cnt = cnt_ref[...]
        padded = jnp.floor((cnt + (MOE_TM - 1)) * (1.0 / MOE_TM)) * MOE_TM
        r = lax.broadcasted_iota(jnp.int32, (LANES, LANES), 0)
        c = lax.broadcasted_iota(jnp.int32, (LANES, LANES), 1)
        upper_strict = (r < c).astype(BF16)
        start = _dot_hl(jnp.broadcast_to(padded, (SUBLANES, LANES)), upper_strict)[0:1]
        start_ref[...] = start
        meta_ref[...] = jnp.broadcast_to(start + padded, (SUBLANES, LANES))

    @pl.when(ps == 1)
    def _():
        r = lax.broadcasted_iota(jnp.int32, (tb, tb), 0)
        c = lax.broadcasted_iota(jnp.int32, (tb, tb), 1)
        lower_strict = (r > c).astype(BF16)
        before = jnp.dot(lower_strict, both.astype(BF16), preferred_element_type=F32) + start_ref[...]
        d0 = jnp.sum(o0 * before, axis=-1, keepdims=True)
        d1 = jnp.sum(o1 * before, axis=-1, keepdims=True)
        dest_ref[...] = jnp.where(lane == 0, d0, jnp.where(lane == 1, d1, 0.0)).astype(jnp.int32)
        start_ref[...] += jnp.sum(both, axis=0, keepdims=True)


def _rank(info):
    n_tok = info.shape[0]
    tb = RANK_TB
    return pl.pallas_call(
        _rank_kernel,
        out_shape=[jax.ShapeDtypeStruct((n_tok, LANES), jnp.int32), jax.ShapeDtypeStruct((SUBLANES, LANES), F32)],
        grid=(2, n_tok // tb),
        in_specs=[pl.BlockSpec((tb, LANES), lambda p, i: (i, 0))],
        out_specs=[pl.BlockSpec((tb, LANES), lambda p, i: (i * p, 0)), pl.BlockSpec((SUBLANES, LANES), lambda p, i: (0, 0))],
        scratch_shapes=[pltpu.VMEM((1, LANES), F32), pltpu.VMEM((1, LANES), F32)],
        compiler_params=_cparams("arbitrary", "arbitrary"),
        name="moe_rank",
    )(info)


assert D_MODEL == SUBLANES * LANES


def _tile_rows_store(ref, x):
    n = x.shape[0]
    for s in range(SUBLANES):
        ref[pl.ds(s, n, stride=SUBLANES), :] = x[:, s * LANES:(s + 1) * LANES]


def _tile_rows_load(ref, n):
    return jnp.concatenate([ref[pl.ds(s, n, stride=SUBLANES), :] for s in range(SUBLANES)], axis=1)


def _row_copy(src_ref, s, dst_ref, d, sem):
    rows = lambda r: pl.ds(pl.multiple_of(r * SUBLANES, SUBLANES), SUBLANES)
    return pltpu.make_async_copy(src_ref.at[rows(s)], dst_ref.at[rows(d)], sem)


def _dispatch_kernel(dest_ref, x_ref, mod_ref, g_ref, xs_in_ref, xs_ref, h_ref, sem):
    del xs_in_ref
    tb = x_ref.shape[0]
    i = pl.program_id(0)
    last = pl.num_programs(0) - 1
    h = _rms_mod(x_ref[...], g_ref[...], mod_ref[0, 0:1, :], mod_ref[0, 1:2, :])

    def drain(slot):
        def wait(j, c):
            _row_copy(h_ref.at[slot], 0, xs_ref, 0, sem.at[slot]).wait()
            return c
        lax.fori_loop(0, 2 * tb, wait, 0, unroll=8)

    for slot in range(2):
        @pl.when(i % 2 == slot)
        def _(slot=slot):
            @pl.when(i >= 2)
            def _():
                drain(slot)

            _tile_rows_store(h_ref.at[slot], h)

            def start(j, c):
                _row_copy(h_ref.at[slot], j, xs_ref, dest_ref[0, 0, 2 * j], sem.at[slot]).start(priority=0)
                _row_copy(h_ref.at[slot], j, xs_ref, dest_ref[0, 0, 2 * j + 1], sem.at[slot]).start(priority=1)
                return c

            lax.fori_loop(0, tb, start, 0, unroll=8)

            @pl.when(i == last)
            def _():
                drain(slot)

                @pl.when(i >= 1)
                def _():
                    drain(1 - slot)


def _dispatch(dest3, x2, mod2, g, xs_zero, T):
    n_tok, D = x2.shape
    tb = ROW_TB
    per_b = T // tb
    return pl.pallas_call(
        _dispatch_kernel,
        out_shape=jax.ShapeDtypeStruct(xs_zero.shape, F32),
        grid=(n_tok // tb,),
        in_specs=[
            pl.BlockSpec((1, 1, 2 * tb), lambda i: (i, 0, 0), memory_space=pltpu.SMEM),
            pl.BlockSpec((tb, D), lambda i: (i, 0)),
            pl.BlockSpec((1, 2, D), lambda i: (i // per_b, 0, 0)),
            pl.BlockSpec((1, D), lambda i: (0, 0)),
            pl.BlockSpec(memory_space=pl.ANY),
        ],
        out_specs=pl.BlockSpec(memory_space=pl.ANY),
        scratch_shapes=[pltpu.VMEM((2, tb * SUBLANES, LANES), F32), pltpu.SemaphoreType.DMA((2,))],
        input_output_aliases={4: 0},
        compiler_params=_cparams("arbitrary"),
        name="moe_dispatch",
    )(dest3, x2, mod2, g, xs_zero)


def _expert_kernel(be_ref, nb_ref, xs_ref, wg_ref, wu_ref, wd_ref, ys_ref, wgb, wub, wdb):
    i = pl.program_id(0)
    changed = (i == 0) | (be_ref[i] != be_ref[jnp.maximum(i - 1, 0)])

    @pl.when(changed)
    def _():
        wgb[...] = wg_ref[0].astype(BF16)
        wub[...] = wu_ref[0].astype(BF16)
        wdb[...] = wd_ref[0].astype(BF16)

    @pl.when(i < nb_ref[0])
    def _():
        x = _tile_rows_load(xs_ref, MOE_TM).astype(BF16)
        a = jnp.dot(x, wgb[...], preferred_element_type=F32)
        u = jnp.dot(x, wub[...], preferred_element_type=F32)
        hdn = (a * jax.nn.sigmoid(a)) * u
        _tile_rows_store(ys_ref, jnp.dot(hdn.astype(BF16), wdb[...], preferred_element_type=F32))

    @pl.when(i >= nb_ref[0])
    def _():
        ys_ref[...] = jnp.zeros_like(ys_ref)


def _experts(blk_e, n_used, xs, w_gate, w_up, w_down, layer):
    D = D_MODEL
    n_rows = xs.shape[0] // SUBLANES
    tm = MOE_TM
    FF = EXPERT_FF
    row_block = pl.BlockSpec((tm * SUBLANES, LANES), lambda i, be, nb: (i, 0))
    grid_spec = pltpu.PrefetchScalarGridSpec(
        num_scalar_prefetch=2,
        grid=(n_rows // tm,),
        in_specs=[
            row_block,
            pl.BlockSpec((None, 1, D, FF), lambda i, be, nb: (layer, be[i], 0, 0)),
            pl.BlockSpec((None, 1, D, FF), lambda i, be, nb: (layer, be[i], 0, 0)),
            pl.BlockSpec((None, 1, FF, D), lambda i, be, nb: (layer, be[i], 0, 0)),
        ],
        out_specs=row_block,
        scratch_shapes=[pltpu.VMEM((D, FF), BF16), pltpu.VMEM((D, FF), BF16), pltpu.VMEM((FF, D), BF16)],
    )
    return pl.pallas_call(
        _expert_kernel,
        out_shape=jax.ShapeDtypeStruct(xs.shape, F32),
        grid_spec=grid_spec,
        compiler_params=_cparams("arbitrary"),
        name="moe_experts",
    )(blk_e, n_used, xs, w_gate, w_up, w_down)


def _combine_kernel(dest_ref, dest_next_ref, info_ref, x_ref, gf_ref, fg_ref, ys_ref, o_ref, y_ref, sem, *,
                    final_norm):
    tb = x_ref.shape[0]
    i = pl.program_id(0)
    last = pl.num_programs(0) - 1

    def gather(d_ref, slot):
        def start(j, c):
            _row_copy(ys_ref, d_ref[0, 0, 2 * j], y_ref.at[slot, 0], j, sem.at[slot]).start(priority=0)
            _row_copy(ys_ref, d_ref[0, 0, 2 * j + 1], y_ref.at[slot, 1], j, sem.at[slot]).start(priority=1)
            return c
        lax.fori_loop(0, tb, start, 0, unroll=8)

    @pl.when(i == 0)
    def _():
        gather(dest_ref, 0)

    for slot in range(2):
        @pl.when(i % 2 == slot)
        def _(slot=slot):
            @pl.when(i < last)
            def _():
                gather(dest_next_ref, 1 - slot)

            def wait(j, c):
                _row_copy(ys_ref, 0, y_ref.at[slot, 0], 0, sem.at[slot]).wait()
                return c

            lax.fori_loop(0, 2 * tb, wait, 0, unroll=8)
            info = info_ref[...]
            moe = (info[:, 2:3] * _tile_rows_load(y_ref.at[slot, 0], tb)
                   + info[:, 3:4] * _tile_rows_load(y_ref.at[slot, 1], tb))
            out = x_ref[...] + gf_ref[0] * moe
            if final_norm:
                ms = jnp.mean(out * out, axis=-1, keepdims=True)
                out = out * lax.rsqrt(ms + RMS_EPS) * fg_ref[...]
            o_ref[...] = out


def _combine(dest3, info, x2, gf, final_g, ys, T, final_norm):
    n_tok, D = x2.shape
    tb = ROW_TB
    per_b = T // tb
    n_steps = n_tok // tb
    return pl.pallas_call(
        functools.partial(_combine_kernel, final_norm=final_norm),
        out_shape=jax.ShapeDtypeStruct((n_tok, D), F32),
        grid=(n_steps,),
        in_specs=[
            pl.BlockSpec((1, 1, 2 * tb), lambda i: (i, 0, 0), memory_space=pltpu.SMEM),
            pl.BlockSpec((1, 1, 2 * tb), lambda i: (jnp.minimum(i + 1, n_steps - 1), 0, 0), memory_space=pltpu.SMEM),
            pl.BlockSpec((tb, LANES), lambda i: (i, 0)),
            pl.BlockSpec((tb, D), lambda i: (i, 0)),
            pl.BlockSpec((1, 1, D), lambda i: (i // per_b, 0, 0)),
            pl.BlockSpec((1, D), lambda i: (0, 0)),
            pl.BlockSpec(memory_space=pl.ANY),
        ],
        out_specs=pl.BlockSpec((tb, D), lambda i: (i, 0)),
        scratch_shapes=[pltpu.VMEM((2, 2, tb * SUBLANES, LANES), F32), pltpu.SemaphoreType.DMA((2,))],
        compiler_params=_cparams("arbitrary"),
        name="moe_combine",
    )(dest3, dest3, info, x2, gf, final_g, ys)


def _moe_layer(x, mod_f, gf, norm_g, w_rg, w_re, w_gate, w_up, w_down, layer, final_g, final_norm):
    B, T, D = x.shape
    n_tok = B * T
    x2 = x.reshape(n_tok, D)
    w_cat = jnp.concatenate([w_rg, w_re, jnp.zeros((D, LANES - N_GROUPS - N_EXPERTS), F32)], axis=1)
    info = _router(x2, mod_f, norm_g, w_cat, T)
    dest, meta = _rank(info)
    pad_end = meta[0, :N_EXPERTS].astype(jnp.int32)
    n_rows = -(-(2 * n_tok + N_EXPERTS * MOE_TM) // MOE_TM) * MOE_TM
    n_blocks = n_rows // MOE_TM
    blk_start = jnp.arange(n_blocks, dtype=jnp.int32) * MOE_TM
    blk_e = jnp.minimum(jnp.sum(pad_end[None, :] <= blk_start[:, None], axis=1), N_EXPERTS - 1).astype(jnp.int32)
    n_used = (pad_end[N_EXPERTS - 1:] // MOE_TM).astype(jnp.int32)
    dest3 = dest[:, :2].reshape(n_tok // ROW_TB, 1, 2 * ROW_TB)
    xs = _dispatch(dest3, x2, mod_f, norm_g, jnp.zeros((n_rows * SUBLANES, LANES), F32), T)
    ys = _experts(blk_e, n_used, xs, w_gate, w_up, w_down, layer)
    out = _combine(dest3, info, x2, gf, final_g, ys, T, final_norm)
    return out.reshape(B, T, D)


def kernel(x, c, ada_w, ada_b, norm_mix_g, norm_ffn_g, rw_mu, rw_w_rkv, rw_w0, rw_w1, rw_w2, rw_a0, rw_a1, rw_a2,
           rw_g1, rw_g2, rw_k_k, rw_k_a, rw_r_k, rw_gn_g, rw_gn_b, rw_w_o, ada_kv_w, ada_kv_b, norm_kv_g, w_kv,
           df_w_q, df_lq1, df_lk1, df_lq2, df_lk2, df_subln_g, df_w_o, moe_w_rg, moe_w_re, moe_w_gate, moe_w_up,
           moe_w_down, final_g):
    B, T, D = x.shape
    c_pad = jnp.zeros((SUBLANES, D), F32).at[:B].set(c)
    mod = _ada(c_pad, ada_w, ada_b, 6 * D // 4)[:, :B]
    mod_kv = _ada(c_pad, ada_kv_w[None], ada_kv_b[None], D)[0, :B]
    bf = lambda w: w.astype(BF16)
    row = lambda v: v.reshape(1, -1)

    for l in range(DEPTH):
        sh_m, sc_m, g_m, sh_f, sc_f, g_f = jnp.split(mod[l], 6, axis=-1)
        mod_m = jnp.stack([sh_m, sc_m], axis=1)
        mod_f = jnp.stack([sh_f, sc_f], axis=1)
        if l < N_A_LAYERS:
            i = l
            vec = jnp.stack([rw_w0[i], rw_a0[i], rw_k_k[i], rw_k_a[i]], axis=0)
            r, k, v, lw, kk, al, gate = _rwkv_proj(
                x, mod_m, row(norm_mix_g[l]), rw_mu[i], bf(rw_w_rkv[i]), bf(rw_w1[i]), bf(rw_w2[i]),
                bf(rw_a1[i]), bf(rw_a2[i]), bf(rw_g1[i]), bf(rw_g2[i]), vec)
            pvec = jnp.stack([rw_r_k[i].reshape(-1), rw_gn_g[i], rw_gn_b[i]], axis=0)
            y = _rwkv_scan(r, k, v, lw, kk, al, pvec)
            x = _proj_res(y, gate, x, g_m[:, None, :], bf(rw_w_o[i]))
        else:
            j = l - N_A_LAYERS
            if l == N_A_LAYERS:
                sh_kv, sc_kv = jnp.split(mod_kv, 2, axis=-1)
                kv = _norm_mm(x, jnp.stack([sh_kv, sc_kv], axis=1), row(norm_kv_g), bf(w_kv), BF16)
            q = _norm_mm(x, mod_m, row(norm_mix_g[l]), bf(df_w_q[j]), BF16)
            lambda_init = 0.8 - 0.6 * math.exp(-0.3 * l)
            lam_vecs = jnp.stack([df_lq1[j], df_lk1[j], df_lq2[j], df_lk2[j]], axis=0)
            o = _diff_attn(q, kv, lam_vecs, row(df_subln_g[j]), lambda_init)
            x = _proj_res(o, None, x, g_m[:, None, :], bf(df_w_o[j]))
        x = _moe_layer(x, mod_f, g_f[:, None, :], row(norm_ffn_g[l]), moe_w_rg[l], moe_w_re[l], moe_w_gate,
                       moe_w_up, moe_w_down, l, row(final_g), final_norm=(l == DEPTH - 1))
    return x
```

```python
import functools
import math

import jax
import jax.numpy as jnp
from jax import lax
from jax.experimental import pallas as pl
from jax.experimental.pallas import tpu as pltpu

F32 = jnp.float32
BF16 = jnp.bfloat16

D_MODEL = 1024
DEPTH = 2
N_A_LAYERS = DEPTH // 2
RWKV_HEAD = 64
RWKV_HEADS = D_MODEL // RWKV_HEAD
RWKV_GN_EPS = 64e-5
DIFF_QK_DIM = 64
DIFF_V_DIM = 2 * DIFF_QK_DIM
DIFF_HEADS = D_MODEL // DIFF_V_DIM
SUBLN_EPS = 1e-5
N_GROUPS = 4
EXPERTS_PER_GROUP = 8
N_EXPERTS = N_GROUPS * EXPERTS_PER_GROUP
EXPERT_FF = 512
RMS_EPS = 1e-6

LANES = 128
SUBLANES = 8
VMEM_LIMIT_BYTES = 56 * 1024 * 1024

SCAN_CHUNK = 64
SCAN_CHUNKS_PER_STEP = 2
PAIR = 2 * RWKV_HEAD
PROJ_TM = 256
DENSE_TM = 512
ATT_TQ = 512
ATT_HB = 2
MOE_TM = 256
ROUTE_TM = 512
RANK_TB = 512
ROW_TB = 256


def _cparams(*sem):
    return pltpu.CompilerParams(dimension_semantics=sem, vmem_limit_bytes=VMEM_LIMIT_BYTES)


def _dot(a, b):
    return jnp.dot(a.astype(BF16), b.astype(BF16), preferred_element_type=F32)


def _dot_nt(a, b):
    return lax.dot_general(a.astype(BF16), b.astype(BF16), (((1,), (1,)), ((), ())), preferred_element_type=F32)


def _split(x):
    hi = x.astype(BF16)
    lo = (x - hi.astype(F32)).astype(BF16)
    return hi, lo


def _dot3(a, b):
    ah, al = _split(a)
    bh, bl = _split(b)
    d = functools.partial(jnp.dot, preferred_element_type=F32)
    return d(ah, bh) + d(ah, bl) + d(al, bh)


def _dot_hl(a, b_exact):
    ah, al = _split(a)
    d = functools.partial(jnp.dot, preferred_element_type=F32)
    return d(ah, b_exact) + d(al, b_exact)


def _rms_mod(x, g, shift, scale):
    ms = jnp.mean(x * x, axis=-1, keepdims=True)
    return (x * lax.rsqrt(ms + RMS_EPS) * g) * (1.0 + scale) + shift


def _ada_kernel(c_ref, w_ref, b_ref, o_ref):
    c = c_ref[...]
    ca = c * jax.nn.sigmoid(c)
    o_ref[...] = _dot3(ca, w_ref[...]) + b_ref[...]


def _ada(c_pad, w, b, tn):
    L, D, N = w.shape
    return pl.pallas_call(
        _ada_kernel,
        out_shape=jax.ShapeDtypeStruct((L, SUBLANES, N), F32),
        grid=(L, N // tn),
        in_specs=[
            pl.BlockSpec((SUBLANES, D), lambda l, j: (0, 0)),
            pl.BlockSpec((None, D, tn), lambda l, j: (l, 0, j)),
            pl.BlockSpec((None, 1, tn), lambda l, j: (l, 0, j)),
        ],
        out_specs=pl.BlockSpec((None, SUBLANES, tn), lambda l, j: (l, 0, j)),
        compiler_params=_cparams("arbitrary", "arbitrary"),
        name="ada_mod",
    )(c_pad, w, b.reshape(L, 1, N))


def _rwkv_proj_kernel(x_ref, xp_ref, mod_ref, g_ref, mu_ref, wrkv_ref, w1_ref, w2_ref, a1_ref, a2_ref,
                      g1_ref, g2_ref, vec_ref, r_ref, k_ref, v_ref, lw_ref, kk_ref, al_ref, gate_ref):
    i = pl.program_id(1)
    g = g_ref[...]
    shift, scale = mod_ref[0, 0:1, :], mod_ref[0, 1:2, :]
    h = _rms_mod(x_ref[0], g, shift, scale)
    hp = _rms_mod(xp_ref[0, SUBLANES - 1:SUBLANES, :], g, shift, scale)
    hp = jnp.where(i == 0, 0.0, hp)
    row = lax.broadcasted_iota(jnp.int32, h.shape, 0)
    h_prev = jnp.where(row == 0, hp, pltpu.roll(h, 1, axis=0))
    xx = h_prev - h
    mu = mu_ref[...]
    xs = [(h + xx * mu[j:j + 1, :]).astype(BF16) for j in range(6)]
    w0, a0, k_k, k_a = (vec_ref[j:j + 1, :] for j in range(4))
    d = functools.partial(jnp.dot, preferred_element_type=F32)
    r = d(xs[0], wrkv_ref[0])
    k = d(xs[1], wrkv_ref[1])
    v = d(xs[2], wrkv_ref[2])
    z = w0 + _dot(jnp.tanh(d(xs[3], w1_ref[...])), w2_ref[...])
    lw = (-math.exp(-0.5)) * jax.nn.sigmoid(z)
    a = jax.nn.sigmoid(a0 + _dot(d(xs[4], a1_ref[...]), a2_ref[...]))
    gate = _dot(jax.nn.sigmoid(d(xs[5], g1_ref[...])), g2_ref[...])
    r_ref[0] = r.astype(BF16)
    k_ref[0] = (k * (1.0 + (a - 1.0) * k_a)).astype(BF16)
    v_ref[0] = v.astype(BF16)
    lw_ref[0] = lw
    kk_ref[0] = (k * k_k).astype(BF16)
    al_ref[0] = a.astype(BF16)
    gate_ref[0] = gate.astype(BF16)


def _rwkv_proj(x, mod2, g, mu, wrkv, w1, w2, a1, a2, g1, g2, vec):
    B, T, D = x.shape
    tm = PROJ_TM
    const2 = lambda b, i: (0, 0)
    const3 = lambda b, i: (0, 0, 0)
    act = pl.BlockSpec((1, tm, D), lambda b, i: (b, i, 0))
    n_sub = tm // SUBLANES
    return pl.pallas_call(
        _rwkv_proj_kernel,
        out_shape=[jax.ShapeDtypeStruct((B, T, D), F32 if n == 3 else BF16) for n in range(7)],
        grid=(B, T // tm),
        in_specs=[
            act,
            pl.BlockSpec((1, SUBLANES, D), lambda b, i: (b, jnp.maximum(i * n_sub - 1, 0), 0)),
            pl.BlockSpec((1, 2, D), lambda b, i: (b, 0, 0)),
            pl.BlockSpec((1, D), const2),
            pl.BlockSpec((6, D), const2),
            pl.BlockSpec((3, D, D), const3),
            pl.BlockSpec(w1.shape, const2), pl.BlockSpec(w2.shape, const2),
            pl.BlockSpec(a1.shape, const2), pl.BlockSpec(a2.shape, const2),
            pl.BlockSpec(g1.shape, const2), pl.BlockSpec(g2.shape, const2),
            pl.BlockSpec((4, D), const2),
        ],
        out_specs=[act] * 7,
        compiler_params=_cparams("arbitrary", "arbitrary"),
        name="rwkv_proj",
    )(x, x, mod2, g, mu, wrkv, w1, w2, a1, a2, g1, g2, vec)


def _rwkv_scan_kernel(r_ref, k_ref, v_ref, lw_ref, kk_ref, al_ref, pv_ref, y_ref, h_ref):
    C = SCAN_CHUNK
    P2 = 2 * C

    @pl.when(pl.program_id(1) == 0)
    def _():
        h_ref[...] = jnp.zeros_like(h_ref)

    lane = lax.broadcasted_iota(jnp.int32, (1, PAIR), 1)
    m_left = (lane < RWKV_HEAD).astype(F32)
    m_right = 1.0 - m_left
    ri = lax.broadcasted_iota(jnp.int32, (P2, P2), 0)
    ci = lax.broadcasted_iota(jnp.int32, (P2, P2), 1)
    same = (ri >= C) == (ci >= C)
    strict = same & (ri > ci)
    incl = same & (ri >= ci)
    eye = ri == ci
    block_ones = same.astype(BF16)
    tri = (lax.broadcasted_iota(jnp.int32, (C, C), 0) >= lax.broadcasted_iota(jnp.int32, (C, C), 1)).astype(BF16)

    def stack(x):
        return jnp.concatenate([x * m_left, x * m_right], axis=0)

    def head_sums(x):
        s_left = jnp.sum(x * m_left, axis=-1, keepdims=True)
        s_right = jnp.sum(x * m_right, axis=-1, keepdims=True)
        return jnp.where(lane < RWKV_HEAD, s_left, s_right)

    inv_n = 1.0 / RWKV_HEAD
    dd = functools.partial(jnp.dot, preferred_element_type=F32)
    n_pairs = RWKV_HEADS // 2
    units = [(ch, p) for ch in range(SCAN_CHUNKS_PER_STEP) for p in range(n_pairs)]
    idx = [(slice(ch * C, (ch + 1) * C), slice(p * PAIR, (p + 1) * PAIR)) for ch, p in units]
    U = range(len(units))
    ld = lambda ref, rs, sl: ref[0, rs, sl].astype(F32)
    kkr = [ld(kk_ref, rs, sl) for rs, sl in idx]
    ss = [head_sums(x * x) for x in kkr]
    lws = [lw_ref[0, rs, sl] for rs, sl in idx]
    Ls = []
    for lw in lws:
        l_hi, l_lo = _split(lw)
        cs = dd(tri, jnp.concatenate([l_hi, l_lo], axis=1))
        Ls.append(cs[:, :PAIR] + cs[:, PAIR:])
    lhs_g, rhs_g, bk_hat, vs, at32, rt32, dec_end = [], [], [], [], [], [], []
    for u in U:
        rs, sl = idx[u]
        L, lw = Ls[u], lws[u]
        kk = kkr[u] * lax.rsqrt(jnp.maximum(ss[u], 1e-24))
        b_vec = kk * ld(al_ref, rs, sl)
        k = ld(k_ref, rs, sl)
        LC = L[C - 1:C, :]
        e_neg = jnp.exp(-L)
        e_end = jnp.exp(LC - L)
        At = stack(-kk * jnp.exp(L - lw))
        Rt = stack(ld(r_ref, rs, sl) * jnp.exp(L))
        at32.append(At)
        rt32.append(Rt)
        lhs_g.append(jnp.concatenate([At, Rt], axis=0).astype(BF16))
        rhs_g.append(jnp.concatenate([stack(b_vec * e_neg), stack(k * e_neg)], axis=0).astype(BF16))
        bk_hat.append(jnp.concatenate([stack(b_vec * e_end), stack(k * e_end)], axis=0))
        vs.append(stack(ld(v_ref, rs, sl)).astype(BF16))
        dec_end.append(jnp.exp(LC))
    G = [lax.dot_general(lhs_g[u], rhs_g[u], (((1,), (1,)), ((), ())), preferred_element_type=F32) for u in U]
    A_ab = [jnp.where(strict, G[u][:P2, :P2], 0.0).astype(BF16) for u in U]
    A_ak = [jnp.where(strict, G[u][:P2, P2:], 0.0).astype(BF16) for u in U]
    A_r = [jnp.concatenate([jnp.where(incl, G[u][P2:, :P2], 0.0), jnp.where(incl, G[u][P2:, P2:], 0.0)],
                           axis=1).astype(BF16) for u in U]
    W = [dd(A_ak[u], vs[u]) for u in U]
    Z = [jnp.concatenate([at32[u], W[u]], axis=1) for u in U]
    Ai = A_ab
    n_dbl = int(math.log2(C))
    for s in range(n_dbl):
        Z = [Z[u] + dd(Ai[u], Z[u].astype(BF16)) for u in U]
        if s + 1 < n_dbl:
            Ai = [dd(Ai[u], Ai[u]).astype(BF16) for u in U]
    rhs = [jnp.concatenate([Z[u].astype(BF16), jnp.concatenate([jnp.zeros_like(vs[u]), vs[u]], axis=1)], axis=0)
           for u in U]
    o6 = [dd(A_r[u], rhs[u]) for u in U]
    o7 = [dd(bk_hat[u].T.astype(BF16), rhs[u]) for u in U]
    H = [h_ref[p] for p in range(n_pairs)]
    Y = [None] * len(units)
    for u in U:
        p = units[u][1]
        Hb = H[p].astype(BF16)
        Y[u] = dd((rt32[u] + o6[u][:, :PAIR]).astype(BF16), Hb) + o6[u][:, PAIR:]
        Mbd = o7[u][:, :PAIR] + jnp.where(eye, dec_end[u], 0.0)
        H[p] = dd(Mbd.astype(BF16), Hb) + o7[u][:, PAIR:]
    for p in range(n_pairs):
        h_ref[p] = H[p]
    ys = [Y[u][:C] + Y[u][C:] for u in U]
    rk = [ld(r_ref, rs, sl) * ld(k_ref, rs, sl) * pv_ref[0:1, sl] for rs, sl in idx]
    st1 = [head_sums(jnp.concatenate([ys[u], rk[u]], axis=0)) for u in U]
    yc = [ys[u] - st1[u][:C] * inv_n for u in U]
    var = [head_sums(yc[u] * yc[u]) * inv_n for u in U]
    for u in U:
        rs, sl = idx[u]
        bonus = st1[u][C:] * ld(v_ref, rs, sl)
        y_ref[0, rs, sl] = (yc[u] * lax.rsqrt(var[u] + RWKV_GN_EPS) * pv_ref[1:2, sl] + pv_ref[2:3, sl]
                            + bonus).astype(y_ref.dtype)


def _rwkv_scan(r, k, v, lw, kk, al, pvec):
    B, T, D = r.shape
    rows = SCAN_CHUNK * SCAN_CHUNKS_PER_STEP
    act = pl.BlockSpec((1, rows, D), lambda b, c: (b, c, 0))
    return pl.pallas_call(
        _rwkv_scan_kernel,
        out_shape=jax.ShapeDtypeStruct((B, T, D), BF16),
        grid=(B, T // rows),
        in_specs=[act] * 6 + [pl.BlockSpec((3, D), lambda b, c: (0, 0))],
        out_specs=act,
        scratch_shapes=[pltpu.VMEM((RWKV_HEADS // 2, PAIR, PAIR), F32)],
        compiler_params=_cparams("arbitrary", "arbitrary"),
        name="rwkv_scan",
    )(r, k, v, lw, kk, al, pvec)


def _proj_res_kernel(*refs, has_gate):
    if has_gate:
        y_ref, g_ref, x_ref, gm_ref, w_ref, o_ref = refs
        y = y_ref[0].astype(F32) * g_ref[0].astype(F32)
    else:
        y_ref, x_ref, gm_ref, w_ref, o_ref = refs
        y = y_ref[0]
    o_ref[0] = x_ref[0] + gm_ref[0] * jnp.dot(y.astype(BF16), w_ref[...], preferred_element_type=F32)


def _proj_res(y, g, x, gm, w):
    B, T, D = x.shape
    tm = DENSE_TM
    act = pl.BlockSpec((1, tm, D), lambda b, i: (b, i, 0))
    ins = [y] + ([g] if g is not None else []) + [x, gm, w]
    specs = [act] * (len(ins) - 2) + [pl.BlockSpec((1, 1, D), lambda b, i: (b, 0, 0)),
                                     pl.BlockSpec((D, D), lambda b, i: (0, 0))]
    return pl.pallas_call(
        functools.partial(_proj_res_kernel, has_gate=g is not None),
        out_shape=jax.ShapeDtypeStruct((B, T, D), F32),
        grid=(B, T // tm),
        in_specs=specs,
        out_specs=act,
        compiler_params=_cparams("arbitrary", "arbitrary"),
        name="proj_res",
    )(*ins)


def _norm_mm_kernel(x_ref, mod_ref, g_ref, w_ref, o_ref):
    h = _rms_mod(x_ref[0], g_ref[...], mod_ref[0, 0:1, :], mod_ref[0, 1:2, :])
    o_ref[0] = jnp.dot(h.astype(BF16), w_ref[...], preferred_element_type=F32).astype(o_ref.dtype)


def _norm_mm(x, mod2, g, w, out_dtype):
    B, T, D = x.shape
    N = w.shape[1]
    tm = DENSE_TM
    return pl.pallas_call(
        _norm_mm_kernel,
        out_shape=jax.ShapeDtypeStruct((B, T, N), out_dtype),
        grid=(B, T // tm),
        in_specs=[
            pl.BlockSpec((1, tm, D), lambda b, i: (b, i, 0)),
            pl.BlockSpec((1, 2, D), lambda b, i: (b, 0, 0)),
            pl.BlockSpec((1, D), lambda b, i: (0, 0)),
            pl.BlockSpec((D, N), lambda b, i: (0, 0)),
        ],
        out_specs=pl.BlockSpec((1, tm, N), lambda b, i: (b, i, 0)),
        compiler_params=_cparams("arbitrary", "arbitrary"),
        name="norm_mm",
    )(x, mod2, g, w)


def _diff_attn_kernel(q_ref, k_ref, v_ref, lam_ref, sg_ref, o_ref, m_ref, acc_ref, *, lambda_init):
    tq, HB, dv = ATT_TQ, ATT_HB, DIFF_V_DIM
    qi = pl.program_id(2)
    heads = range(HB)
    hs = [slice(h * dv, (h + 1) * dv) for h in heads]
    lane = lax.broadcasted_iota(jnp.int32, (1, dv), 1)
    m_left = (lane < DIFF_QK_DIM).astype(F32)
    qs = []
    for h in heads:
        q = q_ref[0, :, hs[h]].astype(F32) * (DIFF_QK_DIM ** -0.5)
        qs.append(jnp.concatenate([q * m_left, q * (1.0 - m_left)], axis=0).astype(BF16))
    ones_col = jnp.ones((tq, dv), BF16)
    m_ref[...] = jnp.full(m_ref.shape, -jnp.inf, F32)
    acc_ref[...] = jnp.zeros(acc_ref.shape, F32)
    causal = (lax.broadcasted_iota(jnp.int32, (2 * tq, tq), 1)
              <= lax.broadcasted_iota(jnp.int32, (2 * tq, tq), 0) % tq)

    def block(j, masked):
        rows = pl.ds(pl.multiple_of(j * tq, tq), tq)

        def scores(h):
            return lax.dot_general(qs[h], k_ref[0, rows, hs[h]], (((1,), (1,)), ((), ())),
                                   preferred_element_type=F32)

        def softmax(h, s):
            if masked:
                s = jnp.where(causal, s, -jnp.inf)
            m_old = m_ref[h]
            m_new = jnp.maximum(m_old, jnp.max(s, axis=-1, keepdims=True))
            m_ref[h] = m_new
            p = jnp.exp(s - jnp.concatenate([m_new] * (tq // LANES), axis=1)).astype(BF16)
            return p, jnp.exp(m_old - m_new)

        def accumulate(h, p, alpha):
            pv = jnp.dot(p, jnp.concatenate([v_ref[0, rows, hs[h]], ones_col], axis=1), preferred_element_type=F32)
            acc_ref[h] = acc_ref[h] * jnp.concatenate([alpha, alpha], axis=1) + pv

        s_next = scores(0)
        for h in heads:
            s_cur = s_next
            if h + 1 < HB:
                s_next = scores(h + 1)
            accumulate(h, *softmax(h, s_cur))

    def body(j, c):
        block(j, False)
        return c

    lax.fori_loop(0, qi, body, 0)
    block(qi, True)
    lv = lam_ref[...]
    lam = (jnp.exp(jnp.sum(lv[0:1] * lv[1:2], axis=-1, keepdims=True))
           - jnp.exp(jnp.sum(lv[2:3] * lv[3:4], axis=-1, keepdims=True)) + lambda_init)
    for h in heads:
        acc = acc_ref[h]
        o = acc[:, :dv] / acc[:, dv:dv + 1]
        o = o[:tq] - lam * o[tq:]
        ms = jnp.mean(o * o, axis=-1, keepdims=True)
        o_ref[0, :, hs[h]] = (o * lax.rsqrt(ms + SUBLN_EPS) * sg_ref[...] * (1.0 - lambda_init)).astype(o_ref.dtype)


def _diff_attn(q, kv, lam_vecs, subln_g, lambda_init):
    B, T, D = q.shape
    tq, HB = ATT_TQ, ATT_HB
    n_hb = DIFF_HEADS // HB
    w = HB * DIFF_V_DIM
    return pl.pallas_call(
        functools.partial(_diff_attn_kernel, lambda_init=lambda_init),
        out_shape=jax.ShapeDtypeStruct((B, T, D), BF16),
        grid=(B, n_hb, T // tq),
        in_specs=[
            pl.BlockSpec((1, tq, w), lambda b, h, i: (b, i, h)),
            pl.BlockSpec((1, T, w), lambda b, h, i: (b, 0, h)),
            pl.BlockSpec((1, T, w), lambda b, h, i: (b, 0, n_hb + h)),
            pl.BlockSpec((4, DIFF_QK_DIM), lambda b, h, i: (0, 0)),
            pl.BlockSpec((1, DIFF_V_DIM), lambda b, h, i: (0, 0)),
        ],
        out_specs=pl.BlockSpec((1, tq, w), lambda b, h, i: (b, i, h)),
        scratch_shapes=[pltpu.VMEM((HB, 2 * tq, LANES), F32), pltpu.VMEM((HB, 2 * tq, 2 * DIFF_V_DIM), F32)],
        compiler_params=_cparams("arbitrary", "arbitrary", "arbitrary"),
        name="diff_attn",
    )(q, kv, kv, lam_vecs, subln_g)


def _router_kernel(x_ref, mod_ref, g_ref, w_ref, info_ref):
    h = _rms_mod(x_ref[...], g_ref[...], mod_ref[0, 0:1, :], mod_ref[0, 1:2, :])
    logit = _dot3(h, w_ref[...])
    lane_i = lax.broadcasted_iota(jnp.int32, logit.shape, 1)
    lane = lane_i.astype(F32)
    neg = -jnp.inf
    big = float(LANES)
    is_grp = lane_i < N_GROUPS
    gl = jnp.where(is_grp, logit, neg)
    gmax = jnp.max(gl, axis=-1, keepdims=True)
    gidx = jnp.min(jnp.where(gl == gmax, lane, big), axis=-1, keepdims=True)
    grp_gate = 1.0 / jnp.sum(jnp.where(is_grp, jnp.exp(logit - gmax), 0.0), axis=-1, keepdims=True)
    lo = N_GROUPS + gidx * EXPERTS_PER_GROUP
    in_grp = (lane >= lo) & (lane < lo + EXPERTS_PER_GROUP)
    el = jnp.where(in_grp, logit, neg)
    t1 = jnp.max(el, axis=-1, keepdims=True)
    i1 = jnp.min(jnp.where(el == t1, lane, big), axis=-1, keepdims=True)
    el2 = jnp.where(lane == i1, neg, el)
    t2 = jnp.max(el2, axis=-1, keepdims=True)
    i2 = jnp.min(jnp.where(el2 == t2, lane, big), axis=-1, keepdims=True)
    e21 = jnp.exp(t2 - t1)
    p1 = 1.0 / (1.0 + e21)
    w1 = grp_gate * p1
    w2 = grp_gate * (e21 * p1)
    e1 = i1 - N_GROUPS
    e2 = i2 - N_GROUPS
    info_ref[...] = jnp.where(lane_i == 0, e1, jnp.where(lane_i == 1, e2, jnp.where(lane_i == 2, w1, jnp.where(lane_i == 3, w2, 0.0))))


def _router(x2, mod2, g, w_cat, T):
    n_tok, D = x2.shape
    tm = ROUTE_TM
    per_b = T // tm
    return pl.pallas_call(
        _router_kernel,
        out_shape=jax.ShapeDtypeStruct((n_tok, LANES), F32),
        grid=(n_tok // tm,),
        in_specs=[
            pl.BlockSpec((tm, D), lambda i: (i, 0)),
            pl.BlockSpec((1, 2, D), lambda i: (i // per_b, 0, 0)),
            pl.BlockSpec((1, D), lambda i: (0, 0)),
            pl.BlockSpec((D, LANES), lambda i: (0, 0)),
        ],
        out_specs=pl.BlockSpec((tm, LANES), lambda i: (i, 0)),
        compiler_params=_cparams("arbitrary"),
        name="moe_router",
    )(x2, mod2, g, w_cat)


def _rank_kernel(info_ref, dest_ref, meta_ref, zero_ref, cnt_ref, start_ref):
    ps = pl.program_id(0)
    i = pl.program_id(1)
    zero_ref[...] = jnp.zeros_like(zero_ref)
    tb = info_ref.shape[0]
    lane = lax.broadcasted_iota(jnp.int32, (tb, LANES), 1)
    info = info_ref[...]
    e0 = info[:, 0:1].astype(jnp.int32)
    e1 = info[:, 1:2].astype(jnp.int32)
    o0 = (lane == e0).astype(F32)
    o1 = (lane == e1).astype(F32)
    both = o0 + o1

    @pl.when((ps == 0) & (i == 0))
    def _():
        cnt_ref[...] = jnp.zeros_like(cnt_ref)

    @pl.when(ps == 0)
    def _():
        cnt_ref[...] += jnp.sum(both, axis=0, keepdims=True)
        dest_ref[...] = jnp.zeros_like(dest_ref)
        meta_ref[...] = jnp.zeros_like(meta_ref)

    @pl.when((ps == 1) & (i == 0))
    def _():
        cnt = cnt_ref[...]
        padded = jnp.floor((cnt + (MOE_TM - 1)) * (1.0 / MOE_TM)) * MOE_TM
        r = lax.broadcasted_iota(jnp.int32, (LANES, LANES), 0)
        c = lax.broadcasted_iota(jnp.int32, (LANES, LANES), 1)
        upper_strict = (r < c).astype(BF16)
        start = _dot_hl(jnp.broadcast_to(padded, (SUBLANES, LANES)), upper_strict)[0:1]
        start_ref[...] = start
        meta_ref[...] = jnp.broadcast_to(start + padded, (SUBLANES, LANES))

    @pl.when(ps == 1)
    def _():
        r = lax.broadcasted_iota(jnp.int32, (tb, tb), 0)
        c = lax.broadcasted_iota(jnp.int32, (tb, tb), 1)
        lower_strict = (r > c).astype(BF16)
        before = jnp.dot(lower_strict, both.astype(BF16), preferred_element_type=F32) + start_ref[...]
        d0 = jnp.sum(o0 * before, axis=-1, keepdims=True)
        d1 = jnp.sum(o1 * before, axis=-1, keepdims=True)
        dest_ref[...] = jnp.where(lane == 0, d0, jnp.where(lane == 1, d1, 0.0)).astype(jnp.int32)
        start_ref[...] += jnp.sum(both, axis=0, keepdims=True)


def _rank(info, n_rows):
    n_tok = info.shape[0]
    tb = RANK_TB
    n_blk = n_tok // tb
    zrows = n_rows * SUBLANES // (2 * n_blk)
    assert zrows * 2 * n_blk == n_rows * SUBLANES and zrows % SUBLANES == 0
    return pl.pallas_call(
        _rank_kernel,
        out_shape=[jax.ShapeDtypeStruct((n_tok, LANES), jnp.int32), jax.ShapeDtypeStruct((SUBLANES, LANES), F32),
                   jax.ShapeDtypeStruct((n_rows * SUBLANES, LANES), F32)],
        grid=(2, n_blk),
        in_specs=[pl.BlockSpec((tb, LANES), lambda p, i: (i, 0))],
        out_specs=[pl.BlockSpec((tb, LANES), lambda p, i: (i * p, 0)), pl.BlockSpec((SUBLANES, LANES), lambda p, i: (0, 0)),
                   pl.BlockSpec((zrows, LANES), lambda p, i: (p * n_blk + i, 0))],
        scratch_shapes=[pltpu.VMEM((1, LANES), F32), pltpu.VMEM((1, LANES), F32)],
        compiler_params=_cparams("arbitrary", "arbitrary"),
        name="moe_rank",
    )(info)


assert D_MODEL == SUBLANES * LANES


def _tile_rows_store(ref, x):
    n = x.shape[0]
    for s in range(SUBLANES):
        ref[pl.ds(s, n, stride=SUBLANES), :] = x[:, s * LANES:(s + 1) * LANES]


def _tile_rows_load(ref, n):
    return jnp.concatenate([ref[pl.ds(s, n, stride=SUBLANES), :] for s in range(SUBLANES)], axis=1)


def _row_copy(src_ref, s, dst_ref, d, sem):
    rows = lambda r: pl.ds(pl.multiple_of(r * SUBLANES, SUBLANES), SUBLANES)
    return pltpu.make_async_copy(src_ref.at[rows(s)], dst_ref.at[rows(d)], sem)


def _dispatch_kernel(dest_ref, x_ref, mod_ref, g_ref, xs_in_ref, xs_ref, h_ref, sem):
    del xs_in_ref
    tb = x_ref.shape[0]
    i = pl.program_id(0)
    last = pl.num_programs(0) - 1
    h = _rms_mod(x_ref[...], g_ref[...], mod_ref[0, 0:1, :], mod_ref[0, 1:2, :])

    def drain(slot):
        def wait(j, c):
            _row_copy(h_ref.at[slot], 0, xs_ref, 0, sem.at[slot]).wait()
            return c
        lax.fori_loop(0, 2 * tb, wait, 0, unroll=8)

    for slot in range(2):
        @pl.when(i % 2 == slot)
        def _(slot=slot):
            @pl.when(i >= 2)
            def _():
                drain(slot)

            _tile_rows_store(h_ref.at[slot], h)

            def start(j, c):
                _row_copy(h_ref.at[slot], j, xs_ref, dest_ref[0, 0, 2 * j], sem.at[slot]).start(priority=0)
                _row_copy(h_ref.at[slot], j, xs_ref, dest_ref[0, 0, 2 * j + 1], sem.at[slot]).start(priority=1)
                return c

            lax.fori_loop(0, tb, start, 0, unroll=8)

            @pl.when(i == last)
            def _():
                drain(slot)

                @pl.when(i >= 1)
                def _():
                    drain(1 - slot)


def _dispatch(dest3, x2, mod2, g, xs_zero, T):
    n_tok, D = x2.shape
    tb = ROW_TB
    per_b = T // tb
    return pl.pallas_call(
        _dispatch_kernel,
        out_shape=jax.ShapeDtypeStruct(xs_zero.shape, F32),
        grid=(n_tok // tb,),
        in_specs=[
            pl.BlockSpec((1, 1, 2 * tb), lambda i: (i, 0, 0), memory_space=pltpu.SMEM),
            pl.BlockSpec((tb, D), lambda i: (i, 0)),
            pl.BlockSpec((1, 2, D), lambda i: (i // per_b, 0, 0)),
            pl.BlockSpec((1, D), lambda i: (0, 0)),
            pl.BlockSpec(memory_space=pl.ANY),
        ],
        out_specs=pl.BlockSpec(memory_space=pl.ANY),
        scratch_shapes=[pltpu.VMEM((2, tb * SUBLANES, LANES), F32), pltpu.SemaphoreType.DMA((2,))],
        input_output_aliases={4: 0},
        compiler_params=_cparams("arbitrary"),
        name="moe_dispatch",
    )(dest3, x2, mod2, g, xs_zero)


def _expert_kernel(be_ref, nb_ref, nxt_ref, xs_ref, wg_hbm, wu_hbm, wd_hbm, ys_ref, stage_g, stage_u, stage_d,
                   wgb, wub, wdb, sem, *, layer):
    i = pl.program_id(0)
    e = be_ref[i]
    changed = (i == 0) | (e != be_ref[jnp.maximum(i - 1, 0)])

    def fetch(ex):
        return (pltpu.make_async_copy(wg_hbm.at[layer, ex], stage_g, sem.at[0]),
                pltpu.make_async_copy(wu_hbm.at[layer, ex], stage_u, sem.at[1]),
                pltpu.make_async_copy(wd_hbm.at[layer, ex], stage_d, sem.at[2]))

    @pl.when(i == 0)
    def _():
        for cp in fetch(e):
            cp.start()

    @pl.when(changed)
    def _():
        for cp in fetch(e):
            cp.wait()
        wgb[...] = stage_g[...].astype(BF16)
        wub[...] = stage_u[...].astype(BF16)
        wdb[...] = stage_d[...].astype(BF16)

        @pl.when(nxt_ref[i] >= 0)
        def _():
            for cp in fetch(nxt_ref[i]):
                cp.start()

    @pl.when(i < nb_ref[0])
    def _():
        x = _tile_rows_load(xs_ref, MOE_TM).astype(BF16)
        a = jnp.dot(x, wgb[...], preferred_element_type=F32)
        u = jnp.dot(x, wub[...], preferred_element_type=F32)
        hdn = (a * jax.nn.sigmoid(a)) * u
        _tile_rows_store(ys_ref, jnp.dot(hdn.astype(BF16), wdb[...], preferred_element_type=F32))

    @pl.when(i >= nb_ref[0])
    def _():
        ys_ref[...] = jnp.zeros_like(ys_ref)


def _experts(blk_e, n_used, next_e, xs, w_gate, w_up, w_down, layer):
    D = D_MODEL
    n_rows = xs.shape[0] // SUBLANES
    tm = MOE_TM
    FF = EXPERT_FF
    row_block = pl.BlockSpec((tm * SUBLANES, LANES), lambda i, be, nb, nx: (i, 0))
    hbm = pl.BlockSpec(memory_space=pl.ANY)
    grid_spec = pltpu.PrefetchScalarGridSpec(
        num_scalar_prefetch=3,
        grid=(n_rows // tm,),
        in_specs=[row_block, hbm, hbm, hbm],
        out_specs=row_block,
        scratch_shapes=[pltpu.VMEM((D, FF), F32), pltpu.VMEM((D, FF), F32), pltpu.VMEM((FF, D), F32),
                        pltpu.VMEM((D, FF), BF16), pltpu.VMEM((D, FF), BF16), pltpu.VMEM((FF, D), BF16),
                        pltpu.SemaphoreType.DMA((3,))],
    )
    return pl.pallas_call(
        functools.partial(_expert_kernel, layer=layer),
        out_shape=jax.ShapeDtypeStruct(xs.shape, F32),
        grid_spec=grid_spec,
        compiler_params=_cparams("arbitrary"),
        name="moe_experts",
    )(blk_e, n_used, next_e, xs, w_gate, w_up, w_down)


def _combine_kernel(dest_ref, dest_next_ref, info_ref, x_ref, gf_ref, fg_ref, ys_ref, o_ref, y_ref, sem, *,
                    final_norm):
    tb = x_ref.shape[0]
    i = pl.program_id(0)
    last = pl.num_programs(0) - 1

    def gather(d_ref, slot):
        def start(j, c):
            _row_copy(ys_ref, d_ref[0, 0, 2 * j], y_ref.at[slot, 0], j, sem.at[slot]).start(priority=0)
            _row_copy(ys_ref, d_ref[0, 0, 2 * j + 1], y_ref.at[slot, 1], j, sem.at[slot]).start(priority=1)
            return c
        lax.fori_loop(0, tb, start, 0, unroll=8)

    @pl.when(i == 0)
    def _():
        gather(dest_ref, 0)

    for slot in range(2):
        @pl.when(i % 2 == slot)
        def _(slot=slot):
            @pl.when(i < last)
            def _():
                gather(dest_next_ref, 1 - slot)

            def wait(j, c):
                _row_copy(ys_ref, 0, y_ref.at[slot, 0], 0, sem.at[slot]).wait()
                return c

            lax.fori_loop(0, 2 * tb, wait, 0, unroll=8)
            info = info_ref[...]
            moe = (info[:, 2:3] * _tile_rows_load(y_ref.at[slot, 0], tb)
                   + info[:, 3:4] * _tile_rows_load(y_ref.at[slot, 1], tb))
            out = x_ref[...] + gf_ref[0] * moe
            if final_norm:
                ms = jnp.mean(out * out, axis=-1, keepdims=True)
                out = out * lax.rsqrt(ms + RMS_EPS) * fg_ref[...]
            o_ref[...] = out


def _combine(dest3, info, x2, gf, final_g, ys, T, final_norm):
    n_tok, D = x2.shape
    tb = ROW_TB
    per_b = T // tb
    n_steps = n_tok // tb
    return pl.pallas_call(
        functools.partial(_combine_kernel, final_norm=final_norm),
        out_shape=jax.ShapeDtypeStruct((n_tok, D), F32),
        grid=(n_steps,),
        in_specs=[
            pl.BlockSpec((1, 1, 2 * tb), lambda i: (i, 0, 0), memory_space=pltpu.SMEM),
            pl.BlockSpec((1, 1, 2 * tb), lambda i: (jnp.minimum(i + 1, n_steps - 1), 0, 0), memory_space=pltpu.SMEM),
            pl.BlockSpec((tb, LANES), lambda i: (i, 0)),
            pl.BlockSpec((tb, D), lambda i: (i, 0)),
            pl.BlockSpec((1, 1, D), lambda i: (i // per_b, 0, 0)),
            pl.BlockSpec((1, D), lambda i: (0, 0)),
            pl.BlockSpec(memory_space=pl.ANY),
        ],
        out_specs=pl.BlockSpec((tb, D), lambda i: (i, 0)),
        scratch_shapes=[pltpu.VMEM((2, 2, tb * SUBLANES, LANES), F32), pltpu.SemaphoreType.DMA((2,))],
        compiler_params=_cparams("arbitrary"),
        name="moe_combine",
    )(dest3, dest3, info, x2, gf, final_g, ys)


def _moe_layer(x, mod_f, gf, norm_g, w_rg, w_re, w_gate, w_up, w_down, layer, final_g, final_norm):
    B, T, D = x.shape
    n_tok = B * T
    x2 = x.reshape(n_tok, D)
    w_cat = jnp.concatenate([w_rg, w_re, jnp.zeros((D, LANES - N_GROUPS - N_EXPERTS), F32)], axis=1)
    info = _router(x2, mod_f, norm_g, w_cat, T)
    n_rows = -(-(2 * n_tok + N_EXPERTS * MOE_TM) // MOE_TM) * MOE_TM
    dest, meta, xs_zero = _rank(info, n_rows)
    pad_end = meta[0, :N_EXPERTS].astype(jnp.int32)
    n_blocks = n_rows // MOE_TM
    blk_start = jnp.arange(n_blocks, dtype=jnp.int32) * MOE_TM
    blk_e = jnp.minimum(jnp.sum(pad_end[None, :] <= blk_start[:, None], axis=1), N_EXPERTS - 1).astype(jnp.int32)
    n_used = (pad_end[N_EXPERTS - 1:] // MOE_TM).astype(jnp.int32)
    dest3 = dest[:, :2].reshape(n_tok // ROW_TB, 1, 2 * ROW_TB)
    xs = _dispatch(dest3, x2, mod_f, norm_g, xs_zero, T)
    seg_end = jnp.sum(blk_e[None, :] <= blk_e[:, None], axis=1)
    next_e = jnp.where(seg_end < n_blocks, blk_e[jnp.minimum(seg_end, n_blocks - 1)], -1).astype(jnp.int32)
    ys = _experts(blk_e, n_used, next_e, xs, w_gate, w_up, w_down, layer)
    out = _combine(dest3, info, x2, gf, final_g, ys, T, final_norm)
    return out.reshape(B, T, D)


def kernel(x, c, ada_w, ada_b, norm_mix_g, norm_ffn_g, rw_mu, rw_w_rkv, rw_w0, rw_w1, rw_w2, rw_a0, rw_a1, rw_a2,
           rw_g1, rw_g2, rw_k_k, rw_k_a, rw_r_k, rw_gn_g, rw_gn_b, rw_w_o, ada_kv_w, ada_kv_b, norm_kv_g, w_kv,
           df_w_q, df_lq1, df_lk1, df_lq2, df_lk2, df_subln_g, df_w_o, moe_w_rg, moe_w_re, moe_w_gate, moe_w_up,
           moe_w_down, final_g):
    B, T, D = x.shape
    c_pad = jnp.zeros((SUBLANES, D), F32).at[:B].set(c)
    mod = _ada(c_pad, ada_w, ada_b, 6 * D // 4)[:, :B]
    mod_kv = _ada(c_pad, ada_kv_w[None], ada_kv_b[None], D)[0, :B]
    bf = lambda w: w.astype(BF16)
    row = lambda v: v.reshape(1, -1)

    for l in range(DEPTH):
        sh_m, sc_m, g_m, sh_f, sc_f, g_f = jnp.split(mod[l], 6, axis=-1)
        mod_m = jnp.stack([sh_m, sc_m], axis=1)
        mod_f = jnp.stack([sh_f, sc_f], axis=1)
        if l < N_A_LAYERS:
            i = l
            vec = jnp.stack([rw_w0[i], rw_a0[i], rw_k_k[i], rw_k_a[i]], axis=0)
            r, k, v, lw, kk, al, gate = _rwkv_proj(
                x, mod_m, row(norm_mix_g[l]), rw_mu[i], bf(rw_w_rkv[i]), bf(rw_w1[i]), bf(rw_w2[i]),
                bf(rw_a1[i]), bf(rw_a2[i]), bf(rw_g1[i]), bf(rw_g2[i]), vec)
            pvec = jnp.stack([rw_r_k[i].reshape(-1), rw_gn_g[i], rw_gn_b[i]], axis=0)
            y = _rwkv_scan(r, k, v, lw, kk, al, pvec)
            x = _proj_res(y, gate, x, g_m[:, None, :], bf(rw_w_o[i]))
        else:
            j = l - N_A_LAYERS
            if l == N_A_LAYERS:
                sh_kv, sc_kv = jnp.split(mod_kv, 2, axis=-1)
                kv = _norm_mm(x, jnp.stack([sh_kv, sc_kv], axis=1), row(norm_kv_g), bf(w_kv), BF16)
            q = _norm_mm(x, mod_m, row(norm_mix_g[l]), bf(df_w_q[j]), BF16)
            lambda_init = 0.8 - 0.6 * math.exp(-0.3 * l)
            lam_vecs = jnp.stack([df_lq1[j], df_lk1[j], df_lq2[j], df_lk2[j]], axis=0)
            o = _diff_attn(q, kv, lam_vecs, row(df_subln_g[j]), lambda_init)
            x = _proj_res(o, None, x, g_m[:, None, :], bf(df_w_o[j]))
        x = _moe_layer(x, mod_f, g_f[:, None, :], row(norm_ffn_g[l]), moe_w_rg[l], moe_w_re[l], moe_w_gate,
                       moe_w_up, moe_w_down, l, row(final_g), final_norm=(l == DEPTH - 1))
    return x
```

```python
import functools
import math

import jax
import jax.numpy as jnp
from jax import lax
from jax.experimental import pallas as pl
from jax.experimental.pallas import tpu as pltpu

F32 = jnp.float32
BF16 = jnp.bfloat16

D_MODEL = 1024
DEPTH = 2
N_A_LAYERS = DEPTH // 2
RWKV_HEAD = 64
RWKV_HEADS = D_MODEL // RWKV_HEAD
RWKV_GN_EPS = 64e-5
DIFF_QK_DIM = 64
DIFF_V_DIM = 2 * DIFF_QK_DIM
DIFF_HEADS = D_MODEL // DIFF_V_DIM
SUBLN_EPS = 1e-5
N_GROUPS = 4
EXPERTS_PER_GROUP = 8
N_EXPERTS = N_GROUPS * EXPERTS_PER_GROUP
EXPERT_FF = 512
RMS_EPS = 1e-6

LANES = 128
SUBLANES = 8
VMEM_LIMIT_BYTES = 56 * 1024 * 1024

SCAN_CHUNK = 64
SCAN_CHUNKS_PER_STEP = 2
PAIR = 2 * RWKV_HEAD
PROJ_TM = 256
DENSE_TM = 512
ATT_TQ = 512
ATT_HB = 2
MOE_TM = 256
ROUTE_TM = 512
RANK_TB = 512
ROW_TB = 256


def _cparams(*sem):
    return pltpu.CompilerParams(dimension_semantics=sem, vmem_limit_bytes=VMEM_LIMIT_BYTES)


def _dot(a, b):
    return jnp.dot(a.astype(BF16), b.astype(BF16), preferred_element_type=F32)


def _dot_nt(a, b):
    return lax.dot_general(a.astype(BF16), b.astype(BF16), (((1,), (1,)), ((), ())), preferred_element_type=F32)


def _split(x):
    hi = x.astype(BF16)
    lo = (x - hi.astype(F32)).astype(BF16)
    return hi, lo


def _dot3(a, b):
    ah, al = _split(a)
    bh, bl = _split(b)
    d = functools.partial(jnp.dot, preferred_element_type=F32)
    return d(ah, bh) + d(ah, bl) + d(al, bh)


def _dot_hl(a, b_exact):
    ah, al = _split(a)
    d = functools.partial(jnp.dot, preferred_element_type=F32)
    return d(ah, b_exact) + d(al, b_exact)


def _rms_mod(x, g, shift, scale):
    ms = jnp.mean(x * x, axis=-1, keepdims=True)
    return (x * lax.rsqrt(ms + RMS_EPS) * g) * (1.0 + scale) + shift


def _ada_kernel(c_ref, w_ref, b_ref, o_ref):
    c = c_ref[...]
    ca = c * jax.nn.sigmoid(c)
    o_ref[...] = _dot3(ca, w_ref[...]) + b_ref[...]


def _ada(c_pad, w, b, tn):
    L, D, N = w.shape
    return pl.pallas_call(
        _ada_kernel,
        out_shape=jax.ShapeDtypeStruct((L, SUBLANES, N), F32),
        grid=(L, N // tn),
        in_specs=[
            pl.BlockSpec((SUBLANES, D), lambda l, j: (0, 0)),
            pl.BlockSpec((None, D, tn), lambda l, j: (l, 0, j)),
            pl.BlockSpec((None, 1, tn), lambda l, j: (l, 0, j)),
        ],
        out_specs=pl.BlockSpec((None, SUBLANES, tn), lambda l, j: (l, 0, j)),
        compiler_params=_cparams("arbitrary", "arbitrary"),
        name="ada_mod",
    )(c_pad, w, b.reshape(L, 1, N))


def _rwkv_proj_kernel(x_ref, xp_ref, mod_ref, g_ref, mu_ref, wrkv_ref, w1_ref, w2_ref, a1_ref, a2_ref,
                      g1_ref, g2_ref, vec_ref, r_ref, k_ref, v_ref, lw_ref, kk_ref, al_ref, gate_ref):
    i = pl.program_id(1)
    g = g_ref[...]
    shift, scale = mod_ref[0, 0:1, :], mod_ref[0, 1:2, :]
    h = _rms_mod(x_ref[0], g, shift, scale)
    hp = _rms_mod(xp_ref[0, SUBLANES - 1:SUBLANES, :], g, shift, scale)
    hp = jnp.where(i == 0, 0.0, hp)
    row = lax.broadcasted_iota(jnp.int32, h.shape, 0)
    h_prev = jnp.where(row == 0, hp, pltpu.roll(h, 1, axis=0))
    xx = h_prev - h
    mu = mu_ref[...]
    xs = [(h + xx * mu[j:j + 1, :]).astype(BF16) for j in range(6)]
    w0, a0, k_k, k_a = (vec_ref[j:j + 1, :] for j in range(4))
    d = functools.partial(jnp.dot, preferred_element_type=F32)
    r = d(xs[0], wrkv_ref[0])
    k = d(xs[1], wrkv_ref[1])
    v = d(xs[2], wrkv_ref[2])
    z = w0 + _dot(jnp.tanh(d(xs[3], w1_ref[...])), w2_ref[...])
    lw = (-math.exp(-0.5)) * jax.nn.sigmoid(z)
    a = jax.nn.sigmoid(a0 + _dot(d(xs[4], a1_ref[...]), a2_ref[...]))
    gate = _dot(jax.nn.sigmoid(d(xs[5], g1_ref[...])), g2_ref[...])
    r_ref[0] = r.astype(BF16)
    k_ref[0] = (k * (1.0 + (a - 1.0) * k_a)).astype(BF16)
    v_ref[0] = v.astype(BF16)
    lw_ref[0] = lw
    kk_ref[0] = (k * k_k).astype(BF16)
    al_ref[0] = a.astype(BF16)
    gate_ref[0] = gate.astype(BF16)


def _rwkv_proj(x, mod2, g, mu, wrkv, w1, w2, a1, a2, g1, g2, vec):
    B, T, D = x.shape
    tm = PROJ_TM
    const2 = lambda b, i: (0, 0)
    const3 = lambda b, i: (0, 0, 0)
    act = pl.BlockSpec((1, tm, D), lambda b, i: (b, i, 0))
    n_sub = tm // SUBLANES
    return pl.pallas_call(
        _rwkv_proj_kernel,
        out_shape=[jax.ShapeDtypeStruct((B, T, D), F32 if n == 3 else BF16) for n in range(7)],
        grid=(B, T // tm),
        in_specs=[
            act,
            pl.BlockSpec((1, SUBLANES, D), lambda b, i: (b, jnp.maximum(i * n_sub - 1, 0), 0)),
            pl.BlockSpec((1, 2, D), lambda b, i: (b, 0, 0)),
            pl.BlockSpec((1, D), const2),
            pl.BlockSpec((6, D), const2),
            pl.BlockSpec((3, D, D), const3),
            pl.BlockSpec(w1.shape, const2), pl.BlockSpec(w2.shape, const2),
            pl.BlockSpec(a1.shape, const2), pl.BlockSpec(a2.shape, const2),
            pl.BlockSpec(g1.shape, const2), pl.BlockSpec(g2.shape, const2),
            pl.BlockSpec((4, D), const2),
        ],
        out_specs=[act] * 7,
        compiler_params=_cparams("arbitrary", "arbitrary"),
        name="rwkv_proj",
    )(x, x, mod2, g, mu, wrkv, w1, w2, a1, a2, g1, g2, vec)


def _rwkv_scan_kernel(r_ref, k_ref, v_ref, lw_ref, kk_ref, al_ref, pv_ref, y_ref, h_ref):
    C = SCAN_CHUNK
    P2 = 2 * C

    @pl.when(pl.program_id(1) == 0)
    def _():
        h_ref[...] = jnp.zeros_like(h_ref)

    lane = lax.broadcasted_iota(jnp.int32, (1, PAIR), 1)
    m_left = (lane < RWKV_HEAD).astype(F32)
    m_right = 1.0 - m_left
    ri = lax.broadcasted_iota(jnp.int32, (P2, P2), 0)
    ci = lax.broadcasted_iota(jnp.int32, (P2, P2), 1)
    same = (ri >= C) == (ci >= C)
    strict = same & (ri > ci)
    incl = same & (ri >= ci)
    eye = ri == ci
    block_ones = same.astype(BF16)
    tri = (lax.broadcasted_iota(jnp.int32, (C, C), 0) >= lax.broadcasted_iota(jnp.int32, (C, C), 1)).astype(BF16)

    def stack(x):
        return jnp.concatenate([x * m_left, x * m_right], axis=0)

    def head_sums(x):
        s_left = jnp.sum(x * m_left, axis=-1, keepdims=True)
        s_right = jnp.sum(x * m_right, axis=-1, keepdims=True)
        return jnp.where(lane < RWKV_HEAD, s_left, s_right)

    inv_n = 1.0 / RWKV_HEAD
    dd = functools.partial(jnp.dot, preferred_element_type=F32)
    n_pairs = RWKV_HEADS // 2
    units = [(ch, p) for ch in range(SCAN_CHUNKS_PER_STEP) for p in range(n_pairs)]
    idx = [(slice(ch * C, (ch + 1) * C), slice(p * PAIR, (p + 1) * PAIR)) for ch, p in units]
    U = range(len(units))
    ld = lambda ref, rs, sl: ref[0, rs, sl].astype(F32)
    kkr = [ld(kk_ref, rs, sl) for rs, sl in idx]
    ss = [head_sums(x * x) for x in kkr]
    lws = [lw_ref[0, rs, sl] for rs, sl in idx]
    Ls = []
    for lw in lws:
        l_hi, l_lo = _split(lw)
        cs = dd(tri, jnp.concatenate([l_hi, l_lo], axis=1))
        Ls.append(cs[:, :PAIR] + cs[:, PAIR:])
    lhs_g, rhs_g, bk_hat, vs, at32, rt32, dec_end = [], [], [], [], [], [], []
    for u in U:
        rs, sl = idx[u]
        L, lw = Ls[u], lws[u]
        kk = kkr[u] * lax.rsqrt(jnp.maximum(ss[u], 1e-24))
        b_vec = kk * ld(al_ref, rs, sl)
        k = ld(k_ref, rs, sl)
        LC = L[C - 1:C, :]
        e_neg = jnp.exp(-L)
        e_end = jnp.exp(LC - L)
        At = stack(-kk * jnp.exp(L - lw))
        Rt = stack(ld(r_ref, rs, sl) * jnp.exp(L))
        at32.append(At)
        rt32.append(Rt)
        lhs_g.append(jnp.concatenate([At, Rt], axis=0).astype(BF16))
        rhs_g.append(jnp.concatenate([stack(b_vec * e_neg), stack(k * e_neg)], axis=0).astype(BF16))
        bk_hat.append(jnp.concatenate([stack(b_vec * e_end), stack(k * e_end)], axis=0))
        vs.append(stack(ld(v_ref, rs, sl)).astype(BF16))
        dec_end.append(jnp.exp(LC))
    G = [lax.dot_general(lhs_g[u], rhs_g[u], (((1,), (1,)), ((), ())), preferred_element_type=F32) for u in U]
    A_ab = [jnp.where(strict, G[u][:P2, :P2], 0.0).astype(BF16) for u in U]
    A_ak = [jnp.where(strict, G[u][:P2, P2:], 0.0).astype(BF16) for u in U]
    A_r = [jnp.concatenate([jnp.where(incl, G[u][P2:, :P2], 0.0), jnp.where(incl, G[u][P2:, P2:], 0.0)],
                           axis=1).astype(BF16) for u in U]
    W = [dd(A_ak[u], vs[u]) for u in U]
    Z = [jnp.concatenate([at32[u], W[u]], axis=1) for u in U]
    Ai = A_ab
    n_dbl = int(math.log2(C))
    for s in range(n_dbl):
        Z = [Z[u] + dd(Ai[u], Z[u].astype(BF16)) for u in U]
        if s + 1 < n_dbl:
            Ai = [dd(Ai[u], Ai[u]).astype(BF16) for u in U]
    rhs = [jnp.concatenate([Z[u].astype(BF16), jnp.concatenate([jnp.zeros_like(vs[u]), vs[u]], axis=1)], axis=0)
           for u in U]
    o6 = [dd(A_r[u], rhs[u]) for u in U]
    o7 = [dd(bk_hat[u].T.astype(BF16), rhs[u]) for u in U]
    H = [h_ref[p] for p in range(n_pairs)]
    Y = [None] * len(units)
    for u in U:
        p = units[u][1]
        Hb = H[p].astype(BF16)
        Y[u] = dd((rt32[u] + o6[u][:, :PAIR]).astype(BF16), Hb) + o6[u][:, PAIR:]
        Mbd = o7[u][:, :PAIR] + jnp.where(eye, dec_end[u], 0.0)
        H[p] = dd(Mbd.astype(BF16), Hb) + o7[u][:, PAIR:]
    for p in range(n_pairs):
        h_ref[p] = H[p]
    ys = [Y[u][:C] + Y[u][C:] for u in U]
    rk = [ld(r_ref, rs, sl) * ld(k_ref, rs, sl) * pv_ref[0:1, sl] for rs, sl in idx]
    st1 = [head_sums(jnp.concatenate([ys[u], rk[u]], axis=0)) for u in U]
    yc = [ys[u] - st1[u][:C] * inv_n for u in U]
    var = [head_sums(yc[u] * yc[u]) * inv_n for u in U]
    for u in U:
        rs, sl = idx[u]
        bonus = st1[u][C:] * ld(v_ref, rs, sl)
        y_ref[0, rs, sl] = (yc[u] * lax.rsqrt(var[u] + RWKV_GN_EPS) * pv_ref[1:2, sl] + pv_ref[2:3, sl]
                            + bonus).astype(y_ref.dtype)


def _rwkv_scan(r, k, v, lw, kk, al, pvec):
    B, T, D = r.shape
    rows = SCAN_CHUNK * SCAN_CHUNKS_PER_STEP
    act = pl.BlockSpec((1, rows, D), lambda b, c: (b, c, 0))
    return pl.pallas_call(
        _rwkv_scan_kernel,
        out_shape=jax.ShapeDtypeStruct((B, T, D), BF16),
        grid=(B, T // rows),
        in_specs=[act] * 6 + [pl.BlockSpec((3, D), lambda b, c: (0, 0))],
        out_specs=act,
        scratch_shapes=[pltpu.VMEM((RWKV_HEADS // 2, PAIR, PAIR), F32)],
        compiler_params=_cparams("arbitrary", "arbitrary"),
        name="rwkv_scan",
    )(r, k, v, lw, kk, al, pvec)


def _proj_res_kernel(*refs, has_gate):
    if has_gate:
        y_ref, g_ref, x_ref, gm_ref, w_ref, o_ref = refs
        y = y_ref[0].astype(F32) * g_ref[0].astype(F32)
    else:
        y_ref, x_ref, gm_ref, w_ref, o_ref = refs
        y = y_ref[0]
    o_ref[0] = x_ref[0] + gm_ref[0] * jnp.dot(y.astype(BF16), w_ref[...], preferred_element_type=F32)


def _proj_res(y, g, x, gm, w):
    B, T, D = x.shape
    tm = DENSE_TM
    act = pl.BlockSpec((1, tm, D), lambda b, i: (b, i, 0))
    ins = [y] + ([g] if g is not None else []) + [x, gm, w]
    specs = [act] * (len(ins) - 2) + [pl.BlockSpec((1, 1, D), lambda b, i: (b, 0, 0)),
                                     pl.BlockSpec((D, D), lambda b, i: (0, 0))]
    return pl.pallas_call(
        functools.partial(_proj_res_kernel, has_gate=g is not None),
        out_shape=jax.ShapeDtypeStruct((B, T, D), F32),
        grid=(B, T // tm),
        in_specs=specs,
        out_specs=act,
        compiler_params=_cparams("arbitrary", "arbitrary"),
        name="proj_res",
    )(*ins)


def _norm_mm_kernel(*refs, n_proj):
    x_ref = refs[0]
    ins, outs = refs[1:1 + 3 * n_proj], refs[1 + 3 * n_proj:]
    x = x_ref[0]
    xn = x * lax.rsqrt(jnp.mean(x * x, axis=-1, keepdims=True) + RMS_EPS)
    for p in range(n_proj):
        mod_ref, g_ref, w_ref = ins[3 * p:3 * p + 3]
        h = (xn * g_ref[...]) * (1.0 + mod_ref[0, 1:2, :]) + mod_ref[0, 0:1, :]
        outs[p][0] = jnp.dot(h.astype(BF16), w_ref[...], preferred_element_type=F32).astype(outs[p].dtype)


def _norm_mm(x, projs, out_dtype):
    B, T, D = x.shape
    tm = DENSE_TM
    in_specs = [pl.BlockSpec((1, tm, D), lambda b, i: (b, i, 0))]
    args = [x]
    for mod2, g, w in projs:
        in_specs += [pl.BlockSpec((1, 2, D), lambda b, i: (b, 0, 0)), pl.BlockSpec((1, D), lambda b, i: (0, 0)),
                     pl.BlockSpec(w.shape, lambda b, i: (0, 0))]
        args += [mod2, g, w]
    return pl.pallas_call(
        functools.partial(_norm_mm_kernel, n_proj=len(projs)),
        out_shape=[jax.ShapeDtypeStruct((B, T, w.shape[1]), out_dtype) for _, _, w in projs],
        grid=(B, T // tm),
        in_specs=in_specs,
        out_specs=[pl.BlockSpec((1, tm, w.shape[1]), lambda b, i: (b, i, 0)) for _, _, w in projs],
        compiler_params=_cparams("arbitrary", "arbitrary"),
        name="norm_mm",
    )(*args)


def _diff_attn_kernel(q_ref, k_ref, v_ref, lam_ref, sg_ref, o_ref, m_ref, acc_ref, s_ref, *, lambda_init):
    tq, HB, dv = ATT_TQ, ATT_HB, DIFF_V_DIM
    qi = pl.program_id(2)
    heads = range(HB)
    hs = [slice(h * dv, (h + 1) * dv) for h in heads]
    lane = lax.broadcasted_iota(jnp.int32, (1, dv), 1)
    m_left = (lane < DIFF_QK_DIM).astype(F32)
    qs = []
    for h in heads:
        q = q_ref[0, :, hs[h]].astype(F32) * (DIFF_QK_DIM ** -0.5)
        qs.append(jnp.concatenate([q * m_left, q * (1.0 - m_left)], axis=0).astype(BF16))
    ones_col = jnp.ones((tq, dv), BF16)
    causal = (lax.broadcasted_iota(jnp.int32, (2 * tq, tq), 1)
              <= lax.broadcasted_iota(jnp.int32, (2 * tq, tq), 0) % tq)

    def key_rows(j):
        return pl.ds(pl.multiple_of(j * tq, tq), tq)

    def scores_into(slot, j):
        for h in heads:
            s_ref[slot, h] = lax.dot_general(qs[h], k_ref[0, key_rows(j), hs[h]], (((1,), (1,)), ((), ())),
                                             preferred_element_type=F32)

    def block(slot, j, first, prefetch):
        if prefetch is not None:
            scores_into(1 - slot, prefetch)
        for h in heads:
            s = s_ref[slot, h]
            if first:
                s = jnp.where(causal, s, -jnp.inf)
                m_new = jnp.broadcast_to(jnp.max(s, axis=-1, keepdims=True), (2 * tq, LANES))
            else:
                m_old = m_ref[h]
                m_new = jnp.maximum(m_old, jnp.max(s, axis=-1, keepdims=True))
            m_ref[h] = m_new
            p = jnp.exp(s - jnp.concatenate([m_new] * (tq // LANES), axis=1)).astype(BF16)
            pv = jnp.dot(p, jnp.concatenate([v_ref[0, key_rows(j), hs[h]], ones_col], axis=1),
                         preferred_element_type=F32)
            if first:
                acc_ref[h] = pv
            else:
                alpha = jnp.exp(m_old - m_new)
                acc_ref[h] = acc_ref[h] * jnp.concatenate([alpha, alpha], axis=1) + pv

    scores_into(0, qi)
    block(0, qi, True, 0)

    def pair(u, c):
        last = jnp.maximum(qi - 1, 0)
        block(1, 2 * u, False, jnp.minimum(2 * u + 1, last))
        block(0, 2 * u + 1, False, jnp.minimum(2 * u + 2, last))
        return c

    lax.fori_loop(0, qi // 2, pair, 0)

    @pl.when(qi % 2 == 1)
    def _():
        block(1, qi - 1, False, None)

    lv = lam_ref[...]
    lam = (jnp.exp(jnp.sum(lv[0:1] * lv[1:2], axis=-1, keepdims=True))
           - jnp.exp(jnp.sum(lv[2:3] * lv[3:4], axis=-1, keepdims=True)) + lambda_init)
    for h in heads:
        acc = acc_ref[h]
        o = acc[:, :dv] / acc[:, dv:dv + 1]
        o = o[:tq] - lam * o[tq:]
        ms = jnp.mean(o * o, axis=-1, keepdims=True)
        o_ref[0, :, hs[h]] = (o * lax.rsqrt(ms + SUBLN_EPS) * sg_ref[...] * (1.0 - lambda_init)).astype(o_ref.dtype)


def _diff_attn(q, kv, lam_vecs, subln_g, lambda_init):
    B, T, D = q.shape
    tq, HB = ATT_TQ, ATT_HB
    n_hb = DIFF_HEADS // HB
    w = HB * DIFF_V_DIM
    return pl.pallas_call(
        functools.partial(_diff_attn_kernel, lambda_init=lambda_init),
        out_shape=jax.ShapeDtypeStruct((B, T, D), BF16),
        grid=(B, n_hb, T // tq),
        in_specs=[
            pl.BlockSpec((1, tq, w), lambda b, h, i: (b, i, h)),
            pl.BlockSpec((1, T, w), lambda b, h, i: (b, 0, h)),
            pl.BlockSpec((1, T, w), lambda b, h, i: (b, 0, n_hb + h)),
            pl.BlockSpec((4, DIFF_QK_DIM), lambda b, h, i: (0, 0)),
            pl.BlockSpec((1, DIFF_V_DIM), lambda b, h, i: (0, 0)),
        ],
        out_specs=pl.BlockSpec((1, tq, w), lambda b, h, i: (b, i, h)),
        scratch_shapes=[pltpu.VMEM((HB, 2 * tq, LANES), F32), pltpu.VMEM((HB, 2 * tq, 2 * DIFF_V_DIM), F32),
                        pltpu.VMEM((2, HB, 2 * tq, tq), F32)],
        compiler_params=_cparams("arbitrary", "arbitrary", "arbitrary"),
        name="diff_attn",
    )(q, kv, kv, lam_vecs, subln_g)


def _router_kernel(x_ref, mod_ref, g_ref, w_ref, info_ref, cnt_ref):
    h = _rms_mod(x_ref[...], g_ref[...], mod_ref[0, 0:1, :], mod_ref[0, 1:2, :])
    logit = _dot3(h, w_ref[...])
    lane_i = lax.broadcasted_iota(jnp.int32, logit.shape, 1)
    lane = lane_i.astype(F32)
    neg = -jnp.inf
    big = float(LANES)
    is_grp = lane_i < N_GROUPS
    gl = jnp.where(is_grp, logit, neg)
    gmax = jnp.max(gl, axis=-1, keepdims=True)
    gidx = jnp.min(jnp.where(gl == gmax, lane, big), axis=-1, keepdims=True)
    grp_gate = 1.0 / jnp.sum(jnp.where(is_grp, jnp.exp(logit - gmax), 0.0), axis=-1, keepdims=True)
    lo = N_GROUPS + gidx * EXPERTS_PER_GROUP
    in_grp = (lane >= lo) & (lane < lo + EXPERTS_PER_GROUP)
    el = jnp.where(in_grp, logit, neg)
    t1 = jnp.max(el, axis=-1, keepdims=True)
    i1 = jnp.min(jnp.where(el == t1, lane, big), axis=-1, keepdims=True)
    el2 = jnp.where(lane == i1, neg, el)
    t2 = jnp.max(el2, axis=-1, keepdims=True)
    i2 = jnp.min(jnp.where(el2 == t2, lane, big), axis=-1, keepdims=True)
    e21 = jnp.exp(t2 - t1)
    p1 = 1.0 / (1.0 + e21)
    w1 = grp_gate * p1
    w2 = grp_gate * (e21 * p1)
    e1 = i1 - N_GROUPS
    e2 = i2 - N_GROUPS
    info_ref[...] = jnp.where(lane_i == 0, e1, jnp.where(lane_i == 1, e2, jnp.where(lane_i == 2, w1, jnp.where(lane_i == 3, w2, 0.0))))

    @pl.when(pl.program_id(0) == 0)
    def _():
        cnt_ref[...] = jnp.zeros_like(cnt_ref)

    picked = ((lane == e1) | (lane == e2)).astype(F32)
    cnt_ref[...] += jnp.broadcast_to(jnp.sum(picked, axis=0, keepdims=True), cnt_ref.shape)


def _router(x2, mod2, g, w_cat, T):
    n_tok, D = x2.shape
    tm = ROUTE_TM
    per_b = T // tm
    return pl.pallas_call(
        _router_kernel,
        out_shape=[jax.ShapeDtypeStruct((n_tok, LANES), F32), jax.ShapeDtypeStruct((SUBLANES, LANES), F32)],
        grid=(n_tok // tm,),
        in_specs=[
            pl.BlockSpec((tm, D), lambda i: (i, 0)),
            pl.BlockSpec((1, 2, D), lambda i: (i // per_b, 0, 0)),
            pl.BlockSpec((1, D), lambda i: (0, 0)),
            pl.BlockSpec((D, LANES), lambda i: (0, 0)),
        ],
        out_specs=[pl.BlockSpec((tm, LANES), lambda i: (i, 0)), pl.BlockSpec((SUBLANES, LANES), lambda i: (0, 0))],
        compiler_params=_cparams("arbitrary"),
        name="moe_router",
    )(x2, mod2, g, w_cat)


def _rank_kernel(info_ref, cnt_ref, dest_ref, meta_ref, zero_ref, start_ref):
    i = pl.program_id(0)
    zero_ref[...] = jnp.zeros_like(zero_ref)
    tb = info_ref.shape[0]
    lane = lax.broadcasted_iota(jnp.int32, (tb, LANES), 1)
    info = info_ref[...]
    e0 = info[:, 0:1].astype(jnp.int32)
    e1 = info[:, 1:2].astype(jnp.int32)
    o0 = (lane == e0).astype(F32)
    o1 = (lane == e1).astype(F32)
    both = o0 + o1

    @pl.when(i == 0)
    def _():
        cnt = cnt_ref[0:1, :]
        padded = jnp.floor((cnt + (MOE_TM - 1)) * (1.0 / MOE_TM)) * MOE_TM
        r = lax.broadcasted_iota(jnp.int32, (LANES, LANES), 0)
        c = lax.broadcasted_iota(jnp.int32, (LANES, LANES), 1)
        upper_strict = (r < c).astype(BF16)
        start = _dot_hl(jnp.broadcast_to(padded, (SUBLANES, LANES)), upper_strict)[0:1]
        start_ref[...] = start
        meta_ref[...] = jnp.broadcast_to(start + padded, (SUBLANES, LANES))

    r = lax.broadcasted_iota(jnp.int32, (tb, tb), 0)
    c = lax.broadcasted_iota(jnp.int32, (tb, tb), 1)
    lower_strict = (r > c).astype(BF16)
    before = jnp.dot(lower_strict, both.astype(BF16), preferred_element_type=F32) + start_ref[...]
    d0 = jnp.sum(o0 * before, axis=-1, keepdims=True)
    d1 = jnp.sum(o1 * before, axis=-1, keepdims=True)
    dest_ref[...] = jnp.where(lane == 0, d0, jnp.where(lane == 1, d1, 0.0)).astype(jnp.int32)
    start_ref[...] += jnp.sum(both, axis=0, keepdims=True)


def _rank(info, cnt, n_rows):
    n_tok = info.shape[0]
    tb = RANK_TB
    n_blk = n_tok // tb
    zrows = n_rows * SUBLANES // n_blk
    assert zrows * n_blk == n_rows * SUBLANES and zrows % SUBLANES == 0
    return pl.pallas_call(
        _rank_kernel,
        out_shape=[jax.ShapeDtypeStruct((n_tok, LANES), jnp.int32), jax.ShapeDtypeStruct((SUBLANES, LANES), F32),
                   jax.ShapeDtypeStruct((n_rows * SUBLANES, LANES), F32)],
        grid=(n_blk,),
        in_specs=[pl.BlockSpec((tb, LANES), lambda i: (i, 0)), pl.BlockSpec((SUBLANES, LANES), lambda i: (0, 0))],
        out_specs=[pl.BlockSpec((tb, LANES), lambda i: (i, 0)), pl.BlockSpec((SUBLANES, LANES), lambda i: (0, 0)),
                   pl.BlockSpec((zrows, LANES), lambda i: (i, 0))],
        scratch_shapes=[pltpu.VMEM((1, LANES), F32)],
        compiler_params=_cparams("arbitrary"),
        name="moe_rank",
    )(info, cnt)


assert D_MODEL == SUBLANES * LANES


def _tile_rows_store(ref, x):
    n = x.shape[0]
    for s in range(SUBLANES):
        ref[pl.ds(s, n, stride=SUBLANES), :] = x[:, s * LANES:(s + 1) * LANES]


def _tile_rows_load(ref, n):
    return jnp.concatenate([ref[pl.ds(s, n, stride=SUBLANES), :] for s in range(SUBLANES)], axis=1)


def _row_copy(src_ref, s, dst_ref, d, sem):
    rows = lambda r: pl.ds(pl.multiple_of(r * SUBLANES, SUBLANES), SUBLANES)
    return pltpu.make_async_copy(src_ref.at[rows(s)], dst_ref.at[rows(d)], sem)


def _dispatch_kernel(dest_ref, x_ref, mod_ref, g_ref, xs_in_ref, xs_ref, h_ref, sem):
    del xs_in_ref
    tb = x_ref.shape[0]
    i = pl.program_id(0)
    last = pl.num_programs(0) - 1
    h = _rms_mod(x_ref[...], g_ref[...], mod_ref[0, 0:1, :], mod_ref[0, 1:2, :])

    def drain(slot):
        def wait(j, c):
            _row_copy(h_ref.at[slot], 0, xs_ref, 0, sem.at[slot]).wait()
            return c
        lax.fori_loop(0, 2 * tb, wait, 0, unroll=8)

    for slot in range(2):
        @pl.when(i % 2 == slot)
        def _(slot=slot):
            @pl.when(i >= 2)
            def _():
                drain(slot)

            _tile_rows_store(h_ref.at[slot], h)

            def start(j, c):
                _row_copy(h_ref.at[slot], j, xs_ref, dest_ref[0, 0, 2 * j], sem.at[slot]).start(priority=0)
                _row_copy(h_ref.at[slot], j, xs_ref, dest_ref[0, 0, 2 * j + 1], sem.at[slot]).start(priority=1)
                return c

            lax.fori_loop(0, tb, start, 0, unroll=8)

            @pl.when(i == last)
            def _():
                drain(slot)

                @pl.when(i >= 1)
                def _():
                    drain(1 - slot)


def _dispatch(dest3, x2, mod2, g, xs_zero, T):
    n_tok, D = x2.shape
    tb = ROW_TB
    per_b = T // tb
    return pl.pallas_call(
        _dispatch_kernel,
        out_shape=jax.ShapeDtypeStruct(xs_zero.shape, F32),
        grid=(n_tok // tb,),
        in_specs=[
            pl.BlockSpec((1, 1, 2 * tb), lambda i: (i, 0, 0), memory_space=pltpu.SMEM),
            pl.BlockSpec((tb, D), lambda i: (i, 0)),
            pl.BlockSpec((1, 2, D), lambda i: (i // per_b, 0, 0)),
            pl.BlockSpec((1, D), lambda i: (0, 0)),
            pl.BlockSpec(memory_space=pl.ANY),
        ],
        out_specs=pl.BlockSpec(memory_space=pl.ANY),
        scratch_shapes=[pltpu.VMEM((2, tb * SUBLANES, LANES), F32), pltpu.SemaphoreType.DMA((2,))],
        input_output_aliases={4: 0},
        compiler_params=_cparams("arbitrary"),
        name="moe_dispatch",
    )(dest3, x2, mod2, g, xs_zero)


def _expert_kernel(be_ref, nb_ref, nxt_ref, xs_ref, wg_hbm, wu_hbm, wd_hbm, ys_ref, stage_g, stage_u, stage_d,
                   wgb, wub, wdb, sem, *, layer):
    i = pl.program_id(0)
    e = be_ref[i]
    changed = (i == 0) | (e != be_ref[jnp.maximum(i - 1, 0)])

    def fetch(ex):
        return (pltpu.make_async_copy(wg_hbm.at[layer, ex], stage_g, sem.at[0]),
                pltpu.make_async_copy(wu_hbm.at[layer, ex], stage_u, sem.at[1]),
                pltpu.make_async_copy(wd_hbm.at[layer, ex], stage_d, sem.at[2]))

    @pl.when(i == 0)
    def _():
        for cp in fetch(e):
            cp.start()

    @pl.when(changed)
    def _():
        for cp in fetch(e):
            cp.wait()
        wgb[...] = stage_g[...].astype(BF16)
        wub[...] = stage_u[...].astype(BF16)
        wdb[...] = stage_d[...].astype(BF16)

        @pl.when(nxt_ref[i] >= 0)
        def _():
            for cp in fetch(nxt_ref[i]):
                cp.start()

    @pl.when(i < nb_ref[0])
    def _():
        x = _tile_rows_load(xs_ref, MOE_TM).astype(BF16)
        a = jnp.dot(x, wgb[...], preferred_element_type=F32)
        u = jnp.dot(x, wub[...], preferred_element_type=F32)
        hdn = (a * jax.nn.sigmoid(a)) * u
        _tile_rows_store(ys_ref, jnp.dot(hdn.astype(BF16), wdb[...], preferred_element_type=F32))

    @pl.when(i >= nb_ref[0])
    def _():
        ys_ref[...] = jnp.zeros_like(ys_ref)


def _experts(blk_e, n_used, next_e, xs, w_gate, w_up, w_down, layer):
    D = D_MODEL
    n_rows = xs.shape[0] // SUBLANES
    tm = MOE_TM
    FF = EXPERT_FF
    row_block = pl.BlockSpec((tm * SUBLANES, LANES), lambda i, be, nb, nx: (i, 0))
    hbm = pl.BlockSpec(memory_space=pl.ANY)
    grid_spec = pltpu.PrefetchScalarGridSpec(
        num_scalar_prefetch=3,
        grid=(n_rows // tm,),
        in_specs=[row_block, hbm, hbm, hbm],
        out_specs=row_block,
        scratch_shapes=[pltpu.VMEM((D, FF), F32), pltpu.VMEM((D, FF), F32), pltpu.VMEM((FF, D), F32),
                        pltpu.VMEM((D, FF), BF16), pltpu.VMEM((D, FF), BF16), pltpu.VMEM((FF, D), BF16),
                        pltpu.SemaphoreType.DMA((3,))],
    )
    return pl.pallas_call(
        functools.partial(_expert_kernel, layer=layer),
        out_shape=jax.ShapeDtypeStruct(xs.shape, F32),
        grid_spec=grid_spec,
        compiler_params=_cparams("arbitrary"),
        name="moe_experts",
    )(blk_e, n_used, next_e, xs, w_gate, w_up, w_down)


def _combine_kernel(dest_ref, dest_next_ref, info_ref, x_ref, gf_ref, fg_ref, ys_ref, o_ref, y_ref, sem, *,
                    final_norm):
    tb = x_ref.shape[0]
    i = pl.program_id(0)
    last = pl.num_programs(0) - 1

    def gather(d_ref, slot):
        def start(j, c):
            _row_copy(ys_ref, d_ref[0, 0, 2 * j], y_ref.at[slot, 0], j, sem.at[slot]).start(priority=0)
            _row_copy(ys_ref, d_ref[0, 0, 2 * j + 1], y_ref.at[slot, 1], j, sem.at[slot]).start(priority=1)
            return c
        lax.fori_loop(0, tb, start, 0, unroll=8)

    @pl.when(i == 0)
    def _():
        gather(dest_ref, 0)

    for slot in range(2):
        @pl.when(i % 2 == slot)
        def _(slot=slot):
            @pl.when(i < last)
            def _():
                gather(dest_next_ref, 1 - slot)

            def wait(j, c):
                _row_copy(ys_ref, 0, y_ref.at[slot, 0], 0, sem.at[slot]).wait()
                return c

            lax.fori_loop(0, 2 * tb, wait, 0, unroll=8)
            info = info_ref[...]
            moe = (info[:, 2:3] * _tile_rows_load(y_ref.at[slot, 0], tb)
                   + info[:, 3:4] * _tile_rows_load(y_ref.at[slot, 1], tb))
            out = x_ref[...] + gf_ref[0] * moe
            if final_norm:
                ms = jnp.mean(out * out, axis=-1, keepdims=True)
                out = out * lax.rsqrt(ms + RMS_EPS) * fg_ref[...]
            o_ref[...] = out


def _combine(dest3, info, x2, gf, final_g, ys, T, final_norm):
    n_tok, D = x2.shape
    tb = ROW_TB
    per_b = T // tb
    n_steps = n_tok // tb
    return pl.pallas_call(
        functools.partial(_combine_kernel, final_norm=final_norm),
        out_shape=jax.ShapeDtypeStruct((n_tok, D), F32),
        grid=(n_steps,),
        in_specs=[
            pl.BlockSpec((1, 1, 2 * tb), lambda i: (i, 0, 0), memory_space=pltpu.SMEM),
            pl.BlockSpec((1, 1, 2 * tb), lambda i: (jnp.minimum(i + 1, n_steps - 1), 0, 0), memory_space=pltpu.SMEM),
            pl.BlockSpec((tb, LANES), lambda i: (i, 0)),
            pl.BlockSpec((tb, D), lambda i: (i, 0)),
            pl.BlockSpec((1, 1, D), lambda i: (i // per_b, 0, 0)),
            pl.BlockSpec((1, D), lambda i: (0, 0)),
            pl.BlockSpec(memory_space=pl.ANY),
        ],
        out_specs=pl.BlockSpec((tb, D), lambda i: (i, 0)),
        scratch_shapes=[pltpu.VMEM((2, 2, tb * SUBLANES, LANES), F32), pltpu.SemaphoreType.DMA((2,))],
        compiler_params=_cparams("arbitrary"),
        name="moe_combine",
    )(dest3, dest3, info, x2, gf, final_g, ys)


def _moe_layer(x, mod_f, gf, norm_g, w_rg, w_re, w_gate, w_up, w_down, layer, final_g, final_norm):
    B, T, D = x.shape
    n_tok = B * T
    x2 = x.reshape(n_tok, D)
    w_cat = jnp.concatenate([w_rg, w_re, jnp.zeros((D, LANES - N_GROUPS - N_EXPERTS), F32)], axis=1)
    info, cnt = _router(x2, mod_f, norm_g, w_cat, T)
    n_rows = -(-(2 * n_tok + N_EXPERTS * MOE_TM) // MOE_TM) * MOE_TM
    dest, meta, xs_zero = _rank(info, cnt, n_rows)
    pad_end = meta[0, :N_EXPERTS].astype(jnp.int32)
    n_blocks = n_rows // MOE_TM
    blk_start = jnp.arange(n_blocks, dtype=jnp.int32) * MOE_TM
    blk_e = jnp.minimum(jnp.sum(pad_end[None, :] <= blk_start[:, None], axis=1), N_EXPERTS - 1).astype(jnp.int32)
    n_used = (pad_end[N_EXPERTS - 1:] // MOE_TM).astype(jnp.int32)
    dest3 = dest[:, :2].reshape(n_tok // ROW_TB, 1, 2 * ROW_TB)
    xs = _dispatch(dest3, x2, mod_f, norm_g, xs_zero, T)
    seg_end = jnp.sum(blk_e[None, :] <= blk_e[:, None], axis=1)
    next_e = jnp.where(seg_end < n_blocks, blk_e[jnp.minimum(seg_end, n_blocks - 1)], -1).astype(jnp.int32)
    ys = _experts(blk_e, n_used, next_e, xs, w_gate, w_up, w_down, layer)
    out = _combine(dest3, info, x2, gf, final_g, ys, T, final_norm)
    return out.reshape(B, T, D)


def kernel(x, c, ada_w, ada_b, norm_mix_g, norm_ffn_g, rw_mu, rw_w_rkv, rw_w0, rw_w1, rw_w2, rw_a0, rw_a1, rw_a2,
           rw_g1, rw_g2, rw_k_k, rw_k_a, rw_r_k, rw_gn_g, rw_gn_b, rw_w_o, ada_kv_w, ada_kv_b, norm_kv_g, w_kv,
           df_w_q, df_lq1, df_lk1, df_lq2, df_lk2, df_subln_g, df_w_o, moe_w_rg, moe_w_re, moe_w_gate, moe_w_up,
           moe_w_down, final_g):
    B, T, D = x.shape
    c_pad = jnp.zeros((SUBLANES, D), F32).at[:B].set(c)
    mod = _ada(c_pad, ada_w, ada_b, 6 * D // 4)[:, :B]
    mod_kv = _ada(c_pad, ada_kv_w[None], ada_kv_b[None], D)[0, :B]
    bf = lambda w: w.astype(BF16)
    row = lambda v: v.reshape(1, -1)

    for l in range(DEPTH):
        sh_m, sc_m, g_m, sh_f, sc_f, g_f = jnp.split(mod[l], 6, axis=-1)
        mod_m = jnp.stack([sh_m, sc_m], axis=1)
        mod_f = jnp.stack([sh_f, sc_f], axis=1)
        if l < N_A_LAYERS:
            i = l
            vec = jnp.stack([rw_w0[i], rw_a0[i], rw_k_k[i], rw_k_a[i]], axis=0)
            r, k, v, lw, kk, al, gate = _rwkv_proj(
                x, mod_m, row(norm_mix_g[l]), rw_mu[i], bf(rw_w_rkv[i]), bf(rw_w1[i]), bf(rw_w2[i]),
                bf(rw_a1[i]), bf(rw_a2[i]), bf(rw_g1[i]), bf(rw_g2[i]), vec)
            pvec = jnp.stack([rw_r_k[i].reshape(-1), rw_gn_g[i], rw_gn_b[i]], axis=0)
            y = _rwkv_scan(r, k, v, lw, kk, al, pvec)
            x = _proj_res(y, gate, x, g_m[:, None, :], bf(rw_w_o[i]))
        else:
            j = l - N_A_LAYERS
            q_proj = (mod_m, row(norm_mix_g[l]), bf(df_w_q[j]))
            if l == N_A_LAYERS:
                sh_kv, sc_kv = jnp.split(mod_kv, 2, axis=-1)
                q, kv = _norm_mm(x, [q_proj, (jnp.stack([sh_kv, sc_kv], axis=1), row(norm_kv_g), bf(w_kv))], BF16)
            else:
                q, = _norm_mm(x, [q_proj], BF16)
            lambda_init = 0.8 - 0.6 * math.exp(-0.3 * l)
            lam_vecs = jnp.stack([df_lq1[j], df_lk1[j], df_lq2[j], df_lk2[j]], axis=0)
            o = _diff_attn(q, kv, lam_vecs, row(df_subln_g[j]), lambda_init)
            x = _proj_res(o, None, x, g_m[:, None, :], bf(df_w_o[j]))
        x = _moe_layer(x, mod_f, g_f[:, None, :], row(norm_ffn_g[l]), moe_w_rg[l], moe_w_re[l], moe_w_gate,
                       moe_w_up, moe_w_down, l, row(final_g), final_norm=(l == DEPTH - 1))
    return x
```

```python
import functools
import math

import jax
import jax.numpy as jnp
from jax import lax
from jax.experimental import pallas as pl
from jax.experimental.pallas import tpu as pltpu

F32 = jnp.float32
BF16 = jnp.bfloat16

D_MODEL = 1024
DEPTH = 2
N_A_LAYERS = DEPTH // 2
RWKV_HEAD = 64
RWKV_HEADS = D_MODEL // RWKV_HEAD
RWKV_GN_EPS = 64e-5
DIFF_QK_DIM = 64
DIFF_V_DIM = 2 * DIFF_QK_DIM
DIFF_HEADS = D_MODEL // DIFF_V_DIM
SUBLN_EPS = 1e-5
N_GROUPS = 4
EXPERTS_PER_GROUP = 8
N_EXPERTS = N_GROUPS * EXPERTS_PER_GROUP
EXPERT_FF = 512
RMS_EPS = 1e-6

LANES = 128
SUBLANES = 8
VMEM_LIMIT_BYTES = 56 * 1024 * 1024

SCAN_CHUNK = 64
SCAN_BASE_BLOCK = 8
SCAN_CHUNKS_PER_STEP = 2
PAIR = 2 * RWKV_HEAD
PROJ_TM = 256
DENSE_TM = 512
ATT_TQ = 512
ATT_HB = 2
MOE_TM = 256
ROUTE_TM = 512
RANK_TB = 512
ROW_TB = 256


def _cparams(*sem):
    return pltpu.CompilerParams(dimension_semantics=sem, vmem_limit_bytes=VMEM_LIMIT_BYTES)


def _dot(a, b):
    return jnp.dot(a.astype(BF16), b.astype(BF16), preferred_element_type=F32)


def _dot_nt(a, b):
    return lax.dot_general(a.astype(BF16), b.astype(BF16), (((1,), (1,)), ((), ())), preferred_element_type=F32)


def _split(x):
    hi = x.astype(BF16)
    lo = (x - hi.astype(F32)).astype(BF16)
    return hi, lo


def _dot3(a, b):
    ah, al = _split(a)
    bh, bl = _split(b)
    d = functools.partial(jnp.dot, preferred_element_type=F32)
    return d(ah, bh) + d(ah, bl) + d(al, bh)


def _dot_hl(a, b_exact):
    ah, al = _split(a)
    d = functools.partial(jnp.dot, preferred_element_type=F32)
    return d(ah, b_exact) + d(al, b_exact)


def _rms_mod(x, g, shift, scale):
    ms = jnp.mean(x * x, axis=-1, keepdims=True)
    return (x * lax.rsqrt(ms + RMS_EPS) * g) * (1.0 + scale) + shift


def _ada_kernel(c_ref, w_ref, b_ref, o_ref):
    c = c_ref[...]
    ca = c * jax.nn.sigmoid(c)
    o_ref[...] = _dot3(ca, w_ref[...]) + b_ref[...]


def _ada(c_pad, w, b, tn):
    L, D, N = w.shape
    return pl.pallas_call(
        _ada_kernel,
        out_shape=jax.ShapeDtypeStruct((L, SUBLANES, N), F32),
        grid=(L, N // tn),
        in_specs=[
            pl.BlockSpec((SUBLANES, D), lambda l, j: (0, 0)),
            pl.BlockSpec((None, D, tn), lambda l, j: (l, 0, j)),
            pl.BlockSpec((None, 1, tn), lambda l, j: (l, 0, j)),
        ],
        out_specs=pl.BlockSpec((None, SUBLANES, tn), lambda l, j: (l, 0, j)),
        compiler_params=_cparams("arbitrary", "arbitrary"),
        name="ada_mod",
    )(c_pad, w, b.reshape(L, 1, N))


def _rwkv_proj_kernel(x_ref, xp_ref, mod_ref, g_ref, mu_ref, wrkv_ref, w1_ref, w2_ref, a1_ref, a2_ref,
                      g1_ref, g2_ref, vec_ref, r_ref, k_ref, v_ref, lw_ref, kk_ref, al_ref, gate_ref):
    i = pl.program_id(1)
    g = g_ref[...]
    shift, scale = mod_ref[0, 0:1, :], mod_ref[0, 1:2, :]
    h = _rms_mod(x_ref[0], g, shift, scale)
    hp = _rms_mod(xp_ref[0, SUBLANES - 1:SUBLANES, :], g, shift, scale)
    hp = jnp.where(i == 0, 0.0, hp)
    row = lax.broadcasted_iota(jnp.int32, h.shape, 0)
    h_prev = jnp.where(row == 0, hp, pltpu.roll(h, 1, axis=0))
    xx = h_prev - h
    mu = mu_ref[...]
    xs = [(h + xx * mu[j:j + 1, :]).astype(BF16) for j in range(6)]
    w0, a0, k_k, k_a = (vec_ref[j:j + 1, :] for j in range(4))
    d = functools.partial(jnp.dot, preferred_element_type=F32)
    r = d(xs[0], wrkv_ref[0])
    k = d(xs[1], wrkv_ref[1])
    v = d(xs[2], wrkv_ref[2])
    z = w0 + _dot(jnp.tanh(d(xs[3], w1_ref[...])), w2_ref[...])
    lw = (-math.exp(-0.5)) * jax.nn.sigmoid(z)
    a = jax.nn.sigmoid(a0 + _dot(d(xs[4], a1_ref[...]), a2_ref[...]))
    gate = _dot(jax.nn.sigmoid(d(xs[5], g1_ref[...])), g2_ref[...])
    r_ref[0] = r.astype(BF16)
    k_ref[0] = (k * (1.0 + (a - 1.0) * k_a)).astype(BF16)
    v_ref[0] = v.astype(BF16)
    lw_ref[0] = lw
    kk_ref[0] = (k * k_k).astype(BF16)
    al_ref[0] = a.astype(BF16)
    gate_ref[0] = gate.astype(BF16)


def _rwkv_proj(x, mod2, g, mu, wrkv, w1, w2, a1, a2, g1, g2, vec):
    B, T, D = x.shape
    tm = PROJ_TM
    const2 = lambda b, i: (0, 0)
    const3 = lambda b, i: (0, 0, 0)
    act = pl.BlockSpec((1, tm, D), lambda b, i: (b, i, 0))
    n_sub = tm // SUBLANES
    return pl.pallas_call(
        _rwkv_proj_kernel,
        out_shape=[jax.ShapeDtypeStruct((B, T, D), F32 if n == 3 else BF16) for n in range(7)],
        grid=(B, T // tm),
        in_specs=[
            act,
            pl.BlockSpec((1, SUBLANES, D), lambda b, i: (b, jnp.maximum(i * n_sub - 1, 0), 0)),
            pl.BlockSpec((1, 2, D), lambda b, i: (b, 0, 0)),
            pl.BlockSpec((1, D), const2),
            pl.BlockSpec((6, D), const2),
            pl.BlockSpec((3, D, D), const3),
            pl.BlockSpec(w1.shape, const2), pl.BlockSpec(w2.shape, const2),
            pl.BlockSpec(a1.shape, const2), pl.BlockSpec(a2.shape, const2),
            pl.BlockSpec(g1.shape, const2), pl.BlockSpec(g2.shape, const2),
            pl.BlockSpec((4, D), const2),
        ],
        out_specs=[act] * 7,
        compiler_params=_cparams("arbitrary", "arbitrary"),
        name="rwkv_proj",
    )(x, x, mod2, g, mu, wrkv, w1, w2, a1, a2, g1, g2, vec)


def _rwkv_scan_kernel(r_ref, k_ref, v_ref, lw_ref, kk_ref, al_ref, pv_ref, y_ref, h_ref):
    C = SCAN_CHUNK
    P2 = 2 * C

    @pl.when(pl.program_id(1) == 0)
    def _():
        h_ref[...] = jnp.zeros_like(h_ref)

    lane = lax.broadcasted_iota(jnp.int32, (1, PAIR), 1)
    m_left = (lane < RWKV_HEAD).astype(F32)
    m_right = 1.0 - m_left
    ri = lax.broadcasted_iota(jnp.int32, (P2, P2), 0)
    ci = lax.broadcasted_iota(jnp.int32, (P2, P2), 1)
    same = (ri >= C) == (ci >= C)
    strict = same & (ri > ci)
    incl = same & (ri >= ci)
    eye = ri == ci
    block_ones = same.astype(BF16)
    tri = (lax.broadcasted_iota(jnp.int32, (C, C), 0) >= lax.broadcasted_iota(jnp.int32, (C, C), 1)).astype(BF16)

    def stack(x):
        return jnp.concatenate([x * m_left, x * m_right], axis=0)

    def head_sums(x):
        s_left = jnp.sum(x * m_left, axis=-1, keepdims=True)
        s_right = jnp.sum(x * m_right, axis=-1, keepdims=True)
        return jnp.where(lane < RWKV_HEAD, s_left, s_right)

    inv_n = 1.0 / RWKV_HEAD
    dd = functools.partial(jnp.dot, preferred_element_type=F32)
    n_pairs = RWKV_HEADS // 2
    units = [(ch, p) for ch in range(SCAN_CHUNKS_PER_STEP) for p in range(n_pairs)]
    idx = [(slice(ch * C, (ch + 1) * C), slice(p * PAIR, (p + 1) * PAIR)) for ch, p in units]
    U = range(len(units))
    ld = lambda ref, rs, sl: ref[0, rs, sl].astype(F32)
    kkr = [ld(kk_ref, rs, sl) for rs, sl in idx]
    ss = [head_sums(x * x) for x in kkr]
    lws = [lw_ref[0, rs, sl] for rs, sl in idx]
    Ls = []
    for lw in lws:
        l_hi, l_lo = _split(lw)
        cs = dd(tri, jnp.concatenate([l_hi, l_lo], axis=1))
        Ls.append(cs[:, :PAIR] + cs[:, PAIR:])
    lhs_g, rhs_g, bk_hat, vs, at32, rt32, dec_end = [], [], [], [], [], [], []
    for u in U:
        rs, sl = idx[u]
        L, lw = Ls[u], lws[u]
        kk = kkr[u] * lax.rsqrt(jnp.maximum(ss[u], 1e-24))
        b_vec = kk * ld(al_ref, rs, sl)
        k = ld(k_ref, rs, sl)
        LC = L[C - 1:C, :]
        e_neg = jnp.exp(-L)
        e_end = jnp.exp(LC - L)
        At = stack(-kk * jnp.exp(L - lw))
        Rt = stack(ld(r_ref, rs, sl) * jnp.exp(L))
        at32.append(At)
        rt32.append(Rt)
        lhs_g.append(jnp.concatenate([At, Rt], axis=0).astype(BF16))
        rhs_g.append(jnp.concatenate([stack(b_vec * e_neg), stack(k * e_neg)], axis=0).astype(BF16))
        bk_hat.append(jnp.concatenate([stack(b_vec * e_end), stack(k * e_end)], axis=0))
        vs.append(stack(ld(v_ref, rs, sl)).astype(BF16))
        dec_end.append(jnp.exp(LC))
    G = [lax.dot_general(lhs_g[u], rhs_g[u], (((1,), (1,)), ((), ())), preferred_element_type=F32) for u in U]
    A_ak = [jnp.where(strict, G[u][:P2, P2:], 0.0).astype(BF16) for u in U]
    A_r = [jnp.concatenate([jnp.where(incl, G[u][P2:, :P2], 0.0), jnp.where(incl, G[u][P2:, P2:], 0.0)],
                           axis=1).astype(BF16) for u in U]
    W = [dd(A_ak[u], vs[u]) for u in U]
    bsz = lambda b: (ri >> int(math.log2(b))) == (ci >> int(math.log2(b)))
    b8 = bsz(SCAN_BASE_BLOCK)
    D1 = [jnp.where(strict & b8, G[u][:P2, :P2], 0.0).astype(BF16) for u in U]
    D2 = [dd(D1[u], D1[u]).astype(BF16) for u in U]
    D4 = [dd(D2[u], D2[u]).astype(BF16) for u in U]
    eye_f = eye.astype(F32)
    P1 = [eye_f + D1[u].astype(F32) + D2[u].astype(F32) + dd(D1[u], D2[u]) for u in U]
    Tm = [P1[u] + dd(P1[u].astype(BF16), D4[u]) for u in U]
    blk = SCAN_BASE_BLOCK
    while blk < C:
        off = strict & bsz(2 * blk) & ~bsz(blk)
        Mo = [jnp.where(off, G[u][:P2, :P2], 0.0).astype(BF16) for u in U]
        Tb = [Tm[u].astype(BF16) for u in U]
        TM = [dd(Tb[u], Mo[u]).astype(BF16) for u in U]
        Tm = [Tm[u] + dd(TM[u], Tb[u]) for u in U]
        blk *= 2
    Z = [dd(Tm[u].astype(BF16), jnp.concatenate([at32[u], W[u]], axis=1).astype(BF16)) for u in U]
    rhs = [jnp.concatenate([Z[u].astype(BF16), jnp.concatenate([jnp.zeros_like(vs[u]), vs[u]], axis=1)], axis=0)
           for u in U]
    o6 = [dd(A_r[u], rhs[u]) for u in U]
    o7 = [dd(bk_hat[u].T.astype(BF16), rhs[u]) for u in U]
    H = [h_ref[p] for p in range(n_pairs)]
    Y = [None] * len(units)
    for u in U:
        p = units[u][1]
        Hb = H[p].astype(BF16)
        Y[u] = dd((rt32[u] + o6[u][:, :PAIR]).astype(BF16), Hb) + o6[u][:, PAIR:]
        Mbd = o7[u][:, :PAIR] + jnp.where(eye, dec_end[u], 0.0)
        H[p] = dd(Mbd.astype(BF16), Hb) + o7[u][:, PAIR:]
    for p in range(n_pairs):
        h_ref[p] = H[p]
    ys = [Y[u][:C] + Y[u][C:] for u in U]
    rk = [ld(r_ref, rs, sl) * ld(k_ref, rs, sl) * pv_ref[0:1, sl] for rs, sl in idx]
    st1 = [head_sums(jnp.concatenate([ys[u], rk[u]], axis=0)) for u in U]
    yc = [ys[u] - st1[u][:C] * inv_n for u in U]
    var = [head_sums(yc[u] * yc[u]) * inv_n for u in U]
    for u in U:
        rs, sl = idx[u]
        bonus = st1[u][C:] * ld(v_ref, rs, sl)
        y_ref[0, rs, sl] = (yc[u] * lax.rsqrt(var[u] + RWKV_GN_EPS) * pv_ref[1:2, sl] + pv_ref[2:3, sl]
                            + bonus).astype(y_ref.dtype)


def _rwkv_scan(r, k, v, lw, kk, al, pvec):
    B, T, D = r.shape
    rows = SCAN_CHUNK * SCAN_CHUNKS_PER_STEP
    act = pl.BlockSpec((1, rows, D), lambda b, c: (b, c, 0))
    return pl.pallas_call(
        _rwkv_scan_kernel,
        out_shape=jax.ShapeDtypeStruct((B, T, D), BF16),
        grid=(B, T // rows),
        in_specs=[act] * 6 + [pl.BlockSpec((3, D), lambda b, c: (0, 0))],
        out_specs=act,
        scratch_shapes=[pltpu.VMEM((RWKV_HEADS // 2, PAIR, PAIR), F32)],
        compiler_params=_cparams("arbitrary", "arbitrary"),
        name="rwkv_scan",
    )(r, k, v, lw, kk, al, pvec)


def _proj_res_kernel(*refs, has_gate):
    if has_gate:
        y_ref, g_ref, x_ref, gm_ref, w_ref, o_ref = refs
        y = y_ref[0].astype(F32) * g_ref[0].astype(F32)
    else:
        y_ref, x_ref, gm_ref, w_ref, o_ref = refs
        y = y_ref[0]
    o_ref[0] = x_ref[0] + gm_ref[0] * jnp.dot(y.astype(BF16), w_ref[...], preferred_element_type=F32)


def _proj_res(y, g, x, gm, w):
    B, T, D = x.shape
    tm = DENSE_TM
    act = pl.BlockSpec((1, tm, D), lambda b, i: (b, i, 0))
    ins = [y] + ([g] if g is not None else []) + [x, gm, w]
    specs = [act] * (len(ins) - 2) + [pl.BlockSpec((1, 1, D), lambda b, i: (b, 0, 0)),
                                     pl.BlockSpec((D, D), lambda b, i: (0, 0))]
    return pl.pallas_call(
        functools.partial(_proj_res_kernel, has_gate=g is not None),
        out_shape=jax.ShapeDtypeStruct((B, T, D), F32),
        grid=(B, T // tm),
        in_specs=specs,
        out_specs=act,
        compiler_params=_cparams("arbitrary", "arbitrary"),
        name="proj_res",
    )(*ins)


def _norm_mm_kernel(*refs, n_proj):
    x_ref = refs[0]
    ins, outs = refs[1:1 + 3 * n_proj], refs[1 + 3 * n_proj:]
    x = x_ref[0]
    xn = x * lax.rsqrt(jnp.mean(x * x, axis=-1, keepdims=True) + RMS_EPS)
    for p in range(n_proj):
        mod_ref, g_ref, w_ref = ins[3 * p:3 * p + 3]
        h = (xn * g_ref[...]) * (1.0 + mod_ref[0, 1:2, :]) + mod_ref[0, 0:1, :]
        outs[p][0] = jnp.dot(h.astype(BF16), w_ref[...], preferred_element_type=F32).astype(outs[p].dtype)


def _norm_mm(x, projs, out_dtype):
    B, T, D = x.shape
    tm = DENSE_TM
    in_specs = [pl.BlockSpec((1, tm, D), lambda b, i: (b, i, 0))]
    args = [x]
    for mod2, g, w in projs:
        in_specs += [pl.BlockSpec((1, 2, D), lambda b, i: (b, 0, 0)), pl.BlockSpec((1, D), lambda b, i: (0, 0)),
                     pl.BlockSpec(w.shape, lambda b, i: (0, 0))]
        args += [mod2, g, w]
    return pl.pallas_call(
        functools.partial(_norm_mm_kernel, n_proj=len(projs)),
        out_shape=[jax.ShapeDtypeStruct((B, T, w.shape[1]), out_dtype) for _, _, w in projs],
        grid=(B, T // tm),
        in_specs=in_specs,
        out_specs=[pl.BlockSpec((1, tm, w.shape[1]), lambda b, i: (b, i, 0)) for _, _, w in projs],
        compiler_params=_cparams("arbitrary", "arbitrary"),
        name="norm_mm",
    )(*args)


def _diff_attn_kernel(q_ref, k_ref, v_ref, lam_ref, sg_ref, o_ref, m_ref, acc_ref, s_ref, *, lambda_init):
    tq, HB, dv = ATT_TQ, ATT_HB, DIFF_V_DIM
    qi = pl.program_id(2)
    heads = range(HB)
    hs = [slice(h * dv, (h + 1) * dv) for h in heads]
    lane = lax.broadcasted_iota(jnp.int32, (1, dv), 1)
    m_left = (lane < DIFF_QK_DIM).astype(F32)
    qs = []
    for h in heads:
        q = q_ref[0, :, hs[h]].astype(F32) * (DIFF_QK_DIM ** -0.5 * math.log2(math.e))
        qs.append(jnp.concatenate([q * m_left, q * (1.0 - m_left)], axis=0).astype(BF16))
    ones_col = jnp.ones((tq, dv), BF16)
    causal = (lax.broadcasted_iota(jnp.int32, (2 * tq, tq), 1)
              <= lax.broadcasted_iota(jnp.int32, (2 * tq, tq), 0) % tq)

    def key_rows(j):
        return pl.ds(pl.multiple_of(j * tq, tq), tq)

    def scores_into(slot, j):
        for h in heads:
            s_ref[slot, h] = lax.dot_general(qs[h], k_ref[0, key_rows(j), hs[h]], (((1,), (1,)), ((), ())),
                                             preferred_element_type=F32)

    def block(slot, j, first, prefetch):
        if prefetch is not None:
            scores_into(1 - slot, prefetch)
        for h in heads:
            s = s_ref[slot, h]
            if first:
                s = jnp.where(causal, s, -jnp.inf)
                m_new = jnp.broadcast_to(jnp.max(s, axis=-1, keepdims=True), (2 * tq, LANES))
            else:
                m_old = m_ref[h]
                m_new = jnp.maximum(m_old, jnp.max(s, axis=-1, keepdims=True))
            m_ref[h] = m_new
            p = jnp.exp2(s - jnp.concatenate([m_new] * (tq // LANES), axis=1)).astype(BF16)
            pv = jnp.dot(p, jnp.concatenate([v_ref[0, key_rows(j), hs[h]], ones_col], axis=1),
                         preferred_element_type=F32)
            if first:
                acc_ref[h] = pv
            else:
                alpha = jnp.exp2(m_old - m_new)
                acc_ref[h] = acc_ref[h] * jnp.concatenate([alpha, alpha], axis=1) + pv

    scores_into(0, qi)
    block(0, qi, True, 0)

    def pair(u, c):
        last = jnp.maximum(qi - 1, 0)
        block(1, 2 * u, False, jnp.minimum(2 * u + 1, last))
        block(0, 2 * u + 1, False, jnp.minimum(2 * u + 2, last))
        return c

    lax.fori_loop(0, qi // 2, pair, 0)

    @pl.when(qi % 2 == 1)
    def _():
        block(1, qi - 1, False, None)

    lv = lam_ref[...]
    lam = (jnp.exp(jnp.sum(lv[0:1] * lv[1:2], axis=-1, keepdims=True))
           - jnp.exp(jnp.sum(lv[2:3] * lv[3:4], axis=-1, keepdims=True)) + lambda_init)
    for h in heads:
        acc = acc_ref[h]
        o = acc[:, :dv] / acc[:, dv:dv + 1]
        o = o[:tq] - lam * o[tq:]
        ms = jnp.mean(o * o, axis=-1, keepdims=True)
        o_ref[0, :, hs[h]] = (o * lax.rsqrt(ms + SUBLN_EPS) * sg_ref[...] * (1.0 - lambda_init)).astype(o_ref.dtype)


def _diff_attn(q, kv, lam_vecs, subln_g, lambda_init):
    B, T, D = q.shape
    tq, HB = ATT_TQ, ATT_HB
    n_hb = DIFF_HEADS // HB
    w = HB * DIFF_V_DIM
    return pl.pallas_call(
        functools.partial(_diff_attn_kernel, lambda_init=lambda_init),
        out_shape=jax.ShapeDtypeStruct((B, T, D), BF16),
        grid=(B, n_hb, T // tq),
        in_specs=[
            pl.BlockSpec((1, tq, w), lambda b, h, i: (b, i, h)),
            pl.BlockSpec((1, T, w), lambda b, h, i: (b, 0, h)),
            pl.BlockSpec((1, T, w), lambda b, h, i: (b, 0, n_hb + h)),
            pl.BlockSpec((4, DIFF_QK_DIM), lambda b, h, i: (0, 0)),
            pl.BlockSpec((1, DIFF_V_DIM), lambda b, h, i: (0, 0)),
        ],
        out_specs=pl.BlockSpec((1, tq, w), lambda b, h, i: (b, i, h)),
        scratch_shapes=[pltpu.VMEM((HB, 2 * tq, LANES), F32), pltpu.VMEM((HB, 2 * tq, 2 * DIFF_V_DIM), F32),
                        pltpu.VMEM((2, HB, 2 * tq, tq), F32)],
        compiler_params=_cparams("arbitrary", "arbitrary", "arbitrary"),
        name="diff_attn",
    )(q, kv, kv, lam_vecs, subln_g)


def _router_kernel(x_ref, mod_ref, g_ref, w_ref, info_ref, cnt_ref):
    h = _rms_mod(x_ref[...], g_ref[...], mod_ref[0, 0:1, :], mod_ref[0, 1:2, :])
    logit = _dot3(h, w_ref[...])
    lane_i = lax.broadcasted_iota(jnp.int32, logit.shape, 1)
    lane = lane_i.astype(F32)
    neg = -jnp.inf
    big = float(LANES)
    is_grp = lane_i < N_GROUPS
    gl = jnp.where(is_grp, logit, neg)
    gmax = jnp.max(gl, axis=-1, keepdims=True)
    gidx = jnp.min(jnp.where(gl == gmax, lane, big), axis=-1, keepdims=True)
    grp_gate = 1.0 / jnp.sum(jnp.where(is_grp, jnp.exp(logit - gmax), 0.0), axis=-1, keepdims=True)
    lo = N_GROUPS + gidx * EXPERTS_PER_GROUP
    in_grp = (lane >= lo) & (lane < lo + EXPERTS_PER_GROUP)
    el = jnp.where(in_grp, logit, neg)
    t1 = jnp.max(el, axis=-1, keepdims=True)
    i1 = jnp.min(jnp.where(el == t1, lane, big), axis=-1, keepdims=True)
    el2 = jnp.where(lane == i1, neg, el)
    t2 = jnp.max(el2, axis=-1, keepdims=True)
    i2 = jnp.min(jnp.where(el2 == t2, lane, big), axis=-1, keepdims=True)
    e21 = jnp.exp(t2 - t1)
    p1 = 1.0 / (1.0 + e21)
    w1 = grp_gate * p1
    w2 = grp_gate * (e21 * p1)
    e1 = i1 - N_GROUPS
    e2 = i2 - N_GROUPS
    info_ref[...] = jnp.where(lane_i == 0, e1, jnp.where(lane_i == 1, e2, jnp.where(lane_i == 2, w1, jnp.where(lane_i == 3, w2, 0.0))))

    @pl.when(pl.program_id(0) == 0)
    def _():
        cnt_ref[...] = jnp.zeros_like(cnt_ref)

    picked = ((lane == e1) | (lane == e2)).astype(F32)
    cnt_ref[...] += jnp.broadcast_to(jnp.sum(picked, axis=0, keepdims=True), cnt_ref.shape)


def _router(x2, mod2, g, w_cat, T):
    n_tok, D = x2.shape
    tm = ROUTE_TM
    per_b = T // tm
    return pl.pallas_call(
        _router_kernel,
        out_shape=[jax.ShapeDtypeStruct((n_tok, LANES), F32), jax.ShapeDtypeStruct((SUBLANES, LANES), F32)],
        grid=(n_tok // tm,),
        in_specs=[
            pl.BlockSpec((tm, D), lambda i: (i, 0)),
            pl.BlockSpec((1, 2, D), lambda i: (i // per_b, 0, 0)),
            pl.BlockSpec((1, D), lambda i: (0, 0)),
            pl.BlockSpec((D, LANES), lambda i: (0, 0)),
        ],
        out_specs=[pl.BlockSpec((tm, LANES), lambda i: (i, 0)), pl.BlockSpec((SUBLANES, LANES), lambda i: (0, 0))],
        compiler_params=_cparams("arbitrary"),
        name="moe_router",
    )(x2, mod2, g, w_cat)


def _rank_kernel(info_ref, cnt_ref, dest_ref, meta_ref, zero_ref, start_ref):
    i = pl.program_id(0)
    zero_ref[...] = jnp.zeros_like(zero_ref)
    tb = info_ref.shape[0]
    lane = lax.broadcasted_iota(jnp.int32, (tb, LANES), 1)
    info = info_ref[...]
    e0 = info[:, 0:1].astype(jnp.int32)
    e1 = info[:, 1:2].astype(jnp.int32)
    o0 = (lane == e0).astype(F32)
    o1 = (lane == e1).astype(F32)
    both = o0 + o1

    @pl.when(i == 0)
    def _():
        cnt = cnt_ref[0:1, :]
        padded = jnp.floor((cnt + (MOE_TM - 1)) * (1.0 / MOE_TM)) * MOE_TM
        r = lax.broadcasted_iota(jnp.int32, (LANES, LANES), 0)
        c = lax.broadcasted_iota(jnp.int32, (LANES, LANES), 1)
        upper_strict = (r < c).astype(BF16)
        start = _dot_hl(jnp.broadcast_to(padded, (SUBLANES, LANES)), upper_strict)[0:1]
        start_ref[...] = start
        meta_ref[...] = jnp.broadcast_to(start + padded, (SUBLANES, LANES))

    r = lax.broadcasted_iota(jnp.int32, (tb, tb), 0)
    c = lax.broadcasted_iota(jnp.int32, (tb, tb), 1)
    lower_strict = (r > c).astype(BF16)
    before = jnp.dot(lower_strict, both.astype(BF16), preferred_element_type=F32) + start_ref[...]
    d0 = jnp.sum(o0 * before, axis=-1, keepdims=True)
    d1 = jnp.sum(o1 * before, axis=-1, keepdims=True)
    dest_ref[...] = jnp.where(lane == 0, d0, jnp.where(lane == 1, d1, 0.0)).astype(jnp.int32)
    start_ref[...] += jnp.sum(both, axis=0, keepdims=True)


def _rank(info, cnt, n_rows):
    n_tok = info.shape[0]
    tb = RANK_TB
    n_blk = n_tok // tb
    zrows = n_rows * SUBLANES // n_blk
    assert zrows * n_blk == n_rows * SUBLANES and zrows % SUBLANES == 0
    return pl.pallas_call(
        _rank_kernel,
        out_shape=[jax.ShapeDtypeStruct((n_tok, LANES), jnp.int32), jax.ShapeDtypeStruct((SUBLANES, LANES), F32),
                   jax.ShapeDtypeStruct((n_rows * SUBLANES, LANES), F32)],
        grid=(n_blk,),
        in_specs=[pl.BlockSpec((tb, LANES), lambda i: (i, 0)), pl.BlockSpec((SUBLANES, LANES), lambda i: (0, 0))],
        out_specs=[pl.BlockSpec((tb, LANES), lambda i: (i, 0)), pl.BlockSpec((SUBLANES, LANES), lambda i: (0, 0)),
                   pl.BlockSpec((zrows, LANES), lambda i: (i, 0))],
        scratch_shapes=[pltpu.VMEM((1, LANES), F32)],
        compiler_params=_cparams("arbitrary"),
        name="moe_rank",
    )(info, cnt)


assert D_MODEL == SUBLANES * LANES


def _tile_rows_store(ref, x):
    n = x.shape[0]
    for s in range(SUBLANES):
        ref[pl.ds(s, n, stride=SUBLANES), :] = x[:, s * LANES:(s + 1) * LANES]


def _tile_rows_load(ref, n):
    return jnp.concatenate([ref[pl.ds(s, n, stride=SUBLANES), :] for s in range(SUBLANES)], axis=1)


def _row_copy(src_ref, s, dst_ref, d, sem):
    rows = lambda r: pl.ds(pl.multiple_of(r * SUBLANES, SUBLANES), SUBLANES)
    return pltpu.make_async_copy(src_ref.at[rows(s)], dst_ref.at[rows(d)], sem)


def _dispatch_kernel(dest_ref, x_ref, mod_ref, g_ref, xs_in_ref, xs_ref, h_ref, sem):
    del xs_in_ref
    tb = x_ref.shape[0]
    i = pl.program_id(0)
    last = pl.num_programs(0) - 1
    h = _rms_mod(x_ref[...], g_ref[...], mod_ref[0, 0:1, :], mod_ref[0, 1:2, :])

    def drain(slot):
        def wait(j, c):
            _row_copy(h_ref.at[slot], 0, xs_ref, 0, sem.at[slot]).wait()
            return c
        lax.fori_loop(0, 2 * tb, wait, 0, unroll=8)

    for slot in range(2):
        @pl.when(i % 2 == slot)
        def _(slot=slot):
            @pl.when(i >= 2)
            def _():
                drain(slot)

            _tile_rows_store(h_ref.at[slot], h)

            def start(j, c):
                _row_copy(h_ref.at[slot], j, xs_ref, dest_ref[0, 0, 2 * j], sem.at[slot]).start(priority=0)
                _row_copy(h_ref.at[slot], j, xs_ref, dest_ref[0, 0, 2 * j + 1], sem.at[slot]).start(priority=1)
                return c

            lax.fori_loop(0, tb, start, 0, unroll=8)

            @pl.when(i == last)
            def _():
                drain(slot)

                @pl.when(i >= 1)
                def _():
                    drain(1 - slot)


def _dispatch(dest3, x2, mod2, g, xs_zero, T):
    n_tok, D = x2.shape
    tb = ROW_TB
    per_b = T // tb
    return pl.pallas_call(
        _dispatch_kernel,
        out_shape=jax.ShapeDtypeStruct(xs_zero.shape, F32),
        grid=(n_tok // tb,),
        in_specs=[
            pl.BlockSpec((1, 1, 2 * tb), lambda i: (i, 0, 0), memory_space=pltpu.SMEM),
            pl.BlockSpec((tb, D), lambda i: (i, 0)),
            pl.BlockSpec((1, 2, D), lambda i: (i // per_b, 0, 0)),
            pl.BlockSpec((1, D), lambda i: (0, 0)),
            pl.BlockSpec(memory_space=pl.ANY),
        ],
        out_specs=pl.BlockSpec(memory_space=pl.ANY),
        scratch_shapes=[pltpu.VMEM((2, tb * SUBLANES, LANES), F32), pltpu.SemaphoreType.DMA((2,))],
        input_output_aliases={4: 0},
        compiler_params=_cparams("arbitrary"),
        name="moe_dispatch",
    )(dest3, x2, mod2, g, xs_zero)


def _expert_kernel(be_ref, nb_ref, nxt_ref, xs_ref, wg_hbm, wu_hbm, wd_hbm, ys_ref, stage_g, stage_u, stage_d,
                   wgb, wub, wdb, sem, *, layer):
    i = pl.program_id(0)
    e = be_ref[i]
    changed = (i == 0) | (e != be_ref[jnp.maximum(i - 1, 0)])

    def fetch(ex):
        return (pltpu.make_async_copy(wg_hbm.at[layer, ex], stage_g, sem.at[0]),
                pltpu.make_async_copy(wu_hbm.at[layer, ex], stage_u, sem.at[1]),
                pltpu.make_async_copy(wd_hbm.at[layer, ex], stage_d, sem.at[2]))

    @pl.when(i == 0)
    def _():
        for cp in fetch(e):
            cp.start()

    @pl.when(changed)
    def _():
        for cp in fetch(e):
            cp.wait()
        wgb[...] = stage_g[...].astype(BF16)
        wub[...] = stage_u[...].astype(BF16)
        wdb[...] = stage_d[...].astype(BF16)

        @pl.when(nxt_ref[i] >= 0)
        def _():
            for cp in fetch(nxt_ref[i]):
                cp.start()

    @pl.when(i < nb_ref[0])
    def _():
        x = _tile_rows_load(xs_ref, MOE_TM).astype(BF16)
        a = jnp.dot(x, wgb[...], preferred_element_type=F32)
        u = jnp.dot(x, wub[...], preferred_element_type=F32)
        hdn = (a * jax.nn.sigmoid(a)) * u
        _tile_rows_store(ys_ref, jnp.dot(hdn.astype(BF16), wdb[...], preferred_element_type=F32))

    @pl.when(i >= nb_ref[0])
    def _():
        ys_ref[...] = jnp.zeros_like(ys_ref)


def _experts(blk_e, n_used, next_e, xs, w_gate, w_up, w_down, layer):
    D = D_MODEL
    n_rows = xs.shape[0] // SUBLANES
    tm = MOE_TM
    FF = EXPERT_FF
    row_block = pl.BlockSpec((tm * SUBLANES, LANES), lambda i, be, nb, nx: (i, 0))
    hbm = pl.BlockSpec(memory_space=pl.ANY)
    grid_spec = pltpu.PrefetchScalarGridSpec(
        num_scalar_prefetch=3,
        grid=(n_rows // tm,),
        in_specs=[row_block, hbm, hbm, hbm],
        out_specs=row_block,
        scratch_shapes=[pltpu.VMEM((D, FF), F32), pltpu.VMEM((D, FF), F32), pltpu.VMEM((FF, D), F32),
                        pltpu.VMEM((D, FF), BF16), pltpu.VMEM((D, FF), BF16), pltpu.VMEM((FF, D), BF16),
                        pltpu.SemaphoreType.DMA((3,))],
    )
    return pl.pallas_call(
        functools.partial(_expert_kernel, layer=layer),
        out_shape=jax.ShapeDtypeStruct(xs.shape, F32),
        grid_spec=grid_spec,
        compiler_params=_cparams("arbitrary"),
        name="moe_experts",
    )(blk_e, n_used, next_e, xs, w_gate, w_up, w_down)


def _combine_kernel(dest_ref, dest_next_ref, info_ref, x_ref, gf_ref, fg_ref, ys_ref, o_ref, y_ref, sem, *,
                    final_norm):
    tb = x_ref.shape[0]
    i = pl.program_id(0)
    last = pl.num_programs(0) - 1

    def gather(d_ref, slot):
        def start(j, c):
            _row_copy(ys_ref, d_ref[0, 0, 2 * j], y_ref.at[slot, 0], j, sem.at[slot]).start(priority=0)
            _row_copy(ys_ref, d_ref[0, 0, 2 * j + 1], y_ref.at[slot, 1], j, sem.at[slot]).start(priority=1)
            return c
        lax.fori_loop(0, tb, start, 0, unroll=8)

    @pl.when(i == 0)
    def _():
        gather(dest_ref, 0)

    for slot in range(2):
        @pl.when(i % 2 == slot)
        def _(slot=slot):
            @pl.when(i < last)
            def _():
                gather(dest_next_ref, 1 - slot)

            def wait(j, c):
                _row_copy(ys_ref, 0, y_ref.at[slot, 0], 0, sem.at[slot]).wait()
                return c

            lax.fori_loop(0, 2 * tb, wait, 0, unroll=8)
            info = info_ref[...]
            moe = (info[:, 2:3] * _tile_rows_load(y_ref.at[slot, 0], tb)
                   + info[:, 3:4] * _tile_rows_load(y_ref.at[slot, 1], tb))
            out = x_ref[...] + gf_ref[0] * moe
            if final_norm:
                ms = jnp.mean(out * out, axis=-1, keepdims=True)
                out = out * lax.rsqrt(ms + RMS_EPS) * fg_ref[...]
            o_ref[...] = out


def _combine(dest3, info, x2, gf, final_g, ys, T, final_norm):
    n_tok, D = x2.shape
    tb = ROW_TB
    per_b = T // tb
    n_steps = n_tok // tb
    return pl.pallas_call(
        functools.partial(_combine_kernel, final_norm=final_norm),
        out_shape=jax.ShapeDtypeStruct((n_tok, D), F32),
        grid=(n_steps,),
        in_specs=[
            pl.BlockSpec((1, 1, 2 * tb), lambda i: (i, 0, 0), memory_space=pltpu.SMEM),
            pl.BlockSpec((1, 1, 2 * tb), lambda i: (jnp.minimum(i + 1, n_steps - 1), 0, 0), memory_space=pltpu.SMEM),
            pl.BlockSpec((tb, LANES), lambda i: (i, 0)),
            pl.BlockSpec((tb, D), lambda i: (i, 0)),
            pl.BlockSpec((1, 1, D), lambda i: (i // per_b, 0, 0)),
            pl.BlockSpec((1, D), lambda i: (0, 0)),
            pl.BlockSpec(memory_space=pl.ANY),
        ],
        out_specs=pl.BlockSpec((tb, D), lambda i: (i, 0)),
        scratch_shapes=[pltpu.VMEM((2, 2, tb * SUBLANES, LANES), F32), pltpu.SemaphoreType.DMA((2,))],
        compiler_params=_cparams("arbitrary"),
        name="moe_combine",
    )(dest3, dest3, info, x2, gf, final_g, ys)


def _moe_layer(x, mod_f, gf, norm_g, w_rg, w_re, w_gate, w_up, w_down, layer, final_g, final_norm):
    B, T, D = x.shape
    n_tok = B * T
    x2 = x.reshape(n_tok, D)
    w_cat = jnp.concatenate([w_rg, w_re, jnp.zeros((D, LANES - N_GROUPS - N_EXPERTS), F32)], axis=1)
    info, cnt = _router(x2, mod_f, norm_g, w_cat, T)
    n_rows = -(-(2 * n_tok + N_EXPERTS * MOE_TM) // MOE_TM) * MOE_TM
    dest, meta, xs_zero = _rank(info, cnt, n_rows)
    pad_end = meta[0, :N_EXPERTS].astype(jnp.int32)
    n_blocks = n_rows // MOE_TM
    blk_start = jnp.arange(n_blocks, dtype=jnp.int32) * MOE_TM
    blk_e = jnp.minimum(jnp.sum(pad_end[None, :] <= blk_start[:, None], axis=1), N_EXPERTS - 1).astype(jnp.int32)
    n_used = (pad_end[N_EXPERTS - 1:] // MOE_TM).astype(jnp.int32)
    dest3 = dest[:, :2].reshape(n_tok // ROW_TB, 1, 2 * ROW_TB)
    xs = _dispatch(dest3, x2, mod_f, norm_g, xs_zero, T)
    seg_end = jnp.sum(blk_e[None, :] <= blk_e[:, None], axis=1)
    next_e = jnp.where(seg_end < n_blocks, blk_e[jnp.minimum(seg_end, n_blocks - 1)], -1).astype(jnp.int32)
    ys = _experts(blk_e, n_used, next_e, xs, w_gate, w_up, w_down, layer)
    out = _combine(dest3, info, x2, gf, final_g, ys, T, final_norm)
    return out.reshape(B, T, D)


def kernel(x, c, ada_w, ada_b, norm_mix_g, norm_ffn_g, rw_mu, rw_w_rkv, rw_w0, rw_w1, rw_w2, rw_a0, rw_a1, rw_a2,
           rw_g1, rw_g2, rw_k_k, rw_k_a, rw_r_k, rw_gn_g, rw_gn_b, rw_w_o, ada_kv_w, ada_kv_b, norm_kv_g, w_kv,
           df_w_q, df_lq1, df_lk1, df_lq2, df_lk2, df_subln_g, df_w_o, moe_w_rg, moe_w_re, moe_w_gate, moe_w_up,
           moe_w_down, final_g):
    B, T, D = x.shape
    c_pad = jnp.zeros((SUBLANES, D), F32).at[:B].set(c)
    mod = _ada(c_pad, ada_w, ada_b, 6 * D // 4)[:, :B]
    mod_kv = _ada(c_pad, ada_kv_w[None], ada_kv_b[None], D)[0, :B]
    bf = lambda w: w.astype(BF16)
    row = lambda v: v.reshape(1, -1)

    for l in range(DEPTH):
        sh_m, sc_m, g_m, sh_f, sc_f, g_f = jnp.split(mod[l], 6, axis=-1)
        mod_m = jnp.stack([sh_m, sc_m], axis=1)
        mod_f = jnp.stack([sh_f, sc_f], axis=1)
        if l < N_A_LAYERS:
            i = l
            vec = jnp.stack([rw_w0[i], rw_a0[i], rw_k_k[i], rw_k_a[i]], axis=0)
            r, k, v, lw, kk, al, gate = _rwkv_proj(
                x, mod_m, row(norm_mix_g[l]), rw_mu[i], bf(rw_w_rkv[i]), bf(rw_w1[i]), bf(rw_w2[i]),
                bf(rw_a1[i]), bf(rw_a2[i]), bf(rw_g1[i]), bf(rw_g2[i]), vec)
            pvec = jnp.stack([rw_r_k[i].reshape(-1), rw_gn_g[i], rw_gn_b[i]], axis=0)
            y = _rwkv_scan(r, k, v, lw, kk, al, pvec)
            x = _proj_res(y, gate, x, g_m[:, None, :], bf(rw_w_o[i]))
        else:
            j = l - N_A_LAYERS
            q_proj = (mod_m, row(norm_mix_g[l]), bf(df_w_q[j]))
            if l == N_A_LAYERS:
                sh_kv, sc_kv = jnp.split(mod_kv, 2, axis=-1)
                q, kv = _norm_mm(x, [q_proj, (jnp.stack([sh_kv, sc_kv], axis=1), row(norm_kv_g), bf(w_kv))], BF16)
            else:
                q, = _norm_mm(x, [q_proj], BF16)
            lambda_init = 0.8 - 0.6 * math.exp(-0.3 * l)
            lam_vecs = jnp.stack([df_lq1[j], df_lk1[j], df_lq2[j], df_lk2[j]], axis=0)
            o = _diff_attn(q, kv, lam_vecs, row(df_subln_g[j]), lambda_init)
            x = _proj_res(o, None, x, g_m[:, None, :], bf(df_w_o[j]))
        x = _moe_layer(x, mod_f, g_f[:, None, :], row(norm_ffn_g[l]), moe_w_rg[l], moe_w_re[l], moe_w_gate,
                       moe_w_up, moe_w_down, l, row(final_g), final_norm=(l == DEPTH - 1))
    return x
```

```python
import functools
import math

import jax
import jax.numpy as jnp
from jax import lax
from jax.experimental import pallas as pl
from jax.experimental.pallas import tpu as pltpu

F32 = jnp.float32
BF16 = jnp.bfloat16

D_MODEL = 1024
DEPTH = 2
N_A_LAYERS = DEPTH // 2
RWKV_HEAD = 64
RWKV_HEADS = D_MODEL // RWKV_HEAD
RWKV_GN_EPS = 64e-5
DIFF_QK_DIM = 64
DIFF_V_DIM = 2 * DIFF_QK_DIM
DIFF_HEADS = D_MODEL // DIFF_V_DIM
SUBLN_EPS = 1e-5
N_GROUPS = 4
EXPERTS_PER_GROUP = 8
N_EXPERTS = N_GROUPS * EXPERTS_PER_GROUP
EXPERT_FF = 512
RMS_EPS = 1e-6

LANES = 128
SUBLANES = 8
VMEM_LIMIT_BYTES = 56 * 1024 * 1024

SCAN_CHUNK = 64
SCAN_BASE_BLOCK = 8
SCAN_CHUNKS_PER_STEP = 2
PAIR = 2 * RWKV_HEAD
PROJ_TM = 256
DENSE_TM = 512
ATT_TQ = 512
ATT_HB = 2
MOE_TM = 256
ROUTE_TM = 512
RANK_TB = 1024
ROW_TB = 256


def _cparams(*sem):
    return pltpu.CompilerParams(dimension_semantics=sem, vmem_limit_bytes=VMEM_LIMIT_BYTES)


def _dot(a, b):
    return jnp.dot(a.astype(BF16), b.astype(BF16), preferred_element_type=F32)


def _dot_nt(a, b):
    return lax.dot_general(a.astype(BF16), b.astype(BF16), (((1,), (1,)), ((), ())), preferred_element_type=F32)


def _split(x):
    hi = x.astype(BF16)
    lo = (x - hi.astype(F32)).astype(BF16)
    return hi, lo


def _dot3(a, b):
    ah, al = _split(a)
    bh, bl = _split(b)
    d = functools.partial(jnp.dot, preferred_element_type=F32)
    return d(ah, bh) + d(ah, bl) + d(al, bh)


def _dot_hl(a, b_exact):
    ah, al = _split(a)
    d = functools.partial(jnp.dot, preferred_element_type=F32)
    return d(ah, b_exact) + d(al, b_exact)


def _rms_mod(x, g, shift, scale):
    ms = jnp.mean(x * x, axis=-1, keepdims=True)
    return (x * lax.rsqrt(ms + RMS_EPS) * g) * (1.0 + scale) + shift


def _ada_kernel(c_ref, w_ref, b_ref, o_ref):
    c = c_ref[...]
    ca = c * jax.nn.sigmoid(c)
    o_ref[...] = _dot3(ca, w_ref[...]) + b_ref[...]


def _ada(c_pad, w, b, tn):
    L, D, N = w.shape
    return pl.pallas_call(
        _ada_kernel,
        out_shape=jax.ShapeDtypeStruct((L, SUBLANES, N), F32),
        grid=(L, N // tn),
        in_specs=[
            pl.BlockSpec((SUBLANES, D), lambda l, j: (0, 0)),
            pl.BlockSpec((None, D, tn), lambda l, j: (l, 0, j)),
            pl.BlockSpec((None, 1, tn), lambda l, j: (l, 0, j)),
        ],
        out_specs=pl.BlockSpec((None, SUBLANES, tn), lambda l, j: (l, 0, j)),
        compiler_params=_cparams("arbitrary", "arbitrary"),
        name="ada_mod",
    )(c_pad, w, b.reshape(L, 1, N))


def _rwkv_proj_kernel(x_ref, xp_ref, mod_ref, g_ref, mu_ref, wrkv_ref, w1_ref, w2_ref, a1_ref, a2_ref,
                      g1_ref, g2_ref, vec_ref, r_ref, k_ref, v_ref, lw_ref, kk_ref, al_ref, gate_ref):
    i = pl.program_id(1)
    g = g_ref[...]
    shift, scale = mod_ref[0, 0:1, :], mod_ref[0, 1:2, :]
    h = _rms_mod(x_ref[0], g, shift, scale)
    hp = _rms_mod(xp_ref[0, SUBLANES - 1:SUBLANES, :], g, shift, scale)
    hp = jnp.where(i == 0, 0.0, hp)
    row = lax.broadcasted_iota(jnp.int32, h.shape, 0)
    h_prev = jnp.where(row == 0, hp, pltpu.roll(h, 1, axis=0))
    xx = h_prev - h
    mu = mu_ref[...]
    xs = [(h + xx * mu[j:j + 1, :]).astype(BF16) for j in range(6)]
    w0, a0, k_k, k_a = (vec_ref[j:j + 1, :] for j in range(4))
    d = functools.partial(jnp.dot, preferred_element_type=F32)
    r = d(xs[0], wrkv_ref[0])
    k = d(xs[1], wrkv_ref[1])
    v = d(xs[2], wrkv_ref[2])
    z = w0 + _dot(jnp.tanh(d(xs[3], w1_ref[...])), w2_ref[...])
    lw = (-math.exp(-0.5)) * jax.nn.sigmoid(z)
    a = jax.nn.sigmoid(a0 + _dot(d(xs[4], a1_ref[...]), a2_ref[...]))
    gate = _dot(jax.nn.sigmoid(d(xs[5], g1_ref[...])), g2_ref[...])
    r_ref[0] = r.astype(BF16)
    k_ref[0] = (k * (1.0 + (a - 1.0) * k_a)).astype(BF16)
    v_ref[0] = v.astype(BF16)
    lw_ref[0] = lw
    kk_ref[0] = (k * k_k).astype(BF16)
    al_ref[0] = a.astype(BF16)
    gate_ref[0] = gate.astype(BF16)


def _rwkv_proj(x, mod2, g, mu, wrkv, w1, w2, a1, a2, g1, g2, vec):
    B, T, D = x.shape
    tm = PROJ_TM
    const2 = lambda b, i: (0, 0)
    const3 = lambda b, i: (0, 0, 0)
    act = pl.BlockSpec((1, tm, D), lambda b, i: (b, i, 0))
    n_sub = tm // SUBLANES
    return pl.pallas_call(
        _rwkv_proj_kernel,
        out_shape=[jax.ShapeDtypeStruct((B, T, D), F32 if n == 3 else BF16) for n in range(7)],
        grid=(B, T // tm),
        in_specs=[
            act,
            pl.BlockSpec((1, SUBLANES, D), lambda b, i: (b, jnp.maximum(i * n_sub - 1, 0), 0)),
            pl.BlockSpec((1, 2, D), lambda b, i: (b, 0, 0)),
            pl.BlockSpec((1, D), const2),
            pl.BlockSpec((6, D), const2),
            pl.BlockSpec((3, D, D), const3),
            pl.BlockSpec(w1.shape, const2), pl.BlockSpec(w2.shape, const2),
            pl.BlockSpec(a1.shape, const2), pl.BlockSpec(a2.shape, const2),
            pl.BlockSpec(g1.shape, const2), pl.BlockSpec(g2.shape, const2),
            pl.BlockSpec((4, D), const2),
        ],
        out_specs=[act] * 7,
        compiler_params=_cparams("arbitrary", "arbitrary"),
        name="rwkv_proj",
    )(x, x, mod2, g, mu, wrkv, w1, w2, a1, a2, g1, g2, vec)


def _rwkv_scan_kernel(r_ref, k_ref, v_ref, lw_ref, kk_ref, al_ref, pv_ref, y_ref, h_ref):
    C = SCAN_CHUNK
    P2 = 2 * C

    @pl.when(pl.program_id(1) == 0)
    def _():
        h_ref[...] = jnp.zeros_like(h_ref)

    lane = lax.broadcasted_iota(jnp.int32, (1, PAIR), 1)
    m_left = (lane < RWKV_HEAD).astype(F32)
    m_right = 1.0 - m_left
    ri = lax.broadcasted_iota(jnp.int32, (P2, P2), 0)
    ci = lax.broadcasted_iota(jnp.int32, (P2, P2), 1)
    same = (ri >= C) == (ci >= C)
    strict = same & (ri > ci)
    incl = same & (ri >= ci)
    eye = ri == ci
    block_ones = same.astype(BF16)
    tri = (lax.broadcasted_iota(jnp.int32, (C, C), 0) >= lax.broadcasted_iota(jnp.int32, (C, C), 1)).astype(BF16)

    def stack(x):
        return jnp.concatenate([x * m_left, x * m_right], axis=0)

    def head_sums(x):
        s_left = jnp.sum(x * m_left, axis=-1, keepdims=True)
        s_right = jnp.sum(x * m_right, axis=-1, keepdims=True)
        return jnp.where(lane < RWKV_HEAD, s_left, s_right)

    inv_n = 1.0 / RWKV_HEAD
    dd = functools.partial(jnp.dot, preferred_element_type=F32)
    n_pairs = RWKV_HEADS // 2
    units = [(ch, p) for ch in range(SCAN_CHUNKS_PER_STEP) for p in range(n_pairs)]
    idx = [(slice(ch * C, (ch + 1) * C), slice(p * PAIR, (p + 1) * PAIR)) for ch, p in units]
    U = range(len(units))
    ld = lambda ref, rs, sl: ref[0, rs, sl].astype(F32)
    kkr = [ld(kk_ref, rs, sl) for rs, sl in idx]
    ss = [head_sums(x * x) for x in kkr]
    lws = [lw_ref[0, rs, sl] for rs, sl in idx]
    Ls = []
    for lw in lws:
        l_hi, l_lo = _split(lw)
        cs = dd(tri, jnp.concatenate([l_hi, l_lo], axis=1))
        Ls.append(cs[:, :PAIR] + cs[:, PAIR:])
    lhs_g, rhs_g, bk_hat, vs, at32, rt32, dec_end = [], [], [], [], [], [], []
    for u in U:
        rs, sl = idx[u]
        L, lw = Ls[u], lws[u]
        kk = kkr[u] * lax.rsqrt(jnp.maximum(ss[u], 1e-24))
        b_vec = kk * ld(al_ref, rs, sl)
        k = ld(k_ref, rs, sl)
        LC = L[C - 1:C, :]
        e_neg = jnp.exp(-L)
        e_end = jnp.exp(LC - L)
        At = stack(-kk * jnp.exp(L - lw))
        Rt = stack(ld(r_ref, rs, sl) * jnp.exp(L))
        at32.append(At)
        rt32.append(Rt)
        lhs_g.append(jnp.concatenate([At, Rt], axis=0).astype(BF16))
        rhs_g.append(jnp.concatenate([stack(b_vec * e_neg), stack(k * e_neg)], axis=0).astype(BF16))
        bk_hat.append(jnp.concatenate([stack(b_vec * e_end), stack(k * e_end)], axis=0))
        vs.append(stack(ld(v_ref, rs, sl)).astype(BF16))
        dec_end.append(jnp.exp(LC))
    G = [lax.dot_general(lhs_g[u], rhs_g[u], (((1,), (1,)), ((), ())), preferred_element_type=F32) for u in U]
    A_ak = [jnp.where(strict, G[u][:P2, P2:], 0.0).astype(BF16) for u in U]
    A_r = [jnp.concatenate([jnp.where(incl, G[u][P2:, :P2], 0.0), jnp.where(incl, G[u][P2:, P2:], 0.0)],
                           axis=1).astype(BF16) for u in U]
    W = [dd(A_ak[u], vs[u]) for u in U]
    bsz = lambda b: (ri >> int(math.log2(b))) == (ci >> int(math.log2(b)))
    b8 = bsz(SCAN_BASE_BLOCK)
    D1 = [jnp.where(strict & b8, G[u][:P2, :P2], 0.0).astype(BF16) for u in U]
    D2 = [dd(D1[u], D1[u]).astype(BF16) for u in U]
    D4 = [dd(D2[u], D2[u]).astype(BF16) for u in U]
    eye_f = eye.astype(F32)
    P1 = [eye_f + D1[u].astype(F32) + D2[u].astype(F32) + dd(D1[u], D2[u]) for u in U]
    Tm = [P1[u] + dd(P1[u].astype(BF16), D4[u]) for u in U]
    blk = SCAN_BASE_BLOCK
    while blk < C:
        off = strict & bsz(2 * blk) & ~bsz(blk)
        Mo = [jnp.where(off, G[u][:P2, :P2], 0.0).astype(BF16) for u in U]
        Tb = [Tm[u].astype(BF16) for u in U]
        TM = [dd(Tb[u], Mo[u]).astype(BF16) for u in U]
        Tm = [Tm[u] + dd(TM[u], Tb[u]) for u in U]
        blk *= 2
    Z = [dd(Tm[u].astype(BF16), jnp.concatenate([at32[u], W[u]], axis=1).astype(BF16)) for u in U]
    rhs = [jnp.concatenate([Z[u].astype(BF16), jnp.concatenate([jnp.zeros_like(vs[u]), vs[u]], axis=1)], axis=0)
           for u in U]
    o6 = [dd(A_r[u], rhs[u]) for u in U]
    o7 = [dd(bk_hat[u].T.astype(BF16), rhs[u]) for u in U]
    H = [h_ref[p] for p in range(n_pairs)]
    Y = [None] * len(units)
    for u in U:
        p = units[u][1]
        Hb = H[p].astype(BF16)
        Y[u] = dd((rt32[u] + o6[u][:, :PAIR]).astype(BF16), Hb) + o6[u][:, PAIR:]
        Mbd = o7[u][:, :PAIR] + jnp.where(eye, dec_end[u], 0.0)
        H[p] = dd(Mbd.astype(BF16), Hb) + o7[u][:, PAIR:]
    for p in range(n_pairs):
        h_ref[p] = H[p]
    ys = [Y[u][:C] + Y[u][C:] for u in U]
    rk = [ld(r_ref, rs, sl) * ld(k_ref, rs, sl) * pv_ref[0:1, sl] for rs, sl in idx]
    st1 = [head_sums(jnp.concatenate([ys[u], rk[u]], axis=0)) for u in U]
    yc = [ys[u] - st1[u][:C] * inv_n for u in U]
    var = [head_sums(yc[u] * yc[u]) * inv_n for u in U]
    for u in U:
        rs, sl = idx[u]
        bonus = st1[u][C:] * ld(v_ref, rs, sl)
        y_ref[0, rs, sl] = (yc[u] * lax.rsqrt(var[u] + RWKV_GN_EPS) * pv_ref[1:2, sl] + pv_ref[2:3, sl]
                            + bonus).astype(y_ref.dtype)


def _rwkv_scan(r, k, v, lw, kk, al, pvec):
    B, T, D = r.shape
    rows = SCAN_CHUNK * SCAN_CHUNKS_PER_STEP
    act = pl.BlockSpec((1, rows, D), lambda b, c: (b, c, 0))
    return pl.pallas_call(
        _rwkv_scan_kernel,
        out_shape=jax.ShapeDtypeStruct((B, T, D), BF16),
        grid=(B, T // rows),
        in_specs=[act] * 6 + [pl.BlockSpec((3, D), lambda b, c: (0, 0))],
        out_specs=act,
        scratch_shapes=[pltpu.VMEM((RWKV_HEADS // 2, PAIR, PAIR), F32)],
        compiler_params=_cparams("arbitrary", "arbitrary"),
        name="rwkv_scan",
    )(r, k, v, lw, kk, al, pvec)


def _proj_res_kernel(*refs, has_gate):
    if has_gate:
        y_ref, g_ref, x_ref, gm_ref, w_ref, o_ref = refs
        y = y_ref[0].astype(F32) * g_ref[0].astype(F32)
    else:
        y_ref, x_ref, gm_ref, w_ref, o_ref = refs
        y = y_ref[0]
    o_ref[0] = x_ref[0] + gm_ref[0] * jnp.dot(y.astype(BF16), w_ref[...], preferred_element_type=F32)


def _proj_res(y, g, x, gm, w):
    B, T, D = x.shape
    tm = DENSE_TM
    act = pl.BlockSpec((1, tm, D), lambda b, i: (b, i, 0))
    ins = [y] + ([g] if g is not None else []) + [x, gm, w]
    specs = [act] * (len(ins) - 2) + [pl.BlockSpec((1, 1, D), lambda b, i: (b, 0, 0)),
                                     pl.BlockSpec((D, D), lambda b, i: (0, 0))]
    return pl.pallas_call(
        functools.partial(_proj_res_kernel, has_gate=g is not None),
        out_shape=jax.ShapeDtypeStruct((B, T, D), F32),
        grid=(B, T // tm),
        in_specs=specs,
        out_specs=act,
        compiler_params=_cparams("arbitrary", "arbitrary"),
        name="proj_res",
    )(*ins)


def _norm_mm_kernel(*refs, n_proj):
    x_ref = refs[0]
    ins, outs = refs[1:1 + 3 * n_proj], refs[1 + 3 * n_proj:]
    x = x_ref[0]
    xn = x * lax.rsqrt(jnp.mean(x * x, axis=-1, keepdims=True) + RMS_EPS)
    for p in range(n_proj):
        mod_ref, g_ref, w_ref = ins[3 * p:3 * p + 3]
        h = (xn * g_ref[...]) * (1.0 + mod_ref[0, 1:2, :]) + mod_ref[0, 0:1, :]
        outs[p][0] = jnp.dot(h.astype(BF16), w_ref[...], preferred_element_type=F32).astype(outs[p].dtype)


def _norm_mm(x, projs, out_dtype):
    B, T, D = x.shape
    tm = DENSE_TM
    in_specs = [pl.BlockSpec((1, tm, D), lambda b, i: (b, i, 0))]
    args = [x]
    for mod2, g, w in projs:
        in_specs += [pl.BlockSpec((1, 2, D), lambda b, i: (b, 0, 0)), pl.BlockSpec((1, D), lambda b, i: (0, 0)),
                     pl.BlockSpec(w.shape, lambda b, i: (0, 0))]
        args += [mod2, g, w]
    return pl.pallas_call(
        functools.partial(_norm_mm_kernel, n_proj=len(projs)),
        out_shape=[jax.ShapeDtypeStruct((B, T, w.shape[1]), out_dtype) for _, _, w in projs],
        grid=(B, T // tm),
        in_specs=in_specs,
        out_specs=[pl.BlockSpec((1, tm, w.shape[1]), lambda b, i: (b, i, 0)) for _, _, w in projs],
        compiler_params=_cparams("arbitrary", "arbitrary"),
        name="norm_mm",
    )(*args)


def _diff_attn_kernel(q_ref, k_ref, v_ref, lam_ref, sg_ref, o_ref, m_ref, acc_ref, s_ref, *, lambda_init):
    tq, HB, dv = ATT_TQ, ATT_HB, DIFF_V_DIM
    qi = pl.program_id(2)
    heads = range(HB)
    hs = [slice(h * dv, (h + 1) * dv) for h in heads]
    lane = lax.broadcasted_iota(jnp.int32, (1, dv), 1)
    m_left = (lane < DIFF_QK_DIM).astype(F32)
    qs = []
    for h in heads:
        q = q_ref[0, :, hs[h]].astype(F32) * (DIFF_QK_DIM ** -0.5 * math.log2(math.e))
        qs.append(jnp.concatenate([q * m_left, q * (1.0 - m_left)], axis=0).astype(BF16))
    ones_col = jnp.ones((tq, dv), BF16)
    causal = (lax.broadcasted_iota(jnp.int32, (2 * tq, tq), 1)
              <= lax.broadcasted_iota(jnp.int32, (2 * tq, tq), 0) % tq)

    def key_rows(j):
        return pl.ds(pl.multiple_of(j * tq, tq), tq)

    def scores_into(slot, j):
        for h in heads:
            s_ref[slot, h] = lax.dot_general(qs[h], k_ref[0, key_rows(j), hs[h]], (((1,), (1,)), ((), ())),
                                             preferred_element_type=F32)

    def block(slot, j, first, prefetch):
        if prefetch is not None:
            scores_into(1 - slot, prefetch)
        for h in heads:
            s = s_ref[slot, h]
            if first:
                s = jnp.where(causal, s, -jnp.inf)
                m_new = jnp.broadcast_to(jnp.max(s, axis=-1, keepdims=True), (2 * tq, LANES))
            else:
                m_old = m_ref[h]
                m_new = jnp.maximum(m_old, jnp.max(s, axis=-1, keepdims=True))
            m_ref[h] = m_new
            p = jnp.exp2(s - jnp.concatenate([m_new] * (tq // LANES), axis=1)).astype(BF16)
            pv = jnp.dot(p, jnp.concatenate([v_ref[0, key_rows(j), hs[h]], ones_col], axis=1),
                         preferred_element_type=F32)
            if first:
                acc_ref[h] = pv
            else:
                alpha = jnp.exp2(m_old - m_new)
                acc_ref[h] = acc_ref[h] * jnp.concatenate([alpha, alpha], axis=1) + pv

    scores_into(0, qi)
    block(0, qi, True, 0)

    def pair(u, c):
        last = jnp.maximum(qi - 1, 0)
        block(1, 2 * u, False, jnp.minimum(2 * u + 1, last))
        block(0, 2 * u + 1, False, jnp.minimum(2 * u + 2, last))
        return c

    lax.fori_loop(0, qi // 2, pair, 0)

    @pl.when(qi % 2 == 1)
    def _():
        block(1, qi - 1, False, None)

    lv = lam_ref[...]
    lam = (jnp.exp(jnp.sum(lv[0:1] * lv[1:2], axis=-1, keepdims=True))
           - jnp.exp(jnp.sum(lv[2:3] * lv[3:4], axis=-1, keepdims=True)) + lambda_init)
    for h in heads:
        acc = acc_ref[h]
        o = acc[:, :dv] / acc[:, dv:dv + 1]
        o = o[:tq] - lam * o[tq:]
        ms = jnp.mean(o * o, axis=-1, keepdims=True)
        o_ref[0, :, hs[h]] = (o * lax.rsqrt(ms + SUBLN_EPS) * sg_ref[...] * (1.0 - lambda_init)).astype(o_ref.dtype)


def _diff_attn(q, kv, lam_vecs, subln_g, lambda_init):
    B, T, D = q.shape
    tq, HB = ATT_TQ, ATT_HB
    n_hb = DIFF_HEADS // HB
    w = HB * DIFF_V_DIM
    return pl.pallas_call(
        functools.partial(_diff_attn_kernel, lambda_init=lambda_init),
        out_shape=jax.ShapeDtypeStruct((B, T, D), BF16),
        grid=(B, n_hb, T // tq),
        in_specs=[
            pl.BlockSpec((1, tq, w), lambda b, h, i: (b, i, h)),
            pl.BlockSpec((1, T, w), lambda b, h, i: (b, 0, h)),
            pl.BlockSpec((1, T, w), lambda b, h, i: (b, 0, n_hb + h)),
            pl.BlockSpec((4, DIFF_QK_DIM), lambda b, h, i: (0, 0)),
            pl.BlockSpec((1, DIFF_V_DIM), lambda b, h, i: (0, 0)),
        ],
        out_specs=pl.BlockSpec((1, tq, w), lambda b, h, i: (b, i, h)),
        scratch_shapes=[pltpu.VMEM((HB, 2 * tq, LANES), F32), pltpu.VMEM((HB, 2 * tq, 2 * DIFF_V_DIM), F32),
                        pltpu.VMEM((2, HB, 2 * tq, tq), F32)],
        compiler_params=_cparams("arbitrary", "arbitrary", "arbitrary"),
        name="diff_attn",
    )(q, kv, kv, lam_vecs, subln_g)


def _router_kernel(x_ref, mod_ref, g_ref, w_ref, info_ref, cnt_ref, zero_ref):
    zero_ref[...] = jnp.zeros_like(zero_ref)
    h = _rms_mod(x_ref[...], g_ref[...], mod_ref[0, 0:1, :], mod_ref[0, 1:2, :])
    logit = _dot3(h, w_ref[...])
    lane_i = lax.broadcasted_iota(jnp.int32, logit.shape, 1)
    lane = lane_i.astype(F32)
    neg = -jnp.inf
    big = float(LANES)
    is_grp = lane_i < N_GROUPS
    gl = jnp.where(is_grp, logit, neg)
    gmax = jnp.max(gl, axis=-1, keepdims=True)
    gidx = jnp.min(jnp.where(gl == gmax, lane, big), axis=-1, keepdims=True)
    grp_gate = 1.0 / jnp.sum(jnp.where(is_grp, jnp.exp(logit - gmax), 0.0), axis=-1, keepdims=True)
    lo = N_GROUPS + gidx * EXPERTS_PER_GROUP
    in_grp = (lane >= lo) & (lane < lo + EXPERTS_PER_GROUP)
    el = jnp.where(in_grp, logit, neg)
    t1 = jnp.max(el, axis=-1, keepdims=True)
    i1 = jnp.min(jnp.where(el == t1, lane, big), axis=-1, keepdims=True)
    el2 = jnp.where(lane == i1, neg, el)
    t2 = jnp.max(el2, axis=-1, keepdims=True)
    i2 = jnp.min(jnp.where(el2 == t2, lane, big), axis=-1, keepdims=True)
    e21 = jnp.exp(t2 - t1)
    p1 = 1.0 / (1.0 + e21)
    w1 = grp_gate * p1
    w2 = grp_gate * (e21 * p1)
    e1 = i1 - N_GROUPS
    e2 = i2 - N_GROUPS
    info_ref[...] = jnp.where(lane_i == 0, e1, jnp.where(lane_i == 1, e2, jnp.where(lane_i == 2, w1, jnp.where(lane_i == 3, w2, 0.0))))

    @pl.when(pl.program_id(0) == 0)
    def _():
        cnt_ref[...] = jnp.zeros_like(cnt_ref)

    picked = ((lane == e1) | (lane == e2)).astype(F32)
    cnt_ref[...] += jnp.broadcast_to(jnp.sum(picked, axis=0, keepdims=True), cnt_ref.shape)


def _router(x2, mod2, g, w_cat, T, n_rows):
    n_tok, D = x2.shape
    tm = ROUTE_TM
    per_b = T // tm
    n_blk = n_tok // tm
    zrows = n_rows * SUBLANES // n_blk
    assert zrows * n_blk == n_rows * SUBLANES and zrows % SUBLANES == 0
    return pl.pallas_call(
        _router_kernel,
        out_shape=[jax.ShapeDtypeStruct((n_tok, LANES), F32), jax.ShapeDtypeStruct((SUBLANES, LANES), F32),
                   jax.ShapeDtypeStruct((n_rows * SUBLANES, LANES), F32)],
        grid=(n_blk,),
        in_specs=[
            pl.BlockSpec((tm, D), lambda i: (i, 0)),
            pl.BlockSpec((1, 2, D), lambda i: (i // per_b, 0, 0)),
            pl.BlockSpec((1, D), lambda i: (0, 0)),
            pl.BlockSpec((D, LANES), lambda i: (0, 0)),
        ],
        out_specs=[pl.BlockSpec((tm, LANES), lambda i: (i, 0)), pl.BlockSpec((SUBLANES, LANES), lambda i: (0, 0)),
                   pl.BlockSpec((zrows, LANES), lambda i: (i, 0))],
        compiler_params=_cparams("arbitrary"),
        name="moe_router",
    )(x2, mod2, g, w_cat)


def _rank_kernel(info_ref, cnt_ref, dest_ref, meta_ref, start_ref):
    i = pl.program_id(0)
    tb = info_ref.shape[0]
    lane = lax.broadcasted_iota(jnp.int32, (tb, LANES), 1)
    info = info_ref[...]
    e0 = info[:, 0:1].astype(jnp.int32)
    e1 = info[:, 1:2].astype(jnp.int32)
    o0 = (lane == e0).astype(F32)
    o1 = (lane == e1).astype(F32)
    both = o0 + o1

    @pl.when(i == 0)
    def _():
        cnt = cnt_ref[0:1, :]
        padded = jnp.floor((cnt + (MOE_TM - 1)) * (1.0 / MOE_TM)) * MOE_TM
        r = lax.broadcasted_iota(jnp.int32, (LANES, LANES), 0)
        c = lax.broadcasted_iota(jnp.int32, (LANES, LANES), 1)
        upper_strict = (r < c).astype(BF16)
        start = _dot_hl(jnp.broadcast_to(padded, (SUBLANES, LANES)), upper_strict)[0:1]
        start_ref[...] = start
        meta_ref[...] = jnp.broadcast_to(start + padded, (SUBLANES, LANES))

    r = lax.broadcasted_iota(jnp.int32, (tb, tb), 0)
    c = lax.broadcasted_iota(jnp.int32, (tb, tb), 1)
    lower_strict = (r > c).astype(BF16)
    before = jnp.dot(lower_strict, both.astype(BF16), preferred_element_type=F32) + start_ref[...]
    d0 = jnp.sum(o0 * before, axis=-1, keepdims=True)
    d1 = jnp.sum(o1 * before, axis=-1, keepdims=True)
    dest_ref[...] = jnp.where(lane == 0, d0, jnp.where(lane == 1, d1, 0.0)).astype(jnp.int32)
    start_ref[...] += jnp.sum(both, axis=0, keepdims=True)


def _rank(info, cnt):
    n_tok = info.shape[0]
    tb = RANK_TB
    n_blk = n_tok // tb
    return pl.pallas_call(
        _rank_kernel,
        out_shape=[jax.ShapeDtypeStruct((n_tok, LANES), jnp.int32), jax.ShapeDtypeStruct((SUBLANES, LANES), F32)],
        grid=(n_blk,),
        in_specs=[pl.BlockSpec((tb, LANES), lambda i: (i, 0)), pl.BlockSpec((SUBLANES, LANES), lambda i: (0, 0))],
        out_specs=[pl.BlockSpec((tb, LANES), lambda i: (i, 0)), pl.BlockSpec((SUBLANES, LANES), lambda i: (0, 0))],
        scratch_shapes=[pltpu.VMEM((1, LANES), F32)],
        compiler_params=_cparams("arbitrary"),
        name="moe_rank",
    )(info, cnt)


assert D_MODEL == SUBLANES * LANES


def _tile_rows_store(ref, x):
    n = x.shape[0]
    for s in range(SUBLANES):
        ref[pl.ds(s, n, stride=SUBLANES), :] = x[:, s * LANES:(s + 1) * LANES]


def _tile_rows_load(ref, n):
    return jnp.concatenate([ref[pl.ds(s, n, stride=SUBLANES), :] for s in range(SUBLANES)], axis=1)


def _row_copy(src_ref, s, dst_ref, d, sem):
    rows = lambda r: pl.ds(pl.multiple_of(r * SUBLANES, SUBLANES), SUBLANES)
    return pltpu.make_async_copy(src_ref.at[rows(s)], dst_ref.at[rows(d)], sem)


def _dispatch_kernel(dest_ref, x_ref, mod_ref, g_ref, xs_in_ref, xs_ref, h_ref, sem):
    del xs_in_ref
    tb = x_ref.shape[0]
    i = pl.program_id(0)
    last = pl.num_programs(0) - 1
    h = _rms_mod(x_ref[...], g_ref[...], mod_ref[0, 0:1, :], mod_ref[0, 1:2, :])

    def drain(slot):
        def wait(j, c):
            _row_copy(h_ref.at[slot], 0, xs_ref, 0, sem.at[slot]).wait()
            return c
        lax.fori_loop(0, 2 * tb, wait, 0, unroll=8)

    for slot in range(2):
        @pl.when(i % 2 == slot)
        def _(slot=slot):
            @pl.when(i >= 2)
            def _():
                drain(slot)

            _tile_rows_store(h_ref.at[slot], h)

            def start(j, c):
                _row_copy(h_ref.at[slot], j, xs_ref, dest_ref[0, 0, 2 * j], sem.at[slot]).start(priority=0)
                _row_copy(h_ref.at[slot], j, xs_ref, dest_ref[0, 0, 2 * j + 1], sem.at[slot]).start(priority=1)
                return c

            lax.fori_loop(0, tb, start, 0, unroll=8)

            @pl.when(i == last)
            def _():
                drain(slot)

                @pl.when(i >= 1)
                def _():
                    drain(1 - slot)


def _dispatch(dest3, x2, mod2, g, xs_zero, T):
    n_tok, D = x2.shape
    tb = ROW_TB
    per_b = T // tb
    return pl.pallas_call(
        _dispatch_kernel,
        out_shape=jax.ShapeDtypeStruct(xs_zero.shape, F32),
        grid=(n_tok // tb,),
        in_specs=[
            pl.BlockSpec((1, 1, 2 * tb), lambda i: (i, 0, 0), memory_space=pltpu.SMEM),
            pl.BlockSpec((tb, D), lambda i: (i, 0)),
            pl.BlockSpec((1, 2, D), lambda i: (i // per_b, 0, 0)),
            pl.BlockSpec((1, D), lambda i: (0, 0)),
            pl.BlockSpec(memory_space=pl.ANY),
        ],
        out_specs=pl.BlockSpec(memory_space=pl.ANY),
        scratch_shapes=[pltpu.VMEM((2, tb * SUBLANES, LANES), F32), pltpu.SemaphoreType.DMA((2,))],
        input_output_aliases={4: 0},
        compiler_params=_cparams("arbitrary"),
        name="moe_dispatch",
    )(dest3, x2, mod2, g, xs_zero)


def _expert_kernel(be_ref, nb_ref, nxt_ref, xs_ref, wg_hbm, wu_hbm, wd_hbm, ys_ref, stage_g, stage_u, stage_d,
                   wgb, wub, wdb, sem, *, layer):
    i = pl.program_id(0)
    e = be_ref[i]
    changed = (i == 0) | (e != be_ref[jnp.maximum(i - 1, 0)])

    def fetch(ex):
        return (pltpu.make_async_copy(wg_hbm.at[layer, ex], stage_g, sem.at[0]),
                pltpu.make_async_copy(wu_hbm.at[layer, ex], stage_u, sem.at[1]),
                pltpu.make_async_copy(wd_hbm.at[layer, ex], stage_d, sem.at[2]))

    @pl.when(i == 0)
    def _():
        for cp in fetch(e):
            cp.start()

    @pl.when(changed)
    def _():
        for cp in fetch(e):
            cp.wait()
        wgb[...] = stage_g[...].astype(BF16)
        wub[...] = stage_u[...].astype(BF16)
        wdb[...] = stage_d[...].astype(BF16)

        @pl.when(nxt_ref[i] >= 0)
        def _():
            for cp in fetch(nxt_ref[i]):
                cp.start()

    @pl.when(i < nb_ref[0])
    def _():
        x = _tile_rows_load(xs_ref, MOE_TM).astype(BF16)
        a = jnp.dot(x, wgb[...], preferred_element_type=F32)
        u = jnp.dot(x, wub[...], preferred_element_type=F32)
        hdn = (a * jax.nn.sigmoid(a)) * u
        _tile_rows_store(ys_ref, jnp.dot(hdn.astype(BF16), wdb[...], preferred_element_type=F32))

    @pl.when(i >= nb_ref[0])
    def _():
        ys_ref[...] = jnp.zeros_like(ys_ref)


def _experts(blk_e, n_used, next_e, xs, w_gate, w_up, w_down, layer):
    D = D_MODEL
    n_rows = xs.shape[0] // SUBLANES
    tm = MOE_TM
    FF = EXPERT_FF
    row_block = pl.BlockSpec((tm * SUBLANES, LANES), lambda i, be, nb, nx: (i, 0))
    hbm = pl.BlockSpec(memory_space=pl.ANY)
    grid_spec = pltpu.PrefetchScalarGridSpec(
        num_scalar_prefetch=3,
        grid=(n_rows // tm,),
        in_specs=[row_block, hbm, hbm, hbm],
        out_specs=row_block,
        scratch_shapes=[pltpu.VMEM((D, FF), F32), pltpu.VMEM((D, FF), F32), pltpu.VMEM((FF, D), F32),
                        pltpu.VMEM((D, FF), BF16), pltpu.VMEM((D, FF), BF16), pltpu.VMEM((FF, D), BF16),
                        pltpu.SemaphoreType.DMA((3,))],
    )
    return pl.pallas_call(
        functools.partial(_expert_kernel, layer=layer),
        out_shape=jax.ShapeDtypeStruct(xs.shape, F32),
        grid_spec=grid_spec,
        compiler_params=_cparams("arbitrary"),
        name="moe_experts",
    )(blk_e, n_used, next_e, xs, w_gate, w_up, w_down)


def _combine_kernel(dest_ref, dest_next_ref, info_ref, x_ref, gf_ref, fg_ref, ys_ref, o_ref, y_ref, sem, *,
                    final_norm):
    tb = x_ref.shape[0]
    i = pl.program_id(0)
    last = pl.num_programs(0) - 1

    def gather(d_ref, slot):
        def start(j, c):
            _row_copy(ys_ref, d_ref[0, 0, 2 * j], y_ref.at[slot, 0], j, sem.at[slot]).start(priority=0)
            _row_copy(ys_ref, d_ref[0, 0, 2 * j + 1], y_ref.at[slot, 1], j, sem.at[slot]).start(priority=1)
            return c
        lax.fori_loop(0, tb, start, 0, unroll=8)

    @pl.when(i == 0)
    def _():
        gather(dest_ref, 0)

    for slot in range(2):
        @pl.when(i % 2 == slot)
        def _(slot=slot):
            @pl.when(i < last)
            def _():
                gather(dest_next_ref, 1 - slot)

            def wait(j, c):
                _row_copy(ys_ref, 0, y_ref.at[slot, 0], 0, sem.at[slot]).wait()
                return c

            lax.fori_loop(0, 2 * tb, wait, 0, unroll=8)
            info = info_ref[...]
            moe = (info[:, 2:3] * _tile_rows_load(y_ref.at[slot, 0], tb)
                   + info[:, 3:4] * _tile_rows_load(y_ref.at[slot, 1], tb))
            out = x_ref[...] + gf_ref[0] * moe
            if final_norm:
                ms = jnp.mean(out * out, axis=-1, keepdims=True)
                out = out * lax.rsqrt(ms + RMS_EPS) * fg_ref[...]
            o_ref[...] = out


def _combine(dest3, info, x2, gf, final_g, ys, T, final_norm):
    n_tok, D = x2.shape
    tb = ROW_TB
    per_b = T // tb
    n_steps = n_tok // tb
    return pl.pallas_call(
        functools.partial(_combine_kernel, final_norm=final_norm),
        out_shape=jax.ShapeDtypeStruct((n_tok, D), F32),
        grid=(n_steps,),
        in_specs=[
            pl.BlockSpec((1, 1, 2 * tb), lambda i: (i, 0, 0), memory_space=pltpu.SMEM),
            pl.BlockSpec((1, 1, 2 * tb), lambda i: (jnp.minimum(i + 1, n_steps - 1), 0, 0), memory_space=pltpu.SMEM),
            pl.BlockSpec((tb, LANES), lambda i: (i, 0)),
            pl.BlockSpec((tb, D), lambda i: (i, 0)),
            pl.BlockSpec((1, 1, D), lambda i: (i // per_b, 0, 0)),
            pl.BlockSpec((1, D), lambda i: (0, 0)),
            pl.BlockSpec(memory_space=pl.ANY),
        ],
        out_specs=pl.BlockSpec((tb, D), lambda i: (i, 0)),
        scratch_shapes=[pltpu.VMEM((2, 2, tb * SUBLANES, LANES), F32), pltpu.SemaphoreType.DMA((2,))],
        compiler_params=_cparams("arbitrary"),
        name="moe_combine",
    )(dest3, dest3, info, x2, gf, final_g, ys)


def _moe_layer(x, mod_f, gf, norm_g, w_rg, w_re, w_gate, w_up, w_down, layer, final_g, final_norm):
    B, T, D = x.shape
    n_tok = B * T
    x2 = x.reshape(n_tok, D)
    w_cat = jnp.concatenate([w_rg, w_re, jnp.zeros((D, LANES - N_GROUPS - N_EXPERTS), F32)], axis=1)
    n_rows = -(-(2 * n_tok + N_EXPERTS * MOE_TM) // MOE_TM) * MOE_TM
    info, cnt, xs_zero = _router(x2, mod_f, norm_g, w_cat, T, n_rows)
    dest, meta = _rank(info, cnt)
    pad_end = meta[0, :N_EXPERTS].astype(jnp.int32)
    n_blocks = n_rows // MOE_TM
    blk_start = jnp.arange(n_blocks, dtype=jnp.int32) * MOE_TM
    blk_e = jnp.minimum(jnp.sum(pad_end[None, :] <= blk_start[:, None], axis=1), N_EXPERTS - 1).astype(jnp.int32)
    n_used = (pad_end[N_EXPERTS - 1:] // MOE_TM).astype(jnp.int32)
    dest3 = dest[:, :2].reshape(n_tok // ROW_TB, 1, 2 * ROW_TB)
    xs = _dispatch(dest3, x2, mod_f, norm_g, xs_zero, T)
    seg_end = jnp.sum(blk_e[None, :] <= blk_e[:, None], axis=1)
    next_e = jnp.where(seg_end < n_blocks, blk_e[jnp.minimum(seg_end, n_blocks - 1)], -1).astype(jnp.int32)
    ys = _experts(blk_e, n_used, next_e, xs, w_gate, w_up, w_down, layer)
    out = _combine(dest3, info, x2, gf, final_g, ys, T, final_norm)
    return out.reshape(B, T, D)


def kernel(x, c, ada_w, ada_b, norm_mix_g, norm_ffn_g, rw_mu, rw_w_rkv, rw_w0, rw_w1, rw_w2, rw_a0, rw_a1, rw_a2,
           rw_g1, rw_g2, rw_k_k, rw_k_a, rw_r_k, rw_gn_g, rw_gn_b, rw_w_o, ada_kv_w, ada_kv_b, norm_kv_g, w_kv,
           df_w_q, df_lq1, df_lk1, df_lq2, df_lk2, df_subln_g, df_w_o, moe_w_rg, moe_w_re, moe_w_gate, moe_w_up,
           moe_w_down, final_g):
    B, T, D = x.shape
    c_pad = jnp.zeros((SUBLANES, D), F32).at[:B].set(c)
    mod = _ada(c_pad, ada_w, ada_b, 6 * D // 4)[:, :B]
    mod_kv = _ada(c_pad, ada_kv_w[None], ada_kv_b[None], D)[0, :B]
    bf = lambda w: w.astype(BF16)
    row = lambda v: v.reshape(1, -1)

    for l in range(DEPTH):
        sh_m, sc_m, g_m, sh_f, sc_f, g_f = jnp.split(mod[l], 6, axis=-1)
        mod_m = jnp.stack([sh_m, sc_m], axis=1)
        mod_f = jnp.stack([sh_f, sc_f], axis=1)
        if l < N_A_LAYERS:
            i = l
            vec = jnp.stack([rw_w0[i], rw_a0[i], rw_k_k[i], rw_k_a[i]], axis=0)
            r, k, v, lw, kk, al, gate = _rwkv_proj(
                x, mod_m, row(norm_mix_g[l]), rw_mu[i], bf(rw_w_rkv[i]), bf(rw_w1[i]), bf(rw_w2[i]),
                bf(rw_a1[i]), bf(rw_a2[i]), bf(rw_g1[i]), bf(rw_g2[i]), vec)
            pvec = jnp.stack([rw_r_k[i].reshape(-1), rw_gn_g[i], rw_gn_b[i]], axis=0)
            y = _rwkv_scan(r, k, v, lw, kk, al, pvec)
            x = _proj_res(y, gate, x, g_m[:, None, :], bf(rw_w_o[i]))
        else:
            j = l - N_A_LAYERS
            q_proj = (mod_m, row(norm_mix_g[l]), bf(df_w_q[j]))
            if l == N_A_LAYERS:
                sh_kv, sc_kv = jnp.split(mod_kv, 2, axis=-1)
                q, kv = _norm_mm(x, [q_proj, (jnp.stack([sh_kv, sc_kv], axis=1), row(norm_kv_g), bf(w_kv))], BF16)
            else:
                q, = _norm_mm(x, [q_proj], BF16)
            lambda_init = 0.8 - 0.6 * math.exp(-0.3 * l)
            lam_vecs = jnp.stack([df_lq1[j], df_lk1[j], df_lq2[j], df_lk2[j]], axis=0)
            o = _diff_attn(q, kv, lam_vecs, row(df_subln_g[j]), lambda_init)
            x = _proj_res(o, None, x, g_m[:, None, :], bf(df_w_o[j]))
        x = _moe_layer(x, mod_f, g_f[:, None, :], row(norm_ffn_g[l]), moe_w_rg[l], moe_w_re[l], moe_w_gate,
                       moe_w_up, moe_w_down, l, row(final_g), final_norm=(l == DEPTH - 1))
    return x
```

```python
import functools
import math

import jax
import jax.numpy as jnp
from jax import lax
from jax.experimental import pallas as pl
from jax.experimental.pallas import tpu as pltpu

F32 = jnp.float32
BF16 = jnp.bfloat16

D_MODEL = 1024
DEPTH = 2
N_A_LAYERS = DEPTH // 2
RWKV_HEAD = 64
RWKV_HEADS = D_MODEL // RWKV_HEAD
RWKV_GN_EPS = 64e-5
DIFF_QK_DIM = 64
DIFF_V_DIM = 2 * DIFF_QK_DIM
DIFF_HEADS = D_MODEL // DIFF_V_DIM
SUBLN_EPS = 1e-5
N_GROUPS = 4
EXPERTS_PER_GROUP = 8
N_EXPERTS = N_GROUPS * EXPERTS_PER_GROUP
EXPERT_FF = 512
RMS_EPS = 1e-6

LANES = 128
SUBLANES = 8
VMEM_LIMIT_BYTES = 56 * 1024 * 1024

SCAN_CHUNK = 64
SCAN_BASE_BLOCK = 8
SCAN_CHUNKS_PER_STEP = 2
PAIR = 2 * RWKV_HEAD
PROJ_TM = 256
DENSE_TM = 512
ATT_TQ = 512
ATT_HB = 2
MOE_TM = 256
ROUTE_TM = 512
RANK_TB = 1024
ROW_TB = 256
assert SCAN_CHUNK == RWKV_HEAD and PAIR == LANES


def _cparams(*sem):
    return pltpu.CompilerParams(dimension_semantics=sem, vmem_limit_bytes=VMEM_LIMIT_BYTES)


def _dot(a, b):
    return jnp.dot(a.astype(BF16), b.astype(BF16), preferred_element_type=F32)


def _split(x):
    hi = x.astype(BF16)
    lo = (x - hi.astype(F32)).astype(BF16)
    return hi, lo


def _dot3(a, b):
    ah, al = _split(a)
    bh, bl = _split(b)
    d = functools.partial(jnp.dot, preferred_element_type=F32)
    return d(ah, bh) + d(ah, bl) + d(al, bh)


def _dot_hl(a, b_exact):
    ah, al = _split(a)
    d = functools.partial(jnp.dot, preferred_element_type=F32)
    return d(ah, b_exact) + d(al, b_exact)


def _rms_mod(x, g, shift, scale):
    ms = jnp.mean(x * x, axis=-1, keepdims=True)
    return (x * lax.rsqrt(ms + RMS_EPS) * g) * (1.0 + scale) + shift


def _ada_kernel(c_ref, w_ref, b_ref, o_ref):
    c = c_ref[...]
    ca = c * jax.nn.sigmoid(c)
    o_ref[...] = _dot3(ca, w_ref[...]) + b_ref[...]


def _ada(c_pad, w, b, tn):
    L, D, N = w.shape
    return pl.pallas_call(
        _ada_kernel,
        out_shape=jax.ShapeDtypeStruct((L, SUBLANES, N), F32),
        grid=(L, N // tn),
        in_specs=[
            pl.BlockSpec((SUBLANES, D), lambda l, j: (0, 0)),
            pl.BlockSpec((None, D, tn), lambda l, j: (l, 0, j)),
            pl.BlockSpec((None, 1, tn), lambda l, j: (l, 0, j)),
        ],
        out_specs=pl.BlockSpec((None, SUBLANES, tn), lambda l, j: (l, 0, j)),
        compiler_params=_cparams("arbitrary", "arbitrary"),
        name="ada_mod",
    )(c_pad, w, b.reshape(L, 1, N))


def _rwkv_proj_kernel(x_ref, xp_ref, mod_ref, g_ref, mu_ref, wrkv_ref, w1_ref, w2_ref, a1_ref, a2_ref,
                      g1_ref, g2_ref, vec_ref, r_ref, k_ref, v_ref, lw_ref, kk_ref, al_ref, gate_ref):
    i = pl.program_id(1)
    g = g_ref[...]
    shift, scale = mod_ref[0, 0:1, :], mod_ref[0, 1:2, :]
    h = _rms_mod(x_ref[0], g, shift, scale)
    hp = _rms_mod(xp_ref[0, SUBLANES - 1:SUBLANES, :], g, shift, scale)
    hp = jnp.where(i == 0, 0.0, hp)
    row = lax.broadcasted_iota(jnp.int32, h.shape, 0)
    h_prev = jnp.where(row == 0, hp, pltpu.roll(h, 1, axis=0))
    xx = h_prev - h
    mu = mu_ref[...]
    xs = [(h + xx * mu[j:j + 1, :]).astype(BF16) for j in range(6)]
    w0, a0, k_k, k_a = (vec_ref[j:j + 1, :] for j in range(4))
    d = functools.partial(jnp.dot, preferred_element_type=F32)
    r = d(xs[0], wrkv_ref[0])
    k = d(xs[1], wrkv_ref[1])
    v = d(xs[2], wrkv_ref[2])
    z = w0 + _dot(jnp.tanh(d(xs[3], w1_ref[...])), w2_ref[...])
    lw = (-math.exp(-0.5)) * jax.nn.sigmoid(z)
    a = jax.nn.sigmoid(a0 + _dot(d(xs[4], a1_ref[...]), a2_ref[...]))
    gate = _dot(jax.nn.sigmoid(d(xs[5], g1_ref[...])), g2_ref[...])
    r_ref[0] = r.astype(BF16)
    k_ref[0] = (k * (1.0 + (a - 1.0) * k_a)).astype(BF16)
    v_ref[0] = v.astype(BF16)
    lw_ref[0] = lw
    kk_ref[0] = (k * k_k).astype(BF16)
    al_ref[0] = a.astype(BF16)
    gate_ref[0] = gate.astype(BF16)


def _rwkv_proj(x, mod2, g, mu, wrkv, w1, w2, a1, a2, g1, g2, vec):
    B, T, D = x.shape
    tm = PROJ_TM
    const2 = lambda b, i: (0, 0)
    const3 = lambda b, i: (0, 0, 0)
    act = pl.BlockSpec((1, tm, D), lambda b, i: (b, i, 0))
    n_sub = tm // SUBLANES
    return pl.pallas_call(
        _rwkv_proj_kernel,
        out_shape=[jax.ShapeDtypeStruct((B, T, D), F32 if n == 3 else BF16) for n in range(7)],
        grid=(B, T // tm),
        in_specs=[
            act,
            pl.BlockSpec((1, SUBLANES, D), lambda b, i: (b, jnp.maximum(i * n_sub - 1, 0), 0)),
            pl.BlockSpec((1, 2, D), lambda b, i: (b, 0, 0)),
            pl.BlockSpec((1, D), const2),
            pl.BlockSpec((6, D), const2),
            pl.BlockSpec((3, D, D), const3),
            pl.BlockSpec(w1.shape, const2), pl.BlockSpec(w2.shape, const2),
            pl.BlockSpec(a1.shape, const2), pl.BlockSpec(a2.shape, const2),
            pl.BlockSpec(g1.shape, const2), pl.BlockSpec(g2.shape, const2),
            pl.BlockSpec((4, D), const2),
        ],
        out_specs=[act] * 7,
        compiler_params=_cparams("arbitrary", "arbitrary"),
        name="rwkv_proj",
    )(x, x, mod2, g, mu, wrkv, w1, w2, a1, a2, g1, g2, vec)


def _rwkv_scan_kernel(r_ref, k_ref, v_ref, lw_ref, kk_ref, al_ref, pv_ref, y_ref, h_ref):
    C = SCAN_CHUNK
    P2 = 2 * C

    @pl.when(pl.program_id(1) == 0)
    def _():
        h_ref[...] = jnp.zeros_like(h_ref)

    lane = lax.broadcasted_iota(jnp.int32, (1, PAIR), 1)
    m_left = (lane < RWKV_HEAD).astype(F32)
    m_right = 1.0 - m_left
    ri = lax.broadcasted_iota(jnp.int32, (P2, P2), 0)
    ci = lax.broadcasted_iota(jnp.int32, (P2, P2), 1)
    same = (ri >= C) == (ci >= C)
    strict = same & (ri > ci)
    incl = same & (ri >= ci)
    eye = ri == ci
    block_ones = same.astype(BF16)
    tri = (lax.broadcasted_iota(jnp.int32, (C, C), 0) >= lax.broadcasted_iota(jnp.int32, (C, C), 1)).astype(BF16)

    def stack(x):
        return jnp.concatenate([x * m_left, x * m_right], axis=0)

    def head_sums(x):
        s_left = jnp.sum(x * m_left, axis=-1, keepdims=True)
        s_right = jnp.sum(x * m_right, axis=-1, keepdims=True)
        return jnp.where(lane < RWKV_HEAD, s_left, s_right)

    inv_n = 1.0 / RWKV_HEAD
    dd = functools.partial(jnp.dot, preferred_element_type=F32)
    n_pairs = RWKV_HEADS // 2
    units = [(ch, p) for ch in range(SCAN_CHUNKS_PER_STEP) for p in range(n_pairs)]
    idx = [(slice(ch * C, (ch + 1) * C), slice(p * PAIR, (p + 1) * PAIR)) for ch, p in units]
    U = range(len(units))
    ld = lambda ref, rs, sl: ref[0, rs, sl].astype(F32)
    kkr = [ld(kk_ref, rs, sl) for rs, sl in idx]
    ss = [head_sums(x * x) for x in kkr]
    lws = [lw_ref[0, rs, sl] for rs, sl in idx]
    Ls = []
    for lw in lws:
        l_hi, l_lo = _split(lw)
        cs = dd(tri, jnp.concatenate([l_hi, l_lo], axis=1))
        Ls.append(cs[:, :PAIR] + cs[:, PAIR:])
    lhs_g, rhs_g, bk_hat, vs, at32, rt32, dec_end = [], [], [], [], [], [], []
    for u in U:
        rs, sl = idx[u]
        L, lw = Ls[u], lws[u]
        kk = kkr[u] * lax.rsqrt(jnp.maximum(ss[u], 1e-24))
        b_vec = kk * ld(al_ref, rs, sl)
        k = ld(k_ref, rs, sl)
        LC = L[C - 1:C, :]
        e_neg = jnp.exp(-L)
        e_end = jnp.exp(LC - L)
        At = stack(-kk * jnp.exp(L - lw))
        Rt = stack(ld(r_ref, rs, sl) * jnp.exp(L))
        at32.append(At)
        rt32.append(Rt)
        lhs_g.append(jnp.concatenate([At, Rt], axis=0).astype(BF16))
        rhs_g.append(jnp.concatenate([stack(b_vec * e_neg), stack(k * e_neg)], axis=0).astype(BF16))
        bk_hat.append(jnp.concatenate([stack(b_vec * e_end), stack(k * e_end)], axis=0))
        vs.append(stack(ld(v_ref, rs, sl)).astype(BF16))
        dec_end.append(jnp.exp(LC))
    G = [lax.dot_general(lhs_g[u], rhs_g[u], (((1,), (1,)), ((), ())), preferred_element_type=F32) for u in U]
    A_ak = [jnp.where(strict, G[u][:P2, P2:], 0.0).astype(BF16) for u in U]
    A_r = [jnp.concatenate([jnp.where(incl, G[u][P2:, :P2], 0.0), jnp.where(incl, G[u][P2:, P2:], 0.0)],
                           axis=1).astype(BF16) for u in U]
    W = [dd(A_ak[u], vs[u]) for u in U]
    bsz = lambda b: (ri >> int(math.log2(b))) == (ci >> int(math.log2(b)))
    b8 = bsz(SCAN_BASE_BLOCK)
    D1 = [jnp.where(strict & b8, G[u][:P2, :P2], 0.0).astype(BF16) for u in U]
    D2 = [dd(D1[u], D1[u]).astype(BF16) for u in U]
    D4 = [dd(D2[u], D2[u]).astype(BF16) for u in U]
    eye_f = eye.astype(F32)
    P1 = [eye_f + D1[u].astype(F32) + D2[u].astype(F32) + dd(D1[u], D2[u]) for u in U]
    Tm = [P1[u] + dd(P1[u].astype(BF16), D4[u]) for u in U]
    blk = SCAN_BASE_BLOCK
    while blk < C:
        off = strict & bsz(2 * blk) & ~bsz(blk)
        Mo = [jnp.where(off, G[u][:P2, :P2], 0.0).astype(BF16) for u in U]
        Tb = [Tm[u].astype(BF16) for u in U]
        TM = [dd(Tb[u], Mo[u]).astype(BF16) for u in U]
        Tm = [Tm[u] + dd(TM[u], Tb[u]) for u in U]
        blk *= 2
    Z = [dd(Tm[u].astype(BF16), jnp.concatenate([at32[u], W[u]], axis=1).astype(BF16)) for u in U]
    rhs = [jnp.concatenate([Z[u].astype(BF16), jnp.concatenate([jnp.zeros_like(vs[u]), vs[u]], axis=1)], axis=0)
           for u in U]
    o6 = [dd(A_r[u], rhs[u]) for u in U]
    o7 = [dd(bk_hat[u].T.astype(BF16), rhs[u]) for u in U]
    H = [h_ref[p] for p in range(n_pairs)]
    Y = [None] * len(units)
    for u in U:
        p = units[u][1]
        Hb = H[p].astype(BF16)
        Y[u] = dd((rt32[u] + o6[u][:, :PAIR]).astype(BF16), Hb) + o6[u][:, PAIR:]
        Mbd = o7[u][:, :PAIR] + jnp.where(eye, dec_end[u], 0.0)
        H[p] = dd(Mbd.astype(BF16), Hb) + o7[u][:, PAIR:]
    for p in range(n_pairs):
        h_ref[p] = H[p]
    ys = [Y[u][:C] + Y[u][C:] for u in U]
    rk = [ld(r_ref, rs, sl) * ld(k_ref, rs, sl) * pv_ref[0:1, sl] for rs, sl in idx]
    st1 = [head_sums(jnp.concatenate([ys[u], rk[u]], axis=0)) for u in U]
    yc = [ys[u] - st1[u][:C] * inv_n for u in U]
    var = [head_sums(yc[u] * yc[u]) * inv_n for u in U]
    for u in U:
        rs, sl = idx[u]
        bonus = st1[u][C:] * ld(v_ref, rs, sl)
        y_ref[0, rs, sl] = (yc[u] * lax.rsqrt(var[u] + RWKV_GN_EPS) * pv_ref[1:2, sl] + pv_ref[2:3, sl]
                            + bonus).astype(y_ref.dtype)


def _rwkv_scan(r, k, v, lw, kk, al, pvec):
    B, T, D = r.shape
    rows = SCAN_CHUNK * SCAN_CHUNKS_PER_STEP
    act = pl.BlockSpec((1, rows, D), lambda b, c: (b, c, 0))
    return pl.pallas_call(
        _rwkv_scan_kernel,
        out_shape=jax.ShapeDtypeStruct((B, T, D), BF16),
        grid=(B, T // rows),
        in_specs=[act] * 6 + [pl.BlockSpec((3, D), lambda b, c: (0, 0))],
        out_specs=act,
        scratch_shapes=[pltpu.VMEM((RWKV_HEADS // 2, PAIR, PAIR), F32)],
        compiler_params=_cparams("arbitrary", "arbitrary"),
        name="rwkv_scan",
    )(r, k, v, lw, kk, al, pvec)


def _proj_res_kernel(*refs, has_gate):
    if has_gate:
        y_ref, g_ref, x_ref, gm_ref, w_ref, o_ref = refs
        y = y_ref[0].astype(F32) * g_ref[0].astype(F32)
    else:
        y_ref, x_ref, gm_ref, w_ref, o_ref = refs
        y = y_ref[0]
    o_ref[0] = x_ref[0] + gm_ref[0] * jnp.dot(y.astype(BF16), w_ref[...], preferred_element_type=F32)


def _proj_res(y, g, x, gm, w):
    B, T, D = x.shape
    tm = DENSE_TM
    act = pl.BlockSpec((1, tm, D), lambda b, i: (b, i, 0))
    ins = [y] + ([g] if g is not None else []) + [x, gm, w]
    specs = [act] * (len(ins) - 2) + [pl.BlockSpec((1, 1, D), lambda b, i: (b, 0, 0)),
                                     pl.BlockSpec((D, D), lambda b, i: (0, 0))]
    return pl.pallas_call(
        functools.partial(_proj_res_kernel, has_gate=g is not None),
        out_shape=jax.ShapeDtypeStruct((B, T, D), F32),
        grid=(B, T // tm),
        in_specs=specs,
        out_specs=act,
        compiler_params=_cparams("arbitrary", "arbitrary"),
        name="proj_res",
    )(*ins)


def _norm_mm_kernel(*refs, n_proj):
    x_ref = refs[0]
    ins, outs = refs[1:1 + 3 * n_proj], refs[1 + 3 * n_proj:]
    x = x_ref[0]
    xn = x * lax.rsqrt(jnp.mean(x * x, axis=-1, keepdims=True) + RMS_EPS)
    for p in range(n_proj):
        mod_ref, g_ref, w_ref = ins[3 * p:3 * p + 3]
        h = (xn * g_ref[...]) * (1.0 + mod_ref[0, 1:2, :]) + mod_ref[0, 0:1, :]
        outs[p][0] = jnp.dot(h.astype(BF16), w_ref[...], preferred_element_type=F32).astype(outs[p].dtype)


def _norm_mm(x, projs, out_dtype):
    B, T, D = x.shape
    tm = DENSE_TM
    in_specs = [pl.BlockSpec((1, tm, D), lambda b, i: (b, i, 0))]
    args = [x]
    for mod2, g, w in projs:
        in_specs += [pl.BlockSpec((1, 2, D), lambda b, i: (b, 0, 0)), pl.BlockSpec((1, D), lambda b, i: (0, 0)),
                     pl.BlockSpec(w.shape, lambda b, i: (0, 0))]
        args += [mod2, g, w]
    return pl.pallas_call(
        functools.partial(_norm_mm_kernel, n_proj=len(projs)),
        out_shape=[jax.ShapeDtypeStruct((B, T, w.shape[1]), out_dtype) for _, _, w in projs],
        grid=(B, T // tm),
        in_specs=in_specs,
        out_specs=[pl.BlockSpec((1, tm, w.shape[1]), lambda b, i: (b, i, 0)) for _, _, w in projs],
        compiler_params=_cparams("arbitrary", "arbitrary"),
        name="norm_mm",
    )(*args)


def _diff_attn_kernel(q_ref, k_ref, v_ref, lam_ref, sg_ref, o_ref, m_ref, acc_ref, s_ref, *, lambda_init):
    tq, HB, dv = ATT_TQ, ATT_HB, DIFF_V_DIM
    qi = pl.program_id(2)
    heads = range(HB)
    hs = [slice(h * dv, (h + 1) * dv) for h in heads]
    lane = lax.broadcasted_iota(jnp.int32, (1, dv), 1)
    m_left = (lane < DIFF_QK_DIM).astype(F32)
    qs = []
    for h in heads:
        q = q_ref[0, :, hs[h]].astype(F32) * (DIFF_QK_DIM ** -0.5 * math.log2(math.e))
        qs.append(jnp.concatenate([q * m_left, q * (1.0 - m_left)], axis=0).astype(BF16))
    ones_col = jnp.ones((tq, dv), BF16)
    causal = (lax.broadcasted_iota(jnp.int32, (2 * tq, tq), 1)
              <= lax.broadcasted_iota(jnp.int32, (2 * tq, tq), 0) % tq)

    def key_rows(j):
        return pl.ds(pl.multiple_of(j * tq, tq), tq)

    def scores_into(slot, j):
        for h in heads:
            s_ref[slot, h] = lax.dot_general(qs[h], k_ref[0, key_rows(j), hs[h]], (((1,), (1,)), ((), ())),
                                             preferred_element_type=F32)

    def block(slot, j, first, prefetch):
        if prefetch is not None:
            scores_into(1 - slot, prefetch)
        for h in heads:
            s = s_ref[slot, h]
            if first:
                s = jnp.where(causal, s, -jnp.inf)
                m_new = jnp.broadcast_to(jnp.max(s, axis=-1, keepdims=True), (2 * tq, LANES))
            else:
                m_old = m_ref[h]
                m_new = jnp.maximum(m_old, jnp.max(s, axis=-1, keepdims=True))
            m_ref[h] = m_new
            p = jnp.exp2(s - jnp.concatenate([m_new] * (tq // LANES), axis=1)).astype(BF16)
            pv = jnp.dot(p, jnp.concatenate([v_ref[0, key_rows(j), hs[h]], ones_col], axis=1),
                         preferred_element_type=F32)
            if first:
                acc_ref[h] = pv
            else:
                alpha = jnp.exp2(m_old - m_new)
                acc_ref[h] = acc_ref[h] * jnp.concatenate([alpha, alpha], axis=1) + pv

    scores_into(0, qi)
    block(0, qi, True, 0)

    def pair(u, c):
        last = jnp.maximum(qi - 1, 0)
        block(1, 2 * u, False, jnp.minimum(2 * u + 1, last))
        block(0, 2 * u + 1, False, jnp.minimum(2 * u + 2, last))
        return c

    lax.fori_loop(0, qi // 2, pair, 0)

    @pl.when(qi % 2 == 1)
    def _():
        block(1, qi - 1, False, None)

    lv = lam_ref[...]
    lam = (jnp.exp(jnp.sum(lv[0:1] * lv[1:2], axis=-1, keepdims=True))
           - jnp.exp(jnp.sum(lv[2:3] * lv[3:4], axis=-1, keepdims=True)) + lambda_init)
    for h in heads:
        acc = acc_ref[h]
        o = acc[:, :dv] / acc[:, dv:]
        o = o[:tq] - lam * o[tq:]
        ms = jnp.mean(o * o, axis=-1, keepdims=True)
        o_ref[0, :, hs[h]] = (o * lax.rsqrt(ms + SUBLN_EPS) * sg_ref[...] * (1.0 - lambda_init)).astype(o_ref.dtype)


def _diff_attn(q, kv, lam_vecs, subln_g, lambda_init):
    B, T, D = q.shape
    tq, HB = ATT_TQ, ATT_HB
    n_hb = DIFF_HEADS // HB
    w = HB * DIFF_V_DIM
    return pl.pallas_call(
        functools.partial(_diff_attn_kernel, lambda_init=lambda_init),
        out_shape=jax.ShapeDtypeStruct((B, T, D), BF16),
        grid=(B, n_hb, T // tq),
        in_specs=[
            pl.BlockSpec((1, tq, w), lambda b, h, i: (b, i, h)),
            pl.BlockSpec((1, T, w), lambda b, h, i: (b, 0, h)),
            pl.BlockSpec((1, T, w), lambda b, h, i: (b, 0, n_hb + h)),
            pl.BlockSpec((4, DIFF_QK_DIM), lambda b, h, i: (0, 0)),
            pl.BlockSpec((1, DIFF_V_DIM), lambda b, h, i: (0, 0)),
        ],
        out_specs=pl.BlockSpec((1, tq, w), lambda b, h, i: (b, i, h)),
        scratch_shapes=[pltpu.VMEM((HB, 2 * tq, LANES), F32), pltpu.VMEM((HB, 2 * tq, 2 * DIFF_V_DIM), F32),
                        pltpu.VMEM((2, HB, 2 * tq, tq), F32)],
        compiler_params=_cparams("arbitrary", "arbitrary", "arbitrary"),
        name="diff_attn",
    )(q, kv, kv, lam_vecs, subln_g)


def _router_kernel(x_ref, mod_ref, g_ref, w_ref, info_ref, cnt_ref, zero_ref):
    zero_ref[...] = jnp.zeros_like(zero_ref)
    h = _rms_mod(x_ref[...], g_ref[...], mod_ref[0, 0:1, :], mod_ref[0, 1:2, :])
    logit = _dot3(h, w_ref[...])
    lane_i = lax.broadcasted_iota(jnp.int32, logit.shape, 1)
    lane = lane_i.astype(F32)
    neg = -jnp.inf
    big = float(LANES)
    is_grp = lane_i < N_GROUPS
    gl = jnp.where(is_grp, logit, neg)
    gmax = jnp.max(gl, axis=-1, keepdims=True)
    gidx = jnp.min(jnp.where(gl == gmax, lane, big), axis=-1, keepdims=True)
    grp_gate = 1.0 / jnp.sum(jnp.where(is_grp, jnp.exp(logit - gmax), 0.0), axis=-1, keepdims=True)
    lo = N_GROUPS + gidx * EXPERTS_PER_GROUP
    in_grp = (lane >= lo) & (lane < lo + EXPERTS_PER_GROUP)
    el = jnp.where(in_grp, logit, neg)
    t1 = jnp.max(el, axis=-1, keepdims=True)
    i1 = jnp.min(jnp.where(el == t1, lane, big), axis=-1, keepdims=True)
    el2 = jnp.where(lane == i1, neg, el)
    t2 = jnp.max(el2, axis=-1, keepdims=True)
    i2 = jnp.min(jnp.where(el2 == t2, lane, big), axis=-1, keepdims=True)
    e21 = jnp.exp(t2 - t1)
    p1 = 1.0 / (1.0 + e21)
    w1 = grp_gate * p1
    w2 = grp_gate * (e21 * p1)
    e1 = i1 - N_GROUPS
    e2 = i2 - N_GROUPS
    info_ref[...] = jnp.where(lane_i == 0, e1, jnp.where(lane_i == 1, e2, jnp.where(lane_i == 2, w1, jnp.where(lane_i == 3, w2, 0.0))))

    @pl.when(pl.program_id(0) == 0)
    def _():
        cnt_ref[...] = jnp.zeros_like(cnt_ref)

    picked = ((lane == e1) | (lane == e2)).astype(F32)
    cnt_ref[...] += jnp.broadcast_to(jnp.sum(picked, axis=0, keepdims=True), cnt_ref.shape)


def _router(x2, mod2, g, w_cat, T, n_rows):
    n_tok, D = x2.shape
    tm = ROUTE_TM
    per_b = T // tm
    n_blk = n_tok // tm
    zrows = n_rows * SUBLANES // n_blk
    assert zrows * n_blk == n_rows * SUBLANES and zrows % SUBLANES == 0
    return pl.pallas_call(
        _router_kernel,
        out_shape=[jax.ShapeDtypeStruct((n_tok, LANES), F32), jax.ShapeDtypeStruct((SUBLANES, LANES), F32),
                   jax.ShapeDtypeStruct((n_rows * SUBLANES, LANES), F32)],
        grid=(n_blk,),
        in_specs=[
            pl.BlockSpec((tm, D), lambda i: (i, 0)),
            pl.BlockSpec((1, 2, D), lambda i: (i // per_b, 0, 0)),
            pl.BlockSpec((1, D), lambda i: (0, 0)),
            pl.BlockSpec((D, LANES), lambda i: (0, 0)),
        ],
        out_specs=[pl.BlockSpec((tm, LANES), lambda i: (i, 0)), pl.BlockSpec((SUBLANES, LANES), lambda i: (0, 0)),
                   pl.BlockSpec((zrows, LANES), lambda i: (i, 0))],
        compiler_params=_cparams("arbitrary"),
        name="moe_router",
    )(x2, mod2, g, w_cat)


def _rank_kernel(info_ref, cnt_ref, dest_ref, meta_ref, start_ref):
    i = pl.program_id(0)
    tb = info_ref.shape[0]
    lane = lax.broadcasted_iota(jnp.int32, (tb, LANES), 1)
    info = info_ref[...]
    e0 = info[:, 0:1].astype(jnp.int32)
    e1 = info[:, 1:2].astype(jnp.int32)
    o0 = (lane == e0).astype(F32)
    o1 = (lane == e1).astype(F32)
    both = o0 + o1

    @pl.when(i == 0)
    def _():
        cnt = cnt_ref[0:1, :]
        padded = jnp.floor((cnt + (MOE_TM - 1)) * (1.0 / MOE_TM)) * MOE_TM
        r = lax.broadcasted_iota(jnp.int32, (LANES, LANES), 0)
        c = lax.broadcasted_iota(jnp.int32, (LANES, LANES), 1)
        upper_strict = (r < c).astype(BF16)
        start = _dot_hl(jnp.broadcast_to(padded, (SUBLANES, LANES)), upper_strict)[0:1]
        start_ref[...] = start
        meta_ref[...] = jnp.broadcast_to(start + padded, (SUBLANES, LANES))

    r = lax.broadcasted_iota(jnp.int32, (tb, tb), 0)
    c = lax.broadcasted_iota(jnp.int32, (tb, tb), 1)
    lower_strict = (r > c).astype(BF16)
    before = jnp.dot(lower_strict, both.astype(BF16), preferred_element_type=F32) + start_ref[...]
    d0 = jnp.sum(o0 * before, axis=-1, keepdims=True)
    d1 = jnp.sum(o1 * before, axis=-1, keepdims=True)
    dest_ref[...] = jnp.where(lane == 0, d0, jnp.where(lane == 1, d1, 0.0)).astype(jnp.int32)
    start_ref[...] += jnp.sum(both, axis=0, keepdims=True)


def _rank(info, cnt):
    n_tok = info.shape[0]
    tb = RANK_TB
    n_blk = n_tok // tb
    return pl.pallas_call(
        _rank_kernel,
        out_shape=[jax.ShapeDtypeStruct((n_tok, LANES), jnp.int32), jax.ShapeDtypeStruct((SUBLANES, LANES), F32)],
        grid=(n_blk,),
        in_specs=[pl.BlockSpec((tb, LANES), lambda i: (i, 0)), pl.BlockSpec((SUBLANES, LANES), lambda i: (0, 0))],
        out_specs=[pl.BlockSpec((tb, LANES), lambda i: (i, 0)), pl.BlockSpec((SUBLANES, LANES), lambda i: (0, 0))],
        scratch_shapes=[pltpu.VMEM((1, LANES), F32)],
        compiler_params=_cparams("arbitrary"),
        name="moe_rank",
    )(info, cnt)


assert D_MODEL == SUBLANES * LANES


def _tile_rows_store(ref, x):
    n = x.shape[0]
    for s in range(SUBLANES):
        ref[pl.ds(s, n, stride=SUBLANES), :] = x[:, s * LANES:(s + 1) * LANES]


def _tile_rows_load(ref, n):
    return jnp.concatenate([ref[pl.ds(s, n, stride=SUBLANES), :] for s in range(SUBLANES)], axis=1)


def _row_copy(src_ref, s, dst_ref, d, sem):
    rows = lambda r: pl.ds(pl.multiple_of(r * SUBLANES, SUBLANES), SUBLANES)
    return pltpu.make_async_copy(src_ref.at[rows(s)], dst_ref.at[rows(d)], sem)


def _dispatch_kernel(dest_ref, x_ref, mod_ref, g_ref, xs_in_ref, xs_ref, h_ref, sem):
    del xs_in_ref
    tb = x_ref.shape[0]
    i = pl.program_id(0)
    last = pl.num_programs(0) - 1
    h = _rms_mod(x_ref[...], g_ref[...], mod_ref[0, 0:1, :], mod_ref[0, 1:2, :])

    def drain(slot):
        def wait(j, c):
            _row_copy(h_ref.at[slot], 0, xs_ref, 0, sem.at[slot]).wait()
            return c
        lax.fori_loop(0, 2 * tb, wait, 0, unroll=8)

    for slot in range(2):
        @pl.when(i % 2 == slot)
        def _(slot=slot):
            @pl.when(i >= 2)
            def _():
                drain(slot)

            _tile_rows_store(h_ref.at[slot], h)

            def start(j, c):
                _row_copy(h_ref.at[slot], j, xs_ref, dest_ref[0, 0, 2 * j], sem.at[slot]).start(priority=0)
                _row_copy(h_ref.at[slot], j, xs_ref, dest_ref[0, 0, 2 * j + 1], sem.at[slot]).start(priority=1)
                return c

            lax.fori_loop(0, tb, start, 0, unroll=8)

            @pl.when(i == last)
            def _():
                drain(slot)

                @pl.when(i >= 1)
                def _():
                    drain(1 - slot)


def _dispatch(dest3, x2, mod2, g, xs_zero, T):
    n_tok, D = x2.shape
    tb = ROW_TB
    per_b = T // tb
    return pl.pallas_call(
        _dispatch_kernel,
        out_shape=jax.ShapeDtypeStruct(xs_zero.shape, F32),
        grid=(n_tok // tb,),
        in_specs=[
            pl.BlockSpec((1, 1, 2 * tb), lambda i: (i, 0, 0), memory_space=pltpu.SMEM),
            pl.BlockSpec((tb, D), lambda i: (i, 0)),
            pl.BlockSpec((1, 2, D), lambda i: (i // per_b, 0, 0)),
            pl.BlockSpec((1, D), lambda i: (0, 0)),
            pl.BlockSpec(memory_space=pl.ANY),
        ],
        out_specs=pl.BlockSpec(memory_space=pl.ANY),
        scratch_shapes=[pltpu.VMEM((2, tb * SUBLANES, LANES), F32), pltpu.SemaphoreType.DMA((2,))],
        input_output_aliases={4: 0},
        compiler_params=_cparams("arbitrary"),
        name="moe_dispatch",
    )(dest3, x2, mod2, g, xs_zero)


def _expert_kernel(be_ref, nb_ref, nxt_ref, xs_ref, wg_hbm, wu_hbm, wd_hbm, ys_ref, stage_g, stage_u, stage_d,
                   wgb, wub, wdb, sem, *, layer):
    i = pl.program_id(0)
    e = be_ref[i]
    changed = (i == 0) | (e != be_ref[jnp.maximum(i - 1, 0)])

    def fetch(ex):
        return (pltpu.make_async_copy(wg_hbm.at[layer, ex], stage_g, sem.at[0]),
                pltpu.make_async_copy(wu_hbm.at[layer, ex], stage_u, sem.at[1]),
                pltpu.make_async_copy(wd_hbm.at[layer, ex], stage_d, sem.at[2]))

    @pl.when(i == 0)
    def _():
        for cp in fetch(e):
            cp.start()

    @pl.when(changed)
    def _():
        for cp in fetch(e):
            cp.wait()
        wgb[...] = stage_g[...].astype(BF16)
        wub[...] = stage_u[...].astype(BF16)
        wdb[...] = stage_d[...].astype(BF16)

        @pl.when(nxt_ref[i] >= 0)
        def _():
            for cp in fetch(nxt_ref[i]):
                cp.start()

    @pl.when(i < nb_ref[0])
    def _():
        x = _tile_rows_load(xs_ref, MOE_TM).astype(BF16)
        a = jnp.dot(x, wgb[...], preferred_element_type=F32)
        u = jnp.dot(x, wub[...], preferred_element_type=F32)
        hdn = (a * jax.nn.sigmoid(a)) * u
        _tile_rows_store(ys_ref, jnp.dot(hdn.astype(BF16), wdb[...], preferred_element_type=F32))

    @pl.when(i >= nb_ref[0])
    def _():
        ys_ref[...] = jnp.zeros_like(ys_ref)


def _experts(blk_e, n_used, next_e, xs, w_gate, w_up, w_down, layer):
    D = D_MODEL
    n_rows = xs.shape[0] // SUBLANES
    tm = MOE_TM
    FF = EXPERT_FF
    row_block = pl.BlockSpec((tm * SUBLANES, LANES), lambda i, be, nb, nx: (i, 0))
    hbm = pl.BlockSpec(memory_space=pl.ANY)
    grid_spec = pltpu.PrefetchScalarGridSpec(
        num_scalar_prefetch=3,
        grid=(n_rows // tm,),
        in_specs=[row_block, hbm, hbm, hbm],
        out_specs=row_block,
        scratch_shapes=[pltpu.VMEM((D, FF), F32), pltpu.VMEM((D, FF), F32), pltpu.VMEM((FF, D), F32),
                        pltpu.VMEM((D, FF), BF16), pltpu.VMEM((D, FF), BF16), pltpu.VMEM((FF, D), BF16),
                        pltpu.SemaphoreType.DMA((3,))],
    )
    return pl.pallas_call(
        functools.partial(_expert_kernel, layer=layer),
        out_shape=jax.ShapeDtypeStruct(xs.shape, F32),
        grid_spec=grid_spec,
        compiler_params=_cparams("arbitrary"),
        name="moe_experts",
    )(blk_e, n_used, next_e, xs, w_gate, w_up, w_down)


def _combine_kernel(dest_ref, dest_next_ref, info_ref, x_ref, gf_ref, fg_ref, ys_ref, o_ref, y_ref, sem, *,
                    final_norm):
    tb = x_ref.shape[0]
    i = pl.program_id(0)
    last = pl.num_programs(0) - 1

    def gather(d_ref, slot):
        def start(j, c):
            _row_copy(ys_ref, d_ref[0, 0, 2 * j], y_ref.at[slot, 0], j, sem.at[slot]).start(priority=0)
            _row_copy(ys_ref, d_ref[0, 0, 2 * j + 1], y_ref.at[slot, 1], j, sem.at[slot]).start(priority=1)
            return c
        lax.fori_loop(0, tb, start, 0, unroll=8)

    @pl.when(i == 0)
    def _():
        gather(dest_ref, 0)

    for slot in range(2):
        @pl.when(i % 2 == slot)
        def _(slot=slot):
            @pl.when(i < last)
            def _():
                gather(dest_next_ref, 1 - slot)

            def wait(j, c):
                _row_copy(ys_ref, 0, y_ref.at[slot, 0], 0, sem.at[slot]).wait()
                return c

            lax.fori_loop(0, 2 * tb, wait, 0, unroll=8)
            info = info_ref[...]
            moe = (info[:, 2:3] * _tile_rows_load(y_ref.at[slot, 0], tb)
                   + info[:, 3:4] * _tile_rows_load(y_ref.at[slot, 1], tb))
            out = x_ref[...] + gf_ref[0] * moe
            if final_norm:
                ms = jnp.mean(out * out, axis=-1, keepdims=True)
                out = out * lax.rsqrt(ms + RMS_EPS) * fg_ref[...]
            o_ref[...] = out


def _combine(dest3, info, x2, gf, final_g, ys, T, final_norm):
    n_tok, D = x2.shape
    tb = ROW_TB
    per_b = T // tb
    n_steps = n_tok // tb
    return pl.pallas_call(
        functools.partial(_combine_kernel, final_norm=final_norm),
        out_shape=jax.ShapeDtypeStruct((n_tok, D), F32),
        grid=(n_steps,),
        in_specs=[
            pl.BlockSpec((1, 1, 2 * tb), lambda i: (i, 0, 0), memory_space=pltpu.SMEM),
            pl.BlockSpec((1, 1, 2 * tb), lambda i: (jnp.minimum(i + 1, n_steps - 1), 0, 0), memory_space=pltpu.SMEM),
            pl.BlockSpec((tb, LANES), lambda i: (i, 0)),
            pl.BlockSpec((tb, D), lambda i: (i, 0)),
            pl.BlockSpec((1, 1, D), lambda i: (i // per_b, 0, 0)),
            pl.BlockSpec((1, D), lambda i: (0, 0)),
            pl.BlockSpec(memory_space=pl.ANY),
        ],
        out_specs=pl.BlockSpec((tb, D), lambda i: (i, 0)),
        scratch_shapes=[pltpu.VMEM((2, 2, tb * SUBLANES, LANES), F32), pltpu.SemaphoreType.DMA((2,))],
        compiler_params=_cparams("arbitrary"),
        name="moe_combine",
    )(dest3, dest3, info, x2, gf, final_g, ys)


def _moe_layer(x, mod_f, gf, norm_g, w_rg, w_re, w_gate, w_up, w_down, layer, final_g, final_norm):
    B, T, D = x.shape
    n_tok = B * T
    x2 = x.reshape(n_tok, D)
    w_cat = jnp.concatenate([w_rg, w_re, jnp.zeros((D, LANES - N_GROUPS - N_EXPERTS), F32)], axis=1)
    n_rows = -(-(2 * n_tok + N_EXPERTS * MOE_TM) // MOE_TM) * MOE_TM
    info, cnt, xs_zero = _router(x2, mod_f, norm_g, w_cat, T, n_rows)
    dest, meta = _rank(info, cnt)
    pad_end = meta[0, :N_EXPERTS].astype(jnp.int32)
    n_blocks = n_rows // MOE_TM
    blk_start = jnp.arange(n_blocks, dtype=jnp.int32) * MOE_TM
    blk_e = jnp.minimum(jnp.sum(pad_end[None, :] <= blk_start[:, None], axis=1), N_EXPERTS - 1).astype(jnp.int32)
    n_used = (pad_end[N_EXPERTS - 1:] // MOE_TM).astype(jnp.int32)
    dest3 = dest[:, :2].reshape(n_tok // ROW_TB, 1, 2 * ROW_TB)
    xs = _dispatch(dest3, x2, mod_f, norm_g, xs_zero, T)
    seg_end = jnp.sum(blk_e[None, :] <= blk_e[:, None], axis=1)
    next_e = jnp.where(seg_end < n_blocks, blk_e[jnp.minimum(seg_end, n_blocks - 1)], -1).astype(jnp.int32)
    ys = _experts(blk_e, n_used, next_e, xs, w_gate, w_up, w_down, layer)
    out = _combine(dest3, info, x2, gf, final_g, ys, T, final_norm)
    return out.reshape(B, T, D)


def kernel(x, c, ada_w, ada_b, norm_mix_g, norm_ffn_g, rw_mu, rw_w_rkv, rw_w0, rw_w1, rw_w2, rw_a0, rw_a1, rw_a2,
           rw_g1, rw_g2, rw_k_k, rw_k_a, rw_r_k, rw_gn_g, rw_gn_b, rw_w_o, ada_kv_w, ada_kv_b, norm_kv_g, w_kv,
           df_w_q, df_lq1, df_lk1, df_lq2, df_lk2, df_subln_g, df_w_o, moe_w_rg, moe_w_re, moe_w_gate, moe_w_up,
           moe_w_down, final_g):
    B, T, D = x.shape
    c_pad = jnp.zeros((SUBLANES, D), F32).at[:B].set(c)
    mod = _ada(c_pad, ada_w, ada_b, 6 * D // 4)[:, :B]
    mod_kv = _ada(c_pad, ada_kv_w[None], ada_kv_b[None], D)[0, :B]
    bf = lambda w: w.astype(BF16)
    row = lambda v: v.reshape(1, -1)

    for l in range(DEPTH):
        sh_m, sc_m, g_m, sh_f, sc_f, g_f = jnp.split(mod[l], 6, axis=-1)
        mod_m = jnp.stack([sh_m, sc_m], axis=1)
        mod_f = jnp.stack([sh_f, sc_f], axis=1)
        if l < N_A_LAYERS:
            i = l
            vec = jnp.stack([rw_w0[i], rw_a0[i], rw_k_k[i], rw_k_a[i]], axis=0)
            r, k, v, lw, kk, al, gate = _rwkv_proj(
                x, mod_m, row(norm_mix_g[l]), rw_mu[i], bf(rw_w_rkv[i]), bf(rw_w1[i]), bf(rw_w2[i]),
                bf(rw_a1[i]), bf(rw_a2[i]), bf(rw_g1[i]), bf(rw_g2[i]), vec)
            pvec = jnp.stack([rw_r_k[i].reshape(-1), rw_gn_g[i], rw_gn_b[i]], axis=0)
            y = _rwkv_scan(r, k, v, lw, kk, al, pvec)
            x = _proj_res(y, gate, x, g_m[:, None, :], bf(rw_w_o[i]))
        else:
            j = l - N_A_LAYERS
            q_proj = (mod_m, row(norm_mix_g[l]), bf(df_w_q[j]))
            if l == N_A_LAYERS:
                sh_kv, sc_kv = jnp.split(mod_kv, 2, axis=-1)
                q, kv = _norm_mm(x, [q_proj, (jnp.stack([sh_kv, sc_kv], axis=1), row(norm_kv_g), bf(w_kv))], BF16)
            else:
                q, = _norm_mm(x, [q_proj], BF16)
            lambda_init = 0.8 - 0.6 * math.exp(-0.3 * l)
            lam_vecs = jnp.stack([df_lq1[j], df_lk1[j], df_lq2[j], df_lk2[j]], axis=0)
            o = _diff_attn(q, kv, lam_vecs, row(df_subln_g[j]), lambda_init)
            x = _proj_res(o, None, x, g_m[:, None, :], bf(df_w_o[j]))
        x = _moe_layer(x, mod_f, g_f[:, None, :], row(norm_ffn_g[l]), moe_w_rg[l], moe_w_re[l], moe_w_gate,
                       moe_w_up, moe_w_down, l, row(final_g), final_norm=(l == DEPTH - 1))
    return x
```

```python
import functools
import math

import jax
import jax.numpy as jnp
from jax import lax
from jax.experimental import pallas as pl
from jax.experimental.pallas import tpu as pltpu

F32 = jnp.float32
BF16 = jnp.bfloat16

D_MODEL = 1024
DEPTH = 2
N_A_LAYERS = DEPTH // 2
RWKV_HEAD = 64
RWKV_HEADS = D_MODEL // RWKV_HEAD
RWKV_GN_EPS = 64e-5
DIFF_QK_DIM = 64
DIFF_V_DIM = 2 * DIFF_QK_DIM
DIFF_HEADS = D_MODEL // DIFF_V_DIM
SUBLN_EPS = 1e-5
N_GROUPS = 4
EXPERTS_PER_GROUP = 8
N_EXPERTS = N_GROUPS * EXPERTS_PER_GROUP
EXPERT_FF = 512
RMS_EPS = 1e-6

LANES = 128
SUBLANES = 8
VMEM_LIMIT_BYTES = 56 * 1024 * 1024

SCAN_CHUNK = 64
SCAN_BASE_BLOCK = 8
SCAN_CHUNKS_PER_STEP = 2
PAIR = 2 * RWKV_HEAD
PROJ_TM = 256
DENSE_TM = 512
ATT_TQ = 512
ATT_HB = 2
MOE_TM = 256
RANK_TB = 1024
ROW_TB = 256
assert SCAN_CHUNK == RWKV_HEAD and PAIR == LANES


def _cparams(*sem):
    return pltpu.CompilerParams(dimension_semantics=sem, vmem_limit_bytes=VMEM_LIMIT_BYTES)


def _dot(a, b):
    return jnp.dot(a.astype(BF16), b.astype(BF16), preferred_element_type=F32)


def _split(x):
    hi = x.astype(BF16)
    lo = (x - hi.astype(F32)).astype(BF16)
    return hi, lo


def _dot3(a, b):
    ah, al = _split(a)
    bh, bl = _split(b)
    d = functools.partial(jnp.dot, preferred_element_type=F32)
    return d(ah, bh) + d(ah, bl) + d(al, bh)


def _dot_hl(a, b_exact):
    ah, al = _split(a)
    d = functools.partial(jnp.dot, preferred_element_type=F32)
    return d(ah, b_exact) + d(al, b_exact)


def _rms_mod(x, g, shift, scale):
    ms = jnp.mean(x * x, axis=-1, keepdims=True)
    return (x * lax.rsqrt(ms + RMS_EPS) * g) * (1.0 + scale) + shift


def _ada_kernel(c_ref, w_ref, b_ref, o_ref):
    c = c_ref[...]
    ca = c * jax.nn.sigmoid(c)
    o_ref[...] = _dot3(ca, w_ref[...]) + b_ref[...]


def _ada(c_pad, w, b, tn):
    L, D, N = w.shape
    return pl.pallas_call(
        _ada_kernel,
        out_shape=jax.ShapeDtypeStruct((L, SUBLANES, N), F32),
        grid=(L, N // tn),
        in_specs=[
            pl.BlockSpec((SUBLANES, D), lambda l, j: (0, 0)),
            pl.BlockSpec((None, D, tn), lambda l, j: (l, 0, j)),
            pl.BlockSpec((None, 1, tn), lambda l, j: (l, 0, j)),
        ],
        out_specs=pl.BlockSpec((None, SUBLANES, tn), lambda l, j: (l, 0, j)),
        compiler_params=_cparams("arbitrary", "arbitrary"),
        name="ada_mod",
    )(c_pad, w, b.reshape(L, 1, N))


def _rwkv_proj_kernel(x_ref, xp_ref, mod_ref, g_ref, mu_ref, wrkv_ref, w1_ref, w2_ref, a1_ref, a2_ref,
                      g1_ref, g2_ref, vec_ref, r_ref, k_ref, v_ref, lw_ref, kk_ref, al_ref, gate_ref):
    i = pl.program_id(1)
    g = g_ref[...]
    shift, scale = mod_ref[0, 0:1, :], mod_ref[0, 1:2, :]
    h = _rms_mod(x_ref[0], g, shift, scale)
    hp = _rms_mod(xp_ref[0, SUBLANES - 1:SUBLANES, :], g, shift, scale)
    hp = jnp.where(i == 0, 0.0, hp)
    row = lax.broadcasted_iota(jnp.int32, h.shape, 0)
    h_prev = jnp.where(row == 0, hp, pltpu.roll(h, 1, axis=0))
    xx = h_prev - h
    mu = mu_ref[...]
    xs = [(h + xx * mu[j:j + 1, :]).astype(BF16) for j in range(6)]
    w0, a0, k_k, k_a = (vec_ref[j:j + 1, :] for j in range(4))
    d = functools.partial(jnp.dot, preferred_element_type=F32)
    r = d(xs[0], wrkv_ref[0])
    k = d(xs[1], wrkv_ref[1])
    v = d(xs[2], wrkv_ref[2])
    z = w0 + _dot(jnp.tanh(d(xs[3], w1_ref[...])), w2_ref[...])
    lw = (-math.exp(-0.5)) * jax.nn.sigmoid(z)
    a = jax.nn.sigmoid(a0 + _dot(d(xs[4], a1_ref[...]), a2_ref[...]))
    gate = _dot(jax.nn.sigmoid(d(xs[5], g1_ref[...])), g2_ref[...])
    r_ref[0] = r.astype(BF16)
    k_ref[0] = (k * (1.0 + (a - 1.0) * k_a)).astype(BF16)
    v_ref[0] = v.astype(BF16)
    lw_ref[0] = lw
    kk_ref[0] = (k * k_k).astype(BF16)
    al_ref[0] = a.astype(BF16)
    gate_ref[0] = gate.astype(BF16)


def _rwkv_proj(x, mod2, g, mu, wrkv, w1, w2, a1, a2, g1, g2, vec):
    B, T, D = x.shape
    tm = PROJ_TM
    const2 = lambda b, i: (0, 0)
    const3 = lambda b, i: (0, 0, 0)
    act = pl.BlockSpec((1, tm, D), lambda b, i: (b, i, 0))
    n_sub = tm // SUBLANES
    return pl.pallas_call(
        _rwkv_proj_kernel,
        out_shape=[jax.ShapeDtypeStruct((B, T, D), F32 if n == 3 else BF16) for n in range(7)],
        grid=(B, T // tm),
        in_specs=[
            act,
            pl.BlockSpec((1, SUBLANES, D), lambda b, i: (b, jnp.maximum(i * n_sub - 1, 0), 0)),
            pl.BlockSpec((1, 2, D), lambda b, i: (b, 0, 0)),
            pl.BlockSpec((1, D), const2),
            pl.BlockSpec((6, D), const2),
            pl.BlockSpec((3, D, D), const3),
            pl.BlockSpec(w1.shape, const2), pl.BlockSpec(w2.shape, const2),
            pl.BlockSpec(a1.shape, const2), pl.BlockSpec(a2.shape, const2),
            pl.BlockSpec(g1.shape, const2), pl.BlockSpec(g2.shape, const2),
            pl.BlockSpec((4, D), const2),
        ],
        out_specs=[act] * 7,
        compiler_params=_cparams("arbitrary", "arbitrary"),
        name="rwkv_proj",
    )(x, x, mod2, g, mu, wrkv, w1, w2, a1, a2, g1, g2, vec)


def _rwkv_scan_kernel(r_ref, k_ref, v_ref, lw_ref, kk_ref, al_ref, pv_ref, y_ref, h_ref):
    C = SCAN_CHUNK
    P2 = 2 * C

    @pl.when(pl.program_id(1) == 0)
    def _():
        h_ref[...] = jnp.zeros_like(h_ref)

    lane = lax.broadcasted_iota(jnp.int32, (1, PAIR), 1)
    m_left = (lane < RWKV_HEAD).astype(F32)
    m_right = 1.0 - m_left
    ri = lax.broadcasted_iota(jnp.int32, (P2, P2), 0)
    ci = lax.broadcasted_iota(jnp.int32, (P2, P2), 1)
    same = (ri >= C) == (ci >= C)
    strict = same & (ri > ci)
    incl = same & (ri >= ci)
    eye = ri == ci
    block_ones = same.astype(BF16)
    tri = (lax.broadcasted_iota(jnp.int32, (C, C), 0) >= lax.broadcasted_iota(jnp.int32, (C, C), 1)).astype(BF16)

    def stack(x):
        return jnp.concatenate([x * m_left, x * m_right], axis=0)

    def head_sums(x):
        s_left = jnp.sum(x * m_left, axis=-1, keepdims=True)
        s_right = jnp.sum(x * m_right, axis=-1, keepdims=True)
        return jnp.where(lane < RWKV_HEAD, s_left, s_right)

    inv_n = 1.0 / RWKV_HEAD
    dd = functools.partial(jnp.dot, preferred_element_type=F32)
    n_pairs = RWKV_HEADS // 2
    units = [(ch, p) for ch in range(SCAN_CHUNKS_PER_STEP) for p in range(n_pairs)]
    idx = [(slice(ch * C, (ch + 1) * C), slice(p * PAIR, (p + 1) * PAIR)) for ch, p in units]
    U = range(len(units))
    ld = lambda ref, rs, sl: ref[0, rs, sl].astype(F32)
    kkr = [ld(kk_ref, rs, sl) for rs, sl in idx]
    ss = [head_sums(x * x) for x in kkr]
    lws = [lw_ref[0, rs, sl] for rs, sl in idx]
    Ls = []
    for lw in lws:
        l_hi, l_lo = _split(lw)
        cs = dd(tri, jnp.concatenate([l_hi, l_lo], axis=1))
        Ls.append(cs[:, :PAIR] + cs[:, PAIR:])
    lhs_g, rhs_g, bk_hat, vs, at32, rt32, dec_end = [], [], [], [], [], [], []
    for u in U:
        rs, sl = idx[u]
        L, lw = Ls[u], lws[u]
        kk = kkr[u] * lax.rsqrt(jnp.maximum(ss[u], 1e-24))
        b_vec = kk * ld(al_ref, rs, sl)
        k = ld(k_ref, rs, sl)
        LC = L[C - 1:C, :]
        e_neg = jnp.exp(-L)
        e_end = jnp.exp(LC - L)
        At = stack(-kk * jnp.exp(L - lw))
        Rt = stack(ld(r_ref, rs, sl) * jnp.exp(L))
        at32.append(At)
        rt32.append(Rt)
        lhs_g.append(jnp.concatenate([At, Rt], axis=0).astype(BF16))
        rhs_g.append(jnp.concatenate([stack(b_vec * e_neg), stack(k * e_neg)], axis=0).astype(BF16))
        bk_hat.append(jnp.concatenate([stack(b_vec * e_end), stack(k * e_end)], axis=0))
        vs.append(stack(ld(v_ref, rs, sl)).astype(BF16))
        dec_end.append(jnp.exp(LC))
    G = [lax.dot_general(lhs_g[u], rhs_g[u], (((1,), (1,)), ((), ())), preferred_element_type=F32) for u in U]
    A_ak = [jnp.where(strict, G[u][:P2, P2:], 0.0).astype(BF16) for u in U]
    A_r = [jnp.concatenate([jnp.where(incl, G[u][P2:, :P2], 0.0), jnp.where(incl, G[u][P2:, P2:], 0.0)],
                           axis=1).astype(BF16) for u in U]
    W = [dd(A_ak[u], vs[u]) for u in U]
    bsz = lambda b: (ri >> int(math.log2(b))) == (ci >> int(math.log2(b)))
    b8 = bsz(SCAN_BASE_BLOCK)
    D1 = [jnp.where(strict & b8, G[u][:P2, :P2], 0.0).astype(BF16) for u in U]
    D2 = [dd(D1[u], D1[u]).astype(BF16) for u in U]
    D4 = [dd(D2[u], D2[u]).astype(BF16) for u in U]
    eye_f = eye.astype(F32)
    P1 = [eye_f + D1[u].astype(F32) + D2[u].astype(F32) + dd(D1[u], D2[u]) for u in U]
    Tm = [P1[u] + dd(P1[u].astype(BF16), D4[u]) for u in U]
    blk = SCAN_BASE_BLOCK
    while blk < C:
        off = strict & bsz(2 * blk) & ~bsz(blk)
        Mo = [jnp.where(off, G[u][:P2, :P2], 0.0).astype(BF16) for u in U]
        Tb = [Tm[u].astype(BF16) for u in U]
        TM = [dd(Tb[u], Mo[u]).astype(BF16) for u in U]
        Tm = [Tm[u] + dd(TM[u], Tb[u]) for u in U]
        blk *= 2
    Z = [dd(Tm[u].astype(BF16), jnp.concatenate([at32[u], W[u]], axis=1).astype(BF16)) for u in U]
    rhs = [jnp.concatenate([Z[u].astype(BF16), jnp.concatenate([jnp.zeros_like(vs[u]), vs[u]], axis=1)], axis=0)
           for u in U]
    o6 = [dd(A_r[u], rhs[u]) for u in U]
    o7 = [dd(bk_hat[u].T.astype(BF16), rhs[u]) for u in U]
    H = [h_ref[p] for p in range(n_pairs)]
    Y = [None] * len(units)
    for u in U:
        p = units[u][1]
        Hb = H[p].astype(BF16)
        Y[u] = dd((rt32[u] + o6[u][:, :PAIR]).astype(BF16), Hb) + o6[u][:, PAIR:]
        Mbd = o7[u][:, :PAIR] + jnp.where(eye, dec_end[u], 0.0)
        H[p] = dd(Mbd.astype(BF16), Hb) + o7[u][:, PAIR:]
    for p in range(n_pairs):
        h_ref[p] = H[p]
    ys = [Y[u][:C] + Y[u][C:] for u in U]
    rk = [ld(r_ref, rs, sl) * ld(k_ref, rs, sl) * pv_ref[0:1, sl] for rs, sl in idx]
    st1 = [head_sums(jnp.concatenate([ys[u], rk[u]], axis=0)) for u in U]
    yc = [ys[u] - st1[u][:C] * inv_n for u in U]
    var = [head_sums(yc[u] * yc[u]) * inv_n for u in U]
    for u in U:
        rs, sl = idx[u]
        bonus = st1[u][C:] * ld(v_ref, rs, sl)
        y_ref[0, rs, sl] = (yc[u] * lax.rsqrt(var[u] + RWKV_GN_EPS) * pv_ref[1:2, sl] + pv_ref[2:3, sl]
                            + bonus).astype(y_ref.dtype)


def _rwkv_scan(r, k, v, lw, kk, al, pvec):
    B, T, D = r.shape
    rows = SCAN_CHUNK * SCAN_CHUNKS_PER_STEP
    act = pl.BlockSpec((1, rows, D), lambda b, c: (b, c, 0))
    return pl.pallas_call(
        _rwkv_scan_kernel,
        out_shape=jax.ShapeDtypeStruct((B, T, D), BF16),
        grid=(B, T // rows),
        in_specs=[act] * 6 + [pl.BlockSpec((3, D), lambda b, c: (0, 0))],
        out_specs=act,
        scratch_shapes=[pltpu.VMEM((RWKV_HEADS // 2, PAIR, PAIR), F32)],
        compiler_params=_cparams("arbitrary", "arbitrary"),
        name="rwkv_scan",
    )(r, k, v, lw, kk, al, pvec)


def _proj_res_route_kernel(*refs, has_gate):
    if has_gate:
        y_ref, g_ref, x_ref, gm_ref, w_ref, modf_ref, gf_ref, wr_ref, o_ref, info_ref, cnt_ref, zero_ref = refs
        y = y_ref[0].astype(F32) * g_ref[0].astype(F32)
    else:
        y_ref, x_ref, gm_ref, w_ref, modf_ref, gf_ref, wr_ref, o_ref, info_ref, cnt_ref, zero_ref = refs
        y = y_ref[0]
    zero_ref[...] = jnp.zeros_like(zero_ref)
    x_new = x_ref[0] + gm_ref[0] * jnp.dot(y.astype(BF16), w_ref[...], preferred_element_type=F32)
    o_ref[0] = x_new
    h = _rms_mod(x_new, gf_ref[...], modf_ref[0, 0:1, :], modf_ref[0, 1:2, :])
    first = (pl.program_id(0) == 0) & (pl.program_id(1) == 0)
    _route(h, wr_ref[...], info_ref, cnt_ref, first)


def _proj_res_route(y, g, x, gm, w, mod_f, norm_f, w_cat, n_rows):
    B, T, D = x.shape
    tm = DENSE_TM
    per_b = T // tm
    n_blk = B * per_b
    zrows = n_rows * SUBLANES // n_blk
    assert zrows * n_blk == n_rows * SUBLANES and zrows % SUBLANES == 0
    act = pl.BlockSpec((1, tm, D), lambda b, i: (b, i, 0))
    const = lambda b, i: (0, 0)
    flat = lambda b, i: (b * per_b + i, 0)
    ins = [y] + ([g] if g is not None else []) + [x, gm, w, mod_f, norm_f, w_cat]
    specs = [act] * (len(ins) - 5) + [pl.BlockSpec((1, 1, D), lambda b, i: (b, 0, 0)), pl.BlockSpec((D, D), const),
                                     pl.BlockSpec((1, 2, D), lambda b, i: (b, 0, 0)), pl.BlockSpec((1, D), const),
                                     pl.BlockSpec((D, LANES), const)]
    return pl.pallas_call(
        functools.partial(_proj_res_route_kernel, has_gate=g is not None),
        out_shape=[jax.ShapeDtypeStruct((B, T, D), F32), jax.ShapeDtypeStruct((B * T, LANES), F32),
                   jax.ShapeDtypeStruct((SUBLANES, LANES), F32), jax.ShapeDtypeStruct((n_rows * SUBLANES, LANES), F32)],
        grid=(B, per_b),
        in_specs=specs,
        out_specs=[act, pl.BlockSpec((tm, LANES), flat), pl.BlockSpec((SUBLANES, LANES), const),
                   pl.BlockSpec((zrows, LANES), flat)],
        compiler_params=_cparams("arbitrary", "arbitrary"),
        name="proj_res_route",
    )(*ins)


def _norm_mm_kernel(*refs, n_proj):
    x_ref = refs[0]
    ins, outs = refs[1:1 + 3 * n_proj], refs[1 + 3 * n_proj:]
    x = x_ref[0]
    xn = x * lax.rsqrt(jnp.mean(x * x, axis=-1, keepdims=True) + RMS_EPS)
    for p in range(n_proj):
        mod_ref, g_ref, w_ref = ins[3 * p:3 * p + 3]
        h = (xn * g_ref[...]) * (1.0 + mod_ref[0, 1:2, :]) + mod_ref[0, 0:1, :]
        outs[p][0] = jnp.dot(h.astype(BF16), w_ref[...], preferred_element_type=F32).astype(outs[p].dtype)


def _norm_mm(x, projs, out_dtype):
    B, T, D = x.shape
    tm = DENSE_TM
    in_specs = [pl.BlockSpec((1, tm, D), lambda b, i: (b, i, 0))]
    args = [x]
    for mod2, g, w in projs:
        in_specs += [pl.BlockSpec((1, 2, D), lambda b, i: (b, 0, 0)), pl.BlockSpec((1, D), lambda b, i: (0, 0)),
                     pl.BlockSpec(w.shape, lambda b, i: (0, 0))]
        args += [mod2, g, w]
    return pl.pallas_call(
        functools.partial(_norm_mm_kernel, n_proj=len(projs)),
        out_shape=[jax.ShapeDtypeStruct((B, T, w.shape[1]), out_dtype) for _, _, w in projs],
        grid=(B, T // tm),
        in_specs=in_specs,
        out_specs=[pl.BlockSpec((1, tm, w.shape[1]), lambda b, i: (b, i, 0)) for _, _, w in projs],
        compiler_params=_cparams("arbitrary", "arbitrary"),
        name="norm_mm",
    )(*args)


def _diff_attn_kernel(q_ref, k_ref, v_ref, lam_ref, sg_ref, o_ref, m_ref, acc_ref, s_ref, *, lambda_init):
    tq, HB, dv = ATT_TQ, ATT_HB, DIFF_V_DIM
    qi = pl.program_id(2)
    heads = range(HB)
    hs = [slice(h * dv, (h + 1) * dv) for h in heads]
    lane = lax.broadcasted_iota(jnp.int32, (1, dv), 1)
    m_left = (lane < DIFF_QK_DIM).astype(F32)
    qs = []
    for h in heads:
        q = q_ref[0, :, hs[h]].astype(F32) * (DIFF_QK_DIM ** -0.5 * math.log2(math.e))
        qs.append(jnp.concatenate([q * m_left, q * (1.0 - m_left)], axis=0).astype(BF16))
    ones_col = jnp.ones((tq, dv), BF16)
    causal = (lax.broadcasted_iota(jnp.int32, (2 * tq, tq), 1)
              <= lax.broadcasted_iota(jnp.int32, (2 * tq, tq), 0) % tq)

    def key_rows(j):
        return pl.ds(pl.multiple_of(j * tq, tq), tq)

    def scores_into(slot, j):
        for h in heads:
            s_ref[slot, h] = lax.dot_general(qs[h], k_ref[0, key_rows(j), hs[h]], (((1,), (1,)), ((), ())),
                                             preferred_element_type=F32)

    def block(slot, j, first, prefetch):
        if prefetch is not None:
            scores_into(1 - slot, prefetch)
        for h in heads:
            s = s_ref[slot, h]
            if first:
                s = jnp.where(causal, s, -jnp.inf)
                m_new = jnp.broadcast_to(jnp.max(s, axis=-1, keepdims=True), (2 * tq, LANES))
            else:
                m_old = m_ref[h]
                m_new = jnp.maximum(m_old, jnp.max(s, axis=-1, keepdims=True))
            m_ref[h] = m_new
            p = jnp.exp2(s - jnp.concatenate([m_new] * (tq // LANES), axis=1)).astype(BF16)
            pv = jnp.dot(p, jnp.concatenate([v_ref[0, key_rows(j), hs[h]], ones_col], axis=1),
                         preferred_element_type=F32)
            if first:
                acc_ref[h] = pv
            else:
                alpha = jnp.exp2(m_old - m_new)
                acc_ref[h] = acc_ref[h] * jnp.concatenate([alpha, alpha], axis=1) + pv

    scores_into(0, qi)
    block(0, qi, True, 0)

    def pair(u, c):
        last = jnp.maximum(qi - 1, 0)
        block(1, 2 * u, False, jnp.minimum(2 * u + 1, last))
        block(0, 2 * u + 1, False, jnp.minimum(2 * u + 2, last))
        return c

    lax.fori_loop(0, qi // 2, pair, 0)

    @pl.when(qi % 2 == 1)
    def _():
        block(1, qi - 1, False, None)

    lv = lam_ref[...]
    lam = (jnp.exp(jnp.sum(lv[0:1] * lv[1:2], axis=-1, keepdims=True))
           - jnp.exp(jnp.sum(lv[2:3] * lv[3:4], axis=-1, keepdims=True)) + lambda_init)
    for h in heads:
        acc = acc_ref[h]
        o = acc[:, :dv] / acc[:, dv:]
        o = o[:tq] - lam * o[tq:]
        ms = jnp.mean(o * o, axis=-1, keepdims=True)
        o_ref[0, :, hs[h]] = (o * lax.rsqrt(ms + SUBLN_EPS) * sg_ref[...] * (1.0 - lambda_init)).astype(o_ref.dtype)


def _diff_attn(q, kv, lam_vecs, subln_g, lambda_init):
    B, T, D = q.shape
    tq, HB = ATT_TQ, ATT_HB
    n_hb = DIFF_HEADS // HB
    w = HB * DIFF_V_DIM
    return pl.pallas_call(
        functools.partial(_diff_attn_kernel, lambda_init=lambda_init),
        out_shape=jax.ShapeDtypeStruct((B, T, D), BF16),
        grid=(B, n_hb, T // tq),
        in_specs=[
            pl.BlockSpec((1, tq, w), lambda b, h, i: (b, i, h)),
            pl.BlockSpec((1, T, w), lambda b, h, i: (b, 0, h)),
            pl.BlockSpec((1, T, w), lambda b, h, i: (b, 0, n_hb + h)),
            pl.BlockSpec((4, DIFF_QK_DIM), lambda b, h, i: (0, 0)),
            pl.BlockSpec((1, DIFF_V_DIM), lambda b, h, i: (0, 0)),
        ],
        out_specs=pl.BlockSpec((1, tq, w), lambda b, h, i: (b, i, h)),
        scratch_shapes=[pltpu.VMEM((HB, 2 * tq, LANES), F32), pltpu.VMEM((HB, 2 * tq, 2 * DIFF_V_DIM), F32),
                        pltpu.VMEM((2, HB, 2 * tq, tq), F32)],
        compiler_params=_cparams("arbitrary", "arbitrary", "arbitrary"),
        name="diff_attn",
    )(q, kv, kv, lam_vecs, subln_g)


def _route(h, w_cat, info_ref, cnt_ref, first_step):
    logit = _dot3(h, w_cat)
    lane_i = lax.broadcasted_iota(jnp.int32, logit.shape, 1)
    lane = lane_i.astype(F32)
    neg = -jnp.inf
    big = float(LANES)
    is_grp = lane_i < N_GROUPS
    gl = jnp.where(is_grp, logit, neg)
    gmax = jnp.max(gl, axis=-1, keepdims=True)
    gidx = jnp.min(jnp.where(gl == gmax, lane, big), axis=-1, keepdims=True)
    grp_gate = 1.0 / jnp.sum(jnp.where(is_grp, jnp.exp(logit - gmax), 0.0), axis=-1, keepdims=True)
    lo = N_GROUPS + gidx * EXPERTS_PER_GROUP
    in_grp = (lane >= lo) & (lane < lo + EXPERTS_PER_GROUP)
    el = jnp.where(in_grp, logit, neg)
    t1 = jnp.max(el, axis=-1, keepdims=True)
    i1 = jnp.min(jnp.where(el == t1, lane, big), axis=-1, keepdims=True)
    el2 = jnp.where(lane == i1, neg, el)
    t2 = jnp.max(el2, axis=-1, keepdims=True)
    i2 = jnp.min(jnp.where(el2 == t2, lane, big), axis=-1, keepdims=True)
    e21 = jnp.exp(t2 - t1)
    p1 = 1.0 / (1.0 + e21)
    w1 = grp_gate * p1
    w2 = grp_gate * (e21 * p1)
    e1 = i1 - N_GROUPS
    e2 = i2 - N_GROUPS
    info_ref[...] = jnp.where(lane_i == 0, e1, jnp.where(lane_i == 1, e2, jnp.where(lane_i == 2, w1, jnp.where(lane_i == 3, w2, 0.0))))

    @pl.when(first_step)
    def _():
        cnt_ref[...] = jnp.zeros_like(cnt_ref)

    picked = ((lane == e1) | (lane == e2)).astype(F32)
    cnt_ref[...] += jnp.broadcast_to(jnp.sum(picked, axis=0, keepdims=True), cnt_ref.shape)


def _rank_kernel(info_ref, cnt_ref, dest_ref, meta_ref, start_ref):
    i = pl.program_id(0)
    tb = info_ref.shape[0]
    lane = lax.broadcasted_iota(jnp.int32, (tb, LANES), 1)
    info = info_ref[...]
    e0 = info[:, 0:1].astype(jnp.int32)
    e1 = info[:, 1:2].astype(jnp.int32)
    o0 = (lane == e0).astype(F32)
    o1 = (lane == e1).astype(F32)
    both = o0 + o1

    @pl.when(i == 0)
    def _():
        cnt = cnt_ref[0:1, :]
        padded = jnp.floor((cnt + (MOE_TM - 1)) * (1.0 / MOE_TM)) * MOE_TM
        r = lax.broadcasted_iota(jnp.int32, (LANES, LANES), 0)
        c = lax.broadcasted_iota(jnp.int32, (LANES, LANES), 1)
        upper_strict = (r < c).astype(BF16)
        start = _dot_hl(jnp.broadcast_to(padded, (SUBLANES, LANES)), upper_strict)[0:1]
        start_ref[...] = start
        meta_ref[...] = jnp.broadcast_to(start + padded, (SUBLANES, LANES))

    r = lax.broadcasted_iota(jnp.int32, (tb, tb), 0)
    c = lax.broadcasted_iota(jnp.int32, (tb, tb), 1)
    lower_strict = (r > c).astype(BF16)
    before = jnp.dot(lower_strict, both.astype(BF16), preferred_element_type=F32) + start_ref[...]
    d0 = jnp.sum(o0 * before, axis=-1, keepdims=True)
    d1 = jnp.sum(o1 * before, axis=-1, keepdims=True)
    dest_ref[...] = jnp.where(lane == 0, d0, jnp.where(lane == 1, d1, 0.0)).astype(jnp.int32)
    start_ref[...] += jnp.sum(both, axis=0, keepdims=True)


def _rank(info, cnt):
    n_tok = info.shape[0]
    tb = RANK_TB
    n_blk = n_tok // tb
    return pl.pallas_call(
        _rank_kernel,
        out_shape=[jax.ShapeDtypeStruct((n_tok, LANES), jnp.int32), jax.ShapeDtypeStruct((SUBLANES, LANES), F32)],
        grid=(n_blk,),
        in_specs=[pl.BlockSpec((tb, LANES), lambda i: (i, 0)), pl.BlockSpec((SUBLANES, LANES), lambda i: (0, 0))],
        out_specs=[pl.BlockSpec((tb, LANES), lambda i: (i, 0)), pl.BlockSpec((SUBLANES, LANES), lambda i: (0, 0))],
        scratch_shapes=[pltpu.VMEM((1, LANES), F32)],
        compiler_params=_cparams("arbitrary"),
        name="moe_rank",
    )(info, cnt)


assert D_MODEL == SUBLANES * LANES


def _tile_rows_store(ref, x):
    n = x.shape[0]
    for s in range(SUBLANES):
        ref[pl.ds(s, n, stride=SUBLANES), :] = x[:, s * LANES:(s + 1) * LANES]


def _tile_rows_load(ref, n):
    return jnp.concatenate([ref[pl.ds(s, n, stride=SUBLANES), :] for s in range(SUBLANES)], axis=1)


def _row_copy(src_ref, s, dst_ref, d, sem):
    rows = lambda r: pl.ds(pl.multiple_of(r * SUBLANES, SUBLANES), SUBLANES)
    return pltpu.make_async_copy(src_ref.at[rows(s)], dst_ref.at[rows(d)], sem)


def _dispatch_kernel(dest_ref, x_ref, mod_ref, g_ref, xs_in_ref, xs_ref, h_ref, sem):
    del xs_in_ref
    tb = x_ref.shape[0]
    i = pl.program_id(0)
    last = pl.num_programs(0) - 1
    h = _rms_mod(x_ref[...], g_ref[...], mod_ref[0, 0:1, :], mod_ref[0, 1:2, :])

    def drain(slot):
        def wait(j, c):
            _row_copy(h_ref.at[slot], 0, xs_ref, 0, sem.at[slot]).wait()
            return c
        lax.fori_loop(0, 2 * tb, wait, 0, unroll=8)

    for slot in range(2):
        @pl.when(i % 2 == slot)
        def _(slot=slot):
            @pl.when(i >= 2)
            def _():
                drain(slot)

            _tile_rows_store(h_ref.at[slot], h)

            def start(j, c):
                _row_copy(h_ref.at[slot], j, xs_ref, dest_ref[0, 0, 2 * j], sem.at[slot]).start(priority=0)
                _row_copy(h_ref.at[slot], j, xs_ref, dest_ref[0, 0, 2 * j + 1], sem.at[slot]).start(priority=1)
                return c

            lax.fori_loop(0, tb, start, 0, unroll=8)

            @pl.when(i == last)
            def _():
                drain(slot)

                @pl.when(i >= 1)
                def _():
                    drain(1 - slot)


def _dispatch(dest3, x2, mod2, g, xs_zero, T):
    n_tok, D = x2.shape
    tb = ROW_TB
    per_b = T // tb
    return pl.pallas_call(
        _dispatch_kernel,
        out_shape=jax.ShapeDtypeStruct(xs_zero.shape, F32),
        grid=(n_tok // tb,),
        in_specs=[
            pl.BlockSpec((1, 1, 2 * tb), lambda i: (i, 0, 0), memory_space=pltpu.SMEM),
            pl.BlockSpec((tb, D), lambda i: (i, 0)),
            pl.BlockSpec((1, 2, D), lambda i: (i // per_b, 0, 0)),
            pl.BlockSpec((1, D), lambda i: (0, 0)),
            pl.BlockSpec(memory_space=pl.ANY),
        ],
        out_specs=pl.BlockSpec(memory_space=pl.ANY),
        scratch_shapes=[pltpu.VMEM((2, tb * SUBLANES, LANES), F32), pltpu.SemaphoreType.DMA((2,))],
        input_output_aliases={4: 0},
        compiler_params=_cparams("arbitrary"),
        name="moe_dispatch",
    )(dest3, x2, mod2, g, xs_zero)


def _expert_kernel(be_ref, nb_ref, nxt_ref, xs_ref, wg_hbm, wu_hbm, wd_hbm, ys_ref, stage_g, stage_u, stage_d,
                   wgb, wub, wdb, sem, *, layer):
    i = pl.program_id(0)
    e = be_ref[i]
    changed = (i == 0) | (e != be_ref[jnp.maximum(i - 1, 0)])

    def fetch(ex):
        return (pltpu.make_async_copy(wg_hbm.at[layer, ex], stage_g, sem.at[0]),
                pltpu.make_async_copy(wu_hbm.at[layer, ex], stage_u, sem.at[1]),
                pltpu.make_async_copy(wd_hbm.at[layer, ex], stage_d, sem.at[2]))

    @pl.when(i == 0)
    def _():
        for cp in fetch(e):
            cp.start()

    @pl.when(changed)
    def _():
        for cp in fetch(e):
            cp.wait()
        wgb[...] = stage_g[...].astype(BF16)
        wub[...] = stage_u[...].astype(BF16)
        wdb[...] = stage_d[...].astype(BF16)

        @pl.when(nxt_ref[i] >= 0)
        def _():
            for cp in fetch(nxt_ref[i]):
                cp.start()

    @pl.when(i < nb_ref[0])
    def _():
        x = _tile_rows_load(xs_ref, MOE_TM).astype(BF16)
        a = jnp.dot(x, wgb[...], preferred_element_type=F32)
        u = jnp.dot(x, wub[...], preferred_element_type=F32)
        hdn = (a * jax.nn.sigmoid(a)) * u
        _tile_rows_store(ys_ref, jnp.dot(hdn.astype(BF16), wdb[...], preferred_element_type=F32))

    @pl.when(i >= nb_ref[0])
    def _():
        ys_ref[...] = jnp.zeros_like(ys_ref)


def _experts(blk_e, n_used, next_e, xs, w_gate, w_up, w_down, layer):
    D = D_MODEL
    n_rows = xs.shape[0] // SUBLANES
    tm = MOE_TM
    FF = EXPERT_FF
    row_block = pl.BlockSpec((tm * SUBLANES, LANES), lambda i, be, nb, nx: (i, 0))
    hbm = pl.BlockSpec(memory_space=pl.ANY)
    grid_spec = pltpu.PrefetchScalarGridSpec(
        num_scalar_prefetch=3,
        grid=(n_rows // tm,),
        in_specs=[row_block, hbm, hbm, hbm],
        out_specs=row_block,
        scratch_shapes=[pltpu.VMEM((D, FF), F32), pltpu.VMEM((D, FF), F32), pltpu.VMEM((FF, D), F32),
                        pltpu.VMEM((D, FF), BF16), pltpu.VMEM((D, FF), BF16), pltpu.VMEM((FF, D), BF16),
                        pltpu.SemaphoreType.DMA((3,))],
    )
    return pl.pallas_call(
        functools.partial(_expert_kernel, layer=layer),
        out_shape=jax.ShapeDtypeStruct(xs.shape, F32),
        grid_spec=grid_spec,
        compiler_params=_cparams("arbitrary"),
        name="moe_experts",
    )(blk_e, n_used, next_e, xs, w_gate, w_up, w_down)


def _combine_kernel(dest_ref, dest_next_ref, info_ref, x_ref, gf_ref, fg_ref, ys_ref, o_ref, y_ref, sem, *,
                    final_norm):
    tb = x_ref.shape[0]
    i = pl.program_id(0)
    last = pl.num_programs(0) - 1

    def gather(d_ref, slot):
        def start(j, c):
            _row_copy(ys_ref, d_ref[0, 0, 2 * j], y_ref.at[slot, 0], j, sem.at[slot]).start(priority=0)
            _row_copy(ys_ref, d_ref[0, 0, 2 * j + 1], y_ref.at[slot, 1], j, sem.at[slot]).start(priority=1)
            return c
        lax.fori_loop(0, tb, start, 0, unroll=8)

    @pl.when(i == 0)
    def _():
        gather(dest_ref, 0)

    for slot in range(2):
        @pl.when(i % 2 == slot)
        def _(slot=slot):
            @pl.when(i < last)
            def _():
                gather(dest_next_ref, 1 - slot)

            def wait(j, c):
                _row_copy(ys_ref, 0, y_ref.at[slot, 0], 0, sem.at[slot]).wait()
                return c

            lax.fori_loop(0, 2 * tb, wait, 0, unroll=8)
            info = info_ref[...]
            moe = (info[:, 2:3] * _tile_rows_load(y_ref.at[slot, 0], tb)
                   + info[:, 3:4] * _tile_rows_load(y_ref.at[slot, 1], tb))
            out = x_ref[...] + gf_ref[0] * moe
            if final_norm:
                ms = jnp.mean(out * out, axis=-1, keepdims=True)
                out = out * lax.rsqrt(ms + RMS_EPS) * fg_ref[...]
            o_ref[...] = out


def _combine(dest3, info, x2, gf, final_g, ys, T, final_norm):
    n_tok, D = x2.shape
    tb = ROW_TB
    per_b = T // tb
    n_steps = n_tok // tb
    return pl.pallas_call(
        functools.partial(_combine_kernel, final_norm=final_norm),
        out_shape=jax.ShapeDtypeStruct((n_tok, D), F32),
        grid=(n_steps,),
        in_specs=[
            pl.BlockSpec((1, 1, 2 * tb), lambda i: (i, 0, 0), memory_space=pltpu.SMEM),
            pl.BlockSpec((1, 1, 2 * tb), lambda i: (jnp.minimum(i + 1, n_steps - 1), 0, 0), memory_space=pltpu.SMEM),
            pl.BlockSpec((tb, LANES), lambda i: (i, 0)),
            pl.BlockSpec((tb, D), lambda i: (i, 0)),
            pl.BlockSpec((1, 1, D), lambda i: (i // per_b, 0, 0)),
            pl.BlockSpec((1, D), lambda i: (0, 0)),
            pl.BlockSpec(memory_space=pl.ANY),
        ],
        out_specs=pl.BlockSpec((tb, D), lambda i: (i, 0)),
        scratch_shapes=[pltpu.VMEM((2, 2, tb * SUBLANES, LANES), F32), pltpu.SemaphoreType.DMA((2,))],
        compiler_params=_cparams("arbitrary"),
        name="moe_combine",
    )(dest3, dest3, info, x2, gf, final_g, ys)


def _moe_rows(n_tok):
    return -(-(2 * n_tok + N_EXPERTS * MOE_TM) // MOE_TM) * MOE_TM


def _moe_layer(x, info, cnt, xs_zero, mod_f, gf, norm_g, w_gate, w_up, w_down, layer, final_g, final_norm):
    B, T, D = x.shape
    n_tok = B * T
    x2 = x.reshape(n_tok, D)
    n_rows = _moe_rows(n_tok)
    dest, meta = _rank(info, cnt)
    pad_end = meta[0, :N_EXPERTS].astype(jnp.int32)
    n_blocks = n_rows // MOE_TM
    blk_start = jnp.arange(n_blocks, dtype=jnp.int32) * MOE_TM
    blk_e = jnp.minimum(jnp.sum(pad_end[None, :] <= blk_start[:, None], axis=1), N_EXPERTS - 1).astype(jnp.int32)
    n_used = (pad_end[N_EXPERTS - 1:] // MOE_TM).astype(jnp.int32)
    dest3 = dest[:, :2].reshape(n_tok // ROW_TB, 1, 2 * ROW_TB)
    xs = _dispatch(dest3, x2, mod_f, norm_g, xs_zero, T)
    seg_end = jnp.sum(blk_e[None, :] <= blk_e[:, None], axis=1)
    next_e = jnp.where(seg_end < n_blocks, blk_e[jnp.minimum(seg_end, n_blocks - 1)], -1).astype(jnp.int32)
    ys = _experts(blk_e, n_used, next_e, xs, w_gate, w_up, w_down, layer)
    out = _combine(dest3, info, x2, gf, final_g, ys, T, final_norm)
    return out.reshape(B, T, D)


def kernel(x, c, ada_w, ada_b, norm_mix_g, norm_ffn_g, rw_mu, rw_w_rkv, rw_w0, rw_w1, rw_w2, rw_a0, rw_a1, rw_a2,
           rw_g1, rw_g2, rw_k_k, rw_k_a, rw_r_k, rw_gn_g, rw_gn_b, rw_w_o, ada_kv_w, ada_kv_b, norm_kv_g, w_kv,
           df_w_q, df_lq1, df_lk1, df_lq2, df_lk2, df_subln_g, df_w_o, moe_w_rg, moe_w_re, moe_w_gate, moe_w_up,
           moe_w_down, final_g):
    B, T, D = x.shape
    c_pad = jnp.zeros((SUBLANES, D), F32).at[:B].set(c)
    mod = _ada(c_pad, ada_w, ada_b, 6 * D // 4)[:, :B]
    mod_kv = _ada(c_pad, ada_kv_w[None], ada_kv_b[None], D)[0, :B]
    bf = lambda w: w.astype(BF16)
    row = lambda v: v.reshape(1, -1)

    for l in range(DEPTH):
        sh_m, sc_m, g_m, sh_f, sc_f, g_f = jnp.split(mod[l], 6, axis=-1)
        mod_m = jnp.stack([sh_m, sc_m], axis=1)
        mod_f = jnp.stack([sh_f, sc_f], axis=1)
        if l < N_A_LAYERS:
            i = l
            vec = jnp.stack([rw_w0[i], rw_a0[i], rw_k_k[i], rw_k_a[i]], axis=0)
            r, k, v, lw, kk, al, gate = _rwkv_proj(
                x, mod_m, row(norm_mix_g[l]), rw_mu[i], bf(rw_w_rkv[i]), bf(rw_w1[i]), bf(rw_w2[i]),
                bf(rw_a1[i]), bf(rw_a2[i]), bf(rw_g1[i]), bf(rw_g2[i]), vec)
            pvec = jnp.stack([rw_r_k[i].reshape(-1), rw_gn_g[i], rw_gn_b[i]], axis=0)
            y = _rwkv_scan(r, k, v, lw, kk, al, pvec)
            w_o = bf(rw_w_o[i])
        else:
            j = l - N_A_LAYERS
            q_proj = (mod_m, row(norm_mix_g[l]), bf(df_w_q[j]))
            if l == N_A_LAYERS:
                sh_kv, sc_kv = jnp.split(mod_kv, 2, axis=-1)
                q, kv = _norm_mm(x, [q_proj, (jnp.stack([sh_kv, sc_kv], axis=1), row(norm_kv_g), bf(w_kv))], BF16)
            else:
                q, = _norm_mm(x, [q_proj], BF16)
            lambda_init = 0.8 - 0.6 * math.exp(-0.3 * l)
            lam_vecs = jnp.stack([df_lq1[j], df_lk1[j], df_lq2[j], df_lk2[j]], axis=0)
            y, gate = _diff_attn(q, kv, lam_vecs, row(df_subln_g[j]), lambda_init), None
            w_o = bf(df_w_o[j])
        w_cat = jnp.concatenate([moe_w_rg[l], moe_w_re[l], jnp.zeros((D, LANES - N_GROUPS - N_EXPERTS), F32)], axis=1)
        x, info, cnt, xs_zero = _proj_res_route(y, gate, x, g_m[:, None, :], w_o, mod_f, row(norm_ffn_g[l]), w_cat,
                                                _moe_rows(B * T))
        x = _moe_layer(x, info, cnt, xs_zero, mod_f, g_f[:, None, :], row(norm_ffn_g[l]), moe_w_gate, moe_w_up,
                       moe_w_down, l, row(final_g), final_norm=(l == DEPTH - 1))
    return x
```

```python
import functools
import math

import jax
import jax.numpy as jnp
from jax import lax
from jax.experimental import pallas as pl
from jax.experimental.pallas import tpu as pltpu

F32 = jnp.float32
BF16 = jnp.bfloat16

D_MODEL = 1024
DEPTH = 2
N_A_LAYERS = DEPTH // 2
RWKV_HEAD = 64
RWKV_HEADS = D_MODEL // RWKV_HEAD
RWKV_GN_EPS = 64e-5
DIFF_QK_DIM = 64
DIFF_V_DIM = 2 * DIFF_QK_DIM
DIFF_HEADS = D_MODEL // DIFF_V_DIM
SUBLN_EPS = 1e-5
N_GROUPS = 4
EXPERTS_PER_GROUP = 8
N_EXPERTS = N_GROUPS * EXPERTS_PER_GROUP
EXPERT_FF = 512
RMS_EPS = 1e-6

LANES = 128
SUBLANES = 8
VMEM_LIMIT_BYTES = 56 * 1024 * 1024

SCAN_CHUNK = 64
SCAN_BASE_BLOCK = 8
SCAN_CHUNKS_PER_STEP = 2
PAIR = 2 * RWKV_HEAD
PROJ_TM = 256
DENSE_TM = 512
ATT_TQ = 512
ATT_HB = 2
MOE_TM = 256
RANK_TB = 1024
ROW_TB = 256
assert SCAN_CHUNK == RWKV_HEAD and PAIR == LANES


def _cparams(*sem):
    return pltpu.CompilerParams(dimension_semantics=sem, vmem_limit_bytes=VMEM_LIMIT_BYTES)


def _dot(a, b):
    return jnp.dot(a.astype(BF16), b.astype(BF16), preferred_element_type=F32)


def _split(x):
    hi = x.astype(BF16)
    lo = (x - hi.astype(F32)).astype(BF16)
    return hi, lo


def _dot3(a, b):
    ah, al = _split(a)
    bh, bl = _split(b)
    d = functools.partial(jnp.dot, preferred_element_type=F32)
    return d(ah, bh) + d(ah, bl) + d(al, bh)


def _dot_hl(a, b_exact):
    ah, al = _split(a)
    d = functools.partial(jnp.dot, preferred_element_type=F32)
    return d(ah, b_exact) + d(al, b_exact)


def _rms_mod(x, g, shift, scale):
    ms = jnp.mean(x * x, axis=-1, keepdims=True)
    return (x * lax.rsqrt(ms + RMS_EPS) * g) * (1.0 + scale) + shift


def _ada_kernel(c_ref, w_ref, b_ref, o_ref):
    c = c_ref[...]
    ca = c * jax.nn.sigmoid(c)
    o_ref[...] = _dot3(ca, w_ref[...]) + b_ref[...]


def _ada(c_pad, w, b, tn):
    L, D, N = w.shape
    return pl.pallas_call(
        _ada_kernel,
        out_shape=jax.ShapeDtypeStruct((L, SUBLANES, N), F32),
        grid=(L, N // tn),
        in_specs=[
            pl.BlockSpec((SUBLANES, D), lambda l, j: (0, 0)),
            pl.BlockSpec((None, D, tn), lambda l, j: (l, 0, j)),
            pl.BlockSpec((None, 1, tn), lambda l, j: (l, 0, j)),
        ],
        out_specs=pl.BlockSpec((None, SUBLANES, tn), lambda l, j: (l, 0, j)),
        compiler_params=_cparams("arbitrary", "arbitrary"),
        name="ada_mod",
    )(c_pad, w, b.reshape(L, 1, N))


def _rwkv_proj_kernel(x_ref, xp_ref, mod_ref, g_ref, mu_ref, wrkv_ref, w1_ref, w2_ref, a1_ref, a2_ref,
                      g1_ref, g2_ref, vec_ref, r_ref, k_ref, v_ref, lw_ref, kk_ref, al_ref, gate_ref):
    i = pl.program_id(1)
    g = g_ref[...]
    shift, scale = mod_ref[0, 0:1, :], mod_ref[0, 1:2, :]
    h = _rms_mod(x_ref[0], g, shift, scale)
    hp = _rms_mod(xp_ref[0, SUBLANES - 1:SUBLANES, :], g, shift, scale)
    hp = jnp.where(i == 0, 0.0, hp)
    row = lax.broadcasted_iota(jnp.int32, h.shape, 0)
    h_prev = jnp.where(row == 0, hp, pltpu.roll(h, 1, axis=0))
    xx = h_prev - h
    mu = mu_ref[...]
    xs = [(h + xx * mu[j:j + 1, :]).astype(BF16) for j in range(6)]
    w0, a0, k_k, k_a = (vec_ref[j:j + 1, :] for j in range(4))
    d = functools.partial(jnp.dot, preferred_element_type=F32)
    r = d(xs[0], wrkv_ref[0])
    k = d(xs[1], wrkv_ref[1])
    v = d(xs[2], wrkv_ref[2])
    z = w0 + _dot(jnp.tanh(d(xs[3], w1_ref[...])), w2_ref[...])
    lw = (-math.exp(-0.5)) * jax.nn.sigmoid(z)
    a = jax.nn.sigmoid(a0 + _dot(d(xs[4], a1_ref[...]), a2_ref[...]))
    gate = _dot(jax.nn.sigmoid(d(xs[5], g1_ref[...])), g2_ref[...])
    r_ref[0] = r.astype(BF16)
    k_ref[0] = (k * (1.0 + (a - 1.0) * k_a)).astype(BF16)
    v_ref[0] = v.astype(BF16)
    lw_ref[0] = lw
    kk_ref[0] = (k * k_k).astype(BF16)
    al_ref[0] = a.astype(BF16)
    gate_ref[0] = gate.astype(BF16)


def _rwkv_proj(x, mod2, g, mu, wrkv, w1, w2, a1, a2, g1, g2, vec):
    B, T, D = x.shape
    tm = PROJ_TM
    const2 = lambda b, i: (0, 0)
    const3 = lambda b, i: (0, 0, 0)
    act = pl.BlockSpec((1, tm, D), lambda b, i: (b, i, 0))
    n_sub = tm // SUBLANES
    return pl.pallas_call(
        _rwkv_proj_kernel,
        out_shape=[jax.ShapeDtypeStruct((B, T, D), F32 if n == 3 else BF16) for n in range(7)],
        grid=(B, T // tm),
        in_specs=[
            act,
            pl.BlockSpec((1, SUBLANES, D), lambda b, i: (b, jnp.maximum(i * n_sub - 1, 0), 0)),
            pl.BlockSpec((1, 2, D), lambda b, i: (b, 0, 0)),
            pl.BlockSpec((1, D), const2),
            pl.BlockSpec((6, D), const2),
            pl.BlockSpec((3, D, D), const3),
            pl.BlockSpec(w1.shape, const2), pl.BlockSpec(w2.shape, const2),
            pl.BlockSpec(a1.shape, const2), pl.BlockSpec(a2.shape, const2),
            pl.BlockSpec(g1.shape, const2), pl.BlockSpec(g2.shape, const2),
            pl.BlockSpec((4, D), const2),
        ],
        out_specs=[act] * 7,
        compiler_params=_cparams("arbitrary", "arbitrary"),
        name="rwkv_proj",
    )(x, x, mod2, g, mu, wrkv, w1, w2, a1, a2, g1, g2, vec)


def _rwkv_scan_kernel(r_ref, k_ref, v_ref, lw_ref, kk_ref, al_ref, pv_ref, y_ref, h_ref):
    C = SCAN_CHUNK
    P2 = 2 * C

    @pl.when(pl.program_id(1) == 0)
    def _():
        h_ref[...] = jnp.zeros_like(h_ref)

    lane = lax.broadcasted_iota(jnp.int32, (1, PAIR), 1)
    m_left = (lane < RWKV_HEAD).astype(F32)
    m_right = 1.0 - m_left
    ri = lax.broadcasted_iota(jnp.int32, (P2, P2), 0)
    ci = lax.broadcasted_iota(jnp.int32, (P2, P2), 1)
    same = (ri >= C) == (ci >= C)
    strict = same & (ri > ci)
    incl = same & (ri >= ci)
    eye = ri == ci
    block_ones = same.astype(BF16)
    tri = (lax.broadcasted_iota(jnp.int32, (C, C), 0) >= lax.broadcasted_iota(jnp.int32, (C, C), 1)).astype(BF16)

    def stack(x):
        return jnp.concatenate([x * m_left, x * m_right], axis=0)

    def head_sums(x):
        s_left = jnp.sum(x * m_left, axis=-1, keepdims=True)
        s_right = jnp.sum(x * m_right, axis=-1, keepdims=True)
        return jnp.where(lane < RWKV_HEAD, s_left, s_right)

    inv_n = 1.0 / RWKV_HEAD
    dd = functools.partial(jnp.dot, preferred_element_type=F32)
    n_pairs = RWKV_HEADS // 2
    units = [(ch, p) for ch in range(SCAN_CHUNKS_PER_STEP) for p in range(n_pairs)]
    idx = [(slice(ch * C, (ch + 1) * C), slice(p * PAIR, (p + 1) * PAIR)) for ch, p in units]
    U = range(len(units))
    ld = lambda ref, rs, sl: ref[0, rs, sl].astype(F32)
    kkr = [ld(kk_ref, rs, sl) for rs, sl in idx]
    ss = [head_sums(x * x) for x in kkr]
    lws = [lw_ref[0, rs, sl] for rs, sl in idx]
    Ls = []
    for lw in lws:
        l_hi, l_lo = _split(lw)
        cs = dd(tri, jnp.concatenate([l_hi, l_lo], axis=1))
        Ls.append(cs[:, :PAIR] + cs[:, PAIR:])
    lhs_g, rhs_g, bk_hat, vs, at32, rt32, dec_end = [], [], [], [], [], [], []
    for u in U:
        rs, sl = idx[u]
        L, lw = Ls[u], lws[u]
        kk = kkr[u] * lax.rsqrt(jnp.maximum(ss[u], 1e-24))
        b_vec = kk * ld(al_ref, rs, sl)
        k = ld(k_ref, rs, sl)
        LC = L[C - 1:C, :]
        e_neg = jnp.exp(-L)
        e_end = jnp.exp(LC - L)
        At = stack(-kk * jnp.exp(L - lw))
        Rt = stack(ld(r_ref, rs, sl) * jnp.exp(L))
        at32.append(At)
        rt32.append(Rt)
        lhs_g.append(jnp.concatenate([At, Rt], axis=0).astype(BF16))
        rhs_g.append(jnp.concatenate([stack(b_vec * e_neg), stack(k * e_neg)], axis=0).astype(BF16))
        bk_hat.append(jnp.concatenate([stack(b_vec * e_end), stack(k * e_end)], axis=0))
        vs.append(stack(ld(v_ref, rs, sl)).astype(BF16))
        dec_end.append(jnp.exp(LC))
    G = [lax.dot_general(lhs_g[u], rhs_g[u], (((1,), (1,)), ((), ())), preferred_element_type=F32) for u in U]
    A_ak = [jnp.where(strict, G[u][:P2, P2:], 0.0).astype(BF16) for u in U]
    A_r = [jnp.concatenate([jnp.where(incl, G[u][P2:, :P2], 0.0), jnp.where(incl, G[u][P2:, P2:], 0.0)],
                           axis=1).astype(BF16) for u in U]
    W = [dd(A_ak[u], vs[u]) for u in U]
    bsz = lambda b: (ri >> int(math.log2(b))) == (ci >> int(math.log2(b)))
    b8 = bsz(SCAN_BASE_BLOCK)
    D1 = [jnp.where(strict & b8, G[u][:P2, :P2], 0.0).astype(BF16) for u in U]
    D2 = [dd(D1[u], D1[u]).astype(BF16) for u in U]
    D4 = [dd(D2[u], D2[u]).astype(BF16) for u in U]
    eye_f = eye.astype(F32)
    P1 = [eye_f + D1[u].astype(F32) + D2[u].astype(F32) + dd(D1[u], D2[u]) for u in U]
    Tm = [P1[u] + dd(P1[u].astype(BF16), D4[u]) for u in U]
    blk = SCAN_BASE_BLOCK
    while blk < C:
        off = strict & bsz(2 * blk) & ~bsz(blk)
        Mo = [jnp.where(off, G[u][:P2, :P2], 0.0).astype(BF16) for u in U]
        Tb = [Tm[u].astype(BF16) for u in U]
        TM = [dd(Tb[u], Mo[u]).astype(BF16) for u in U]
        Tm = [Tm[u] + dd(TM[u], Tb[u]) for u in U]
        blk *= 2
    Z = [dd(Tm[u].astype(BF16), jnp.concatenate([at32[u], W[u]], axis=1).astype(BF16)) for u in U]
    rhs = [jnp.concatenate([Z[u].astype(BF16), jnp.concatenate([jnp.zeros_like(vs[u]), vs[u]], axis=1)], axis=0)
           for u in U]
    o6 = [dd(A_r[u], rhs[u]) for u in U]
    o7 = [dd(bk_hat[u].T.astype(BF16), rhs[u]) for u in U]
    H = [h_ref[p] for p in range(n_pairs)]
    Y = [None] * len(units)
    for u in U:
        p = units[u][1]
        Hb = H[p].astype(BF16)
        Y[u] = dd((rt32[u] + o6[u][:, :PAIR]).astype(BF16), Hb) + o6[u][:, PAIR:]
        Mbd = o7[u][:, :PAIR] + jnp.where(eye, dec_end[u], 0.0)
        H[p] = dd(Mbd.astype(BF16), Hb) + o7[u][:, PAIR:]
    for p in range(n_pairs):
        h_ref[p] = H[p]
    ys = [Y[u][:C] + Y[u][C:] for u in U]
    rk = [ld(r_ref, rs, sl) * ld(k_ref, rs, sl) * pv_ref[0:1, sl] for rs, sl in idx]
    st1 = [head_sums(jnp.concatenate([ys[u], rk[u]], axis=0)) for u in U]
    yc = [ys[u] - st1[u][:C] * inv_n for u in U]
    var = [head_sums(yc[u] * yc[u]) * inv_n for u in U]
    for u in U:
        rs, sl = idx[u]
        bonus = st1[u][C:] * ld(v_ref, rs, sl)
        y_ref[0, rs, sl] = (yc[u] * lax.rsqrt(var[u] + RWKV_GN_EPS) * pv_ref[1:2, sl] + pv_ref[2:3, sl]
                            + bonus).astype(y_ref.dtype)


def _rwkv_scan(r, k, v, lw, kk, al, pvec):
    B, T, D = r.shape
    rows = SCAN_CHUNK * SCAN_CHUNKS_PER_STEP
    act = pl.BlockSpec((1, rows, D), lambda b, c: (b, c, 0))
    return pl.pallas_call(
        _rwkv_scan_kernel,
        out_shape=jax.ShapeDtypeStruct((B, T, D), BF16),
        grid=(B, T // rows),
        in_specs=[act] * 6 + [pl.BlockSpec((3, D), lambda b, c: (0, 0))],
        out_specs=act,
        scratch_shapes=[pltpu.VMEM((RWKV_HEADS // 2, PAIR, PAIR), F32)],
        compiler_params=_cparams("arbitrary", "arbitrary"),
        name="rwkv_scan",
    )(r, k, v, lw, kk, al, pvec)


def _proj_res_route_kernel(*refs, has_gate):
    if has_gate:
        y_ref, g_ref, x_ref, gm_ref, w_ref, modf_ref, gf_ref, wr_ref, o_ref, info_ref, cnt_ref = refs
        y = y_ref[0].astype(F32) * g_ref[0].astype(F32)
    else:
        y_ref, x_ref, gm_ref, w_ref, modf_ref, gf_ref, wr_ref, o_ref, info_ref, cnt_ref = refs
        y = y_ref[0]
    x_new =x_ref[0] + gm_ref[0] * jnp.dot(y.astype(BF16), w_ref[...], preferred_element_type=F32)
    o_ref[0] = x_new
    h = _rms_mod(x_new, gf_ref[...], modf_ref[0, 0:1, :], modf_ref[0, 1:2, :])
    first = (pl.program_id(0) == 0) & (pl.program_id(1) == 0)
    _route(h, wr_ref[...], info_ref, cnt_ref, first)


def _proj_res_route(y, g, x, gm, w, mod_f, norm_f, w_cat):
    B, T, D = x.shape
    tm = DENSE_TM
    per_b = T // tm
    act =pl.BlockSpec((1, tm, D), lambda b, i: (b, i, 0))
    const = lambda b, i: (0, 0)
    flat = lambda b, i: (b * per_b + i, 0)
    ins = [y] + ([g] if g is not None else []) + [x, gm, w, mod_f, norm_f, w_cat]
    specs = [act] * (len(ins) - 5) + [pl.BlockSpec((1, 1, D), lambda b, i: (b, 0, 0)), pl.BlockSpec((D, D), const),
                                     pl.BlockSpec((1, 2, D), lambda b, i: (b, 0, 0)), pl.BlockSpec((1, D), const),
                                     pl.BlockSpec((D, LANES), const)]
    return pl.pallas_call(
        functools.partial(_proj_res_route_kernel, has_gate=g is not None),
        out_shape=[jax.ShapeDtypeStruct((B, T, D), F32), jax.ShapeDtypeStruct((B * T, LANES), F32),
                   jax.ShapeDtypeStruct((SUBLANES, LANES), F32)],
        grid=(B, per_b),
        in_specs=specs,
        out_specs=[act, pl.BlockSpec((tm, LANES), flat), pl.BlockSpec((SUBLANES, LANES), const)],
        compiler_params=_cparams("arbitrary", "arbitrary"),
        name="proj_res_route",
    )(*ins)


def _norm_mm_kernel(*refs, n_proj):
    x_ref = refs[0]
    ins, outs = refs[1:1 + 3 * n_proj], refs[1 + 3 * n_proj:]
    x = x_ref[0]
    xn = x * lax.rsqrt(jnp.mean(x * x, axis=-1, keepdims=True) + RMS_EPS)
    for p in range(n_proj):
        mod_ref, g_ref, w_ref = ins[3 * p:3 * p + 3]
        h = (xn * g_ref[...]) * (1.0 + mod_ref[0, 1:2, :]) + mod_ref[0, 0:1, :]
        outs[p][0] = jnp.dot(h.astype(BF16), w_ref[...], preferred_element_type=F32).astype(outs[p].dtype)


def _norm_mm(x, projs, out_dtype):
    B, T, D = x.shape
    tm = DENSE_TM
    in_specs = [pl.BlockSpec((1, tm, D), lambda b, i: (b, i, 0))]
    args = [x]
    for mod2, g, w in projs:
        in_specs += [pl.BlockSpec((1, 2, D), lambda b, i: (b, 0, 0)), pl.BlockSpec((1, D), lambda b, i: (0, 0)),
                     pl.BlockSpec(w.shape, lambda b, i: (0, 0))]
        args += [mod2, g, w]
    return pl.pallas_call(
        functools.partial(_norm_mm_kernel, n_proj=len(projs)),
        out_shape=[jax.ShapeDtypeStruct((B, T, w.shape[1]), out_dtype) for _, _, w in projs],
        grid=(B, T // tm),
        in_specs=in_specs,
        out_specs=[pl.BlockSpec((1, tm, w.shape[1]), lambda b, i: (b, i, 0)) for _, _, w in projs],
        compiler_params=_cparams("arbitrary", "arbitrary"),
        name="norm_mm",
    )(*args)


def _diff_attn_kernel(q_ref, k_ref, v_ref, lam_ref, sg_ref, o_ref, m_ref, acc_ref, s_ref, *, lambda_init):
    tq, HB, dv = ATT_TQ, ATT_HB, DIFF_V_DIM
    qi = pl.program_id(2)
    heads = range(HB)
    hs = [slice(h * dv, (h + 1) * dv) for h in heads]
    lane = lax.broadcasted_iota(jnp.int32, (1, dv), 1)
    m_left = (lane < DIFF_QK_DIM).astype(F32)
    qs = []
    for h in heads:
        q = q_ref[0, :, hs[h]].astype(F32) * (DIFF_QK_DIM ** -0.5 * math.log2(math.e))
        qs.append(jnp.concatenate([q * m_left, q * (1.0 - m_left)], axis=0).astype(BF16))
    ones_col = jnp.ones((tq, dv), BF16)
    causal = (lax.broadcasted_iota(jnp.int32, (2 * tq, tq), 1)
              <= lax.broadcasted_iota(jnp.int32, (2 * tq, tq), 0) % tq)

    def key_rows(j):
        return pl.ds(pl.multiple_of(j * tq, tq), tq)

    def scores_into(slot, j):
        for h in heads:
            s_ref[slot, h] = lax.dot_general(qs[h], k_ref[0, key_rows(j), hs[h]], (((1,), (1,)), ((), ())),
                                             preferred_element_type=F32)

    def block(slot, j, first, prefetch):
        if prefetch is not None:
            scores_into(1 - slot, prefetch)
        for h in heads:
            s = s_ref[slot, h]
            if first:
                s = jnp.where(causal, s, -jnp.inf)
                m_new = jnp.broadcast_to(jnp.max(s, axis=-1, keepdims=True), (2 * tq, LANES))
            else:
                m_old = m_ref[h]
                m_new = jnp.maximum(m_old, jnp.max(s, axis=-1, keepdims=True))
            m_ref[h] = m_new
            p = jnp.exp2(s - jnp.concatenate([m_new] * (tq // LANES), axis=1)).astype(BF16)
            pv = jnp.dot(p, jnp.concatenate([v_ref[0, key_rows(j), hs[h]], ones_col], axis=1),
                         preferred_element_type=F32)
            if first:
                acc_ref[h] = pv
            else:
                alpha = jnp.exp2(m_old - m_new)
                acc_ref[h] = acc_ref[h] * jnp.concatenate([alpha, alpha], axis=1) + pv

    scores_into(0, qi)
    block(0, qi, True, 0)

    def pair(u, c):
        last = jnp.maximum(qi - 1, 0)
        block(1, 2 * u, False, jnp.minimum(2 * u + 1, last))
        block(0, 2 * u + 1, False, jnp.minimum(2 * u + 2, last))
        return c

    lax.fori_loop(0, qi // 2, pair, 0)

    @pl.when(qi % 2 == 1)
    def _():
        block(1, qi - 1, False, None)

    lv = lam_ref[...]
    lam = (jnp.exp(jnp.sum(lv[0:1] * lv[1:2], axis=-1, keepdims=True))
           - jnp.exp(jnp.sum(lv[2:3] * lv[3:4], axis=-1, keepdims=True)) + lambda_init)
    for h in heads:
        acc = acc_ref[h]
        o = acc[:, :dv] / acc[:, dv:]
        o = o[:tq] - lam * o[tq:]
        ms = jnp.mean(o * o, axis=-1, keepdims=True)
        o_ref[0, :, hs[h]] = (o * lax.rsqrt(ms + SUBLN_EPS) * sg_ref[...] * (1.0 - lambda_init)).astype(o_ref.dtype)


def _diff_attn(q, kv, lam_vecs, subln_g, lambda_init):
    B, T, D = q.shape
    tq, HB = ATT_TQ, ATT_HB
    n_hb = DIFF_HEADS // HB
    w = HB * DIFF_V_DIM
    return pl.pallas_call(
        functools.partial(_diff_attn_kernel, lambda_init=lambda_init),
        out_shape=jax.ShapeDtypeStruct((B, T, D), BF16),
        grid=(B, n_hb, T // tq),
        in_specs=[
            pl.BlockSpec((1, tq, w), lambda b, h, i: (b, i, h)),
            pl.BlockSpec((1, T, w), lambda b, h, i: (b, 0, h)),
            pl.BlockSpec((1, T, w), lambda b, h, i: (b, 0, n_hb + h)),
            pl.BlockSpec((4, DIFF_QK_DIM), lambda b, h, i: (0, 0)),
            pl.BlockSpec((1, DIFF_V_DIM), lambda b, h, i: (0, 0)),
        ],
        out_specs=pl.BlockSpec((1, tq, w), lambda b, h, i: (b, i, h)),
        scratch_shapes=[pltpu.VMEM((HB, 2 * tq, LANES), F32), pltpu.VMEM((HB, 2 * tq, 2 * DIFF_V_DIM), F32),
                        pltpu.VMEM((2, HB, 2 * tq, tq), F32)],
        compiler_params=_cparams("arbitrary", "arbitrary", "arbitrary"),
        name="diff_attn",
    )(q, kv, kv, lam_vecs, subln_g)


def _route(h, w_cat, info_ref, cnt_ref, first_step):
    logit = _dot3(h, w_cat)
    lane_i = lax.broadcasted_iota(jnp.int32, logit.shape, 1)
    lane = lane_i.astype(F32)
    neg = -jnp.inf
    big = float(LANES)
    is_grp = lane_i < N_GROUPS
    gl = jnp.where(is_grp, logit, neg)
    gmax = jnp.max(gl, axis=-1, keepdims=True)
    gidx = jnp.min(jnp.where(gl == gmax, lane, big), axis=-1, keepdims=True)
    grp_gate = 1.0 / jnp.sum(jnp.where(is_grp, jnp.exp(logit - gmax), 0.0), axis=-1, keepdims=True)
    lo = N_GROUPS + gidx * EXPERTS_PER_GROUP
    in_grp = (lane >= lo) & (lane < lo + EXPERTS_PER_GROUP)
    el = jnp.where(in_grp, logit, neg)
    t1 = jnp.max(el, axis=-1, keepdims=True)
    i1 = jnp.min(jnp.where(el == t1, lane, big), axis=-1, keepdims=True)
    el2 = jnp.where(lane == i1, neg, el)
    t2 = jnp.max(el2, axis=-1, keepdims=True)
    i2 = jnp.min(jnp.where(el2 == t2, lane, big), axis=-1, keepdims=True)
    e21 = jnp.exp(t2 - t1)
    p1 = 1.0 / (1.0 + e21)
    w1 = grp_gate * p1
    w2 = grp_gate * (e21 * p1)
    e1 = i1 - N_GROUPS
    e2 = i2 - N_GROUPS
    info_ref[...] = jnp.where(lane_i == 0, e1, jnp.where(lane_i == 1, e2, jnp.where(lane_i == 2, w1, jnp.where(lane_i == 3, w2, 0.0))))

    @pl.when(first_step)
    def _():
        cnt_ref[...] = jnp.zeros_like(cnt_ref)

    picked = ((lane == e1) | (lane == e2)).astype(F32)
    cnt_ref[...] += jnp.broadcast_to(jnp.sum(picked, axis=0, keepdims=True), cnt_ref.shape)


def _rank_kernel(info_ref, cnt_ref, dest_ref, meta_ref, start_ref):
    i = pl.program_id(0)
    tb = info_ref.shape[0]
    lane = lax.broadcasted_iota(jnp.int32, (tb, LANES), 1)
    info = info_ref[...]
    e0 = info[:, 0:1].astype(jnp.int32)
    e1 = info[:, 1:2].astype(jnp.int32)
    o0 = (lane == e0).astype(F32)
    o1 = (lane == e1).astype(F32)
    both = o0 + o1

    @pl.when(i == 0)
    def _():
        cnt = cnt_ref[0:1, :]
        padded = jnp.floor((cnt + (MOE_TM - 1)) * (1.0 / MOE_TM)) * MOE_TM
        r = lax.broadcasted_iota(jnp.int32, (LANES, LANES), 0)
        c = lax.broadcasted_iota(jnp.int32, (LANES, LANES), 1)
        upper_strict = (r < c).astype(BF16)
        start = _dot_hl(jnp.broadcast_to(padded, (SUBLANES, LANES)), upper_strict)[0:1]
        start_ref[...] = start
        row = lax.broadcasted_iota(jnp.int32, (SUBLANES, LANES), 0)
        meta_ref[...] = jnp.where(row == 0, start + padded, start + cnt)

    r = lax.broadcasted_iota(jnp.int32, (tb, tb), 0)
    c = lax.broadcasted_iota(jnp.int32, (tb, tb), 1)
    lower_strict = (r > c).astype(BF16)
    before = jnp.dot(lower_strict, both.astype(BF16), preferred_element_type=F32) + start_ref[...]
    d0 = jnp.sum(o0 * before, axis=-1, keepdims=True)
    d1 = jnp.sum(o1 * before, axis=-1, keepdims=True)
    dest_ref[...] = jnp.where(lane == 0, d0, jnp.where(lane == 1, d1, 0.0)).astype(jnp.int32)
    start_ref[...] += jnp.sum(both, axis=0, keepdims=True)


def _rank(info, cnt):
    n_tok = info.shape[0]
    tb = RANK_TB
    n_blk = n_tok // tb
    return pl.pallas_call(
        _rank_kernel,
        out_shape=[jax.ShapeDtypeStruct((n_tok, LANES), jnp.int32), jax.ShapeDtypeStruct((SUBLANES, LANES), F32)],
        grid=(n_blk,),
        in_specs=[pl.BlockSpec((tb, LANES), lambda i: (i, 0)), pl.BlockSpec((SUBLANES, LANES), lambda i: (0, 0))],
        out_specs=[pl.BlockSpec((tb, LANES), lambda i: (i, 0)), pl.BlockSpec((SUBLANES, LANES), lambda i: (0, 0))],
        scratch_shapes=[pltpu.VMEM((1, LANES), F32)],
        compiler_params=_cparams("arbitrary"),
        name="moe_rank",
    )(info, cnt)


assert D_MODEL == SUBLANES * LANES


def _tile_rows_store(ref, x):
    n = x.shape[0]
    for s in range(SUBLANES):
        ref[pl.ds(s, n, stride=SUBLANES), :] = x[:, s * LANES:(s + 1) * LANES]


def _tile_rows_load(ref, n):
    return jnp.concatenate([ref[pl.ds(s, n, stride=SUBLANES), :] for s in range(SUBLANES)], axis=1)


def _row_copy(src_ref, s, dst_ref, d, sem):
    rows = lambda r: pl.ds(pl.multiple_of(r * SUBLANES, SUBLANES), SUBLANES)
    return pltpu.make_async_copy(src_ref.at[rows(s)], dst_ref.at[rows(d)], sem)


def _zero_padding_rows(seg_ref, xs_ref, zero_ref, zsem, *, start):
    def copy(first_row, n_rows):
        cp = pltpu.make_async_copy(zero_ref.at[pl.ds(0, n_rows * SUBLANES)],
                                   xs_ref.at[pl.ds(pl.multiple_of(first_row * SUBLANES, SUBLANES), n_rows * SUBLANES)],
                                   zsem)
        cp.start() if start else cp.wait()

    for e in range(N_EXPERTS):
        lo, n = seg_ref[1, e], seg_ref[0, e] - seg_ref[1, e]
        bit = MOE_TM // 2
        while bit >= 1:
            @pl.when((n & bit) != 0)
            def _(lo=lo, n=n, bit=bit):
                copy(lo + (n & ~(2 * bit - 1)), bit)
            bit //= 2
    n_total = xs_ref.shape[0] // SUBLANES

    def tail(j, c):
        copy(seg_ref[0, N_EXPERTS - 1] + j * MOE_TM, MOE_TM)
        return c

    lax.fori_loop(0, (n_total - seg_ref[0, N_EXPERTS - 1]) // MOE_TM, tail, 0)


def _dispatch_kernel(dest_ref, seg_ref, x_ref, mod_ref, g_ref, xs_ref, h_ref, zero_ref, sem, zsem):
    tb = x_ref.shape[0]
    i = pl.program_id(0)
    last = pl.num_programs(0) - 1
    h = _rms_mod(x_ref[...], g_ref[...], mod_ref[0, 0:1, :], mod_ref[0, 1:2, :])

    @pl.when(i == 0)
    def _():
        zero_ref[...] = jnp.zeros_like(zero_ref)
        _zero_padding_rows(seg_ref, xs_ref, zero_ref, zsem, start=True)

    @pl.when(i == last)
    def _():
        _zero_padding_rows(seg_ref, xs_ref, zero_ref, zsem, start=False)

    def drain(slot):
        def wait(j, c):
            _row_copy(h_ref.at[slot], 0, xs_ref, 0, sem.at[slot]).wait()
            return c
        lax.fori_loop(0, 2 * tb, wait, 0, unroll=8)

    for slot in range(2):
        @pl.when(i % 2 == slot)
        def _(slot=slot):
            @pl.when(i >= 2)
            def _():
                drain(slot)

            _tile_rows_store(h_ref.at[slot], h)

            def start(j, c):
                _row_copy(h_ref.at[slot], j, xs_ref, dest_ref[0, 0, 2 * j], sem.at[slot]).start(priority=0)
                _row_copy(h_ref.at[slot], j, xs_ref, dest_ref[0, 0, 2 * j + 1], sem.at[slot]).start(priority=1)
                return c

            lax.fori_loop(0, tb, start, 0, unroll=8)

            @pl.when(i == last)
            def _():
                drain(slot)

                @pl.when(i >= 1)
                def _():
                    drain(1 - slot)


def _dispatch(dest3, seg, x2, mod2, g, n_rows, T):
    n_tok, D = x2.shape
    tb = ROW_TB
    per_b = T // tb
    return pl.pallas_call(
        _dispatch_kernel,
        out_shape=jax.ShapeDtypeStruct((n_rows * SUBLANES, LANES), F32),
        grid=(n_tok // tb,),
        in_specs=[
            pl.BlockSpec((1, 1, 2 * tb), lambda i: (i, 0, 0), memory_space=pltpu.SMEM),
            pl.BlockSpec(memory_space=pltpu.SMEM),
            pl.BlockSpec((tb, D), lambda i: (i, 0)),
            pl.BlockSpec((1, 2, D), lambda i: (i // per_b, 0, 0)),
            pl.BlockSpec((1, D), lambda i: (0, 0)),
        ],
        out_specs=pl.BlockSpec(memory_space=pl.ANY),
        scratch_shapes=[pltpu.VMEM((2, tb * SUBLANES, LANES), F32), pltpu.VMEM((MOE_TM * SUBLANES, LANES), F32),
                        pltpu.SemaphoreType.DMA((2,)), pltpu.SemaphoreType.DMA(())],
        compiler_params=_cparams("arbitrary"),
        name="moe_dispatch",
    )(dest3, seg, x2, mod2, g)


def _expert_kernel(be_ref, nb_ref, nxt_ref, xs_ref, wg_hbm, wu_hbm, wd_hbm, ys_ref, stage_g, stage_u, stage_d,
                   wgb, wub, wdb, sem, *, layer):
    i = pl.program_id(0)
    e = be_ref[i]
    changed = (i == 0) | (e != be_ref[jnp.maximum(i - 1, 0)])

    def fetch(ex):
        return (pltpu.make_async_copy(wg_hbm.at[layer, ex], stage_g, sem.at[0]),
                pltpu.make_async_copy(wu_hbm.at[layer, ex], stage_u, sem.at[1]),
                pltpu.make_async_copy(wd_hbm.at[layer, ex], stage_d, sem.at[2]))

    @pl.when(i == 0)
    def _():
        for cp in fetch(e):
            cp.start()

    @pl.when(changed)
    def _():
        for cp in fetch(e):
            cp.wait()
        wgb[...] = stage_g[...].astype(BF16)
        wub[...] = stage_u[...].astype(BF16)
        wdb[...] = stage_d[...].astype(BF16)

        @pl.when(nxt_ref[i] >= 0)
        def _():
            for cp in fetch(nxt_ref[i]):
                cp.start()

    @pl.when(i < nb_ref[0])
    def _():
        x = _tile_rows_load(xs_ref, MOE_TM).astype(BF16)
        a = jnp.dot(x, wgb[...], preferred_element_type=F32)
        u = jnp.dot(x, wub[...], preferred_element_type=F32)
        hdn = (a * jax.nn.sigmoid(a)) * u
        _tile_rows_store(ys_ref, jnp.dot(hdn.astype(BF16), wdb[...], preferred_element_type=F32))

    @pl.when(i >= nb_ref[0])
    def _():
        ys_ref[...] = jnp.zeros_like(ys_ref)


def _experts(blk_e, n_used, next_e, xs, w_gate, w_up, w_down, layer):
    D = D_MODEL
    n_rows = xs.shape[0] // SUBLANES
    tm = MOE_TM
    FF = EXPERT_FF
    row_block = pl.BlockSpec((tm * SUBLANES, LANES), lambda i, be, nb, nx: (i, 0))
    hbm = pl.BlockSpec(memory_space=pl.ANY)
    grid_spec = pltpu.PrefetchScalarGridSpec(
        num_scalar_prefetch=3,
        grid=(n_rows // tm,),
        in_specs=[row_block, hbm, hbm, hbm],
        out_specs=row_block,
        scratch_shapes=[pltpu.VMEM((D, FF), F32), pltpu.VMEM((D, FF), F32), pltpu.VMEM((FF, D), F32),
                        pltpu.VMEM((D, FF), BF16), pltpu.VMEM((D, FF), BF16), pltpu.VMEM((FF, D), BF16),
                        pltpu.SemaphoreType.DMA((3,))],
    )
    return pl.pallas_call(
        functools.partial(_expert_kernel, layer=layer),
        out_shape=jax.ShapeDtypeStruct(xs.shape, F32),
        grid_spec=grid_spec,
        compiler_params=_cparams("arbitrary"),
        name="moe_experts",
    )(blk_e, n_used, next_e, xs, w_gate, w_up, w_down)


def _combine_kernel(dest_ref, dest_next_ref, info_ref, x_ref, gf_ref, fg_ref, ys_ref, o_ref, y_ref, sem, *,
                    final_norm):
    tb = x_ref.shape[0]
    i = pl.program_id(0)
    last = pl.num_programs(0) - 1

    def gather(d_ref, slot):
        def start(j, c):
            _row_copy(ys_ref, d_ref[0, 0, 2 * j], y_ref.at[slot, 0], j, sem.at[slot]).start(priority=0)
            _row_copy(ys_ref, d_ref[0, 0, 2 * j + 1], y_ref.at[slot, 1], j, sem.at[slot]).start(priority=1)
            return c
        lax.fori_loop(0, tb, start, 0, unroll=8)

    @pl.when(i == 0)
    def _():
        gather(dest_ref, 0)

    for slot in range(2):
        @pl.when(i % 2 == slot)
        def _(slot=slot):
            @pl.when(i < last)
            def _():
                gather(dest_next_ref, 1 - slot)

            def wait(j, c):
                _row_copy(ys_ref, 0, y_ref.at[slot, 0], 0, sem.at[slot]).wait()
                return c

            lax.fori_loop(0, 2 * tb, wait, 0, unroll=8)
            info = info_ref[...]
            moe = (info[:, 2:3] * _tile_rows_load(y_ref.at[slot, 0], tb)
                   + info[:, 3:4] * _tile_rows_load(y_ref.at[slot, 1], tb))
            out = x_ref[...] + gf_ref[0] * moe
            if final_norm:
                ms = jnp.mean(out * out, axis=-1, keepdims=True)
                out = out * lax.rsqrt(ms + RMS_EPS) * fg_ref[...]
            o_ref[...] = out


def _combine(dest3, info, x2, gf, final_g, ys, T, final_norm):
    n_tok, D = x2.shape
    tb = ROW_TB
    per_b = T // tb
    n_steps = n_tok // tb
    return pl.pallas_call(
        functools.partial(_combine_kernel, final_norm=final_norm),
        out_shape=jax.ShapeDtypeStruct((n_tok, D), F32),
        grid=(n_steps,),
        in_specs=[
            pl.BlockSpec((1, 1, 2 * tb), lambda i: (i, 0, 0), memory_space=pltpu.SMEM),
            pl.BlockSpec((1, 1, 2 * tb), lambda i: (jnp.minimum(i + 1, n_steps - 1), 0, 0), memory_space=pltpu.SMEM),
            pl.BlockSpec((tb, LANES), lambda i: (i, 0)),
            pl.BlockSpec((tb, D), lambda i: (i, 0)),
            pl.BlockSpec((1, 1, D), lambda i: (i // per_b, 0, 0)),
            pl.BlockSpec((1, D), lambda i: (0, 0)),
            pl.BlockSpec(memory_space=pl.ANY),
        ],
        out_specs=pl.BlockSpec((tb, D), lambda i: (i, 0)),
        scratch_shapes=[pltpu.VMEM((2, 2, tb * SUBLANES, LANES), F32), pltpu.SemaphoreType.DMA((2,))],
        compiler_params=_cparams("arbitrary"),
        name="moe_combine",
    )(dest3, dest3, info, x2, gf, final_g, ys)


def _moe_rows(n_tok):
    return -(-(2 * n_tok + N_EXPERTS * MOE_TM) // MOE_TM) * MOE_TM


def _moe_layer(x, info, cnt, mod_f, gf, norm_g, w_gate, w_up, w_down, layer, final_g, final_norm):
    B, T, D = x.shape
    n_tok = B * T
    x2 = x.reshape(n_tok, D)
    n_rows = _moe_rows(n_tok)
    dest, meta = _rank(info, cnt)
    seg = meta[0:2, :N_EXPERTS].astype(jnp.int32)
    pad_end = seg[0]
    n_blocks = n_rows // MOE_TM
    blk_start = jnp.arange(n_blocks, dtype=jnp.int32) * MOE_TM
    blk_e = jnp.minimum(jnp.sum(pad_end[None, :] <= blk_start[:, None], axis=1), N_EXPERTS - 1).astype(jnp.int32)
    n_used = (pad_end[N_EXPERTS - 1:] // MOE_TM).astype(jnp.int32)
    dest3 = dest[:, :2].reshape(n_tok // ROW_TB, 1, 2 * ROW_TB)
    xs = _dispatch(dest3, seg, x2, mod_f, norm_g, n_rows, T)
    seg_end = jnp.sum(blk_e[None, :] <= blk_e[:, None], axis=1)
    next_e = jnp.where(seg_end < n_blocks, blk_e[jnp.minimum(seg_end, n_blocks - 1)], -1).astype(jnp.int32)
    ys = _experts(blk_e, n_used, next_e, xs, w_gate, w_up, w_down, layer)
    out = _combine(dest3, info, x2, gf, final_g, ys, T, final_norm)
    return out.reshape(B, T, D)


def kernel(x, c, ada_w, ada_b, norm_mix_g, norm_ffn_g, rw_mu, rw_w_rkv, rw_w0, rw_w1, rw_w2, rw_a0, rw_a1, rw_a2,
           rw_g1, rw_g2, rw_k_k, rw_k_a, rw_r_k, rw_gn_g, rw_gn_b, rw_w_o, ada_kv_w, ada_kv_b, norm_kv_g, w_kv,
           df_w_q, df_lq1, df_lk1, df_lq2, df_lk2, df_subln_g, df_w_o, moe_w_rg, moe_w_re, moe_w_gate, moe_w_up,
           moe_w_down, final_g):
    B, T, D = x.shape
    c_pad = jnp.zeros((SUBLANES, D), F32).at[:B].set(c)
    mod = _ada(c_pad, ada_w, ada_b, 6 * D // 4)[:, :B]
    mod_kv = _ada(c_pad, ada_kv_w[None], ada_kv_b[None], D)[0, :B]
    bf = lambda w: w.astype(BF16)
    row = lambda v: v.reshape(1, -1)

    for l in range(DEPTH):
        sh_m, sc_m, g_m, sh_f, sc_f, g_f = jnp.split(mod[l], 6, axis=-1)
        mod_m = jnp.stack([sh_m, sc_m], axis=1)
        mod_f = jnp.stack([sh_f, sc_f], axis=1)
        if l < N_A_LAYERS:
            i = l
            vec = jnp.stack([rw_w0[i], rw_a0[i], rw_k_k[i], rw_k_a[i]], axis=0)
            r, k, v, lw, kk, al, gate = _rwkv_proj(
                x, mod_m, row(norm_mix_g[l]), rw_mu[i], bf(rw_w_rkv[i]), bf(rw_w1[i]), bf(rw_w2[i]),
                bf(rw_a1[i]), bf(rw_a2[i]), bf(rw_g1[i]), bf(rw_g2[i]), vec)
            pvec = jnp.stack([rw_r_k[i].reshape(-1), rw_gn_g[i], rw_gn_b[i]], axis=0)
            y = _rwkv_scan(r, k, v, lw, kk, al, pvec)
            w_o = bf(rw_w_o[i])
        else:
            j = l - N_A_LAYERS
            q_proj = (mod_m, row(norm_mix_g[l]), bf(df_w_q[j]))
            if l == N_A_LAYERS:
                sh_kv, sc_kv = jnp.split(mod_kv, 2, axis=-1)
                q, kv = _norm_mm(x, [q_proj, (jnp.stack([sh_kv, sc_kv], axis=1), row(norm_kv_g), bf(w_kv))], BF16)
            else:
                q, = _norm_mm(x, [q_proj], BF16)
            lambda_init = 0.8 - 0.6 * math.exp(-0.3 * l)
            lam_vecs = jnp.stack([df_lq1[j], df_lk1[j], df_lq2[j], df_lk2[j]], axis=0)
            y, gate = _diff_attn(q, kv, lam_vecs, row(df_subln_g[j]), lambda_init), None
            w_o = bf(df_w_o[j])
        w_cat = jnp.concatenate([moe_w_rg[l], moe_w_re[l], jnp.zeros((D, LANES - N_GROUPS - N_EXPERTS), F32)], axis=1)
        x, info, cnt = _proj_res_route(y, gate, x, g_m[:, None, :], w_o, mod_f, row(norm_ffn_g[l]), w_cat)
        x = _moe_layer(x, info, cnt, mod_f, g_f[:, None, :], row(norm_ffn_g[l]), moe_w_gate, moe_w_up,
                       moe_w_down, l, row(final_g), final_norm=(l == DEPTH - 1))
    return x
```

```python
import functools
import math

import jax
import jax.numpy as jnp
from jax import lax
from jax.experimental import pallas as pl
from jax.experimental.pallas import tpu as pltpu

F32 = jnp.float32
BF16 = jnp.bfloat16

D_MODEL = 1024
DEPTH = 2
N_A_LAYERS = DEPTH // 2
RWKV_HEAD = 64
RWKV_HEADS = D_MODEL // RWKV_HEAD
RWKV_GN_EPS = 64e-5
DIFF_QK_DIM = 64
DIFF_V_DIM = 2 * DIFF_QK_DIM
DIFF_HEADS = D_MODEL // DIFF_V_DIM
SUBLN_EPS = 1e-5
N_GROUPS = 4
EXPERTS_PER_GROUP = 8
N_EXPERTS = N_GROUPS * EXPERTS_PER_GROUP
EXPERT_FF = 512
RMS_EPS = 1e-6

LANES = 128
SUBLANES = 8
VMEM_LIMIT_BYTES = 56 * 1024 * 1024

SCAN_CHUNK = 64
SCAN_BASE_BLOCK = 8
SCAN_CHUNKS_PER_STEP = 2
PAIR = 2 * RWKV_HEAD
PROJ_TM = 512
DENSE_TM = 512
ATT_TQ = 512
ATT_HB = 2
MOE_TM = 256
RANK_TB = 1024
ROW_TB = 256
assert SCAN_CHUNK == RWKV_HEAD and PAIR == LANES


def _cparams(*sem):
    return pltpu.CompilerParams(dimension_semantics=sem, vmem_limit_bytes=VMEM_LIMIT_BYTES)


def _dot(a, b):
    return jnp.dot(a.astype(BF16), b.astype(BF16), preferred_element_type=F32)


def _split(x):
    hi = x.astype(BF16)
    lo = (x - hi.astype(F32)).astype(BF16)
    return hi, lo


def _dot3(a, b):
    ah, al = _split(a)
    bh, bl = _split(b)
    d = functools.partial(jnp.dot, preferred_element_type=F32)
    return d(ah, bh) + d(ah, bl) + d(al, bh)


def _dot_hl(a, b_exact):
    ah, al = _split(a)
    d = functools.partial(jnp.dot, preferred_element_type=F32)
    return d(ah, b_exact) + d(al, b_exact)


def _rms_mod(x, g, shift, scale):
    ms = jnp.mean(x * x, axis=-1, keepdims=True)
    return (x * lax.rsqrt(ms + RMS_EPS) * g) * (1.0 + scale) + shift


def _ada_kernel(c_ref, w_ref, b_ref, o_ref):
    c = c_ref[...]
    ca = c * jax.nn.sigmoid(c)
    o_ref[...] = _dot3(ca, w_ref[...]) + b_ref[...]


def _ada(c_pad, w, b, tn):
    L, D, N = w.shape
    return pl.pallas_call(
        _ada_kernel,
        out_shape=jax.ShapeDtypeStruct((L, SUBLANES, N), F32),
        grid=(L, N // tn),
        in_specs=[
            pl.BlockSpec((SUBLANES, D), lambda l, j: (0, 0)),
            pl.BlockSpec((None, D, tn), lambda l, j: (l, 0, j)),
            pl.BlockSpec((None, 1, tn), lambda l, j: (l, 0, j)),
        ],
        out_specs=pl.BlockSpec((None, SUBLANES, tn), lambda l, j: (l, 0, j)),
        compiler_params=_cparams("arbitrary", "arbitrary"),
        name="ada_mod",
    )(c_pad, w, b.reshape(L, 1, N))


def _rwkv_proj_kernel(x_ref, xp_ref, mod_ref, g_ref, mu_ref, wrkv_ref, w1_ref, w2_ref, a1_ref, a2_ref,
                      g1_ref, g2_ref, vec_ref, r_ref, k_ref, v_ref, lw_ref, kk_ref, al_ref, gate_ref):
    i = pl.program_id(1)
    g = g_ref[...]
    shift, scale = mod_ref[0, 0:1, :], mod_ref[0, 1:2, :]
    h = _rms_mod(x_ref[0], g, shift, scale)
    hp = _rms_mod(xp_ref[0, SUBLANES - 1:SUBLANES, :], g, shift, scale)
    hp = jnp.where(i == 0, 0.0, hp)
    row = lax.broadcasted_iota(jnp.int32, h.shape, 0)
    h_prev = jnp.where(row == 0, hp, pltpu.roll(h, 1, axis=0))
    xx = h_prev - h
    mu = mu_ref[...]
    xs = [(h + xx * mu[j:j + 1, :]).astype(BF16) for j in range(6)]
    w0, a0, k_k, k_a = (vec_ref[j:j + 1, :] for j in range(4))
    d = functools.partial(jnp.dot, preferred_element_type=F32)
    r = d(xs[0], wrkv_ref[0])
    k = d(xs[1], wrkv_ref[1])
    v = d(xs[2], wrkv_ref[2])
    z = w0 + _dot(jnp.tanh(d(xs[3], w1_ref[...])), w2_ref[...])
    lw = (-math.exp(-0.5)) * jax.nn.sigmoid(z)
    a = jax.nn.sigmoid(a0 + _dot(d(xs[4], a1_ref[...]), a2_ref[...]))
    gate = _dot(jax.nn.sigmoid(d(xs[5], g1_ref[...])), g2_ref[...])
    r_ref[0] = r.astype(BF16)
    k_ref[0] = (k * (1.0 + (a - 1.0) * k_a)).astype(BF16)
    v_ref[0] = v.astype(BF16)
    lw_ref[0] = lw
    kk_ref[0] = (k * k_k).astype(BF16)
    al_ref[0] = a.astype(BF16)
    gate_ref[0] = gate.astype(BF16)


def _rwkv_proj(x, mod2, g, mu, wrkv, w1, w2, a1, a2, g1, g2, vec):
    B, T, D = x.shape
    tm = PROJ_TM
    const2 = lambda b, i: (0, 0)
    const3 = lambda b, i: (0, 0, 0)
    act = pl.BlockSpec((1, tm, D), lambda b, i: (b, i, 0))
    n_sub = tm // SUBLANES
    return pl.pallas_call(
        _rwkv_proj_kernel,
        out_shape=[jax.ShapeDtypeStruct((B, T, D), F32 if n == 3 else BF16) for n in range(7)],
        grid=(B, T // tm),
        in_specs=[
            act,
            pl.BlockSpec((1, SUBLANES, D), lambda b, i: (b, jnp.maximum(i * n_sub - 1, 0), 0)),
            pl.BlockSpec((1, 2, D), lambda b, i: (b, 0, 0)),
            pl.BlockSpec((1, D), const2),
            pl.BlockSpec((6, D), const2),
            pl.BlockSpec((3, D, D), const3),
            pl.BlockSpec(w1.shape, const2), pl.BlockSpec(w2.shape, const2),
            pl.BlockSpec(a1.shape, const2), pl.BlockSpec(a2.shape, const2),
            pl.BlockSpec(g1.shape, const2), pl.BlockSpec(g2.shape, const2),
            pl.BlockSpec((4, D), const2),
        ],
        out_specs=[act] * 7,
        compiler_params=_cparams("arbitrary", "arbitrary"),
        name="rwkv_proj",
    )(x, x, mod2, g, mu, wrkv, w1, w2, a1, a2, g1, g2, vec)


def _rwkv_scan_kernel(r_ref, k_ref, v_ref, lw_ref, kk_ref, al_ref, pv_ref, y_ref, h_ref):
    C = SCAN_CHUNK
    P2 = 2 * C

    @pl.when(pl.program_id(1) == 0)
    def _():
        h_ref[...] = jnp.zeros_like(h_ref)

    lane = lax.broadcasted_iota(jnp.int32, (1, PAIR), 1)
    m_left = (lane < RWKV_HEAD).astype(F32)
    m_right = 1.0 - m_left
    ri = lax.broadcasted_iota(jnp.int32, (P2, P2), 0)
    ci = lax.broadcasted_iota(jnp.int32, (P2, P2), 1)
    same = (ri >= C) == (ci >= C)
    strict = same & (ri > ci)
    incl = same & (ri >= ci)
    eye = ri == ci
    block_ones = same.astype(BF16)
    tri = (lax.broadcasted_iota(jnp.int32, (C, C), 0) >= lax.broadcasted_iota(jnp.int32, (C, C), 1)).astype(BF16)

    def stack(x):
        return jnp.concatenate([x * m_left, x * m_right], axis=0)

    def head_sums(x):
        s_left = jnp.sum(x * m_left, axis=-1, keepdims=True)
        s_right = jnp.sum(x * m_right, axis=-1, keepdims=True)
        return jnp.where(lane < RWKV_HEAD, s_left, s_right)

    inv_n = 1.0 / RWKV_HEAD
    dd = functools.partial(jnp.dot, preferred_element_type=F32)
    n_pairs = RWKV_HEADS // 2
    units = [(ch, p) for ch in range(SCAN_CHUNKS_PER_STEP) for p in range(n_pairs)]
    idx = [(slice(ch * C, (ch + 1) * C), slice(p * PAIR, (p + 1) * PAIR)) for ch, p in units]
    U = range(len(units))
    ld = lambda ref, rs, sl: ref[0, rs, sl].astype(F32)
    kkr = [ld(kk_ref, rs, sl) for rs, sl in idx]
    ss = [head_sums(x * x) for x in kkr]
    lws = [lw_ref[0, rs, sl] for rs, sl in idx]
    Ls = []
    for lw in lws:
        l_hi, l_lo = _split(lw)
        cs = dd(tri, jnp.concatenate([l_hi, l_lo], axis=1))
        Ls.append(cs[:, :PAIR] + cs[:, PAIR:])
    lhs_g, rhs_g, bk_hat, vs, at32, rt32, dec_end = [], [], [], [], [], [], []
    for u in U:
        rs, sl = idx[u]
        L, lw = Ls[u], lws[u]
        kk = kkr[u] * lax.rsqrt(jnp.maximum(ss[u], 1e-24))
        b_vec = kk * ld(al_ref, rs, sl)
        k = ld(k_ref, rs, sl)
        LC = L[C - 1:C, :]
        e_neg = jnp.exp(-L)
        e_end = jnp.exp(LC - L)
        At = stack(-kk * jnp.exp(L - lw))
        Rt = stack(ld(r_ref, rs, sl) * jnp.exp(L))
        at32.append(At)
        rt32.append(Rt)
        lhs_g.append(jnp.concatenate([At, Rt], axis=0).astype(BF16))
        rhs_g.append(jnp.concatenate([stack(b_vec * e_neg), stack(k * e_neg)], axis=0).astype(BF16))
        bk_hat.append(jnp.concatenate([stack(b_vec * e_end), stack(k * e_end)], axis=0))
        vs.append(stack(ld(v_ref, rs, sl)).astype(BF16))
        dec_end.append(jnp.exp(LC))
    G = [lax.dot_general(lhs_g[u], rhs_g[u], (((1,), (1,)), ((), ())), preferred_element_type=F32) for u in U]
    A_ak = [jnp.where(strict, G[u][:P2, P2:], 0.0).astype(BF16) for u in U]
    A_r = [jnp.concatenate([jnp.where(incl, G[u][P2:, :P2], 0.0), jnp.where(incl, G[u][P2:, P2:], 0.0)],
                           axis=1).astype(BF16) for u in U]
    W = [dd(A_ak[u], vs[u]) for u in U]
    bsz = lambda b: (ri >> int(math.log2(b))) == (ci >> int(math.log2(b)))
    b8 = bsz(SCAN_BASE_BLOCK)
    D1 = [jnp.where(strict & b8, G[u][:P2, :P2], 0.0).astype(BF16) for u in U]
    D2 = [dd(D1[u], D1[u]).astype(BF16) for u in U]
    D4 = [dd(D2[u], D2[u]).astype(BF16) for u in U]
    eye_f = eye.astype(F32)
    P1 = [eye_f + D1[u].astype(F32) + D2[u].astype(F32) + dd(D1[u], D2[u]) for u in U]
    Tm = [P1[u] + dd(P1[u].astype(BF16), D4[u]) for u in U]
    blk = SCAN_BASE_BLOCK
    while blk < C:
        off = strict & bsz(2 * blk) & ~bsz(blk)
        Mo = [jnp.where(off, G[u][:P2, :P2], 0.0).astype(BF16) for u in U]
        Tb = [Tm[u].astype(BF16) for u in U]
        TM = [dd(Tb[u], Mo[u]).astype(BF16) for u in U]
        Tm = [Tm[u] + dd(TM[u], Tb[u]) for u in U]
        blk *= 2
    Z = [dd(Tm[u].astype(BF16), jnp.concatenate([at32[u], W[u]], axis=1).astype(BF16)) for u in U]
    rhs = [jnp.concatenate([Z[u].astype(BF16), jnp.concatenate([jnp.zeros_like(vs[u]), vs[u]], axis=1)], axis=0)
           for u in U]
    o6 = [dd(A_r[u], rhs[u]) for u in U]
    o7 = [dd(bk_hat[u].T.astype(BF16), rhs[u]) for u in U]
    H = [h_ref[p] for p in range(n_pairs)]
    Y = [None] * len(units)
    for u in U:
        p = units[u][1]
        Hb = H[p].astype(BF16)
        Y[u] = dd((rt32[u] + o6[u][:, :PAIR]).astype(BF16), Hb) + o6[u][:, PAIR:]
        Mbd = o7[u][:, :PAIR] + jnp.where(eye, dec_end[u], 0.0)
        H[p] = dd(Mbd.astype(BF16), Hb) + o7[u][:, PAIR:]
    for p in range(n_pairs):
        h_ref[p] = H[p]
    ys = [Y[u][:C] + Y[u][C:] for u in U]
    rk = [ld(r_ref, rs, sl) * ld(k_ref, rs, sl) * pv_ref[0:1, sl] for rs, sl in idx]
    st1 = [head_sums(jnp.concatenate([ys[u], rk[u]], axis=0)) for u in U]
    yc = [ys[u] - st1[u][:C] * inv_n for u in U]
    var = [head_sums(yc[u] * yc[u]) * inv_n for u in U]
    for u in U:
        rs, sl = idx[u]
        bonus = st1[u][C:] * ld(v_ref, rs, sl)
        y_ref[0, rs, sl] = (yc[u] * lax.rsqrt(var[u] + RWKV_GN_EPS) * pv_ref[1:2, sl] + pv_ref[2:3, sl]
                            + bonus).astype(y_ref.dtype)


def _rwkv_scan(r, k, v, lw, kk, al, pvec):
    B, T, D = r.shape
    rows = SCAN_CHUNK * SCAN_CHUNKS_PER_STEP
    act = pl.BlockSpec((1, rows, D), lambda b, c: (b, c, 0))
    return pl.pallas_call(
        _rwkv_scan_kernel,
        out_shape=jax.ShapeDtypeStruct((B, T, D), BF16),
        grid=(B, T // rows),
        in_specs=[act] * 6 + [pl.BlockSpec((3, D), lambda b, c: (0, 0))],
        out_specs=act,
        scratch_shapes=[pltpu.VMEM((RWKV_HEADS // 2, PAIR, PAIR), F32)],
        compiler_params=_cparams("arbitrary", "arbitrary"),
        name="rwkv_scan",
    )(r, k, v, lw, kk, al, pvec)


def _proj_res_route_kernel(*refs, has_gate):
    if has_gate:
        y_ref, g_ref, x_ref, gm_ref, w_ref, modf_ref, gf_ref, wr_ref, o_ref, info_ref, cnt_ref = refs
        y = y_ref[0].astype(F32) * g_ref[0].astype(F32)
    else:
        y_ref, x_ref, gm_ref, w_ref, modf_ref, gf_ref, wr_ref, o_ref, info_ref, cnt_ref = refs
        y = y_ref[0]
    x_new =x_ref[0] + gm_ref[0] * jnp.dot(y.astype(BF16), w_ref[...], preferred_element_type=F32)
    o_ref[0] = x_new
    h = _rms_mod(x_new, gf_ref[...], modf_ref[0, 0:1, :], modf_ref[0, 1:2, :])
    first = (pl.program_id(0) == 0) & (pl.program_id(1) == 0)
    _route(h, wr_ref[...], info_ref, cnt_ref, first)


def _proj_res_route(y, g, x, gm, w, mod_f, norm_f, w_cat):
    B, T, D = x.shape
    tm = DENSE_TM
    per_b = T // tm
    act =pl.BlockSpec((1, tm, D), lambda b, i: (b, i, 0))
    const = lambda b, i: (0, 0)
    flat = lambda b, i: (b * per_b + i, 0)
    ins = [y] + ([g] if g is not None else []) + [x, gm, w, mod_f, norm_f, w_cat]
    specs = [act] * (len(ins) - 5) + [pl.BlockSpec((1, 1, D), lambda b, i: (b, 0, 0)), pl.BlockSpec((D, D), const),
                                     pl.BlockSpec((1, 2, D), lambda b, i: (b, 0, 0)), pl.BlockSpec((1, D), const),
                                     pl.BlockSpec((D, 2 * LANES), const)]
    return pl.pallas_call(
        functools.partial(_proj_res_route_kernel, has_gate=g is not None),
        out_shape=[jax.ShapeDtypeStruct((B, T, D), F32), jax.ShapeDtypeStruct((B * T, LANES), F32),
                   jax.ShapeDtypeStruct((SUBLANES, LANES), F32)],
        grid=(B, per_b),
        in_specs=specs,
        out_specs=[act, pl.BlockSpec((tm, LANES), flat), pl.BlockSpec((SUBLANES, LANES), const)],
        compiler_params=_cparams("arbitrary", "arbitrary"),
        name="proj_res_route",
    )(*ins)


def _norm_mm_kernel(*refs, n_proj):
    x_ref = refs[0]
    ins, outs = refs[1:1 + 3 * n_proj], refs[1 + 3 * n_proj:]
    x = x_ref[0]
    xn = x * lax.rsqrt(jnp.mean(x * x, axis=-1, keepdims=True) + RMS_EPS)
    for p in range(n_proj):
        mod_ref, g_ref, w_ref = ins[3 * p:3 * p + 3]
        h = (xn * g_ref[...]) * (1.0 + mod_ref[0, 1:2, :]) + mod_ref[0, 0:1, :]
        outs[p][0] = jnp.dot(h.astype(BF16), w_ref[...], preferred_element_type=F32).astype(outs[p].dtype)


def _norm_mm(x, projs, out_dtype):
    B, T, D = x.shape
    tm = DENSE_TM
    in_specs = [pl.BlockSpec((1, tm, D), lambda b, i: (b, i, 0))]
    args = [x]
    for mod2, g, w in projs:
        in_specs += [pl.BlockSpec((1, 2, D), lambda b, i: (b, 0, 0)), pl.BlockSpec((1, D), lambda b, i: (0, 0)),
                     pl.BlockSpec(w.shape, lambda b, i: (0, 0))]
        args += [mod2, g, w]
    return pl.pallas_call(
        functools.partial(_norm_mm_kernel, n_proj=len(projs)),
        out_shape=[jax.ShapeDtypeStruct((B, T, w.shape[1]), out_dtype) for _, _, w in projs],
        grid=(B, T // tm),
        in_specs=in_specs,
        out_specs=[pl.BlockSpec((1, tm, w.shape[1]), lambda b, i: (b, i, 0)) for _, _, w in projs],
        compiler_params=_cparams("arbitrary", "arbitrary"),
        name="norm_mm",
    )(*args)


def _diff_attn_kernel(q_ref, k_ref, v_ref, lam_ref, sg_ref, o_ref, m_ref, acc_ref, s_ref, *, lambda_init):
    tq, HB, dv = ATT_TQ, ATT_HB, DIFF_V_DIM
    qi = pl.program_id(2)
    heads = range(HB)
    hs = [slice(h * dv, (h + 1) * dv) for h in heads]
    lane = lax.broadcasted_iota(jnp.int32, (1, dv), 1)
    m_left = (lane < DIFF_QK_DIM).astype(F32)
    qs = []
    for h in heads:
        q = q_ref[0, :, hs[h]].astype(F32) * (DIFF_QK_DIM ** -0.5 * math.log2(math.e))
        qs.append(jnp.concatenate([q * m_left, q * (1.0 - m_left)], axis=0).astype(BF16))
    ones_col = jnp.ones((tq, dv), BF16)
    causal = (lax.broadcasted_iota(jnp.int32, (2 * tq, tq), 1)
              <= lax.broadcasted_iota(jnp.int32, (2 * tq, tq), 0) % tq)

    def key_rows(j):
        return pl.ds(pl.multiple_of(j * tq, tq), tq)

    def scores_into(slot, j):
        for h in heads:
            s_ref[slot, h] = lax.dot_general(qs[h], k_ref[0, key_rows(j), hs[h]], (((1,), (1,)), ((), ())),
                                             preferred_element_type=F32)

    def block(slot, j, first, prefetch):
        if prefetch is not None:
            scores_into(1 - slot, prefetch)
        for h in heads:
            s = s_ref[slot, h]
            if first:
                s = jnp.where(causal, s, -jnp.inf)
                m_new = jnp.broadcast_to(jnp.max(s, axis=-1, keepdims=True), (2 * tq, LANES))
            else:
                m_old = m_ref[h]
                m_new = jnp.maximum(m_old, jnp.max(s, axis=-1, keepdims=True))
            m_ref[h] = m_new
            p = jnp.exp2(s - jnp.concatenate([m_new] * (tq // LANES), axis=1)).astype(BF16)
            pv = jnp.dot(p, jnp.concatenate([v_ref[0, key_rows(j), hs[h]], ones_col], axis=1),
                         preferred_element_type=F32)
            if first:
                acc_ref[h] = pv
            else:
                alpha = jnp.exp2(m_old - m_new)
                acc_ref[h] = acc_ref[h] * jnp.concatenate([alpha, alpha], axis=1) + pv

    scores_into(0, qi)
    block(0, qi, True, 0)

    def pair(u, c):
        last = jnp.maximum(qi - 1, 0)
        block(1, 2 * u, False, jnp.minimum(2 * u + 1, last))
        block(0, 2 * u + 1, False, jnp.minimum(2 * u + 2, last))
        return c

    lax.fori_loop(0, qi // 2, pair, 0)

    @pl.when(qi % 2 == 1)
    def _():
        block(1, qi - 1, False, None)

    lv = lam_ref[...]
    lam = (jnp.exp(jnp.sum(lv[0:1] * lv[1:2], axis=-1, keepdims=True))
           - jnp.exp(jnp.sum(lv[2:3] * lv[3:4], axis=-1, keepdims=True)) + lambda_init)
    for h in heads:
        acc = acc_ref[h]
        o = acc[:, :dv] / acc[:, dv:]
        o = o[:tq] - lam * o[tq:]
        ms = jnp.mean(o * o, axis=-1, keepdims=True)
        o_ref[0, :, hs[h]] = (o * lax.rsqrt(ms + SUBLN_EPS) * sg_ref[...] * (1.0 - lambda_init)).astype(o_ref.dtype)


def _diff_attn(q, kv, lam_vecs, subln_g, lambda_init):
    B, T, D = q.shape
    tq, HB = ATT_TQ, ATT_HB
    n_hb = DIFF_HEADS // HB
    w = HB * DIFF_V_DIM
    return pl.pallas_call(
        functools.partial(_diff_attn_kernel, lambda_init=lambda_init),
        out_shape=jax.ShapeDtypeStruct((B, T, D), BF16),
        grid=(B, n_hb, T // tq),
        in_specs=[
            pl.BlockSpec((1, tq, w), lambda b, h, i: (b, i, h)),
            pl.BlockSpec((1, T, w), lambda b, h, i: (b, 0, h)),
            pl.BlockSpec((1, T, w), lambda b, h, i: (b, 0, n_hb + h)),
            pl.BlockSpec((4, DIFF_QK_DIM), lambda b, h, i: (0, 0)),
            pl.BlockSpec((1, DIFF_V_DIM), lambda b, h, i: (0, 0)),
        ],
        out_specs=pl.BlockSpec((1, tq, w), lambda b, h, i: (b, i, h)),
        scratch_shapes=[pltpu.VMEM((HB, 2 * tq, LANES), F32), pltpu.VMEM((HB, 2 * tq, 2 * DIFF_V_DIM), F32),
                        pltpu.VMEM((2, HB, 2 * tq, tq), F32)],
        compiler_params=_cparams("arbitrary", "arbitrary", "arbitrary"),
        name="diff_attn",
    )(q, kv, kv, lam_vecs, subln_g)


def _route(h, w_cat, info_ref, cnt_ref, first_step):
    h_hi, h_lo = _split(h)
    two = jnp.dot(h_hi, w_cat, preferred_element_type=F32)
    logit = (two[:, :LANES] + two[:, LANES:]
             + jnp.dot(h_lo, w_cat[:, :LANES], preferred_element_type=F32))
    lane_i = lax.broadcasted_iota(jnp.int32, logit.shape, 1)
    lane = lane_i.astype(F32)
    neg = -jnp.inf
    big = float(LANES)
    is_grp = lane_i < N_GROUPS
    gl = jnp.where(is_grp, logit, neg)
    gmax = jnp.max(gl, axis=-1, keepdims=True)
    gidx = jnp.min(jnp.where(gl == gmax, lane, big), axis=-1, keepdims=True)
    grp_gate = 1.0 / jnp.sum(jnp.where(is_grp, jnp.exp(logit - gmax), 0.0), axis=-1, keepdims=True)
    lo = N_GROUPS + gidx * EXPERTS_PER_GROUP
    in_grp = (lane >= lo) & (lane < lo + EXPERTS_PER_GROUP)
    el = jnp.where(in_grp, logit, neg)
    t1 = jnp.max(el, axis=-1, keepdims=True)
    i1 = jnp.min(jnp.where(el == t1, lane, big), axis=-1, keepdims=True)
    el2 = jnp.where(lane == i1, neg, el)
    t2 = jnp.max(el2, axis=-1, keepdims=True)
    i2 = jnp.min(jnp.where(el2 == t2, lane, big), axis=-1, keepdims=True)
    e21 = jnp.exp(t2 - t1)
    p1 = 1.0 / (1.0 + e21)
    w1 = grp_gate * p1
    w2 = grp_gate * (e21 * p1)
    e1 = i1 - N_GROUPS
    e2 = i2 - N_GROUPS
    info_ref[...] = jnp.where(lane_i == 0, e1, jnp.where(lane_i == 1, e2, jnp.where(lane_i == 2, w1, jnp.where(lane_i == 3, w2, 0.0))))

    @pl.when(first_step)
    def _():
        cnt_ref[...] = jnp.zeros_like(cnt_ref)

    picked = ((lane == e1) | (lane == e2)).astype(F32)
    cnt_ref[...] += jnp.broadcast_to(jnp.sum(picked, axis=0, keepdims=True), cnt_ref.shape)


def _rank_kernel(info_ref, cnt_ref, dest_ref, meta_ref, start_ref):
    i = pl.program_id(0)
    tb = info_ref.shape[0]
    lane = lax.broadcasted_iota(jnp.int32, (tb, LANES), 1)
    info = info_ref[...]
    e0 = info[:, 0:1].astype(jnp.int32)
    e1 = info[:, 1:2].astype(jnp.int32)
    o0 = (lane == e0).astype(F32)
    o1 = (lane == e1).astype(F32)
    both = o0 + o1

    @pl.when(i == 0)
    def _():
        cnt = cnt_ref[0:1, :]
        padded = jnp.floor((cnt + (MOE_TM - 1)) * (1.0 / MOE_TM)) * MOE_TM
        r = lax.broadcasted_iota(jnp.int32, (LANES, LANES), 0)
        c = lax.broadcasted_iota(jnp.int32, (LANES, LANES), 1)
        upper_strict = (r < c).astype(BF16)
        start = _dot_hl(jnp.broadcast_to(padded, (SUBLANES, LANES)), upper_strict)[0:1]
        start_ref[...] = start
        row = lax.broadcasted_iota(jnp.int32, (SUBLANES, LANES), 0)
        meta_ref[...] = jnp.where(row == 0, start + padded, start + cnt)

    r = lax.broadcasted_iota(jnp.int32, (tb, tb), 0)
    c = lax.broadcasted_iota(jnp.int32, (tb, tb), 1)
    lower_strict = (r > c).astype(BF16)
    before = jnp.dot(lower_strict, both.astype(BF16), preferred_element_type=F32) + start_ref[...]
    d0 = jnp.sum(o0 * before, axis=-1, keepdims=True)
    d1 = jnp.sum(o1 * before, axis=-1, keepdims=True)
    dest_ref[...] = jnp.where(lane == 0, d0, jnp.where(lane == 1, d1, 0.0)).astype(jnp.int32)
    start_ref[...] += jnp.sum(both, axis=0, keepdims=True)


def _rank(info, cnt):
    n_tok = info.shape[0]
    tb = RANK_TB
    n_blk = n_tok // tb
    return pl.pallas_call(
        _rank_kernel,
        out_shape=[jax.ShapeDtypeStruct((n_tok, LANES), jnp.int32), jax.ShapeDtypeStruct((SUBLANES, LANES), F32)],
        grid=(n_blk,),
        in_specs=[pl.BlockSpec((tb, LANES), lambda i: (i, 0)), pl.BlockSpec((SUBLANES, LANES), lambda i: (0, 0))],
        out_specs=[pl.BlockSpec((tb, LANES), lambda i: (i, 0)), pl.BlockSpec((SUBLANES, LANES), lambda i: (0, 0))],
        scratch_shapes=[pltpu.VMEM((1, LANES), F32)],
        compiler_params=_cparams("arbitrary"),
        name="moe_rank",
    )(info, cnt)


assert D_MODEL == SUBLANES * LANES


def _tile_rows_store(ref, x):
    n = x.shape[0]
    for s in range(SUBLANES):
        ref[pl.ds(s, n, stride=SUBLANES), :] = x[:, s * LANES:(s + 1) * LANES]


def _tile_rows_load(ref, n):
    return jnp.concatenate([ref[pl.ds(s, n, stride=SUBLANES), :] for s in range(SUBLANES)], axis=1)


def _row_copy(src_ref, s, dst_ref, d, sem):
    rows = lambda r: pl.ds(pl.multiple_of(r * SUBLANES, SUBLANES), SUBLANES)
    return pltpu.make_async_copy(src_ref.at[rows(s)], dst_ref.at[rows(d)], sem)


def _zero_padding_rows(seg_ref, xs_ref, zero_ref, zsem, *, start):
    def copy(first_row, n_rows):
        cp = pltpu.make_async_copy(zero_ref.at[pl.ds(0, n_rows * SUBLANES)],
                                   xs_ref.at[pl.ds(pl.multiple_of(first_row * SUBLANES, SUBLANES), n_rows * SUBLANES)],
                                   zsem)
        cp.start() if start else cp.wait()

    for e in range(N_EXPERTS):
        lo, n = seg_ref[1, e], seg_ref[0, e] - seg_ref[1, e]
        bit = MOE_TM // 2
        while bit >= 1:
            @pl.when((n & bit) != 0)
            def _(lo=lo, n=n, bit=bit):
                copy(lo + (n & ~(2 * bit - 1)), bit)
            bit //= 2
    n_total = xs_ref.shape[0] // SUBLANES

    def tail(j, c):
        copy(seg_ref[0, N_EXPERTS - 1] + j * MOE_TM, MOE_TM)
        return c

    lax.fori_loop(0, (n_total - seg_ref[0, N_EXPERTS - 1]) // MOE_TM, tail, 0)


def _dispatch_kernel(dest_ref, seg_ref, x_ref, mod_ref, g_ref, xs_ref, h_ref, zero_ref, sem, zsem):
    tb = x_ref.shape[0]
    i = pl.program_id(0)
    last = pl.num_programs(0) - 1
    h = _rms_mod(x_ref[...], g_ref[...], mod_ref[0, 0:1, :], mod_ref[0, 1:2, :])

    @pl.when(i == 0)
    def _():
        zero_ref[...] = jnp.zeros_like(zero_ref)
        _zero_padding_rows(seg_ref, xs_ref, zero_ref, zsem, start=True)

    @pl.when(i == last)
    def _():
        _zero_padding_rows(seg_ref, xs_ref, zero_ref, zsem, start=False)

    def drain(slot):
        def wait(j, c):
            _row_copy(h_ref.at[slot], 0, xs_ref, 0, sem.at[slot]).wait()
            return c
        lax.fori_loop(0, 2 * tb, wait, 0, unroll=8)

    for slot in range(2):
        @pl.when(i % 2 == slot)
        def _(slot=slot):
            @pl.when(i >= 2)
            def _():
                drain(slot)

            _tile_rows_store(h_ref.at[slot], h)

            def start(j, c):
                _row_copy(h_ref.at[slot], j, xs_ref, dest_ref[0, 0, 2 * j], sem.at[slot]).start(priority=0)
                _row_copy(h_ref.at[slot], j, xs_ref, dest_ref[0, 0, 2 * j + 1], sem.at[slot]).start(priority=1)
                return c

            lax.fori_loop(0, tb, start, 0, unroll=8)

            @pl.when(i == last)
            def _():
                drain(slot)

                @pl.when(i >= 1)
                def _():
                    drain(1 - slot)


def _dispatch(dest3, seg, x2, mod2, g, n_rows, T):
    n_tok, D = x2.shape
    tb = ROW_TB
    per_b = T // tb
    return pl.pallas_call(
        _dispatch_kernel,
        out_shape=jax.ShapeDtypeStruct((n_rows * SUBLANES, LANES), F32),
        grid=(n_tok // tb,),
        in_specs=[
            pl.BlockSpec((1, 1, 2 * tb), lambda i: (i, 0, 0), memory_space=pltpu.SMEM),
            pl.BlockSpec(memory_space=pltpu.SMEM),
            pl.BlockSpec((tb, D), lambda i: (i, 0)),
            pl.BlockSpec((1, 2, D), lambda i: (i // per_b, 0, 0)),
            pl.BlockSpec((1, D), lambda i: (0, 0)),
        ],
        out_specs=pl.BlockSpec(memory_space=pl.ANY),
        scratch_shapes=[pltpu.VMEM((2, tb * SUBLANES, LANES), F32), pltpu.VMEM((MOE_TM * SUBLANES, LANES), F32),
                        pltpu.SemaphoreType.DMA((2,)), pltpu.SemaphoreType.DMA(())],
        compiler_params=_cparams("arbitrary"),
        name="moe_dispatch",
    )(dest3, seg, x2, mod2, g)


def _expert_kernel(be_ref, nb_ref, nxt_ref, xs_ref, wg_hbm, wu_hbm, wd_hbm, ys_ref, stage_g, stage_u, stage_d,
                   wgb, wub, wdb, sem, *, layer):
    i = pl.program_id(0)
    e = be_ref[i]
    changed = (i == 0) | (e != be_ref[jnp.maximum(i - 1, 0)])

    def fetch(ex):
        return (pltpu.make_async_copy(wg_hbm.at[layer, ex], stage_g, sem.at[0]),
                pltpu.make_async_copy(wu_hbm.at[layer, ex], stage_u, sem.at[1]),
                pltpu.make_async_copy(wd_hbm.at[layer, ex], stage_d, sem.at[2]))

    @pl.when(i == 0)
    def _():
        for cp in fetch(e):
            cp.start()

    @pl.when(changed)
    def _():
        for cp in fetch(e):
            cp.wait()
        wgb[...] = stage_g[...].astype(BF16)
        wub[...] = stage_u[...].astype(BF16)
        wdb[...] = stage_d[...].astype(BF16)

        @pl.when(nxt_ref[i] >= 0)
        def _():
            for cp in fetch(nxt_ref[i]):
                cp.start()

    @pl.when(i < nb_ref[0])
    def _():
        x = _tile_rows_load(xs_ref, MOE_TM).astype(BF16)
        a = jnp.dot(x, wgb[...], preferred_element_type=F32)
        u = jnp.dot(x, wub[...], preferred_element_type=F32)
        hdn = (a * jax.nn.sigmoid(a)) * u
        _tile_rows_store(ys_ref, jnp.dot(hdn.astype(BF16), wdb[...], preferred_element_type=F32))

    @pl.when(i >= nb_ref[0])
    def _():
        ys_ref[...] = jnp.zeros_like(ys_ref)


def _experts(blk_e, n_used, next_e, xs, w_gate, w_up, w_down, layer):
    D = D_MODEL
    n_rows = xs.shape[0] // SUBLANES
    tm = MOE_TM
    FF = EXPERT_FF
    row_block = pl.BlockSpec((tm * SUBLANES, LANES), lambda i, be, nb, nx: (i, 0))
    hbm = pl.BlockSpec(memory_space=pl.ANY)
    grid_spec = pltpu.PrefetchScalarGridSpec(
        num_scalar_prefetch=3,
        grid=(n_rows // tm,),
        in_specs=[row_block, hbm, hbm, hbm],
        out_specs=row_block,
        scratch_shapes=[pltpu.VMEM((D, FF), F32), pltpu.VMEM((D, FF), F32), pltpu.VMEM((FF, D), F32),
                        pltpu.VMEM((D, FF), BF16), pltpu.VMEM((D, FF), BF16), pltpu.VMEM((FF, D), BF16),
                        pltpu.SemaphoreType.DMA((3,))],
    )
    return pl.pallas_call(
        functools.partial(_expert_kernel, layer=layer),
        out_shape=jax.ShapeDtypeStruct(xs.shape, F32),
        grid_spec=grid_spec,
        compiler_params=_cparams("arbitrary"),
        name="moe_experts",
    )(blk_e, n_used, next_e, xs, w_gate, w_up, w_down)


def _combine_kernel(dest_ref, dest_next_ref, info_ref, x_ref, gf_ref, fg_ref, ys_ref, o_ref, y_ref, sem, *,
                    final_norm):
    tb = x_ref.shape[0]
    i = pl.program_id(0)
    last = pl.num_programs(0) - 1

    def gather(d_ref, slot):
        def start(j, c):
            _row_copy(ys_ref, d_ref[0, 0, 2 * j], y_ref.at[slot, 0], j, sem.at[slot]).start(priority=0)
            _row_copy(ys_ref, d_ref[0, 0, 2 * j + 1], y_ref.at[slot, 1], j, sem.at[slot]).start(priority=1)
            return c
        lax.fori_loop(0, tb, start, 0, unroll=8)

    @pl.when(i == 0)
    def _():
        gather(dest_ref, 0)

    for slot in range(2):
        @pl.when(i % 2 == slot)
        def _(slot=slot):
            @pl.when(i < last)
            def _():
                gather(dest_next_ref, 1 - slot)

            def wait(j, c):
                _row_copy(ys_ref, 0, y_ref.at[slot, 0], 0, sem.at[slot]).wait()
                return c

            lax.fori_loop(0, 2 * tb, wait, 0, unroll=8)
            info = info_ref[...]
            moe = (info[:, 2:3] * _tile_rows_load(y_ref.at[slot, 0], tb)
                   + info[:, 3:4] * _tile_rows_load(y_ref.at[slot, 1], tb))
            out = x_ref[...] + gf_ref[0] * moe
            if final_norm:
                ms = jnp.mean(out * out, axis=-1, keepdims=True)
                out = out * lax.rsqrt(ms + RMS_EPS) * fg_ref[...]
            o_ref[...] = out


def _combine(dest3, info, x2, gf, final_g, ys, T, final_norm):
    n_tok, D = x2.shape
    tb = ROW_TB
    per_b = T // tb
    n_steps = n_tok // tb
    return pl.pallas_call(
        functools.partial(_combine_kernel, final_norm=final_norm),
        out_shape=jax.ShapeDtypeStruct((n_tok, D), F32),
        grid=(n_steps,),
        in_specs=[
            pl.BlockSpec((1, 1, 2 * tb), lambda i: (i, 0, 0), memory_space=pltpu.SMEM),
            pl.BlockSpec((1, 1, 2 * tb), lambda i: (jnp.minimum(i + 1, n_steps - 1), 0, 0), memory_space=pltpu.SMEM),
            pl.BlockSpec((tb, LANES), lambda i: (i, 0)),
            pl.BlockSpec((tb, D), lambda i: (i, 0)),
            pl.BlockSpec((1, 1, D), lambda i: (i // per_b, 0, 0)),
            pl.BlockSpec((1, D), lambda i: (0, 0)),
            pl.BlockSpec(memory_space=pl.ANY),
        ],
        out_specs=pl.BlockSpec((tb, D), lambda i: (i, 0)),
        scratch_shapes=[pltpu.VMEM((2, 2, tb * SUBLANES, LANES), F32), pltpu.SemaphoreType.DMA((2,))],
        compiler_params=_cparams("arbitrary"),
        name="moe_combine",
    )(dest3, dest3, info, x2, gf, final_g, ys)


def _moe_rows(n_tok):
    return -(-(2 * n_tok + N_EXPERTS * MOE_TM) // MOE_TM) * MOE_TM


def _moe_layer(x, info, cnt, mod_f, gf, norm_g, w_gate, w_up, w_down, layer, final_g, final_norm):
    B, T, D = x.shape
    n_tok = B * T
    x2 = x.reshape(n_tok, D)
    n_rows = _moe_rows(n_tok)
    dest, meta = _rank(info, cnt)
    seg = meta[0:2, :N_EXPERTS].astype(jnp.int32)
    pad_end = seg[0]
    n_blocks = n_rows // MOE_TM
    blk_start = jnp.arange(n_blocks, dtype=jnp.int32) * MOE_TM
    blk_e = jnp.minimum(jnp.sum(pad_end[None, :] <= blk_start[:, None], axis=1), N_EXPERTS - 1).astype(jnp.int32)
    n_used = (pad_end[N_EXPERTS - 1:] // MOE_TM).astype(jnp.int32)
    dest3 = dest[:, :2].reshape(n_tok // ROW_TB, 1, 2 * ROW_TB)
    xs = _dispatch(dest3, seg, x2, mod_f, norm_g, n_rows, T)
    seg_end = jnp.sum(blk_e[None, :] <= blk_e[:, None], axis=1)
    next_e = jnp.where(seg_end < n_blocks, blk_e[jnp.minimum(seg_end, n_blocks - 1)], -1).astype(jnp.int32)
    ys = _experts(blk_e, n_used, next_e, xs, w_gate, w_up, w_down, layer)
    out = _combine(dest3, info, x2, gf, final_g, ys, T, final_norm)
    return out.reshape(B, T, D)


def kernel(x, c, ada_w, ada_b, norm_mix_g, norm_ffn_g, rw_mu, rw_w_rkv, rw_w0, rw_w1, rw_w2, rw_a0, rw_a1, rw_a2,
           rw_g1, rw_g2, rw_k_k, rw_k_a, rw_r_k, rw_gn_g, rw_gn_b, rw_w_o, ada_kv_w, ada_kv_b, norm_kv_g, w_kv,
           df_w_q, df_lq1, df_lk1, df_lq2, df_lk2, df_subln_g, df_w_o, moe_w_rg, moe_w_re, moe_w_gate, moe_w_up,
           moe_w_down, final_g):
    B, T, D = x.shape
    c_pad = jnp.zeros((SUBLANES, D), F32).at[:B].set(c)
    mod = _ada(c_pad, ada_w, ada_b, 6 * D // 4)[:, :B]
    mod_kv = _ada(c_pad, ada_kv_w[None], ada_kv_b[None], D)[0, :B]
    bf = lambda w: w.astype(BF16)
    row = lambda v: v.reshape(1, -1)

    for l in range(DEPTH):
        sh_m, sc_m, g_m, sh_f, sc_f, g_f = jnp.split(mod[l], 6, axis=-1)
        mod_m = jnp.stack([sh_m, sc_m], axis=1)
        mod_f = jnp.stack([sh_f, sc_f], axis=1)
        if l < N_A_LAYERS:
            i = l
            vec = jnp.stack([rw_w0[i], rw_a0[i], rw_k_k[i], rw_k_a[i]], axis=0)
            r, k, v, lw, kk, al, gate = _rwkv_proj(
                x, mod_m, row(norm_mix_g[l]), rw_mu[i], bf(rw_w_rkv[i]), bf(rw_w1[i]), bf(rw_w2[i]),
                bf(rw_a1[i]), bf(rw_a2[i]), bf(rw_g1[i]), bf(rw_g2[i]), vec)
            pvec = jnp.stack([rw_r_k[i].reshape(-1), rw_gn_g[i], rw_gn_b[i]], axis=0)
            y = _rwkv_scan(r, k, v, lw, kk, al, pvec)
            w_o = bf(rw_w_o[i])
        else:
            j = l - N_A_LAYERS
            q_proj = (mod_m, row(norm_mix_g[l]), bf(df_w_q[j]))
            if l == N_A_LAYERS:
                sh_kv, sc_kv = jnp.split(mod_kv, 2, axis=-1)
                q, kv = _norm_mm(x, [q_proj, (jnp.stack([sh_kv, sc_kv], axis=1), row(norm_kv_g), bf(w_kv))], BF16)
            else:
                q, = _norm_mm(x, [q_proj], BF16)
            lambda_init = 0.8 - 0.6 * math.exp(-0.3 * l)
            lam_vecs = jnp.stack([df_lq1[j], df_lk1[j], df_lq2[j], df_lk2[j]], axis=0)
            y, gate = _diff_attn(q, kv, lam_vecs, row(df_subln_g[j]), lambda_init), None
            w_o = bf(df_w_o[j])
        w_r = jnp.concatenate([moe_w_rg[l], moe_w_re[l], jnp.zeros((D, LANES - N_GROUPS - N_EXPERTS), F32)], axis=1)
        w_r_hi = bf(w_r)
        w_cat = jnp.concatenate([w_r_hi, bf(w_r - w_r_hi.astype(F32))], axis=1)
        x, info, cnt = _proj_res_route(y, gate, x, g_m[:, None, :], w_o, mod_f, row(norm_ffn_g[l]), w_cat)
        x = _moe_layer(x, info, cnt, mod_f, g_f[:, None, :], row(norm_ffn_g[l]), moe_w_gate, moe_w_up,
                       moe_w_down, l, row(final_g), final_norm=(l == DEPTH - 1))
    return x
```

```python
import functools
import math

import jax
import jax.numpy as jnp
from jax import lax
from jax.experimental import pallas as pl
from jax.experimental.pallas import tpu as pltpu

F32 = jnp.float32
BF16 = jnp.bfloat16

D_MODEL = 1024
DEPTH = 2
N_A_LAYERS = DEPTH // 2
RWKV_HEAD = 64
RWKV_HEADS = D_MODEL // RWKV_HEAD
RWKV_GN_EPS = 64e-5
DIFF_QK_DIM = 64
DIFF_V_DIM = 2 * DIFF_QK_DIM
DIFF_HEADS = D_MODEL // DIFF_V_DIM
SUBLN_EPS = 1e-5
N_GROUPS = 4
EXPERTS_PER_GROUP = 8
N_EXPERTS = N_GROUPS * EXPERTS_PER_GROUP
EXPERT_FF = 512
RMS_EPS = 1e-6

LANES = 128
SUBLANES = 8
VMEM_LIMIT_BYTES = 56 * 1024 * 1024

SCAN_CHUNK = 64
SCAN_BASE_BLOCK = 8
SCAN_CHUNKS_PER_STEP = 2
PAIR = 2 * RWKV_HEAD
PROJ_TM = 512
DENSE_TM = 512
ATT_TQ = 512
ATT_HB = 2
MOE_TM = 256
RANK_TB = 1024
ROW_TB = 512
assert SCAN_CHUNK == RWKV_HEAD and PAIR == LANES


def _cparams(*sem):
    return pltpu.CompilerParams(dimension_semantics=sem, vmem_limit_bytes=VMEM_LIMIT_BYTES)


def _dot(a, b):
    return jnp.dot(a.astype(BF16), b.astype(BF16), preferred_element_type=F32)


def _split(x):
    hi = x.astype(BF16)
    lo = (x - hi.astype(F32)).astype(BF16)
    return hi, lo


def _dot3(a, b):
    ah, al = _split(a)
    bh, bl = _split(b)
    d = functools.partial(jnp.dot, preferred_element_type=F32)
    return d(ah, bh) + d(ah, bl) + d(al, bh)


def _dot_hl(a, b_exact):
    ah, al = _split(a)
    d = functools.partial(jnp.dot, preferred_element_type=F32)
    return d(ah, b_exact) + d(al, b_exact)


def _rms_mod(x, g, shift, scale):
    ms = jnp.mean(x * x, axis=-1, keepdims=True)
    return (x * lax.rsqrt(ms + RMS_EPS) * g) * (1.0 + scale) + shift


def _ada_kernel(c_ref, w_ref, b_ref, o_ref):
    c = c_ref[...]
    ca = c * jax.nn.sigmoid(c)
    o_ref[...] = _dot3(ca, w_ref[...]) + b_ref[...]


def _ada(c_pad, w, b, tn):
    L, D, N = w.shape
    return pl.pallas_call(
        _ada_kernel,
        out_shape=jax.ShapeDtypeStruct((L, SUBLANES, N), F32),
        grid=(L, N // tn),
        in_specs=[
            pl.BlockSpec((SUBLANES, D), lambda l, j: (0, 0)),
            pl.BlockSpec((None, D, tn), lambda l, j: (l, 0, j)),
            pl.BlockSpec((None, 1, tn), lambda l, j: (l, 0, j)),
        ],
        out_specs=pl.BlockSpec((None, SUBLANES, tn), lambda l, j: (l, 0, j)),
        compiler_params=_cparams("arbitrary", "arbitrary"),
        name="ada_mod",
    )(c_pad, w, b.reshape(L, 1, N))


def _rwkv_proj_kernel(x_ref, xp_ref, mod_ref, g_ref, mu_ref, wrkv_ref, w1_ref, w2_ref, a1_ref, a2_ref,
                      g1_ref, g2_ref, vec_ref, r_ref, k_ref, v_ref, lw_ref, kk_ref, al_ref, gate_ref):
    i = pl.program_id(1)
    g = g_ref[...]
    shift, scale = mod_ref[0, 0:1, :], mod_ref[0, 1:2, :]
    h = _rms_mod(x_ref[0], g, shift, scale)
    hp = _rms_mod(xp_ref[0, SUBLANES - 1:SUBLANES, :], g, shift, scale)
    hp = jnp.where(i == 0, 0.0, hp)
    row = lax.broadcasted_iota(jnp.int32, h.shape, 0)
    h_prev = jnp.where(row == 0, hp, pltpu.roll(h, 1, axis=0))
    xx = h_prev - h
    mu = mu_ref[...]
    xs = [(h + xx * mu[j:j + 1, :]).astype(BF16) for j in range(6)]
    w0, a0, k_k, k_a = (vec_ref[j:j + 1, :] for j in range(4))
    d = functools.partial(jnp.dot, preferred_element_type=F32)
    r = d(xs[0], wrkv_ref[0])
    k = d(xs[1], wrkv_ref[1])
    v = d(xs[2], wrkv_ref[2])
    z = w0 + _dot(jnp.tanh(d(xs[3], w1_ref[...])), w2_ref[...])
    lw = (-math.exp(-0.5)) * jax.nn.sigmoid(z)
    a = jax.nn.sigmoid(a0 + _dot(d(xs[4], a1_ref[...]), a2_ref[...]))
    gate = _dot(jax.nn.sigmoid(d(xs[5], g1_ref[...])), g2_ref[...])
    r_ref[0] = r.astype(BF16)
    k_ref[0] = (k * (1.0 + (a - 1.0) * k_a)).astype(BF16)
    v_ref[0] = v.astype(BF16)
    lw_ref[0] = lw
    kk_ref[0] = (k * k_k).astype(BF16)
    al_ref[0] = a.astype(BF16)
    gate_ref[0] = gate.astype(BF16)


def _rwkv_proj(x, mod2, g, mu, wrkv, w1, w2, a1, a2, g1, g2, vec):
    B, T, D = x.shape
    tm = PROJ_TM
    const2 = lambda b, i: (0, 0)
    const3 = lambda b, i: (0, 0, 0)
    act = pl.BlockSpec((1, tm, D), lambda b, i: (b, i, 0))
    n_sub = tm // SUBLANES
    return pl.pallas_call(
        _rwkv_proj_kernel,
        out_shape=[jax.ShapeDtypeStruct((B, T, D), F32 if n == 3 else BF16) for n in range(7)],
        grid=(B, T // tm),
        in_specs=[
            act,
            pl.BlockSpec((1, SUBLANES, D), lambda b, i: (b, jnp.maximum(i * n_sub - 1, 0), 0)),
            pl.BlockSpec((1, 2, D), lambda b, i: (b, 0, 0)),
            pl.BlockSpec((1, D), const2),
            pl.BlockSpec((6, D), const2),
            pl.BlockSpec((3, D, D), const3),
            pl.BlockSpec(w1.shape, const2), pl.BlockSpec(w2.shape, const2),
            pl.BlockSpec(a1.shape, const2), pl.BlockSpec(a2.shape, const2),
            pl.BlockSpec(g1.shape, const2), pl.BlockSpec(g2.shape, const2),
            pl.BlockSpec((4, D), const2),
        ],
        out_specs=[act] * 7,
        compiler_params=_cparams("arbitrary", "arbitrary"),
        name="rwkv_proj",
    )(x, x, mod2, g, mu, wrkv, w1, w2, a1, a2, g1, g2, vec)


def _rwkv_scan_kernel(r_ref, k_ref, v_ref, lw_ref, kk_ref, al_ref, pv_ref, y_ref, h_ref):
    C = SCAN_CHUNK
    P2 = 2 * C

    @pl.when(pl.program_id(1) == 0)
    def _():
        h_ref[...] = jnp.zeros_like(h_ref)

    lane = lax.broadcasted_iota(jnp.int32, (1, PAIR), 1)
    m_left = (lane < RWKV_HEAD).astype(F32)
    m_right = 1.0 - m_left
    ri = lax.broadcasted_iota(jnp.int32, (P2, P2), 0)
    ci = lax.broadcasted_iota(jnp.int32, (P2, P2), 1)
    same = (ri >= C) == (ci >= C)
    strict = same & (ri > ci)
    incl = same & (ri >= ci)
    eye = ri == ci
    block_ones = same.astype(BF16)
    tri = (lax.broadcasted_iota(jnp.int32, (C, C), 0) >= lax.broadcasted_iota(jnp.int32, (C, C), 1)).astype(BF16)

    def stack(x):
        return jnp.concatenate([x * m_left, x * m_right], axis=0)

    def head_sums(x):
        s_left = jnp.sum(x * m_left, axis=-1, keepdims=True)
        s_right = jnp.sum(x * m_right, axis=-1, keepdims=True)
        return jnp.where(lane < RWKV_HEAD, s_left, s_right)

    inv_n = 1.0 / RWKV_HEAD
    dd = functools.partial(jnp.dot, preferred_element_type=F32)
    n_pairs = RWKV_HEADS // 2
    units = [(ch, p) for ch in range(SCAN_CHUNKS_PER_STEP) for p in range(n_pairs)]
    idx = [(slice(ch * C, (ch + 1) * C), slice(p * PAIR, (p + 1) * PAIR)) for ch, p in units]
    U = range(len(units))
    ld = lambda ref, rs, sl: ref[0, rs, sl].astype(F32)
    kkr = [ld(kk_ref, rs, sl) for rs, sl in idx]
    ss = [head_sums(x * x) for x in kkr]
    lws = [lw_ref[0, rs, sl] for rs, sl in idx]
    Ls = []
    for lw in lws:
        l_hi, l_lo = _split(lw)
        cs = dd(tri, jnp.concatenate([l_hi, l_lo], axis=1))
        Ls.append(cs[:, :PAIR] + cs[:, PAIR:])
    lhs_g, rhs_g, bk_hat, vs, at32, rt32, dec_end = [], [], [], [], [], [], []
    for u in U:
        rs, sl = idx[u]
        L, lw = Ls[u], lws[u]
        kk = kkr[u] * lax.rsqrt(jnp.maximum(ss[u], 1e-24))
        b_vec = kk * ld(al_ref, rs, sl)
        k = ld(k_ref, rs, sl)
        LC = L[C - 1:C, :]
        e_neg = jnp.exp(-L)
        e_end = jnp.exp(LC - L)
        At = stack(-kk * jnp.exp(L - lw))
        Rt = stack(ld(r_ref, rs, sl) * jnp.exp(L))
        at32.append(At)
        rt32.append(Rt)
        lhs_g.append(jnp.concatenate([At, Rt], axis=0).astype(BF16))
        rhs_g.append(jnp.concatenate([stack(b_vec * e_neg), stack(k * e_neg)], axis=0).astype(BF16))
        bk_hat.append(jnp.concatenate([stack(b_vec * e_end), stack(k * e_end)], axis=0))
        vs.append(stack(ld(v_ref, rs, sl)).astype(BF16))
        dec_end.append(jnp.exp(LC))
    G = [lax.dot_general(lhs_g[u], rhs_g[u], (((1,), (1,)), ((), ())), preferred_element_type=F32) for u in U]
    A_ak = [jnp.where(strict, G[u][:P2, P2:], 0.0).astype(BF16) for u in U]
    A_r = [jnp.concatenate([jnp.where(incl, G[u][P2:, :P2], 0.0), jnp.where(incl, G[u][P2:, P2:], 0.0)],
                           axis=1).astype(BF16) for u in U]
    W = [dd(A_ak[u], vs[u]) for u in U]
    bsz = lambda b: (ri >> int(math.log2(b))) == (ci >> int(math.log2(b)))
    b8 = bsz(SCAN_BASE_BLOCK)
    D1 = [jnp.where(strict & b8, G[u][:P2, :P2], 0.0).astype(BF16) for u in U]
    D2 = [dd(D1[u], D1[u]).astype(BF16) for u in U]
    D4 = [dd(D2[u], D2[u]).astype(BF16) for u in U]
    eye_f = eye.astype(F32)
    P1 = [eye_f + D1[u].astype(F32) + D2[u].astype(F32) + dd(D1[u], D2[u]) for u in U]
    Tm = [P1[u] + dd(P1[u].astype(BF16), D4[u]) for u in U]
    blk = SCAN_BASE_BLOCK
    while blk < C:
        off = strict & bsz(2 * blk) & ~bsz(blk)
        Mo = [jnp.where(off, G[u][:P2, :P2], 0.0).astype(BF16) for u in U]
        Tb = [Tm[u].astype(BF16) for u in U]
        TM = [dd(Tb[u], Mo[u]).astype(BF16) for u in U]
        Tm = [Tm[u] + dd(TM[u], Tb[u]) for u in U]
        blk *= 2
    Z = [dd(Tm[u].astype(BF16), jnp.concatenate([at32[u], W[u]], axis=1).astype(BF16)) for u in U]
    rhs = [jnp.concatenate([Z[u].astype(BF16), jnp.concatenate([jnp.zeros_like(vs[u]), vs[u]], axis=1)], axis=0)
           for u in U]
    o6 = [dd(A_r[u], rhs[u]) for u in U]
    o7 = [dd(bk_hat[u].T.astype(BF16), rhs[u]) for u in U]
    H = [h_ref[p] for p in range(n_pairs)]
    Y = [None] * len(units)
    for u in U:
        p = units[u][1]
        Hb = H[p].astype(BF16)
        Y[u] = dd((rt32[u] + o6[u][:, :PAIR]).astype(BF16), Hb) + o6[u][:, PAIR:]
        Mbd = o7[u][:, :PAIR] + jnp.where(eye, dec_end[u], 0.0)
        H[p] = dd(Mbd.astype(BF16), Hb) + o7[u][:, PAIR:]
    for p in range(n_pairs):
        h_ref[p] = H[p]
    ys = [Y[u][:C] + Y[u][C:] for u in U]
    rk = [ld(r_ref, rs, sl) * ld(k_ref, rs, sl) * pv_ref[0:1, sl] for rs, sl in idx]
    st1 = [head_sums(jnp.concatenate([ys[u], rk[u]], axis=0)) for u in U]
    yc = [ys[u] - st1[u][:C] * inv_n for u in U]
    var = [head_sums(yc[u] * yc[u]) * inv_n for u in U]
    for u in U:
        rs, sl = idx[u]
        bonus = st1[u][C:] * ld(v_ref, rs, sl)
        y_ref[0, rs, sl] = (yc[u] * lax.rsqrt(var[u] + RWKV_GN_EPS) * pv_ref[1:2, sl] + pv_ref[2:3, sl]
                            + bonus).astype(y_ref.dtype)


def _rwkv_scan(r, k, v, lw, kk, al, pvec):
    B, T, D = r.shape
    rows = SCAN_CHUNK * SCAN_CHUNKS_PER_STEP
    act = pl.BlockSpec((1, rows, D), lambda b, c: (b, c, 0))
    return pl.pallas_call(
        _rwkv_scan_kernel,
        out_shape=jax.ShapeDtypeStruct((B, T, D), BF16),
        grid=(B, T // rows),
        in_specs=[act] * 6 + [pl.BlockSpec((3, D), lambda b, c: (0, 0))],
        out_specs=act,
        scratch_shapes=[pltpu.VMEM((RWKV_HEADS // 2, PAIR, PAIR), F32)],
        compiler_params=_cparams("arbitrary", "arbitrary"),
        name="rwkv_scan",
    )(r, k, v, lw, kk, al, pvec)


def _proj_res_route_kernel(*refs, has_gate):
    if has_gate:
        y_ref, g_ref, x_ref, gm_ref, w_ref, modf_ref, gf_ref, wr_ref, o_ref, info_ref, cnt_ref = refs
        y = y_ref[0].astype(F32) * g_ref[0].astype(F32)
    else:
        y_ref, x_ref, gm_ref, w_ref, modf_ref, gf_ref, wr_ref, o_ref, info_ref, cnt_ref = refs
        y = y_ref[0]
    x_new =x_ref[0] + gm_ref[0] * jnp.dot(y.astype(BF16), w_ref[...], preferred_element_type=F32)
    o_ref[0] = x_new
    h = _rms_mod(x_new, gf_ref[...], modf_ref[0, 0:1, :], modf_ref[0, 1:2, :])
    first = (pl.program_id(0) == 0) & (pl.program_id(1) == 0)
    _route(h, wr_ref[...], info_ref, cnt_ref, first)


def _proj_res_route(y, g, x, gm, w, mod_f, norm_f, w_cat):
    B, T, D = x.shape
    tm = DENSE_TM
    per_b = T // tm
    act =pl.BlockSpec((1, tm, D), lambda b, i: (b, i, 0))
    const = lambda b, i: (0, 0)
    flat = lambda b, i: (b * per_b + i, 0)
    ins = [y] + ([g] if g is not None else []) + [x, gm, w, mod_f, norm_f, w_cat]
    specs = [act] * (len(ins) - 5) + [pl.BlockSpec((1, 1, D), lambda b, i: (b, 0, 0)), pl.BlockSpec((D, D), const),
                                     pl.BlockSpec((1, 2, D), lambda b, i: (b, 0, 0)), pl.BlockSpec((1, D), const),
                                     pl.BlockSpec((D, 2 * LANES), const)]
    return pl.pallas_call(
        functools.partial(_proj_res_route_kernel, has_gate=g is not None),
        out_shape=[jax.ShapeDtypeStruct((B, T, D), F32), jax.ShapeDtypeStruct((B * T, LANES), F32),
                   jax.ShapeDtypeStruct((SUBLANES, LANES), F32)],
        grid=(B, per_b),
        in_specs=specs,
        out_specs=[act, pl.BlockSpec((tm, LANES), flat), pl.BlockSpec((SUBLANES, LANES), const)],
        compiler_params=_cparams("arbitrary", "arbitrary"),
        name="proj_res_route",
    )(*ins)


def _norm_mm_kernel(*refs, n_proj):
    x_ref = refs[0]
    ins, outs = refs[1:1 + 3 * n_proj], refs[1 + 3 * n_proj:]
    x = x_ref[0]
    xn = x * lax.rsqrt(jnp.mean(x * x, axis=-1, keepdims=True) + RMS_EPS)
    for p in range(n_proj):
        mod_ref, g_ref, w_ref = ins[3 * p:3 * p + 3]
        h = (xn * g_ref[...]) * (1.0 + mod_ref[0, 1:2, :]) + mod_ref[0, 0:1, :]
        outs[p][0] = jnp.dot(h.astype(BF16), w_ref[...], preferred_element_type=F32).astype(outs[p].dtype)


def _norm_mm(x, projs, out_dtype):
    B, T, D = x.shape
    tm = DENSE_TM
    in_specs = [pl.BlockSpec((1, tm, D), lambda b, i: (b, i, 0))]
    args = [x]
    for mod2, g, w in projs:
        in_specs += [pl.BlockSpec((1, 2, D), lambda b, i: (b, 0, 0)), pl.BlockSpec((1, D), lambda b, i: (0, 0)),
                     pl.BlockSpec(w.shape, lambda b, i: (0, 0))]
        args += [mod2, g, w]
    return pl.pallas_call(
        functools.partial(_norm_mm_kernel, n_proj=len(projs)),
        out_shape=[jax.ShapeDtypeStruct((B, T, w.shape[1]), out_dtype) for _, _, w in projs],
        grid=(B, T // tm),
        in_specs=in_specs,
        out_specs=[pl.BlockSpec((1, tm, w.shape[1]), lambda b, i: (b, i, 0)) for _, _, w in projs],
        compiler_params=_cparams("arbitrary", "arbitrary"),
        name="norm_mm",
    )(*args)


def _diff_attn_kernel(q_ref, k_ref, v_ref, lam_ref, sg_ref, o_ref, m_ref, acc_ref, s_ref, *, lambda_init):
    tq, HB, dv = ATT_TQ, ATT_HB, DIFF_V_DIM
    qi = pl.program_id(2)
    heads = range(HB)
    hs = [slice(h * dv, (h + 1) * dv) for h in heads]
    lane = lax.broadcasted_iota(jnp.int32, (1, dv), 1)
    m_left = (lane < DIFF_QK_DIM).astype(F32)
    qs = []
    for h in heads:
        q = q_ref[0, :, hs[h]].astype(F32) * (DIFF_QK_DIM ** -0.5 * math.log2(math.e))
        qs.append(jnp.concatenate([q * m_left, q * (1.0 - m_left)], axis=0).astype(BF16))
    ones_col = jnp.ones((tq, dv), BF16)
    causal = lax.broadcasted_iota(jnp.int32, (tq, tq), 1) <= lax.broadcasted_iota(jnp.int32, (tq, tq), 0)

    def key_rows(j):
        return pl.ds(pl.multiple_of(j * tq, tq), tq)

    def scores_into(slot, j):
        for h in heads:
            s_ref[slot, h] = lax.dot_general(qs[h], k_ref[0, key_rows(j), hs[h]], (((1,), (1,)), ((), ())),
                                             preferred_element_type=F32)

    def block(slot, j, first, prefetch):
        if prefetch is not None:
            scores_into(1 - slot, prefetch)
        for h in heads:
            s = s_ref[slot, h]
            if first:
                s = jnp.concatenate([jnp.where(causal, s[:tq], -jnp.inf), jnp.where(causal, s[tq:], -jnp.inf)],
                                    axis=0)
                m_new = jnp.broadcast_to(jnp.max(s, axis=-1, keepdims=True), (2 * tq, LANES))
            else:
                m_old = m_ref[h]
                m_new = jnp.maximum(m_old, jnp.max(s, axis=-1, keepdims=True))
            m_ref[h] = m_new
            p = jnp.exp2(s - jnp.concatenate([m_new] * (tq // LANES), axis=1)).astype(BF16)
            pv = jnp.dot(p, jnp.concatenate([v_ref[0, key_rows(j), hs[h]], ones_col], axis=1),
                         preferred_element_type=F32)
            if first:
                acc_ref[h] = pv
            else:
                alpha = jnp.exp2(m_old - m_new)
                acc_ref[h] = acc_ref[h] * jnp.concatenate([alpha, alpha], axis=1) + pv

    scores_into(0, qi)
    block(0, qi, True, 0)

    def pair(u, c):
        last = jnp.maximum(qi - 1, 0)
        block(1, 2 * u, False, jnp.minimum(2 * u + 1, last))
        block(0, 2 * u + 1, False, jnp.minimum(2 * u + 2, last))
        return c

    lax.fori_loop(0, qi // 2, pair, 0)

    @pl.when(qi % 2 == 1)
    def _():
        block(1, qi - 1, False, None)

    lv = lam_ref[...]
    lam = (jnp.exp(jnp.sum(lv[0:1] * lv[1:2], axis=-1, keepdims=True))
           - jnp.exp(jnp.sum(lv[2:3] * lv[3:4], axis=-1, keepdims=True)) + lambda_init)
    for h in heads:
        acc = acc_ref[h]
        o = acc[:, :dv] / acc[:, dv:]
        o = o[:tq] - lam * o[tq:]
        ms = jnp.mean(o * o, axis=-1, keepdims=True)
        o_ref[0, :, hs[h]] = (o * lax.rsqrt(ms + SUBLN_EPS) * sg_ref[...] * (1.0 - lambda_init)).astype(o_ref.dtype)


def _diff_attn(q, kv, lam_vecs, subln_g, lambda_init):
    B, T, D = q.shape
    tq, HB = ATT_TQ, ATT_HB
    n_hb = DIFF_HEADS // HB
    w = HB * DIFF_V_DIM
    return pl.pallas_call(
        functools.partial(_diff_attn_kernel, lambda_init=lambda_init),
        out_shape=jax.ShapeDtypeStruct((B, T, D), BF16),
        grid=(B, n_hb, T // tq),
        in_specs=[
            pl.BlockSpec((1, tq, w), lambda b, h, i: (b, i, h)),
            pl.BlockSpec((1, T, w), lambda b, h, i: (b, 0, h)),
            pl.BlockSpec((1, T, w), lambda b, h, i: (b, 0, n_hb + h)),
            pl.BlockSpec((4, DIFF_QK_DIM), lambda b, h, i: (0, 0)),
            pl.BlockSpec((1, DIFF_V_DIM), lambda b, h, i: (0, 0)),
        ],
        out_specs=pl.BlockSpec((1, tq, w), lambda b, h, i: (b, i, h)),
        scratch_shapes=[pltpu.VMEM((HB, 2 * tq, LANES), F32), pltpu.VMEM((HB, 2 * tq, 2 * DIFF_V_DIM), F32),
                        pltpu.VMEM((2, HB, 2 * tq, tq), F32)],
        compiler_params=_cparams("arbitrary", "arbitrary", "arbitrary"),
        name="diff_attn",
    )(q, kv, kv, lam_vecs, subln_g)


def _route(h, w_cat, info_ref, cnt_ref, first_step):
    h_hi, h_lo = _split(h)
    two = jnp.dot(h_hi, w_cat, preferred_element_type=F32)
    logit = (two[:, :LANES] + two[:, LANES:]
             + jnp.dot(h_lo, w_cat[:, :LANES], preferred_element_type=F32))
    lane_i = lax.broadcasted_iota(jnp.int32, logit.shape, 1)
    lane = lane_i.astype(F32)
    neg = -jnp.inf
    big = float(LANES)
    is_grp = lane_i < N_GROUPS
    gl = jnp.where(is_grp, logit, neg)
    gmax = jnp.max(gl, axis=-1, keepdims=True)
    gidx = jnp.min(jnp.where(gl == gmax, lane, big), axis=-1, keepdims=True)
    grp_gate = 1.0 / jnp.sum(jnp.where(is_grp, jnp.exp(logit - gmax), 0.0), axis=-1, keepdims=True)
    lo = N_GROUPS + gidx * EXPERTS_PER_GROUP
    in_grp = (lane >= lo) & (lane < lo + EXPERTS_PER_GROUP)
    el = jnp.where(in_grp, logit, neg)
    t1 = jnp.max(el, axis=-1, keepdims=True)
    i1 = jnp.min(jnp.where(el == t1, lane, big), axis=-1, keepdims=True)
    el2 = jnp.where(lane == i1, neg, el)
    t2 = jnp.max(el2, axis=-1, keepdims=True)
    i2 = jnp.min(jnp.where(el2 == t2, lane, big), axis=-1, keepdims=True)
    e21 = jnp.exp(t2 - t1)
    p1 = 1.0 / (1.0 + e21)
    w1 = grp_gate * p1
    w2 = grp_gate * (e21 * p1)
    e1 = i1 - N_GROUPS
    e2 = i2 - N_GROUPS
    info_ref[...] = jnp.where(lane_i == 0, e1, jnp.where(lane_i == 1, e2, jnp.where(lane_i == 2, w1, jnp.where(lane_i == 3, w2, 0.0))))

    @pl.when(first_step)
    def _():
        cnt_ref[...] = jnp.zeros_like(cnt_ref)

    picked = ((lane == e1) | (lane == e2)).astype(F32)
    cnt_ref[...] += jnp.broadcast_to(jnp.sum(picked, axis=0, keepdims=True), cnt_ref.shape)


def _rank_kernel(info_ref, cnt_ref, dest_ref, meta_ref, start_ref):
    i = pl.program_id(0)
    tb = info_ref.shape[0]
    lane = lax.broadcasted_iota(jnp.int32, (tb, LANES), 1)
    info = info_ref[...]
    e0 = info[:, 0:1].astype(jnp.int32)
    e1 = info[:, 1:2].astype(jnp.int32)
    o0 = (lane == e0).astype(F32)
    o1 = (lane == e1).astype(F32)
    both = o0 + o1

    @pl.when(i == 0)
    def _():
        cnt = cnt_ref[0:1, :]
        padded = jnp.floor((cnt + (MOE_TM - 1)) * (1.0 / MOE_TM)) * MOE_TM
        r = lax.broadcasted_iota(jnp.int32, (LANES, LANES), 0)
        c = lax.broadcasted_iota(jnp.int32, (LANES, LANES), 1)
        upper_strict = (r < c).astype(BF16)
        start = _dot_hl(jnp.broadcast_to(padded, (SUBLANES, LANES)), upper_strict)[0:1]
        start_ref[...] = start
        row = lax.broadcasted_iota(jnp.int32, (SUBLANES, LANES), 0)
        meta_ref[...] = jnp.where(row == 0, start + padded, start + cnt)

    r = lax.broadcasted_iota(jnp.int32, (tb, tb), 0)
    c = lax.broadcasted_iota(jnp.int32, (tb, tb), 1)
    lower_strict = (r > c).astype(BF16)
    before = jnp.dot(lower_strict, both.astype(BF16), preferred_element_type=F32) + start_ref[...]
    d0 = jnp.sum(o0 * before, axis=-1, keepdims=True)
    d1 = jnp.sum(o1 * before, axis=-1, keepdims=True)
    dest_ref[...] = jnp.where(lane == 0, d0, jnp.where(lane == 1, d1, 0.0)).astype(jnp.int32)
    start_ref[...] += jnp.sum(both, axis=0, keepdims=True)


def _rank(info, cnt):
    n_tok = info.shape[0]
    tb = RANK_TB
    n_blk = n_tok // tb
    return pl.pallas_call(
        _rank_kernel,
        out_shape=[jax.ShapeDtypeStruct((n_tok, LANES), jnp.int32), jax.ShapeDtypeStruct((SUBLANES, LANES), F32)],
        grid=(n_blk,),
        in_specs=[pl.BlockSpec((tb, LANES), lambda i: (i, 0)), pl.BlockSpec((SUBLANES, LANES), lambda i: (0, 0))],
        out_specs=[pl.BlockSpec((tb, LANES), lambda i: (i, 0)), pl.BlockSpec((SUBLANES, LANES), lambda i: (0, 0))],
        scratch_shapes=[pltpu.VMEM((1, LANES), F32)],
        compiler_params=_cparams("arbitrary"),
        name="moe_rank",
    )(info, cnt)


assert D_MODEL == SUBLANES * LANES


def _tile_rows_store(ref, x):
    n = x.shape[0]
    for s in range(SUBLANES):
        ref[pl.ds(s, n, stride=SUBLANES), :] = x[:, s * LANES:(s + 1) * LANES]


def _tile_rows_load(ref, n):
    return jnp.concatenate([ref[pl.ds(s, n, stride=SUBLANES), :] for s in range(SUBLANES)], axis=1)


def _row_copy(src_ref, s, dst_ref, d, sem):
    rows = lambda r: pl.ds(pl.multiple_of(r * SUBLANES, SUBLANES), SUBLANES)
    return pltpu.make_async_copy(src_ref.at[rows(s)], dst_ref.at[rows(d)], sem)


def _zero_padding_rows(seg_ref, xs_ref, zero_ref, zsem, *, start):
    def copy(first_row, n_rows):
        cp = pltpu.make_async_copy(zero_ref.at[pl.ds(0, n_rows * SUBLANES)],
                                   xs_ref.at[pl.ds(pl.multiple_of(first_row * SUBLANES, SUBLANES), n_rows * SUBLANES)],
                                   zsem)
        cp.start() if start else cp.wait()

    for e in range(N_EXPERTS):
        lo, n = seg_ref[1, e], seg_ref[0, e] - seg_ref[1, e]
        bit = MOE_TM // 2
        while bit >= 1:
            @pl.when((n & bit) != 0)
            def _(lo=lo, n=n, bit=bit):
                copy(lo + (n & ~(2 * bit - 1)), bit)
            bit //= 2
    n_total = xs_ref.shape[0] // SUBLANES

    def tail(j, c):
        copy(seg_ref[0, N_EXPERTS - 1] + j * MOE_TM, MOE_TM)
        return c

    lax.fori_loop(0, (n_total - seg_ref[0, N_EXPERTS - 1]) // MOE_TM, tail, 0)


def _dispatch_kernel(dest_ref, seg_ref, x_ref, mod_ref, g_ref, xs_ref, h_ref, zero_ref, sem, zsem):
    tb = x_ref.shape[0]
    i = pl.program_id(0)
    last = pl.num_programs(0) - 1
    h = _rms_mod(x_ref[...], g_ref[...], mod_ref[0, 0:1, :], mod_ref[0, 1:2, :])

    @pl.when(i == 0)
    def _():
        zero_ref[...] = jnp.zeros_like(zero_ref)
        _zero_padding_rows(seg_ref, xs_ref, zero_ref, zsem, start=True)

    @pl.when(i == last)
    def _():
        _zero_padding_rows(seg_ref, xs_ref, zero_ref, zsem, start=False)

    def drain(slot):
        def wait(j, c):
            _row_copy(h_ref.at[slot], 0, xs_ref, 0, sem.at[slot]).wait()
            return c
        lax.fori_loop(0, 2 * tb, wait, 0, unroll=8)

    for slot in range(2):
        @pl.when(i % 2 == slot)
        def _(slot=slot):
            @pl.when(i >= 2)
            def _():
                drain(slot)

            _tile_rows_store(h_ref.at[slot], h)

            def start(j, c):
                _row_copy(h_ref.at[slot], j, xs_ref, dest_ref[0, 0, 2 * j], sem.at[slot]).start(priority=0)
                _row_copy(h_ref.at[slot], j, xs_ref, dest_ref[0, 0, 2 * j + 1], sem.at[slot]).start(priority=1)
                return c

            lax.fori_loop(0, tb, start, 0, unroll=8)

            @pl.when(i == last)
            def _():
                drain(slot)

                @pl.when(i >= 1)
                def _():
                    drain(1 - slot)


def _dispatch(dest3, seg, x2, mod2, g, n_rows, T):
    n_tok, D = x2.shape
    tb = ROW_TB
    per_b = T // tb
    return pl.pallas_call(
        _dispatch_kernel,
        out_shape=jax.ShapeDtypeStruct((n_rows * SUBLANES, LANES), F32),
        grid=(n_tok // tb,),
        in_specs=[
            pl.BlockSpec((1, 1, 2 * tb), lambda i: (i, 0, 0), memory_space=pltpu.SMEM),
            pl.BlockSpec(memory_space=pltpu.SMEM),
            pl.BlockSpec((tb, D), lambda i: (i, 0)),
            pl.BlockSpec((1, 2, D), lambda i: (i // per_b, 0, 0)),
            pl.BlockSpec((1, D), lambda i: (0, 0)),
        ],
        out_specs=pl.BlockSpec(memory_space=pl.ANY),
        scratch_shapes=[pltpu.VMEM((2, tb * SUBLANES, LANES), F32), pltpu.VMEM((MOE_TM * SUBLANES, LANES), F32),
                        pltpu.SemaphoreType.DMA((2,)), pltpu.SemaphoreType.DMA(())],
        compiler_params=_cparams("arbitrary"),
        name="moe_dispatch",
    )(dest3, seg, x2, mod2, g)


def _expert_kernel(be_ref, nb_ref, nxt_ref, xs_ref, wg_hbm, wu_hbm, wd_hbm, ys_ref, stage_g, stage_u, stage_d,
                   wgb, wub, wdb, sem, *, layer):
    i = pl.program_id(0)
    e = be_ref[i]
    changed = (i == 0) | (e != be_ref[jnp.maximum(i - 1, 0)])

    def fetch(ex):
        return (pltpu.make_async_copy(wg_hbm.at[layer, ex], stage_g, sem.at[0]),
                pltpu.make_async_copy(wu_hbm.at[layer, ex], stage_u, sem.at[1]),
                pltpu.make_async_copy(wd_hbm.at[layer, ex], stage_d, sem.at[2]))

    @pl.when(i == 0)
    def _():
        for cp in fetch(e):
            cp.start()

    @pl.when(changed)
    def _():
        for cp in fetch(e):
            cp.wait()
        wgb[...] = stage_g[...].astype(BF16)
        wub[...] = stage_u[...].astype(BF16)
        wdb[...] = stage_d[...].astype(BF16)

        @pl.when(nxt_ref[i] >= 0)
        def _():
            for cp in fetch(nxt_ref[i]):
                cp.start()

    @pl.when(i < nb_ref[0])
    def _():
        x = _tile_rows_load(xs_ref, MOE_TM).astype(BF16)
        a = jnp.dot(x, wgb[...], preferred_element_type=F32)
        u = jnp.dot(x, wub[...], preferred_element_type=F32)
        hdn = (a * jax.nn.sigmoid(a)) * u
        _tile_rows_store(ys_ref, jnp.dot(hdn.astype(BF16), wdb[...], preferred_element_type=F32))

    @pl.when(i >= nb_ref[0])
    def _():
        ys_ref[...] = jnp.zeros_like(ys_ref)


def _experts(blk_e, n_used, next_e, xs, w_gate, w_up, w_down, layer):
    D = D_MODEL
    n_rows = xs.shape[0] // SUBLANES
    tm = MOE_TM
    FF = EXPERT_FF
    row_block = pl.BlockSpec((tm * SUBLANES, LANES), lambda i, be, nb, nx: (i, 0))
    hbm = pl.BlockSpec(memory_space=pl.ANY)
    grid_spec = pltpu.PrefetchScalarGridSpec(
        num_scalar_prefetch=3,
        grid=(n_rows // tm,),
        in_specs=[row_block, hbm, hbm, hbm],
        out_specs=row_block,
        scratch_shapes=[pltpu.VMEM((D, FF), F32), pltpu.VMEM((D, FF), F32), pltpu.VMEM((FF, D), F32),
                        pltpu.VMEM((D, FF), BF16), pltpu.VMEM((D, FF), BF16), pltpu.VMEM((FF, D), BF16),
                        pltpu.SemaphoreType.DMA((3,))],
    )
    return pl.pallas_call(
        functools.partial(_expert_kernel, layer=layer),
        out_shape=jax.ShapeDtypeStruct(xs.shape, F32),
        grid_spec=grid_spec,
        compiler_params=_cparams("arbitrary"),
        name="moe_experts",
    )(blk_e, n_used, next_e, xs, w_gate, w_up, w_down)


def _combine_kernel(dest_ref, dest_next_ref, info_ref, x_ref, gf_ref, fg_ref, ys_ref, o_ref, y_ref, sem, *,
                    final_norm):
    tb = x_ref.shape[0]
    i = pl.program_id(0)
    last = pl.num_programs(0) - 1

    def gather(d_ref, slot):
        def start(j, c):
            _row_copy(ys_ref, d_ref[0, 0, 2 * j], y_ref.at[slot, 0], j, sem.at[slot]).start(priority=0)
            _row_copy(ys_ref, d_ref[0, 0, 2 * j + 1], y_ref.at[slot, 1], j, sem.at[slot]).start(priority=1)
            return c
        lax.fori_loop(0, tb, start, 0, unroll=8)

    @pl.when(i == 0)
    def _():
        gather(dest_ref, 0)

    for slot in range(2):
        @pl.when(i % 2 == slot)
        def _(slot=slot):
            @pl.when(i < last)
            def _():
                gather(dest_next_ref, 1 - slot)

            def wait(j, c):
                _row_copy(ys_ref, 0, y_ref.at[slot, 0], 0, sem.at[slot]).wait()
                return c

            lax.fori_loop(0, 2 * tb, wait, 0, unroll=8)
            info = info_ref[...]
            moe = (info[:, 2:3] * _tile_rows_load(y_ref.at[slot, 0], tb)
                   + info[:, 3:4] * _tile_rows_load(y_ref.at[slot, 1], tb))
            out = x_ref[...] + gf_ref[0] * moe
            if final_norm:
                ms = jnp.mean(out * out, axis=-1, keepdims=True)
                out = out * lax.rsqrt(ms + RMS_EPS) * fg_ref[...]
            o_ref[...] = out


def _combine(dest3, info, x2, gf, final_g, ys, T, final_norm):
    n_tok, D = x2.shape
    tb = ROW_TB
    per_b = T // tb
    n_steps = n_tok // tb
    return pl.pallas_call(
        functools.partial(_combine_kernel, final_norm=final_norm),
        out_shape=jax.ShapeDtypeStruct((n_tok, D), F32),
        grid=(n_steps,),
        in_specs=[
            pl.BlockSpec((1, 1, 2 * tb), lambda i: (i, 0, 0), memory_space=pltpu.SMEM),
            pl.BlockSpec((1, 1, 2 * tb), lambda i: (jnp.minimum(i + 1, n_steps - 1), 0, 0), memory_space=pltpu.SMEM),
            pl.BlockSpec((tb, LANES), lambda i: (i, 0)),
            pl.BlockSpec((tb, D), lambda i: (i, 0)),
            pl.BlockSpec((1, 1, D), lambda i: (i // per_b, 0, 0)),
            pl.BlockSpec((1, D), lambda i: (0, 0)),
            pl.BlockSpec(memory_space=pl.ANY),
        ],
        out_specs=pl.BlockSpec((tb, D), lambda i: (i, 0)),
        scratch_shapes=[pltpu.VMEM((2, 2, tb * SUBLANES, LANES), F32), pltpu.SemaphoreType.DMA((2,))],
        compiler_params=_cparams("arbitrary"),
        name="moe_combine",
    )(dest3, dest3, info, x2, gf, final_g, ys)


def _moe_rows(n_tok):
    return -(-(2 * n_tok + N_EXPERTS * MOE_TM) // MOE_TM) * MOE_TM


def _moe_layer(x, info, cnt, mod_f, gf, norm_g, w_gate, w_up, w_down, layer, final_g, final_norm):
    B, T, D = x.shape
    n_tok = B * T
    x2 = x.reshape(n_tok, D)
    n_rows = _moe_rows(n_tok)
    dest, meta = _rank(info, cnt)
    seg = meta[0:2, :N_EXPERTS].astype(jnp.int32)
    pad_end = seg[0]
    n_blocks = n_rows // MOE_TM
    blk_start = jnp.arange(n_blocks, dtype=jnp.int32) * MOE_TM
    blk_e = jnp.minimum(jnp.sum(pad_end[None, :] <= blk_start[:, None], axis=1), N_EXPERTS - 1).astype(jnp.int32)
    n_used = (pad_end[N_EXPERTS - 1:] // MOE_TM).astype(jnp.int32)
    dest3 = dest[:, :2].reshape(n_tok // ROW_TB, 1, 2 * ROW_TB)
    xs = _dispatch(dest3, seg, x2, mod_f, norm_g, n_rows, T)
    seg_end = jnp.sum(blk_e[None, :] <= blk_e[:, None], axis=1)
    next_e = jnp.where(seg_end < n_blocks, blk_e[jnp.minimum(seg_end, n_blocks - 1)], -1).astype(jnp.int32)
    ys = _experts(blk_e, n_used, next_e, xs, w_gate, w_up, w_down, layer)
    out = _combine(dest3, info, x2, gf, final_g, ys, T, final_norm)
    return out.reshape(B, T, D)


def kernel(x, c, ada_w, ada_b, norm_mix_g, norm_ffn_g, rw_mu, rw_w_rkv, rw_w0, rw_w1, rw_w2, rw_a0, rw_a1, rw_a2,
           rw_g1, rw_g2, rw_k_k, rw_k_a, rw_r_k, rw_gn_g, rw_gn_b, rw_w_o, ada_kv_w, ada_kv_b, norm_kv_g, w_kv,
           df_w_q, df_lq1, df_lk1, df_lq2, df_lk2, df_subln_g, df_w_o, moe_w_rg, moe_w_re, moe_w_gate, moe_w_up,
           moe_w_down, final_g):
    B, T, D = x.shape
    c_pad = jnp.zeros((SUBLANES, D), F32).at[:B].set(c)
    mod = _ada(c_pad, ada_w, ada_b, 6 * D // 4)[:, :B]
    mod_kv = _ada(c_pad, ada_kv_w[None], ada_kv_b[None], D)[0, :B]
    bf = lambda w: w.astype(BF16)
    row = lambda v: v.reshape(1, -1)

    for l in range(DEPTH):
        sh_m, sc_m, g_m, sh_f, sc_f, g_f = jnp.split(mod[l], 6, axis=-1)
        mod_m = jnp.stack([sh_m, sc_m], axis=1)
        mod_f = jnp.stack([sh_f, sc_f], axis=1)
        if l < N_A_LAYERS:
            i = l
            vec = jnp.stack([rw_w0[i], rw_a0[i], rw_k_k[i], rw_k_a[i]], axis=0)
            r, k, v, lw, kk, al, gate = _rwkv_proj(
                x, mod_m, row(norm_mix_g[l]), rw_mu[i], bf(rw_w_rkv[i]), bf(rw_w1[i]), bf(rw_w2[i]),
                bf(rw_a1[i]), bf(rw_a2[i]), bf(rw_g1[i]), bf(rw_g2[i]), vec)
            pvec = jnp.stack([rw_r_k[i].reshape(-1), rw_gn_g[i], rw_gn_b[i]], axis=0)
            y = _rwkv_scan(r, k, v, lw, kk, al, pvec)
            w_o = bf(rw_w_o[i])
        else:
            j = l - N_A_LAYERS
            q_proj = (mod_m, row(norm_mix_g[l]), bf(df_w_q[j]))
            if l == N_A_LAYERS:
                sh_kv, sc_kv = jnp.split(mod_kv, 2, axis=-1)
                q, kv = _norm_mm(x, [q_proj, (jnp.stack([sh_kv, sc_kv], axis=1), row(norm_kv_g), bf(w_kv))], BF16)
            else:
                q, = _norm_mm(x, [q_proj], BF16)
            lambda_init = 0.8 - 0.6 * math.exp(-0.3 * l)
            lam_vecs = jnp.stack([df_lq1[j], df_lk1[j], df_lq2[j], df_lk2[j]], axis=0)
            y, gate = _diff_attn(q, kv, lam_vecs, row(df_subln_g[j]), lambda_init), None
            w_o = bf(df_w_o[j])
        w_r = jnp.concatenate([moe_w_rg[l], moe_w_re[l], jnp.zeros((D, LANES - N_GROUPS - N_EXPERTS), F32)], axis=1)
        w_r_hi = bf(w_r)
        w_cat = jnp.concatenate([w_r_hi, bf(w_r - w_r_hi.astype(F32))], axis=1)
        x, info, cnt = _proj_res_route(y, gate, x, g_m[:, None, :], w_o, mod_f, row(norm_ffn_g[l]), w_cat)
        x = _moe_layer(x, info, cnt, mod_f, g_f[:, None, :], row(norm_ffn_g[l]), moe_w_gate, moe_w_up,
                       moe_w_down, l, row(final_g), final_norm=(l == DEPTH - 1))
    return x
```

```python
import functools
import math

import jax
import jax.numpy as jnp
from jax import lax
from jax.experimental import pallas as pl
from jax.experimental.pallas import tpu as pltpu

F32 = jnp.float32
BF16 = jnp.bfloat16

D_MODEL = 1024
DEPTH = 2
N_A_LAYERS = DEPTH // 2
RWKV_HEAD = 64
RWKV_HEADS = D_MODEL // RWKV_HEAD
RWKV_GN_EPS = 64e-5
DIFF_QK_DIM = 64
DIFF_V_DIM = 2 * DIFF_QK_DIM
DIFF_HEADS = D_MODEL // DIFF_V_DIM
SUBLN_EPS = 1e-5
N_GROUPS = 4
EXPERTS_PER_GROUP = 8
N_EXPERTS = N_GROUPS * EXPERTS_PER_GROUP
EXPERT_FF = 512
RMS_EPS = 1e-6

LANES = 128
SUBLANES = 8
VMEM_LIMIT_BYTES = 56 * 1024 * 1024

SCAN_CHUNK = 64
SCAN_BASE_BLOCK = 8
SCAN_CHUNKS_PER_STEP = 4
PAIR = 2 * RWKV_HEAD
PROJ_TM = 512
DENSE_TM = 512
ATT_TQ = 512
ATT_HB = 2
MOE_TM = 256
RANK_TB = 1024
ROW_TB = 256
assert SCAN_CHUNK == RWKV_HEAD and PAIR == LANES


def _cparams(*sem):
    return pltpu.CompilerParams(dimension_semantics=sem, vmem_limit_bytes=VMEM_LIMIT_BYTES)


def _dot(a, b):
    return jnp.dot(a.astype(BF16), b.astype(BF16), preferred_element_type=F32)


def _split(x):
    hi = x.astype(BF16)
    lo = (x - hi.astype(F32)).astype(BF16)
    return hi, lo


def _dot3(a, b):
    ah, al = _split(a)
    bh, bl = _split(b)
    d = functools.partial(jnp.dot, preferred_element_type=F32)
    return d(ah, bh) + d(ah, bl) + d(al, bh)


def _dot_hl(a, b_exact):
    ah, al = _split(a)
    d = functools.partial(jnp.dot, preferred_element_type=F32)
    return d(ah, b_exact) + d(al, b_exact)


def _rms_mod(x, g, shift, scale):
    ms = jnp.mean(x * x, axis=-1, keepdims=True)
    return (x * lax.rsqrt(ms + RMS_EPS) * g) * (1.0 + scale) + shift


def _ada_kernel(c_ref, w_ref, b_ref, o_ref):
    c = c_ref[...]
    ca = c * jax.nn.sigmoid(c)
    o_ref[...] = _dot3(ca, w_ref[...]) + b_ref[...]


def _ada(c_pad, w, b, tn):
    L, D, N = w.shape
    return pl.pallas_call(
        _ada_kernel,
        out_shape=jax.ShapeDtypeStruct((L, SUBLANES, N), F32),
        grid=(L, N // tn),
        in_specs=[
            pl.BlockSpec((SUBLANES, D), lambda l, j: (0, 0)),
            pl.BlockSpec((None, D, tn), lambda l, j: (l, 0, j)),
            pl.BlockSpec((None, 1, tn), lambda l, j: (l, 0, j)),
        ],
        out_specs=pl.BlockSpec((None, SUBLANES, tn), lambda l, j: (l, 0, j)),
        compiler_params=_cparams("arbitrary", "arbitrary"),
        name="ada_mod",
    )(c_pad, w, b.reshape(L, 1, N))


def _rwkv_proj_kernel(x_ref, xp_ref, mod_ref, g_ref, mu_ref, wrkv_ref, w1_ref, w2_ref, a1_ref, a2_ref,
                      g1_ref, g2_ref, vec_ref, r_ref, k_ref, v_ref, lw_ref, kk_ref, al_ref, gate_ref):
    i = pl.program_id(1)
    g = g_ref[...]
    shift, scale = mod_ref[0, 0:1, :], mod_ref[0, 1:2, :]
    h = _rms_mod(x_ref[0], g, shift, scale)
    hp = _rms_mod(xp_ref[0, SUBLANES - 1:SUBLANES, :], g, shift, scale)
    hp = jnp.where(i == 0, 0.0, hp)
    row = lax.broadcasted_iota(jnp.int32, h.shape, 0)
    h_prev = jnp.where(row == 0, hp, pltpu.roll(h, 1, axis=0))
    xx = h_prev - h
    mu = mu_ref[...]
    xs = [(h + xx * mu[j:j + 1, :]).astype(BF16) for j in range(6)]
    w0, a0, k_k, k_a = (vec_ref[j:j + 1, :] for j in range(4))
    d = functools.partial(jnp.dot, preferred_element_type=F32)
    r = d(xs[0], wrkv_ref[0])
    k = d(xs[1], wrkv_ref[1])
    v = d(xs[2], wrkv_ref[2])
    z = w0 + _dot(jnp.tanh(d(xs[3], w1_ref[...])), w2_ref[...])
    lw = (-math.exp(-0.5)) * jax.nn.sigmoid(z)
    a = jax.nn.sigmoid(a0 + _dot(d(xs[4], a1_ref[...]), a2_ref[...]))
    gate = _dot(jax.nn.sigmoid(d(xs[5], g1_ref[...])), g2_ref[...])
    r_ref[0] = r.astype(BF16)
    k_ref[0] = (k * (1.0 + (a - 1.0) * k_a)).astype(BF16)
    v_ref[0] = v.astype(BF16)
    lw_ref[0] = lw
    kk_ref[0] = (k * k_k).astype(BF16)
    al_ref[0] = a.astype(BF16)
    gate_ref[0] = gate.astype(BF16)


def _rwkv_proj(x, mod2, g, mu, wrkv, w1, w2, a1, a2, g1, g2, vec):
    B, T, D = x.shape
    tm = PROJ_TM
    const2 = lambda b, i: (0, 0)
    const3 = lambda b, i: (0, 0, 0)
    act = pl.BlockSpec((1, tm, D), lambda b, i: (b, i, 0))
    n_sub = tm // SUBLANES
    return pl.pallas_call(
        _rwkv_proj_kernel,
        out_shape=[jax.ShapeDtypeStruct((B, T, D), F32 if n == 3 else BF16) for n in range(7)],
        grid=(B, T // tm),
        in_specs=[
            act,
            pl.BlockSpec((1, SUBLANES, D), lambda b, i: (b, jnp.maximum(i * n_sub - 1, 0), 0)),
            pl.BlockSpec((1, 2, D), lambda b, i: (b, 0, 0)),
            pl.BlockSpec((1, D), const2),
            pl.BlockSpec((6, D), const2),
            pl.BlockSpec((3, D, D), const3),
            pl.BlockSpec(w1.shape, const2), pl.BlockSpec(w2.shape, const2),
            pl.BlockSpec(a1.shape, const2), pl.BlockSpec(a2.shape, const2),
            pl.BlockSpec(g1.shape, const2), pl.BlockSpec(g2.shape, const2),
            pl.BlockSpec((4, D), const2),
        ],
        out_specs=[act] * 7,
        compiler_params=_cparams("arbitrary", "arbitrary"),
        name="rwkv_proj",
    )(x, x, mod2, g, mu, wrkv, w1, w2, a1, a2, g1, g2, vec)


def _rwkv_scan_kernel(r_ref, k_ref, v_ref, lw_ref, kk_ref, al_ref, pv_ref, y_ref, h_ref):
    C = SCAN_CHUNK
    P2 = 2 * C

    @pl.when(pl.program_id(1) == 0)
    def _():
        h_ref[...] = jnp.zeros_like(h_ref)

    lane = lax.broadcasted_iota(jnp.int32, (1, PAIR), 1)
    m_left = (lane < RWKV_HEAD).astype(F32)
    m_right = 1.0 - m_left
    ri = lax.broadcasted_iota(jnp.int32, (P2, P2), 0)
    ci = lax.broadcasted_iota(jnp.int32, (P2, P2), 1)
    same = (ri >= C) == (ci >= C)
    strict = same & (ri > ci)
    incl = same & (ri >= ci)
    eye = ri == ci
    tri =(lax.broadcasted_iota(jnp.int32, (C, C), 0) >= lax.broadcasted_iota(jnp.int32, (C, C), 1)).astype(BF16)

    def stack(x):
        return jnp.concatenate([x * m_left, x * m_right], axis=0)

    def head_sums(x):
        s_left = jnp.sum(x * m_left, axis=-1, keepdims=True)
        s_right = jnp.sum(x * m_right, axis=-1, keepdims=True)
        return jnp.where(lane < RWKV_HEAD, s_left, s_right)

    inv_n = 1.0 / RWKV_HEAD
    dd = functools.partial(jnp.dot, preferred_element_type=F32)
    n_pairs = RWKV_HEADS // 2
    units = [(ch, p) for ch in range(SCAN_CHUNKS_PER_STEP) for p in range(n_pairs)]
    idx = [(slice(ch * C, (ch + 1) * C), slice(p * PAIR, (p + 1) * PAIR)) for ch, p in units]
    U = range(len(units))
    ld = lambda ref, rs, sl: ref[0, rs, sl].astype(F32)
    kkr = [ld(kk_ref, rs, sl) for rs, sl in idx]
    ss = [head_sums(x * x) for x in kkr]
    lws = [lw_ref[0, rs, sl] for rs, sl in idx]
    Ls = []
    for lw in lws:
        l_hi, l_lo = _split(lw)
        cs = dd(tri, jnp.concatenate([l_hi, l_lo], axis=1))
        Ls.append(cs[:, :PAIR] + cs[:, PAIR:])
    lhs_g, rhs_g, bk_hat, vs, at32, rt32, dec_end = [], [], [], [], [], [], []
    for u in U:
        rs, sl = idx[u]
        L, lw = Ls[u], lws[u]
        kk = kkr[u] * lax.rsqrt(jnp.maximum(ss[u], 1e-24))
        b_vec = kk * ld(al_ref, rs, sl)
        k = ld(k_ref, rs, sl)
        LC = L[C - 1:C, :]
        e_neg = jnp.exp(-L)
        e_end = jnp.exp(LC - L)
        At = stack(-kk * jnp.exp(L - lw))
        Rt = stack(ld(r_ref, rs, sl) * jnp.exp(L))
        at32.append(At)
        rt32.append(Rt)
        lhs_g.append(jnp.concatenate([At, Rt], axis=0).astype(BF16))
        rhs_g.append(jnp.concatenate([stack(b_vec * e_neg), stack(k * e_neg)], axis=0).astype(BF16))
        bk_hat.append(jnp.concatenate([stack(b_vec * e_end), stack(k * e_end)], axis=0))
        vs.append(stack(ld(v_ref, rs, sl)).astype(BF16))
        dec_end.append(jnp.exp(LC))
    G = [lax.dot_general(lhs_g[u], rhs_g[u], (((1,), (1,)), ((), ())), preferred_element_type=F32) for u in U]
    A_ak = [jnp.where(strict, G[u][:P2, P2:], 0.0).astype(BF16) for u in U]
    A_r = [jnp.concatenate([jnp.where(incl, G[u][P2:, :P2], 0.0), jnp.where(incl, G[u][P2:, P2:], 0.0)],
                           axis=1).astype(BF16) for u in U]
    W = [dd(A_ak[u], vs[u]) for u in U]
    bsz = lambda b: (ri >> int(math.log2(b))) == (ci >> int(math.log2(b)))
    b8 = bsz(SCAN_BASE_BLOCK)
    D1 = [jnp.where(strict & b8, G[u][:P2, :P2], 0.0).astype(BF16) for u in U]
    D2 = [dd(D1[u], D1[u]).astype(BF16) for u in U]
    D4 = [dd(D2[u], D2[u]).astype(BF16) for u in U]
    eye_f = eye.astype(F32)
    P1 = [eye_f + D1[u].astype(F32) + D2[u].astype(F32) + dd(D1[u], D2[u]) for u in U]
    Tm = [P1[u] + dd(P1[u].astype(BF16), D4[u]) for u in U]
    blk = SCAN_BASE_BLOCK
    while blk < C:
        off = strict & bsz(2 * blk) & ~bsz(blk)
        Mo = [jnp.where(off, G[u][:P2, :P2], 0.0).astype(BF16) for u in U]
        Tb = [Tm[u].astype(BF16) for u in U]
        TM = [dd(Tb[u], Mo[u]).astype(BF16) for u in U]
        Tm = [Tm[u] + dd(TM[u], Tb[u]) for u in U]
        blk *= 2
    Z = [dd(Tm[u].astype(BF16), jnp.concatenate([at32[u], W[u]], axis=1).astype(BF16)) for u in U]
    rhs = [jnp.concatenate([Z[u].astype(BF16), jnp.concatenate([jnp.zeros_like(vs[u]), vs[u]], axis=1)], axis=0)
           for u in U]
    o6 = [dd(A_r[u], rhs[u]) for u in U]
    o7 = [dd(bk_hat[u].T.astype(BF16), rhs[u]) for u in U]
    H = [h_ref[p] for p in range(n_pairs)]
    Y = [None] * len(units)
    for u in U:
        p = units[u][1]
        Hb = H[p].astype(BF16)
        Y[u] = dd((rt32[u] + o6[u][:, :PAIR]).astype(BF16), Hb) + o6[u][:, PAIR:]
        Mbd = o7[u][:, :PAIR] + jnp.where(eye, dec_end[u], 0.0)
        H[p] = dd(Mbd.astype(BF16), Hb) + o7[u][:, PAIR:]
    for p in range(n_pairs):
        h_ref[p] = H[p]
    ys = [Y[u][:C] + Y[u][C:] for u in U]
    rk = [ld(r_ref, rs, sl) * ld(k_ref, rs, sl) * pv_ref[0:1, sl] for rs, sl in idx]
    st1 = [head_sums(jnp.concatenate([ys[u], rk[u]], axis=0)) for u in U]
    yc = [ys[u] - st1[u][:C] * inv_n for u in U]
    var = [head_sums(yc[u] * yc[u]) * inv_n for u in U]
    for u in U:
        rs, sl = idx[u]
        bonus = st1[u][C:] * ld(v_ref, rs, sl)
        y_ref[0, rs, sl] = (yc[u] * lax.rsqrt(var[u] + RWKV_GN_EPS) * pv_ref[1:2, sl] + pv_ref[2:3, sl]
                            + bonus).astype(y_ref.dtype)


def _rwkv_scan(r, k, v, lw, kk, al, pvec):
    B, T, D = r.shape
    rows = SCAN_CHUNK * SCAN_CHUNKS_PER_STEP
    act = pl.BlockSpec((1, rows, D), lambda b, c: (b, c, 0))
    return pl.pallas_call(
        _rwkv_scan_kernel,
        out_shape=jax.ShapeDtypeStruct((B, T, D), BF16),
        grid=(B, T // rows),
        in_specs=[act] * 6 + [pl.BlockSpec((3, D), lambda b, c: (0, 0))],
        out_specs=act,
        scratch_shapes=[pltpu.VMEM((RWKV_HEADS // 2, PAIR, PAIR), F32)],
        compiler_params=_cparams("arbitrary", "arbitrary"),
        name="rwkv_scan",
    )(r, k, v, lw, kk, al, pvec)


def _proj_res_route_kernel(*refs, has_gate):
    if has_gate:
        y_ref, g_ref, x_ref, gm_ref, w_ref, modf_ref, gf_ref, wr_ref, o_ref, info_ref, cnt_ref = refs
        y = y_ref[0].astype(F32) * g_ref[0].astype(F32)
    else:
        y_ref, x_ref, gm_ref, w_ref, modf_ref, gf_ref, wr_ref, o_ref, info_ref, cnt_ref = refs
        y = y_ref[0]
    x_new =x_ref[0] + gm_ref[0] * jnp.dot(y.astype(BF16), w_ref[...], preferred_element_type=F32)
    o_ref[0] = x_new
    h = _rms_mod(x_new, gf_ref[...], modf_ref[0, 0:1, :], modf_ref[0, 1:2, :])
    first = (pl.program_id(0) == 0) & (pl.program_id(1) == 0)
    _route(h, wr_ref[...], info_ref, cnt_ref, first)


def _proj_res_route(y, g, x, gm, w, mod_f, norm_f, w_cat):
    B, T, D = x.shape
    tm = DENSE_TM
    per_b = T // tm
    act =pl.BlockSpec((1, tm, D), lambda b, i: (b, i, 0))
    const = lambda b, i: (0, 0)
    flat = lambda b, i: (b * per_b + i, 0)
    ins = [y] + ([g] if g is not None else []) + [x, gm, w, mod_f, norm_f, w_cat]
    specs = [act] * (len(ins) - 5) + [pl.BlockSpec((1, 1, D), lambda b, i: (b, 0, 0)), pl.BlockSpec((D, D), const),
                                     pl.BlockSpec((1, 2, D), lambda b, i: (b, 0, 0)), pl.BlockSpec((1, D), const),
                                     pl.BlockSpec((D, 2 * LANES), const)]
    return pl.pallas_call(
        functools.partial(_proj_res_route_kernel, has_gate=g is not None),
        out_shape=[jax.ShapeDtypeStruct((B, T, D), F32), jax.ShapeDtypeStruct((B * T, LANES), F32),
                   jax.ShapeDtypeStruct((SUBLANES, LANES), F32)],
        grid=(B, per_b),
        in_specs=specs,
        out_specs=[act, pl.BlockSpec((tm, LANES), flat), pl.BlockSpec((SUBLANES, LANES), const)],
        compiler_params=_cparams("arbitrary", "arbitrary"),
        name="proj_res_route",
    )(*ins)


def _norm_mm_kernel(*refs, n_proj):
    x_ref = refs[0]
    ins, outs = refs[1:1 + 3 * n_proj], refs[1 + 3 * n_proj:]
    x = x_ref[0]
    xn = x * lax.rsqrt(jnp.mean(x * x, axis=-1, keepdims=True) + RMS_EPS)
    for p in range(n_proj):
        mod_ref, g_ref, w_ref = ins[3 * p:3 * p + 3]
        h = (xn * g_ref[...]) * (1.0 + mod_ref[0, 1:2, :]) + mod_ref[0, 0:1, :]
        outs[p][0] = jnp.dot(h.astype(BF16), w_ref[...], preferred_element_type=F32).astype(outs[p].dtype)


def _norm_mm(x, projs, out_dtype):
    B, T, D = x.shape
    tm = DENSE_TM
    in_specs = [pl.BlockSpec((1, tm, D), lambda b, i: (b, i, 0))]
    args = [x]
    for mod2, g, w in projs:
        in_specs += [pl.BlockSpec((1, 2, D), lambda b, i: (b, 0, 0)), pl.BlockSpec((1, D), lambda b, i: (0, 0)),
                     pl.BlockSpec(w.shape, lambda b, i: (0, 0))]
        args += [mod2, g, w]
    return pl.pallas_call(
        functools.partial(_norm_mm_kernel, n_proj=len(projs)),
        out_shape=[jax.ShapeDtypeStruct((B, T, w.shape[1]), out_dtype) for _, _, w in projs],
        grid=(B, T // tm),
        in_specs=in_specs,
        out_specs=[pl.BlockSpec((1, tm, w.shape[1]), lambda b, i: (b, i, 0)) for _, _, w in projs],
        compiler_params=_cparams("arbitrary", "arbitrary"),
        name="norm_mm",
    )(*args)


def _diff_attn_kernel(q_ref, k_ref, v_ref, lam_ref, sg_ref, o_ref, m_ref, acc_ref, s_ref, *, lambda_init):
    tq, HB, dv = ATT_TQ, ATT_HB, DIFF_V_DIM
    qi = pl.program_id(2)
    heads = range(HB)
    hs = [slice(h * dv, (h + 1) * dv) for h in heads]
    lane = lax.broadcasted_iota(jnp.int32, (1, dv), 1)
    m_left = (lane < DIFF_QK_DIM).astype(F32)
    qs = []
    for h in heads:
        q = q_ref[0, :, hs[h]].astype(F32) * (DIFF_QK_DIM ** -0.5 * math.log2(math.e))
        qs.append(jnp.concatenate([q * m_left, q * (1.0 - m_left)], axis=0).astype(BF16))
    ones_col = jnp.ones((tq, dv), BF16)
    causal = (lax.broadcasted_iota(jnp.int32, (2 * tq, tq), 1)
              <= lax.broadcasted_iota(jnp.int32, (2 * tq, tq), 0) % tq)

    def key_rows(j):
        return pl.ds(pl.multiple_of(j * tq, tq), tq)

    def scores_into(slot, j):
        for h in heads:
            s_ref[slot, h] = lax.dot_general(qs[h], k_ref[0, key_rows(j), hs[h]], (((1,), (1,)), ((), ())),
                                             preferred_element_type=F32)

    def block(slot, j, first, prefetch):
        if prefetch is not None:
            scores_into(1 - slot, prefetch)
        for h in heads:
            s = s_ref[slot, h]
            if first:
                s = jnp.where(causal, s, -jnp.inf)
                m_new = jnp.broadcast_to(jnp.max(s, axis=-1, keepdims=True), (2 * tq, LANES))
            else:
                m_old = m_ref[h]
                m_new = jnp.maximum(m_old, jnp.max(s, axis=-1, keepdims=True))
            m_ref[h] = m_new
            p = jnp.exp2(s - jnp.concatenate([m_new] * (tq // LANES), axis=1)).astype(BF16)
            pv = jnp.dot(p, jnp.concatenate([v_ref[0, key_rows(j), hs[h]], ones_col], axis=1),
                         preferred_element_type=F32)
            if first:
                acc_ref[h] = pv
            else:
                alpha = jnp.exp2(m_old - m_new)
                acc_ref[h] = acc_ref[h] * jnp.concatenate([alpha, alpha], axis=1) + pv

    scores_into(0, qi)
    block(0, qi, True, 0)

    def pair(u, c):
        last = jnp.maximum(qi - 1, 0)
        block(1, 2 * u, False, jnp.minimum(2 * u + 1, last))
        block(0, 2 * u + 1, False, jnp.minimum(2 * u + 2, last))
        return c

    lax.fori_loop(0, qi // 2, pair, 0)

    @pl.when(qi % 2 == 1)
    def _():
        block(1, qi - 1, False, None)

    lv = lam_ref[...]
    lam = (jnp.exp(jnp.sum(lv[0:1] * lv[1:2], axis=-1, keepdims=True))
           - jnp.exp(jnp.sum(lv[2:3] * lv[3:4], axis=-1, keepdims=True)) + lambda_init)
    for h in heads:
        acc = acc_ref[h]
        o = acc[:, :dv] / acc[:, dv:]
        o = o[:tq] - lam * o[tq:]
        ms = jnp.mean(o * o, axis=-1, keepdims=True)
        o_ref[0, :, hs[h]] = (o * lax.rsqrt(ms + SUBLN_EPS) * sg_ref[...] * (1.0 - lambda_init)).astype(o_ref.dtype)


def _diff_attn(q, kv, lam_vecs, subln_g, lambda_init):
    B, T, D = q.shape
    tq, HB = ATT_TQ, ATT_HB
    n_hb = DIFF_HEADS // HB
    w = HB * DIFF_V_DIM
    return pl.pallas_call(
        functools.partial(_diff_attn_kernel, lambda_init=lambda_init),
        out_shape=jax.ShapeDtypeStruct((B, T, D), BF16),
        grid=(B, n_hb, T // tq),
        in_specs=[
            pl.BlockSpec((1, tq, w), lambda b, h, i: (b, i, h)),
            pl.BlockSpec((1, T, w), lambda b, h, i: (b, 0, h)),
            pl.BlockSpec((1, T, w), lambda b, h, i: (b, 0, n_hb + h)),
            pl.BlockSpec((4, DIFF_QK_DIM), lambda b, h, i: (0, 0)),
            pl.BlockSpec((1, DIFF_V_DIM), lambda b, h, i: (0, 0)),
        ],
        out_specs=pl.BlockSpec((1, tq, w), lambda b, h, i: (b, i, h)),
        scratch_shapes=[pltpu.VMEM((HB, 2 * tq, LANES), F32), pltpu.VMEM((HB, 2 * tq, 2 * DIFF_V_DIM), F32),
                        pltpu.VMEM((2, HB, 2 * tq, tq), F32)],
        compiler_params=_cparams("arbitrary", "arbitrary", "arbitrary"),
        name="diff_attn",
    )(q, kv, kv, lam_vecs, subln_g)


def _route(h, w_cat, info_ref, cnt_ref, first_step):
    h_hi, h_lo = _split(h)
    two = jnp.dot(h_hi, w_cat, preferred_element_type=F32)
    logit = (two[:, :LANES] + two[:, LANES:]
             + jnp.dot(h_lo, w_cat[:, :LANES], preferred_element_type=F32))
    lane_i = lax.broadcasted_iota(jnp.int32, logit.shape, 1)
    lane = lane_i.astype(F32)
    neg = -jnp.inf
    big = float(LANES)
    is_grp = lane_i < N_GROUPS
    gl = jnp.where(is_grp, logit, neg)
    gmax = jnp.max(gl, axis=-1, keepdims=True)
    gidx = jnp.min(jnp.where(gl == gmax, lane, big), axis=-1, keepdims=True)
    grp_gate = 1.0 / jnp.sum(jnp.where(is_grp, jnp.exp(logit - gmax), 0.0), axis=-1, keepdims=True)
    lo = N_GROUPS + gidx * EXPERTS_PER_GROUP
    in_grp = (lane >= lo) & (lane < lo + EXPERTS_PER_GROUP)
    el = jnp.where(in_grp, logit, neg)
    t1 = jnp.max(el, axis=-1, keepdims=True)
    i1 = jnp.min(jnp.where(el == t1, lane, big), axis=-1, keepdims=True)
    el2 = jnp.where(lane == i1, neg, el)
    t2 = jnp.max(el2, axis=-1, keepdims=True)
    i2 = jnp.min(jnp.where(el2 == t2, lane, big), axis=-1, keepdims=True)
    e21 = jnp.exp(t2 - t1)
    p1 = 1.0 / (1.0 + e21)
    w1 = grp_gate * p1
    w2 = grp_gate * (e21 * p1)
    e1 = i1 - N_GROUPS
    e2 = i2 - N_GROUPS
    info_ref[...] = jnp.where(lane_i == 0, e1, jnp.where(lane_i == 1, e2, jnp.where(lane_i == 2, w1, jnp.where(lane_i == 3, w2, 0.0))))

    @pl.when(first_step)
    def _():
        cnt_ref[...] = jnp.zeros_like(cnt_ref)

    picked = ((lane == e1) | (lane == e2)).astype(F32)
    cnt_ref[...] += jnp.broadcast_to(jnp.sum(picked, axis=0, keepdims=True), cnt_ref.shape)


def _rank_kernel(info_ref, cnt_ref, dest_ref, meta_ref, start_ref):
    i = pl.program_id(0)
    tb = info_ref.shape[0]
    lane = lax.broadcasted_iota(jnp.int32, (tb, LANES), 1)
    info = info_ref[...]
    e0 = info[:, 0:1].astype(jnp.int32)
    e1 = info[:, 1:2].astype(jnp.int32)
    o0 = (lane == e0).astype(F32)
    o1 = (lane == e1).astype(F32)
    both = o0 + o1

    @pl.when(i == 0)
    def _():
        cnt = cnt_ref[0:1, :]
        padded = jnp.floor((cnt + (MOE_TM - 1)) * (1.0 / MOE_TM)) * MOE_TM
        r = lax.broadcasted_iota(jnp.int32, (LANES, LANES), 0)
        c = lax.broadcasted_iota(jnp.int32, (LANES, LANES), 1)
        upper_strict = (r < c).astype(BF16)
        start = _dot_hl(jnp.broadcast_to(padded, (SUBLANES, LANES)), upper_strict)[0:1]
        start_ref[...] = start
        row = lax.broadcasted_iota(jnp.int32, (SUBLANES, LANES), 0)
        meta_ref[...] = jnp.where(row == 0, start + padded, start + cnt)

    r = lax.broadcasted_iota(jnp.int32, (tb, tb), 0)
    c = lax.broadcasted_iota(jnp.int32, (tb, tb), 1)
    lower_strict = (r > c).astype(BF16)
    before = jnp.dot(lower_strict, both.astype(BF16), preferred_element_type=F32) + start_ref[...]
    d0 = jnp.sum(o0 * before, axis=-1, keepdims=True)
    d1 = jnp.sum(o1 * before, axis=-1, keepdims=True)
    dest_ref[...] = jnp.where(lane == 0, d0, jnp.where(lane == 1, d1, 0.0)).astype(jnp.int32)
    start_ref[...] += jnp.sum(both, axis=0, keepdims=True)


def _rank(info, cnt):
    n_tok = info.shape[0]
    tb = RANK_TB
    n_blk = n_tok // tb
    return pl.pallas_call(
        _rank_kernel,
        out_shape=[jax.ShapeDtypeStruct((n_tok, LANES), jnp.int32), jax.ShapeDtypeStruct((SUBLANES, LANES), F32)],
        grid=(n_blk,),
        in_specs=[pl.BlockSpec((tb, LANES), lambda i: (i, 0)), pl.BlockSpec((SUBLANES, LANES), lambda i: (0, 0))],
        out_specs=[pl.BlockSpec((tb, LANES), lambda i: (i, 0)), pl.BlockSpec((SUBLANES, LANES), lambda i: (0, 0))],
        scratch_shapes=[pltpu.VMEM((1, LANES), F32)],
        compiler_params=_cparams("arbitrary"),
        name="moe_rank",
    )(info, cnt)


assert D_MODEL == SUBLANES * LANES


def _tile_rows_store(ref, x):
    n = x.shape[0]
    for s in range(SUBLANES):
        ref[pl.ds(s, n, stride=SUBLANES), :] = x[:, s * LANES:(s + 1) * LANES]


def _tile_rows_load(ref, n):
    return jnp.concatenate([ref[pl.ds(s, n, stride=SUBLANES), :] for s in range(SUBLANES)], axis=1)


def _row_copy(src_ref, s, dst_ref, d, sem):
    rows = lambda r: pl.ds(pl.multiple_of(r * SUBLANES, SUBLANES), SUBLANES)
    return pltpu.make_async_copy(src_ref.at[rows(s)], dst_ref.at[rows(d)], sem)


def _zero_padding_rows(seg_ref, xs_ref, zero_ref, zsem, *, start):
    def copy(first_row, n_rows):
        cp = pltpu.make_async_copy(zero_ref.at[pl.ds(0, n_rows * SUBLANES)],
                                   xs_ref.at[pl.ds(pl.multiple_of(first_row * SUBLANES, SUBLANES), n_rows * SUBLANES)],
                                   zsem)
        cp.start() if start else cp.wait()

    for e in range(N_EXPERTS):
        lo, n = seg_ref[1, e], seg_ref[0, e] - seg_ref[1, e]
        bit = MOE_TM // 2
        while bit >= 1:
            @pl.when((n & bit) != 0)
            def _(lo=lo, n=n, bit=bit):
                copy(lo + (n & ~(2 * bit - 1)), bit)
            bit //= 2
    n_total = xs_ref.shape[0] // SUBLANES

    def tail(j, c):
        copy(seg_ref[0, N_EXPERTS - 1] + j * MOE_TM, MOE_TM)
        return c

    lax.fori_loop(0, (n_total - seg_ref[0, N_EXPERTS - 1]) // MOE_TM, tail, 0)


def _dispatch_kernel(dest_ref, seg_ref, x_ref, mod_ref, g_ref, xs_ref, h_ref, zero_ref, sem, zsem):
    tb = x_ref.shape[0]
    i = pl.program_id(0)
    last = pl.num_programs(0) - 1
    h = _rms_mod(x_ref[...], g_ref[...], mod_ref[0, 0:1, :], mod_ref[0, 1:2, :])

    @pl.when(i == 0)
    def _():
        zero_ref[...] = jnp.zeros_like(zero_ref)
        _zero_padding_rows(seg_ref, xs_ref, zero_ref, zsem, start=True)

    @pl.when(i == last)
    def _():
        _zero_padding_rows(seg_ref, xs_ref, zero_ref, zsem, start=False)

    def drain(slot):
        def wait(j, c):
            _row_copy(h_ref.at[slot], 0, xs_ref, 0, sem.at[slot]).wait()
            return c
        lax.fori_loop(0, 2 * tb, wait, 0, unroll=8)

    for slot in range(2):
        @pl.when(i % 2 == slot)
        def _(slot=slot):
            @pl.when(i >= 2)
            def _():
                drain(slot)

            _tile_rows_store(h_ref.at[slot], h)

            def start(j, c):
                _row_copy(h_ref.at[slot], j, xs_ref, dest_ref[0, 0, 2 * j], sem.at[slot]).start(priority=0)
                _row_copy(h_ref.at[slot], j, xs_ref, dest_ref[0, 0, 2 * j + 1], sem.at[slot]).start(priority=1)
                return c

            lax.fori_loop(0, tb, start, 0, unroll=8)

            @pl.when(i == last)
            def _():
                drain(slot)

                @pl.when(i >= 1)
                def _():
                    drain(1 - slot)


def _dispatch(dest3, seg, x2, mod2, g, n_rows, T):
    n_tok, D = x2.shape
    tb = ROW_TB
    per_b = T // tb
    return pl.pallas_call(
        _dispatch_kernel,
        out_shape=jax.ShapeDtypeStruct((n_rows * SUBLANES, LANES), F32),
        grid=(n_tok // tb,),
        in_specs=[
            pl.BlockSpec((1, 1, 2 * tb), lambda i: (i, 0, 0), memory_space=pltpu.SMEM),
            pl.BlockSpec(memory_space=pltpu.SMEM),
            pl.BlockSpec((tb, D), lambda i: (i, 0)),
            pl.BlockSpec((1, 2, D), lambda i: (i // per_b, 0, 0)),
            pl.BlockSpec((1, D), lambda i: (0, 0)),
        ],
        out_specs=pl.BlockSpec(memory_space=pl.ANY),
        scratch_shapes=[pltpu.VMEM((2, tb * SUBLANES, LANES), F32), pltpu.VMEM((MOE_TM * SUBLANES, LANES), F32),
                        pltpu.SemaphoreType.DMA((2,)), pltpu.SemaphoreType.DMA(())],
        compiler_params=_cparams("arbitrary"),
        name="moe_dispatch",
    )(dest3, seg, x2, mod2, g)


def _expert_kernel(be_ref, nb_ref, nxt_ref, xs_ref, wg_hbm, wu_hbm, wd_hbm, ys_ref, stage_g, stage_u, stage_d,
                   wgb, wub, wdb, sem, *, layer):
    i = pl.program_id(0)
    e = be_ref[i]
    changed = (i == 0) | (e != be_ref[jnp.maximum(i - 1, 0)])

    def fetch(ex):
        return (pltpu.make_async_copy(wg_hbm.at[layer, ex], stage_g, sem.at[0]),
                pltpu.make_async_copy(wu_hbm.at[layer, ex], stage_u, sem.at[1]),
                pltpu.make_async_copy(wd_hbm.at[layer, ex], stage_d, sem.at[2]))

    @pl.when(i == 0)
    def _():
        for cp in fetch(e):
            cp.start()

    @pl.when(changed)
    def _():
        for cp in fetch(e):
            cp.wait()
        wgb[...] = stage_g[...].astype(BF16)
        wub[...] = stage_u[...].astype(BF16)
        wdb[...] = stage_d[...].astype(BF16)

        @pl.when(nxt_ref[i] >= 0)
        def _():
            for cp in fetch(nxt_ref[i]):
                cp.start()

    @pl.when(i < nb_ref[0])
    def _():
        x = _tile_rows_load(xs_ref, MOE_TM).astype(BF16)
        a = jnp.dot(x, wgb[...], preferred_element_type=F32)
        u = jnp.dot(x, wub[...], preferred_element_type=F32)
        hdn = (a * jax.nn.sigmoid(a)) * u
        _tile_rows_store(ys_ref, jnp.dot(hdn.astype(BF16), wdb[...], preferred_element_type=F32))

    @pl.when(i >= nb_ref[0])
    def _():
        ys_ref[...] = jnp.zeros_like(ys_ref)


def _experts(blk_e, n_used, next_e, xs, w_gate, w_up, w_down, layer):
    D = D_MODEL
    n_rows = xs.shape[0] // SUBLANES
    tm = MOE_TM
    FF = EXPERT_FF
    row_block = pl.BlockSpec((tm * SUBLANES, LANES), lambda i, be, nb, nx: (i, 0))
    hbm = pl.BlockSpec(memory_space=pl.ANY)
    grid_spec = pltpu.PrefetchScalarGridSpec(
        num_scalar_prefetch=3,
        grid=(n_rows // tm,),
        in_specs=[row_block, hbm, hbm, hbm],
        out_specs=row_block,
        scratch_shapes=[pltpu.VMEM((D, FF), F32), pltpu.VMEM((D, FF), F32), pltpu.VMEM((FF, D), F32),
                        pltpu.VMEM((D, FF), BF16), pltpu.VMEM((D, FF), BF16), pltpu.VMEM((FF, D), BF16),
                        pltpu.SemaphoreType.DMA((3,))],
    )
    return pl.pallas_call(
        functools.partial(_expert_kernel, layer=layer),
        out_shape=jax.ShapeDtypeStruct(xs.shape, F32),
        grid_spec=grid_spec,
        compiler_params=_cparams("arbitrary"),
        name="moe_experts",
    )(blk_e, n_used, next_e, xs, w_gate, w_up, w_down)


def _combine_kernel(dest_ref, dest_next_ref, info_ref, x_ref, gf_ref, fg_ref, ys_ref, o_ref, y_ref, sem, *,
                    final_norm):
    tb = x_ref.shape[0]
    i = pl.program_id(0)
    last = pl.num_programs(0) - 1

    def gather(d_ref, slot):
        def start(j, c):
            _row_copy(ys_ref, d_ref[0, 0, 2 * j], y_ref.at[slot, 0], j, sem.at[slot]).start(priority=0)
            _row_copy(ys_ref, d_ref[0, 0, 2 * j + 1], y_ref.at[slot, 1], j, sem.at[slot]).start(priority=1)
            return c
        lax.fori_loop(0, tb, start, 0, unroll=8)

    @pl.when(i == 0)
    def _():
        gather(dest_ref, 0)

    for slot in range(2):
        @pl.when(i % 2 == slot)
        def _(slot=slot):
            @pl.when(i < last)
            def _():
                gather(dest_next_ref, 1 - slot)

            def wait(j, c):
                _row_copy(ys_ref, 0, y_ref.at[slot, 0], 0, sem.at[slot]).wait()
                return c

            lax.fori_loop(0, 2 * tb, wait, 0, unroll=8)
            info = info_ref[...]
            moe = (info[:, 2:3] * _tile_rows_load(y_ref.at[slot, 0], tb)
                   + info[:, 3:4] * _tile_rows_load(y_ref.at[slot, 1], tb))
            out = x_ref[...] + gf_ref[0] * moe
            if final_norm:
                ms = jnp.mean(out * out, axis=-1, keepdims=True)
                out = out * lax.rsqrt(ms + RMS_EPS) * fg_ref[...]
            o_ref[...] = out


def _combine(dest3, info, x2, gf, final_g, ys, T, final_norm):
    n_tok, D = x2.shape
    tb = ROW_TB
    per_b = T // tb
    n_steps = n_tok // tb
    return pl.pallas_call(
        functools.partial(_combine_kernel, final_norm=final_norm),
        out_shape=jax.ShapeDtypeStruct((n_tok, D), F32),
        grid=(n_steps,),
        in_specs=[
            pl.BlockSpec((1, 1, 2 * tb), lambda i: (i, 0, 0), memory_space=pltpu.SMEM),
            pl.BlockSpec((1, 1, 2 * tb), lambda i: (jnp.minimum(i + 1, n_steps - 1), 0, 0), memory_space=pltpu.SMEM),
            pl.BlockSpec((tb, LANES), lambda i: (i, 0)),
            pl.BlockSpec((tb, D), lambda i: (i, 0)),
            pl.BlockSpec((1, 1, D), lambda i: (i // per_b, 0, 0)),
            pl.BlockSpec((1, D), lambda i: (0, 0)),
            pl.BlockSpec(memory_space=pl.ANY),
        ],
        out_specs=pl.BlockSpec((tb, D), lambda i: (i, 0)),
        scratch_shapes=[pltpu.VMEM((2, 2, tb * SUBLANES, LANES), F32), pltpu.SemaphoreType.DMA((2,))],
        compiler_params=_cparams("arbitrary"),
        name="moe_combine",
    )(dest3, dest3, info, x2, gf, final_g, ys)


def _moe_rows(n_tok):
    return -(-(2 * n_tok + N_EXPERTS * MOE_TM) // MOE_TM) * MOE_TM


def _moe_layer(x, info, cnt, mod_f, gf, norm_g, w_gate, w_up, w_down, layer, final_g, final_norm):
    B, T, D = x.shape
    n_tok = B * T
    x2 = x.reshape(n_tok, D)
    n_rows = _moe_rows(n_tok)
    dest, meta = _rank(info, cnt)
    seg = meta[0:2, :N_EXPERTS].astype(jnp.int32)
    pad_end = seg[0]
    n_blocks = n_rows // MOE_TM
    blk_start = jnp.arange(n_blocks, dtype=jnp.int32) * MOE_TM
    blk_e = jnp.minimum(jnp.sum(pad_end[None, :] <= blk_start[:, None], axis=1), N_EXPERTS - 1).astype(jnp.int32)
    n_used = (pad_end[N_EXPERTS - 1:] // MOE_TM).astype(jnp.int32)
    dest3 = dest[:, :2].reshape(n_tok // ROW_TB, 1, 2 * ROW_TB)
    xs = _dispatch(dest3, seg, x2, mod_f, norm_g, n_rows, T)
    seg_end = jnp.sum(blk_e[None, :] <= blk_e[:, None], axis=1)
    next_e = jnp.where(seg_end < n_blocks, blk_e[jnp.minimum(seg_end, n_blocks - 1)], -1).astype(jnp.int32)
    ys = _experts(blk_e, n_used, next_e, xs, w_gate, w_up, w_down, layer)
    out = _combine(dest3, info, x2, gf, final_g, ys, T, final_norm)
    return out.reshape(B, T, D)


def kernel(x, c, ada_w, ada_b, norm_mix_g, norm_ffn_g, rw_mu, rw_w_rkv, rw_w0, rw_w1, rw_w2, rw_a0, rw_a1, rw_a2,
           rw_g1, rw_g2, rw_k_k, rw_k_a, rw_r_k, rw_gn_g, rw_gn_b, rw_w_o, ada_kv_w, ada_kv_b, norm_kv_g, w_kv,
           df_w_q, df_lq1, df_lk1, df_lq2, df_lk2, df_subln_g, df_w_o, moe_w_rg, moe_w_re, moe_w_gate, moe_w_up,
           moe_w_down, final_g):
    B, T, D = x.shape
    c_pad = jnp.zeros((SUBLANES, D), F32).at[:B].set(c)
    mod = _ada(c_pad, ada_w, ada_b, 6 * D // 4)[:, :B]
    mod_kv = _ada(c_pad, ada_kv_w[None], ada_kv_b[None], D)[0, :B]
    bf = lambda w: w.astype(BF16)
    row = lambda v: v.reshape(1, -1)

    for l in range(DEPTH):
        sh_m, sc_m, g_m, sh_f, sc_f, g_f = jnp.split(mod[l], 6, axis=-1)
        mod_m = jnp.stack([sh_m, sc_m], axis=1)
        mod_f = jnp.stack([sh_f, sc_f], axis=1)
        if l < N_A_LAYERS:
            i = l
            vec = jnp.stack([rw_w0[i], rw_a0[i], rw_k_k[i], rw_k_a[i]], axis=0)
            r, k, v, lw, kk, al, gate = _rwkv_proj(
                x, mod_m, row(norm_mix_g[l]), rw_mu[i], bf(rw_w_rkv[i]), bf(rw_w1[i]), bf(rw_w2[i]),
                bf(rw_a1[i]), bf(rw_a2[i]), bf(rw_g1[i]), bf(rw_g2[i]), vec)
            pvec = jnp.stack([rw_r_k[i].reshape(-1), rw_gn_g[i], rw_gn_b[i]], axis=0)
            y = _rwkv_scan(r, k, v, lw, kk, al, pvec)
            w_o = bf(rw_w_o[i])
        else:
            j = l - N_A_LAYERS
            q_proj = (mod_m, row(norm_mix_g[l]), bf(df_w_q[j]))
            if l == N_A_LAYERS:
                sh_kv, sc_kv = jnp.split(mod_kv, 2, axis=-1)
                q, kv = _norm_mm(x, [q_proj, (jnp.stack([sh_kv, sc_kv], axis=1), row(norm_kv_g), bf(w_kv))], BF16)
            else:
                q, = _norm_mm(x, [q_proj], BF16)
            lambda_init = 0.8 - 0.6 * math.exp(-0.3 * l)
            lam_vecs = jnp.stack([df_lq1[j], df_lk1[j], df_lq2[j], df_lk2[j]], axis=0)
            y, gate = _diff_attn(q, kv, lam_vecs, row(df_subln_g[j]), lambda_init), None
            w_o = bf(df_w_o[j])
        w_r = jnp.concatenate([moe_w_rg[l], moe_w_re[l], jnp.zeros((D, LANES - N_GROUPS - N_EXPERTS), F32)], axis=1)
        w_r_hi = bf(w_r)
        w_cat = jnp.concatenate([w_r_hi, bf(w_r - w_r_hi.astype(F32))], axis=1)
        x, info, cnt = _proj_res_route(y, gate, x, g_m[:, None, :], w_o, mod_f, row(norm_ffn_g[l]), w_cat)
        x = _moe_layer(x, info, cnt, mod_f, g_f[:, None, :], row(norm_ffn_g[l]), moe_w_gate, moe_w_up,
                       moe_w_down, l, row(final_g), final_norm=(l == DEPTH - 1))
    return x
```

```python
import functools
import math

import jax
import jax.numpy as jnp
from jax import lax
from jax.experimental import pallas as pl
from jax.experimental.pallas import tpu as pltpu

F32 = jnp.float32
BF16 = jnp.bfloat16

D_MODEL = 1024
DEPTH = 2
N_A_LAYERS = DEPTH // 2
RWKV_HEAD = 64
RWKV_HEADS = D_MODEL // RWKV_HEAD
RWKV_GN_EPS = 64e-5
DIFF_QK_DIM = 64
DIFF_V_DIM = 2 * DIFF_QK_DIM
DIFF_HEADS = D_MODEL // DIFF_V_DIM
SUBLN_EPS = 1e-5
N_GROUPS = 4
EXPERTS_PER_GROUP = 8
N_EXPERTS = N_GROUPS * EXPERTS_PER_GROUP
EXPERT_FF = 512
RMS_EPS = 1e-6

LANES = 128
SUBLANES = 8
VMEM_LIMIT_BYTES = 56 * 1024 * 1024

SCAN_CHUNK = 64
SCAN_BASE_BLOCK = 8
SCAN_CHUNKS_PER_STEP = 2
PAIR = 2 * RWKV_HEAD
PROJ_TM = 512
DENSE_TM = 512
ATT_TQ = 512
ATT_HB = 2
MOE_TM = 256
RANK_TB = 1024
ROW_TB = 256
assert SCAN_CHUNK == RWKV_HEAD and PAIR == LANES


def _cparams(*sem):
    return pltpu.CompilerParams(dimension_semantics=sem, vmem_limit_bytes=VMEM_LIMIT_BYTES)


def _dot(a, b):
    return jnp.dot(a.astype(BF16), b.astype(BF16), preferred_element_type=F32)


def _split(x):
    hi = x.astype(BF16)
    lo = (x - hi.astype(F32)).astype(BF16)
    return hi, lo


def _dot3(a, b):
    ah, al = _split(a)
    bh, bl = _split(b)
    d = functools.partial(jnp.dot, preferred_element_type=F32)
    return d(ah, bh) + d(ah, bl) + d(al, bh)


def _dot_hl(a, b_exact):
    ah, al = _split(a)
    d = functools.partial(jnp.dot, preferred_element_type=F32)
    return d(ah, b_exact) + d(al, b_exact)


def _rms_mod(x, g, shift, scale):
    ms = jnp.mean(x * x, axis=-1, keepdims=True)
    return (x * lax.rsqrt(ms + RMS_EPS) * g) * (1.0 + scale) + shift


def _ada_kernel(c_ref, w_ref, b_ref, o_ref):
    c = c_ref[...]
    ca = c * jax.nn.sigmoid(c)
    o_ref[...] = _dot3(ca, w_ref[...]) + b_ref[...]


def _ada(c_pad, w, b, tn):
    L, D, N = w.shape
    return pl.pallas_call(
        _ada_kernel,
        out_shape=jax.ShapeDtypeStruct((L, SUBLANES, N), F32),
        grid=(L, N // tn),
        in_specs=[
            pl.BlockSpec((SUBLANES, D), lambda l, j: (0, 0)),
            pl.BlockSpec((None, D, tn), lambda l, j: (l, 0, j)),
            pl.BlockSpec((None, 1, tn), lambda l, j: (l, 0, j)),
        ],
        out_specs=pl.BlockSpec((None, SUBLANES, tn), lambda l, j: (l, 0, j)),
        compiler_params=_cparams("arbitrary", "arbitrary"),
        name="ada_mod",
    )(c_pad, w, b.reshape(L, 1, N))


def _rwkv_proj_kernel(x_ref, xp_ref, mod_ref, g_ref, mu_ref, wrkv_ref, w1_ref, w2_ref, a1_ref, a2_ref,
                      g1_ref, g2_ref, vec_ref, r_ref, k_ref, v_ref, lw_ref, kk_ref, al_ref, gate_ref):
    i = pl.program_id(1)
    g = g_ref[...]
    shift, scale = mod_ref[0, 0:1, :], mod_ref[0, 1:2, :]
    h = _rms_mod(x_ref[0], g, shift, scale)
    hp = _rms_mod(xp_ref[0, SUBLANES - 1:SUBLANES, :], g, shift, scale)
    hp = jnp.where(i == 0, 0.0, hp)
    row = lax.broadcasted_iota(jnp.int32, h.shape, 0)
    h_prev = jnp.where(row == 0, hp, pltpu.roll(h, 1, axis=0))
    xx = h_prev - h
    mu = mu_ref[...]
    xs = [(h + xx * mu[j:j + 1, :]).astype(BF16) for j in range(6)]
    w0, a0, k_k, k_a = (vec_ref[j:j + 1, :] for j in range(4))
    d = functools.partial(jnp.dot, preferred_element_type=F32)
    r = d(xs[0], wrkv_ref[0])
    k = d(xs[1], wrkv_ref[1])
    v = d(xs[2], wrkv_ref[2])
    z = w0 + _dot(jnp.tanh(d(xs[3], w1_ref[...])), w2_ref[...])
    lw = (-math.exp(-0.5)) * jax.nn.sigmoid(z)
    a = jax.nn.sigmoid(a0 + _dot(d(xs[4], a1_ref[...]), a2_ref[...]))
    gate = _dot(jax.nn.sigmoid(d(xs[5], g1_ref[...])), g2_ref[...])
    r_ref[0] = r.astype(BF16)
    k_ref[0] = (k * (1.0 + (a - 1.0) * k_a)).astype(BF16)
    v_ref[0] = v.astype(BF16)
    lw_ref[0] = lw
    kk_ref[0] = (k * k_k).astype(BF16)
    al_ref[0] = a.astype(BF16)
    gate_ref[0] = gate.astype(BF16)


def _rwkv_proj(x, mod2, g, mu, wrkv, w1, w2, a1, a2, g1, g2, vec):
    B, T, D = x.shape
    tm = PROJ_TM
    const2 = lambda b, i: (0, 0)
    const3 = lambda b, i: (0, 0, 0)
    act = pl.BlockSpec((1, tm, D), lambda b, i: (b, i, 0))
    n_sub = tm // SUBLANES
    return pl.pallas_call(
        _rwkv_proj_kernel,
        out_shape=[jax.ShapeDtypeStruct((B, T, D), F32 if n == 3 else BF16) for n in range(7)],
        grid=(B, T // tm),
        in_specs=[
            act,
            pl.BlockSpec((1, SUBLANES, D), lambda b, i: (b, jnp.maximum(i * n_sub - 1, 0), 0)),
            pl.BlockSpec((1, 2, D), lambda b, i: (b, 0, 0)),
            pl.BlockSpec((1, D), const2),
            pl.BlockSpec((6, D), const2),
            pl.BlockSpec((3, D, D), const3),
            pl.BlockSpec(w1.shape, const2), pl.BlockSpec(w2.shape, const2),
            pl.BlockSpec(a1.shape, const2), pl.BlockSpec(a2.shape, const2),
            pl.BlockSpec(g1.shape, const2), pl.BlockSpec(g2.shape, const2),
            pl.BlockSpec((4, D), const2),
        ],
        out_specs=[act] * 7,
        compiler_params=_cparams("arbitrary", "arbitrary"),
        name="rwkv_proj",
    )(x, x, mod2, g, mu, wrkv, w1, w2, a1, a2, g1, g2, vec)


def _rwkv_scan_kernel(r_ref, k_ref, v_ref, lw_ref, kk_ref, al_ref, pv_ref, y_ref, h_ref):
    C = SCAN_CHUNK
    P2 = 2 * C

    @pl.when(pl.program_id(1) == 0)
    def _():
        h_ref[...] = jnp.zeros_like(h_ref)

    lane = lax.broadcasted_iota(jnp.int32, (1, PAIR), 1)
    m_left = (lane < RWKV_HEAD).astype(F32)
    m_right = 1.0 - m_left
    ri = lax.broadcasted_iota(jnp.int32, (P2, P2), 0)
    ci = lax.broadcasted_iota(jnp.int32, (P2, P2), 1)
    same = (ri >= C) == (ci >= C)
    strict = same & (ri > ci)
    incl = same & (ri >= ci)
    eye = ri == ci
    block_ones = same.astype(BF16)
    tri = (lax.broadcasted_iota(jnp.int32, (C, C), 0) >= lax.broadcasted_iota(jnp.int32, (C, C), 1)).astype(BF16)

    def stack(x):
        return jnp.concatenate([x * m_left, x * m_right], axis=0)

    def head_sums(x):
        s_left = jnp.sum(x * m_left, axis=-1, keepdims=True)
        s_right = jnp.sum(x * m_right, axis=-1, keepdims=True)
        return jnp.where(lane < RWKV_HEAD, s_left, s_right)

    inv_n = 1.0 / RWKV_HEAD
    dd = functools.partial(jnp.dot, preferred_element_type=F32)
    n_pairs = RWKV_HEADS // 2
    units = [(ch, p) for ch in range(SCAN_CHUNKS_PER_STEP) for p in range(n_pairs)]
    idx = [(slice(ch * C, (ch + 1) * C), slice(p * PAIR, (p + 1) * PAIR)) for ch, p in units]
    U = range(len(units))
    ld = lambda ref, rs, sl: ref[0, rs, sl].astype(F32)
    kkr = [ld(kk_ref, rs, sl) for rs, sl in idx]
    ss = [head_sums(x * x) for x in kkr]
    lws = [lw_ref[0, rs, sl] for rs, sl in idx]
    Ls = []
    for lw in lws:
        l_hi, l_lo = _split(lw)
        cs = dd(tri, jnp.concatenate([l_hi, l_lo], axis=1))
        Ls.append(cs[:, :PAIR] + cs[:, PAIR:])
    lhs_g, rhs_g, bk_hat, vs, at32, rt32, dec_end = [], [], [], [], [], [], []
    for u in U:
        rs, sl = idx[u]
        L, lw = Ls[u], lws[u]
        kk = kkr[u] * lax.rsqrt(jnp.maximum(ss[u], 1e-24))
        b_vec = kk * ld(al_ref, rs, sl)
        k = ld(k_ref, rs, sl)
        LC = L[C - 1:C, :]
        e_neg = jnp.exp(-L)
        e_end = jnp.exp(LC - L)
        At = stack(-kk * jnp.exp(L - lw))
        Rt = stack(ld(r_ref, rs, sl) * jnp.exp(L))
        at32.append(At)
        rt32.append(Rt)
        lhs_g.append(jnp.concatenate([At, Rt], axis=0).astype(BF16))
        rhs_g.append(jnp.concatenate([stack(b_vec * e_neg), stack(k * e_neg)], axis=0).astype(BF16))
        bk_hat.append(jnp.concatenate([stack(b_vec * e_end), stack(k * e_end)], axis=0))
        vs.append(stack(ld(v_ref, rs, sl)).astype(BF16))
        dec_end.append(jnp.exp(LC))
    G = [lax.dot_general(lhs_g[u], rhs_g[u], (((1,), (1,)), ((), ())), preferred_element_type=F32) for u in U]
    A_ak = [jnp.where(strict, G[u][:P2, P2:], 0.0).astype(BF16) for u in U]
    A_r = [jnp.concatenate([jnp.where(incl, G[u][P2:, :P2], 0.0), jnp.where(incl, G[u][P2:, P2:], 0.0)],
                           axis=1).astype(BF16) for u in U]
    W = [dd(A_ak[u], vs[u]) for u in U]
    bsz = lambda b: (ri >> int(math.log2(b))) == (ci >> int(math.log2(b)))
    b8 = bsz(SCAN_BASE_BLOCK)
    D1 = [jnp.where(strict & b8, G[u][:P2, :P2], 0.0).astype(BF16) for u in U]
    D2 = [dd(D1[u], D1[u]).astype(BF16) for u in U]
    D4 = [dd(D2[u], D2[u]).astype(BF16) for u in U]
    eye_f = eye.astype(F32)
    P1 = [eye_f + D1[u].astype(F32) + D2[u].astype(F32) + dd(D1[u], D2[u]) for u in U]
    Tm = [P1[u] + dd(P1[u].astype(BF16), D4[u]) for u in U]
    blk = SCAN_BASE_BLOCK
    while blk < C:
        off = strict & bsz(2 * blk) & ~bsz(blk)
        Mo = [jnp.where(off, G[u][:P2, :P2], 0.0).astype(BF16) for u in U]
        Tb = [Tm[u].astype(BF16) for u in U]
        TM = [dd(Tb[u], Mo[u]).astype(BF16) for u in U]
        Tm = [Tm[u] + dd(TM[u], Tb[u]) for u in U]
        blk *= 2
    Z = [dd(Tm[u].astype(BF16), jnp.concatenate([at32[u], W[u]], axis=1).astype(BF16)) for u in U]
    rhs = [jnp.concatenate([Z[u].astype(BF16), jnp.concatenate([jnp.zeros_like(vs[u]), vs[u]], axis=1)], axis=0)
           for u in U]
    o6 = [dd(A_r[u], rhs[u]) for u in U]
    o7 = [dd(bk_hat[u].T.astype(BF16), rhs[u]) for u in U]
    H = [h_ref[p] for p in range(n_pairs)]
    Y = [None] * len(units)
    for u in U:
        p = units[u][1]
        Hb = H[p].astype(BF16)
        Y[u] = dd((rt32[u] + o6[u][:, :PAIR]).astype(BF16), Hb) + o6[u][:, PAIR:]
        Mbd = o7[u][:, :PAIR] + jnp.where(eye, dec_end[u], 0.0)
        H[p] = dd(Mbd.astype(BF16), Hb) + o7[u][:, PAIR:]
    for p in range(n_pairs):
        h_ref[p] = H[p]
    ys = [Y[u][:C] + Y[u][C:] for u in U]
    rk = [ld(r_ref, rs, sl) * ld(k_ref, rs, sl) * pv_ref[0:1, sl] for rs, sl in idx]
    st1 = [head_sums(jnp.concatenate([ys[u], rk[u]], axis=0)) for u in U]
    yc = [ys[u] - st1[u][:C] * inv_n for u in U]
    var = [head_sums(yc[u] * yc[u]) * inv_n for u in U]
    for u in U:
        rs, sl = idx[u]
        bonus = st1[u][C:] * ld(v_ref, rs, sl)
        y_ref[0, rs, sl] = (yc[u] * lax.rsqrt(var[u] + RWKV_GN_EPS) * pv_ref[1:2, sl] + pv_ref[2:3, sl]
                            + bonus).astype(y_ref.dtype)


def _rwkv_scan(r, k, v, lw, kk, al, pvec):
    B, T, D = r.shape
    rows = SCAN_CHUNK * SCAN_CHUNKS_PER_STEP
    act = pl.BlockSpec((1, rows, D), lambda b, c: (b, c, 0))
    return pl.pallas_call(
        _rwkv_scan_kernel,
        out_shape=jax.ShapeDtypeStruct((B, T, D), BF16),
        grid=(B, T // rows),
        in_specs=[act] * 6 + [pl.BlockSpec((3, D), lambda b, c: (0, 0))],
        out_specs=act,
        scratch_shapes=[pltpu.VMEM((RWKV_HEADS // 2, PAIR, PAIR), F32)],
        compiler_params=_cparams("arbitrary", "arbitrary"),
        name="rwkv_scan",
    )(r, k, v, lw, kk, al, pvec)


def _proj_res_route_kernel(*refs, has_gate):
    if has_gate:
        y_ref, g_ref, x_ref, gm_ref, w_ref, modf_ref, gf_ref, wr_ref, o_ref, info_ref, cnt_ref = refs
        y = y_ref[0].astype(F32) * g_ref[0].astype(F32)
    else:
        y_ref, x_ref, gm_ref, w_ref, modf_ref, gf_ref, wr_ref, o_ref, info_ref, cnt_ref = refs
        y = y_ref[0]
    x_new =x_ref[0] + gm_ref[0] * jnp.dot(y.astype(BF16), w_ref[...], preferred_element_type=F32)
    o_ref[0] = x_new
    h = _rms_mod(x_new, gf_ref[...], modf_ref[0, 0:1, :], modf_ref[0, 1:2, :])
    first = (pl.program_id(0) == 0) & (pl.program_id(1) == 0)
    _route(h, wr_ref[...], info_ref, cnt_ref, first)


def _proj_res_route(y, g, x, gm, w, mod_f, norm_f, w_cat):
    B, T, D = x.shape
    tm = DENSE_TM
    per_b = T // tm
    act =pl.BlockSpec((1, tm, D), lambda b, i: (b, i, 0))
    const = lambda b, i: (0, 0)
    flat = lambda b, i: (b * per_b + i, 0)
    ins = [y] + ([g] if g is not None else []) + [x, gm, w, mod_f, norm_f, w_cat]
    specs = [act] * (len(ins) - 5) + [pl.BlockSpec((1, 1, D), lambda b, i: (b, 0, 0)), pl.BlockSpec((D, D), const),
                                     pl.BlockSpec((1, 2, D), lambda b, i: (b, 0, 0)), pl.BlockSpec((1, D), const),
                                     pl.BlockSpec((D, 2 * LANES), const)]
    return pl.pallas_call(
        functools.partial(_proj_res_route_kernel, has_gate=g is not None),
        out_shape=[jax.ShapeDtypeStruct((B, T, D), F32), jax.ShapeDtypeStruct((B * T, LANES), F32),
                   jax.ShapeDtypeStruct((SUBLANES, LANES), F32)],
        grid=(B, per_b),
        in_specs=specs,
        out_specs=[act, pl.BlockSpec((tm, LANES), flat), pl.BlockSpec((SUBLANES, LANES), const)],
        compiler_params=_cparams("arbitrary", "arbitrary"),
        name="proj_res_route",
    )(*ins)


def _norm_mm_kernel(*refs, n_proj):
    x_ref = refs[0]
    ins, outs = refs[1:1 + 3 * n_proj], refs[1 + 3 * n_proj:]
    x = x_ref[0]
    xn = x * lax.rsqrt(jnp.mean(x * x, axis=-1, keepdims=True) + RMS_EPS)
    for p in range(n_proj):
        mod_ref, g_ref, w_ref = ins[3 * p:3 * p + 3]
        h = (xn * g_ref[...]) * (1.0 + mod_ref[0, 1:2, :]) + mod_ref[0, 0:1, :]
        outs[p][0] = jnp.dot(h.astype(BF16), w_ref[...], preferred_element_type=F32).astype(outs[p].dtype)


def _norm_mm(x, projs, out_dtype):
    B, T, D = x.shape
    tm = DENSE_TM
    in_specs = [pl.BlockSpec((1, tm, D), lambda b, i: (b, i, 0))]
    args = [x]
    for mod2, g, w in projs:
        in_specs += [pl.BlockSpec((1, 2, D), lambda b, i: (b, 0, 0)), pl.BlockSpec((1, D), lambda b, i: (0, 0)),
                     pl.BlockSpec(w.shape, lambda b, i: (0, 0))]
        args += [mod2, g, w]
    return pl.pallas_call(
        functools.partial(_norm_mm_kernel, n_proj=len(projs)),
        out_shape=[jax.ShapeDtypeStruct((B, T, w.shape[1]), out_dtype) for _, _, w in projs],
        grid=(B, T // tm),
        in_specs=in_specs,
        out_specs=[pl.BlockSpec((1, tm, w.shape[1]), lambda b, i: (b, i, 0)) for _, _, w in projs],
        compiler_params=_cparams("arbitrary", "arbitrary"),
        name="norm_mm",
    )(*args)


def _diff_attn_kernel(q_ref, k_ref, v_ref, lam_ref, sg_ref, o_ref, m_ref, acc_ref, s_ref, *, lambda_init):
    tq, HB, dv = ATT_TQ, ATT_HB, DIFF_V_DIM
    qi = pl.program_id(2)
    heads = range(HB)
    hs = [slice(h * dv, (h + 1) * dv) for h in heads]
    lane = lax.broadcasted_iota(jnp.int32, (1, dv), 1)
    m_left = (lane < DIFF_QK_DIM).astype(F32)
    qs = []
    for h in heads:
        q = q_ref[0, :, hs[h]].astype(F32) * (DIFF_QK_DIM ** -0.5 * math.log2(math.e))
        qs.append(jnp.concatenate([q * m_left, q * (1.0 - m_left)], axis=0).astype(BF16))
    ones_col = jnp.ones((tq, dv), BF16)
    causal = (lax.broadcasted_iota(jnp.int32, (2 * tq, tq), 1)
              <= lax.broadcasted_iota(jnp.int32, (2 * tq, tq), 0) % tq)

    def key_rows(j):
        return pl.ds(pl.multiple_of(j * tq, tq), tq)

    def scores_into(slot, j):
        for h in heads:
            s_ref[slot, h] = lax.dot_general(qs[h], k_ref[0, key_rows(j), hs[h]], (((1,), (1,)), ((), ())),
                                             preferred_element_type=F32)

    def block(slot, j, first, prefetch):
        if prefetch is not None:
            scores_into(1 - slot, prefetch)
        for h in heads:
            s = s_ref[slot, h]
            if first:
                s = jnp.where(causal, s, -jnp.inf)
                m_new = jnp.broadcast_to(jnp.max(s, axis=-1, keepdims=True), (2 * tq, LANES))
            else:
                m_old = m_ref[h]
                m_new = jnp.maximum(m_old, jnp.max(s, axis=-1, keepdims=True))
            m_ref[h] = m_new
            p = jnp.exp2(s - jnp.concatenate([m_new] * (tq // LANES), axis=1)).astype(BF16)
            pv = jnp.dot(p, jnp.concatenate([v_ref[0, key_rows(j), hs[h]], ones_col], axis=1),
                         preferred_element_type=F32)
            if first:
                acc_ref[h] = pv
            else:
                alpha = jnp.exp2(m_old - m_new)
                acc_ref[h] = acc_ref[h] * jnp.concatenate([alpha, alpha], axis=1) + pv

    scores_into(0, qi)
    block(0, qi, True, 0)

    def pair(u, c):
        last = jnp.maximum(qi - 1, 0)
        block(1, 2 * u, False, jnp.minimum(2 * u + 1, last))
        block(0, 2 * u + 1, False, jnp.minimum(2 * u + 2, last))
        return c

    lax.fori_loop(0, qi // 2, pair, 0)

    @pl.when(qi % 2 == 1)
    def _():
        block(1, qi - 1, False, None)

    lv = lam_ref[...]
    lam = (jnp.exp(jnp.sum(lv[0:1] * lv[1:2], axis=-1, keepdims=True))
           - jnp.exp(jnp.sum(lv[2:3] * lv[3:4], axis=-1, keepdims=True)) + lambda_init)
    for h in heads:
        acc = acc_ref[h]
        o = acc[:, :dv] / acc[:, dv:]
        o = o[:tq] - lam * o[tq:]
        ms = jnp.mean(o * o, axis=-1, keepdims=True)
        o_ref[0, :, hs[h]] = (o * lax.rsqrt(ms + SUBLN_EPS) * sg_ref[...] * (1.0 - lambda_init)).astype(o_ref.dtype)


def _diff_attn(q, kv, lam_vecs, subln_g, lambda_init):
    B, T, D = q.shape
    tq, HB = ATT_TQ, ATT_HB
    n_hb = DIFF_HEADS // HB
    w = HB * DIFF_V_DIM
    return pl.pallas_call(
        functools.partial(_diff_attn_kernel, lambda_init=lambda_init),
        out_shape=jax.ShapeDtypeStruct((B, T, D), BF16),
        grid=(B, n_hb, T // tq),
        in_specs=[
            pl.BlockSpec((1, tq, w), lambda b, h, i: (b, i, h)),
            pl.BlockSpec((1, T, w), lambda b, h, i: (b, 0, h)),
            pl.BlockSpec((1, T, w), lambda b, h, i: (b, 0, n_hb + h)),
            pl.BlockSpec((4, DIFF_QK_DIM), lambda b, h, i: (0, 0)),
            pl.BlockSpec((1, DIFF_V_DIM), lambda b, h, i: (0, 0)),
        ],
        out_specs=pl.BlockSpec((1, tq, w), lambda b, h, i: (b, i, h)),
        scratch_shapes=[pltpu.VMEM((HB, 2 * tq, LANES), F32), pltpu.VMEM((HB, 2 * tq, 2 * DIFF_V_DIM), F32),
                        pltpu.VMEM((2, HB, 2 * tq, tq), F32)],
        compiler_params=_cparams("arbitrary", "arbitrary", "arbitrary"),
        name="diff_attn",
    )(q, kv, kv, lam_vecs, subln_g)


def _route(h, w_cat, info_ref, cnt_ref, first_step):
    h_hi, h_lo = _split(h)
    two = jnp.dot(h_hi, w_cat, preferred_element_type=F32)
    logit = (two[:, :LANES] + two[:, LANES:]
             + jnp.dot(h_lo, w_cat[:, :LANES], preferred_element_type=F32))
    lane_i = lax.broadcasted_iota(jnp.int32, logit.shape, 1)
    lane = lane_i.astype(F32)
    neg = -jnp.inf
    big = float(LANES)
    is_grp = lane_i < N_GROUPS
    gl = jnp.where(is_grp, logit, neg)
    gmax = jnp.max(gl, axis=-1, keepdims=True)
    gidx = jnp.min(jnp.where(gl == gmax, lane, big), axis=-1, keepdims=True)
    grp_gate = 1.0 / jnp.sum(jnp.where(is_grp, jnp.exp(logit - gmax), 0.0), axis=-1, keepdims=True)
    lo = N_GROUPS + gidx * EXPERTS_PER_GROUP
    in_grp = (lane >= lo) & (lane < lo + EXPERTS_PER_GROUP)
    el = jnp.where(in_grp, logit, neg)
    t1 = jnp.max(el, axis=-1, keepdims=True)
    i1 = jnp.min(jnp.where(el == t1, lane, big), axis=-1, keepdims=True)
    el2 = jnp.where(lane == i1, neg, el)
    t2 = jnp.max(el2, axis=-1, keepdims=True)
    i2 = jnp.min(jnp.where(el2 == t2, lane, big), axis=-1, keepdims=True)
    e21 = jnp.exp(t2 - t1)
    p1 = 1.0 / (1.0 + e21)
    w1 = grp_gate * p1
    w2 = grp_gate * (e21 * p1)
    e1 = i1 - N_GROUPS
    e2 = i2 - N_GROUPS
    info_ref[...] = jnp.where(lane_i == 0, e1, jnp.where(lane_i == 1, e2, jnp.where(lane_i == 2, w1, jnp.where(lane_i == 3, w2, 0.0))))

    @pl.when(first_step)
    def _():
        cnt_ref[...] = jnp.zeros_like(cnt_ref)

    picked = ((lane == e1) | (lane == e2)).astype(F32)
    cnt_ref[...] += jnp.broadcast_to(jnp.sum(picked, axis=0, keepdims=True), cnt_ref.shape)


def _rank_kernel(info_ref, cnt_ref, dest_ref, meta_ref, start_ref):
    i = pl.program_id(0)
    tb = info_ref.shape[0]
    lane = lax.broadcasted_iota(jnp.int32, (tb, LANES), 1)
    info = info_ref[...]
    e0 = info[:, 0:1].astype(jnp.int32)
    e1 = info[:, 1:2].astype(jnp.int32)
    o0 = (lane == e0).astype(F32)
    o1 = (lane == e1).astype(F32)
    both = o0 + o1

    @pl.when(i == 0)
    def _():
        cnt = cnt_ref[0:1, :]
        padded = jnp.floor((cnt + (MOE_TM - 1)) * (1.0 / MOE_TM)) * MOE_TM
        r = lax.broadcasted_iota(jnp.int32, (LANES, LANES), 0)
        c = lax.broadcasted_iota(jnp.int32, (LANES, LANES), 1)
        upper_strict = (r < c).astype(BF16)
        start = _dot_hl(jnp.broadcast_to(padded, (SUBLANES, LANES)), upper_strict)[0:1]
        start_ref[...] = start
        row = lax.broadcasted_iota(jnp.int32, (SUBLANES, LANES), 0)
        meta_ref[...] = jnp.where(row == 0, start + padded, start + cnt)

    r = lax.broadcasted_iota(jnp.int32, (tb, tb), 0)
    c = lax.broadcasted_iota(jnp.int32, (tb, tb), 1)
    lower_strict = (r > c).astype(BF16)
    before = jnp.dot(lower_strict, both.astype(BF16), preferred_element_type=F32) + start_ref[...]
    d0 = jnp.sum(o0 * before, axis=-1, keepdims=True)
    d1 = jnp.sum(o1 * before, axis=-1, keepdims=True)
    dest_ref[...] = jnp.where(lane == 0, d0, jnp.where(lane == 1, d1, 0.0)).astype(jnp.int32)
    start_ref[...] += jnp.sum(both, axis=0, keepdims=True)


def _rank(info, cnt):
    n_tok = info.shape[0]
    tb = RANK_TB
    n_blk = n_tok // tb
    return pl.pallas_call(
        _rank_kernel,
        out_shape=[jax.ShapeDtypeStruct((n_tok, LANES), jnp.int32), jax.ShapeDtypeStruct((SUBLANES, LANES), F32)],
        grid=(n_blk,),
        in_specs=[pl.BlockSpec((tb, LANES), lambda i: (i, 0)), pl.BlockSpec((SUBLANES, LANES), lambda i: (0, 0))],
        out_specs=[pl.BlockSpec((tb, LANES), lambda i: (i, 0)), pl.BlockSpec((SUBLANES, LANES), lambda i: (0, 0))],
        scratch_shapes=[pltpu.VMEM((1, LANES), F32)],
        compiler_params=_cparams("arbitrary"),
        name="moe_rank",
    )(info, cnt)


assert D_MODEL == SUBLANES * LANES


def _tile_rows_store(ref, x):
    n = x.shape[0]
    for s in range(SUBLANES):
        ref[pl.ds(s, n, stride=SUBLANES), :] = x[:, s * LANES:(s + 1) * LANES]


def _tile_rows_load(ref, n):
    return jnp.concatenate([ref[pl.ds(s, n, stride=SUBLANES), :] for s in range(SUBLANES)], axis=1)


def _row_copy(src_ref, s, dst_ref, d, sem):
    rows = lambda r: pl.ds(pl.multiple_of(r * SUBLANES, SUBLANES), SUBLANES)
    return pltpu.make_async_copy(src_ref.at[rows(s)], dst_ref.at[rows(d)], sem)


def _zero_padding_rows(seg_ref, xs_ref, zero_ref, zsem, *, start):
    def copy(first_row, n_rows):
        cp = pltpu.make_async_copy(zero_ref.at[pl.ds(0, n_rows * SUBLANES)],
                                   xs_ref.at[pl.ds(pl.multiple_of(first_row * SUBLANES, SUBLANES), n_rows * SUBLANES)],
                                   zsem)
        cp.start() if start else cp.wait()

    for e in range(N_EXPERTS):
        lo, n = seg_ref[1, e], seg_ref[0, e] - seg_ref[1, e]
        bit = MOE_TM // 2
        while bit >= 1:
            @pl.when((n & bit) != 0)
            def _(lo=lo, n=n, bit=bit):
                copy(lo + (n & ~(2 * bit - 1)), bit)
            bit //= 2
    n_total = xs_ref.shape[0] // SUBLANES

    def tail(j, c):
        copy(seg_ref[0, N_EXPERTS - 1] + j * MOE_TM, MOE_TM)
        return c

    lax.fori_loop(0, (n_total - seg_ref[0, N_EXPERTS - 1]) // MOE_TM, tail, 0)


def _dispatch_kernel(dest_ref, seg_ref, x_ref, mod_ref, g_ref, xs_ref, h_ref, zero_ref, sem, zsem):
    tb = x_ref.shape[0]
    i = pl.program_id(0)
    last = pl.num_programs(0) - 1
    h = _rms_mod(x_ref[...], g_ref[...], mod_ref[0, 0:1, :], mod_ref[0, 1:2, :])

    @pl.when(i == 0)
    def _():
        zero_ref[...] = jnp.zeros_like(zero_ref)
        _zero_padding_rows(seg_ref, xs_ref, zero_ref, zsem, start=True)

    @pl.when(i == last)
    def _():
        _zero_padding_rows(seg_ref, xs_ref, zero_ref, zsem, start=False)

    def drain(slot):
        def wait(j, c):
            _row_copy(h_ref.at[slot], 0, xs_ref, 0, sem.at[slot]).wait()
            return c
        lax.fori_loop(0, 2 * tb, wait, 0, unroll=8)

    for slot in range(2):
        @pl.when(i % 2 == slot)
        def _(slot=slot):
            @pl.when(i >= 2)
            def _():
                drain(slot)

            _tile_rows_store(h_ref.at[slot], h)

            def start(j, c):
                _row_copy(h_ref.at[slot], j, xs_ref, dest_ref[0, 0, 2 * j], sem.at[slot]).start(priority=0)
                _row_copy(h_ref.at[slot], j, xs_ref, dest_ref[0, 0, 2 * j + 1], sem.at[slot]).start(priority=1)
                return c

            lax.fori_loop(0, tb, start, 0, unroll=8)

            @pl.when(i == last)
            def _():
                drain(slot)

                @pl.when(i >= 1)
                def _():
                    drain(1 - slot)


def _dispatch(dest3, seg, x2, mod2, g, n_rows, T):
    n_tok, D = x2.shape
    tb = ROW_TB
    per_b = T // tb
    return pl.pallas_call(
        _dispatch_kernel,
        out_shape=jax.ShapeDtypeStruct((n_rows * SUBLANES, LANES), F32),
        grid=(n_tok // tb,),
        in_specs=[
            pl.BlockSpec((1, 1, 2 * tb), lambda i: (i, 0, 0), memory_space=pltpu.SMEM),
            pl.BlockSpec(memory_space=pltpu.SMEM),
            pl.BlockSpec((tb, D), lambda i: (i, 0)),
            pl.BlockSpec((1, 2, D), lambda i: (i // per_b, 0, 0)),
            pl.BlockSpec((1, D), lambda i: (0, 0)),
        ],
        out_specs=pl.BlockSpec(memory_space=pl.ANY),
        scratch_shapes=[pltpu.VMEM((2, tb * SUBLANES, LANES), F32), pltpu.VMEM((MOE_TM * SUBLANES, LANES), F32),
                        pltpu.SemaphoreType.DMA((2,)), pltpu.SemaphoreType.DMA(())],
        compiler_params=_cparams("arbitrary"),
        name="moe_dispatch",
    )(dest3, seg, x2, mod2, g)


def _expert_kernel(be_ref, nb_ref, nxt_ref, xs_ref, wg_hbm, wu_hbm, wd_hbm, ys_ref, stage_g, stage_u, stage_d,
                   wgb, wub, wdb, sem, *, layer):
    i = pl.program_id(0)
    e = be_ref[i]
    changed = (i == 0) | (e != be_ref[jnp.maximum(i - 1, 0)])

    def fetch(ex):
        return (pltpu.make_async_copy(wg_hbm.at[layer, ex], stage_g, sem.at[0]),
                pltpu.make_async_copy(wu_hbm.at[layer, ex], stage_u, sem.at[1]),
                pltpu.make_async_copy(wd_hbm.at[layer, ex], stage_d, sem.at[2]))

    @pl.when(i == 0)
    def _():
        for cp in fetch(e):
            cp.start()

    @pl.when(changed)
    def _():
        for cp in fetch(e):
            cp.wait()
        wgb[...] = stage_g[...].astype(BF16)
        wub[...] = stage_u[...].astype(BF16)
        wdb[...] = stage_d[...].astype(BF16)

        @pl.when(nxt_ref[i] >= 0)
        def _():
            for cp in fetch(nxt_ref[i]):
                cp.start()

    @pl.when(i < nb_ref[0])
    def _():
        x = _tile_rows_load(xs_ref, MOE_TM).astype(BF16)
        a = jnp.dot(x, wgb[...], preferred_element_type=F32)
        u = jnp.dot(x, wub[...], preferred_element_type=F32)
        hdn = (a * jax.nn.sigmoid(a)) * u
        _tile_rows_store(ys_ref, jnp.dot(hdn.astype(BF16), wdb[...], preferred_element_type=F32))

    @pl.when(i >= nb_ref[0])
    def _():
        ys_ref[...] = jnp.zeros_like(ys_ref)


def _experts(blk_e, n_used, next_e, xs, w_gate, w_up, w_down, layer):
    D = D_MODEL
    n_rows = xs.shape[0] // SUBLANES
    tm = MOE_TM
    FF = EXPERT_FF
    row_block = pl.BlockSpec((tm * SUBLANES, LANES), lambda i, be, nb, nx: (i, 0))
    hbm = pl.BlockSpec(memory_space=pl.ANY)
    grid_spec = pltpu.PrefetchScalarGridSpec(
        num_scalar_prefetch=3,
        grid=(n_rows // tm,),
        in_specs=[row_block, hbm, hbm, hbm],
        out_specs=row_block,
        scratch_shapes=[pltpu.VMEM((D, FF), F32), pltpu.VMEM((D, FF), F32), pltpu.VMEM((FF, D), F32),
                        pltpu.VMEM((D, FF), BF16), pltpu.VMEM((D, FF), BF16), pltpu.VMEM((FF, D), BF16),
                        pltpu.SemaphoreType.DMA((3,))],
    )
    return pl.pallas_call(
        functools.partial(_expert_kernel, layer=layer),
        out_shape=jax.ShapeDtypeStruct(xs.shape, F32),
        grid_spec=grid_spec,
        compiler_params=_cparams("arbitrary"),
        name="moe_experts",
    )(blk_e, n_used, next_e, xs, w_gate, w_up, w_down)


def _combine_kernel(*refs, final_norm, n_proj):
    dest_ref, dest_next_ref, info_ref, x_ref, gf_ref, fg_ref, ys_ref = refs[:7]
    proj_in = refs[7:7 + 3 * n_proj]
    o_ref = refs[7 + 3 * n_proj]
    proj_out = refs[8 + 3 * n_proj:8 + 4 * n_proj]
    y_ref, sem = refs[8 + 4 * n_proj:]
    tb = x_ref.shape[0]
    i = pl.program_id(0)
    last = pl.num_programs(0) - 1

    def gather(d_ref, slot):
        def start(j, c):
            _row_copy(ys_ref, d_ref[0, 0, 2 * j], y_ref.at[slot, 0], j, sem.at[slot]).start(priority=0)
            _row_copy(ys_ref, d_ref[0, 0, 2 * j + 1], y_ref.at[slot, 1], j, sem.at[slot]).start(priority=1)
            return c
        lax.fori_loop(0, tb, start, 0, unroll=8)

    @pl.when(i == 0)
    def _():
        gather(dest_ref, 0)

    for slot in range(2):
        @pl.when(i % 2 == slot)
        def _(slot=slot):
            @pl.when(i < last)
            def _():
                gather(dest_next_ref, 1 - slot)

            def wait(j, c):
                _row_copy(ys_ref, 0, y_ref.at[slot, 0], 0, sem.at[slot]).wait()
                return c

            lax.fori_loop(0, 2 * tb, wait, 0, unroll=8)
            info = info_ref[...]
            moe = (info[:, 2:3] * _tile_rows_load(y_ref.at[slot, 0], tb)
                   + info[:, 3:4] * _tile_rows_load(y_ref.at[slot, 1], tb))
            out = x_ref[...] + gf_ref[0] * moe
            if final_norm:
                ms = jnp.mean(out * out, axis=-1, keepdims=True)
                out = out * lax.rsqrt(ms + RMS_EPS) * fg_ref[...]
            o_ref[...] = out
            if n_proj:
                xn = out * lax.rsqrt(jnp.mean(out * out, axis=-1, keepdims=True) + RMS_EPS)
                for p in range(n_proj):
                    mod_ref, g_ref, w_ref = proj_in[3 * p:3 * p + 3]
                    h = (xn * g_ref[...]) * (1.0 + mod_ref[0, 1:2, :]) + mod_ref[0, 0:1, :]
                    proj_out[p][...] = jnp.dot(h.astype(BF16), w_ref[...],
                                               preferred_element_type=F32).astype(proj_out[p].dtype)


def _combine(dest3, info, x2, gf, final_g, ys, T, final_norm, projs=()):
    n_tok, D = x2.shape
    tb = ROW_TB
    per_b = T // tb
    n_steps = n_tok // tb
    const = lambda i: (0, 0)
    in_specs = [
        pl.BlockSpec((1, 1, 2 * tb), lambda i: (i, 0, 0), memory_space=pltpu.SMEM),
        pl.BlockSpec((1, 1, 2 * tb), lambda i: (jnp.minimum(i + 1, n_steps - 1), 0, 0), memory_space=pltpu.SMEM),
        pl.BlockSpec((tb, LANES), lambda i: (i, 0)),
        pl.BlockSpec((tb, D), lambda i: (i, 0)),
        pl.BlockSpec((1, 1, D), lambda i: (i // per_b, 0, 0)),
        pl.BlockSpec((1, D), const),
        pl.BlockSpec(memory_space=pl.ANY),
    ]
    args = [dest3, dest3, info, x2, gf, final_g, ys]
    for mod2, g, w in projs:
        in_specs += [pl.BlockSpec((1, 2, D), lambda i: (i // per_b, 0, 0)), pl.BlockSpec((1, D), const),
                     pl.BlockSpec(w.shape, const)]
        args += [mod2, g, w]
    return pl.pallas_call(
        functools.partial(_combine_kernel, final_norm=final_norm, n_proj=len(projs)),
        out_shape=[jax.ShapeDtypeStruct((n_tok, D), F32)]
        + [jax.ShapeDtypeStruct((n_tok, w.shape[1]), BF16) for _, _, w in projs],
        grid=(n_steps,),
        in_specs=in_specs,
        out_specs=[pl.BlockSpec((tb, D), lambda i: (i, 0))]
        + [pl.BlockSpec((tb, w.shape[1]), lambda i: (i, 0)) for _, _, w in projs],
        scratch_shapes=[pltpu.VMEM((2, 2, tb * SUBLANES, LANES), F32), pltpu.SemaphoreType.DMA((2,))],
        compiler_params=_cparams("arbitrary"),
        name="moe_combine",
    )(*args)


def _moe_rows(n_tok):
    return -(-(2 * n_tok + N_EXPERTS * MOE_TM) // MOE_TM) * MOE_TM


def _moe_layer(x, info, cnt, mod_f, gf, norm_g, w_gate, w_up, w_down, layer, final_g, final_norm, next_projs=()):
    B, T, D = x.shape
    n_tok = B * T
    x2 = x.reshape(n_tok, D)
    n_rows = _moe_rows(n_tok)
    dest, meta = _rank(info, cnt)
    seg = meta[0:2, :N_EXPERTS].astype(jnp.int32)
    pad_end = seg[0]
    n_blocks = n_rows // MOE_TM
    blk_start = jnp.arange(n_blocks, dtype=jnp.int32) * MOE_TM
    blk_e = jnp.minimum(jnp.sum(pad_end[None, :] <= blk_start[:, None], axis=1), N_EXPERTS - 1).astype(jnp.int32)
    n_used = (pad_end[N_EXPERTS - 1:] // MOE_TM).astype(jnp.int32)
    dest3 = dest[:, :2].reshape(n_tok // ROW_TB, 1, 2 * ROW_TB)
    xs = _dispatch(dest3, seg, x2, mod_f, norm_g, n_rows, T)
    seg_end = jnp.sum(blk_e[None, :] <= blk_e[:, None], axis=1)
    next_e = jnp.where(seg_end < n_blocks, blk_e[jnp.minimum(seg_end, n_blocks - 1)], -1).astype(jnp.int32)
    ys = _experts(blk_e, n_used, next_e, xs, w_gate, w_up, w_down, layer)
    outs = _combine(dest3, info, x2, gf, final_g, ys, T, final_norm, next_projs)
    return [o.reshape(B, T, -1) for o in outs]


def kernel(x, c, ada_w, ada_b, norm_mix_g, norm_ffn_g, rw_mu, rw_w_rkv, rw_w0, rw_w1, rw_w2, rw_a0, rw_a1, rw_a2,
           rw_g1, rw_g2, rw_k_k, rw_k_a, rw_r_k, rw_gn_g, rw_gn_b, rw_w_o, ada_kv_w, ada_kv_b, norm_kv_g, w_kv,
           df_w_q, df_lq1, df_lk1, df_lq2, df_lk2, df_subln_g, df_w_o, moe_w_rg, moe_w_re, moe_w_gate, moe_w_up,
           moe_w_down, final_g):
    B, T, D = x.shape
    c_pad = jnp.zeros((SUBLANES, D), F32).at[:B].set(c)
    mod = _ada(c_pad, ada_w, ada_b, 6 * D // 4)[:, :B]
    mod_kv = _ada(c_pad, ada_kv_w[None], ada_kv_b[None], D)[0, :B]
    bf = lambda w: w.astype(BF16)
    row = lambda v: v.reshape(1, -1)

    def attn_projs(l):
        sh_m, sc_m = jnp.split(mod[l], 6, axis=-1)[:2]
        projs = [(jnp.stack([sh_m, sc_m], axis=1), row(norm_mix_g[l]), bf(df_w_q[l - N_A_LAYERS]))]
        if l == N_A_LAYERS:
            sh_kv, sc_kv = jnp.split(mod_kv, 2, axis=-1)
            projs.append((jnp.stack([sh_kv, sc_kv], axis=1), row(norm_kv_g), bf(w_kv)))
        return projs

    pre = None
    for l in range(DEPTH):
        sh_m, sc_m, g_m, sh_f, sc_f, g_f = jnp.split(mod[l], 6, axis=-1)
        mod_m = jnp.stack([sh_m, sc_m], axis=1)
        mod_f = jnp.stack([sh_f, sc_f], axis=1)
        if l < N_A_LAYERS:
            i = l
            vec = jnp.stack([rw_w0[i], rw_a0[i], rw_k_k[i], rw_k_a[i]], axis=0)
            r, k, v, lw, kk, al, gate = _rwkv_proj(
                x, mod_m, row(norm_mix_g[l]), rw_mu[i], bf(rw_w_rkv[i]), bf(rw_w1[i]), bf(rw_w2[i]),
                bf(rw_a1[i]), bf(rw_a2[i]), bf(rw_g1[i]), bf(rw_g2[i]), vec)
            pvec = jnp.stack([rw_r_k[i].reshape(-1), rw_gn_g[i], rw_gn_b[i]], axis=0)
            y = _rwkv_scan(r, k, v, lw, kk, al, pvec)
            w_o = bf(rw_w_o[i])
        else:
            j = l - N_A_LAYERS
            outs = pre if pre is not None else _norm_mm(x, attn_projs(l), BF16)
            q = outs[0]
            if l == N_A_LAYERS:
                kv = outs[1]
            lambda_init = 0.8 - 0.6 * math.exp(-0.3 * l)
            lam_vecs = jnp.stack([df_lq1[j], df_lk1[j], df_lq2[j], df_lk2[j]], axis=0)
            y, gate = _diff_attn(q, kv, lam_vecs, row(df_subln_g[j]), lambda_init), None
            w_o = bf(df_w_o[j])
        w_r = jnp.concatenate([moe_w_rg[l], moe_w_re[l], jnp.zeros((D, LANES - N_GROUPS - N_EXPERTS), F32)], axis=1)
        w_r_hi = bf(w_r)
        w_cat = jnp.concatenate([w_r_hi, bf(w_r - w_r_hi.astype(F32))], axis=1)
        x, info, cnt = _proj_res_route(y, gate, x, g_m[:, None, :], w_o, mod_f, row(norm_ffn_g[l]), w_cat)
        nxt = attn_projs(l + 1) if N_A_LAYERS <= l + 1 < DEPTH else ()
        x, *pre = _moe_layer(x, info, cnt, mod_f, g_f[:, None, :], row(norm_ffn_g[l]), moe_w_gate, moe_w_up,
                             moe_w_down, l, row(final_g), final_norm=(l == DEPTH - 1), next_projs=nxt)
        pre = pre or None
    return x
```

```python
import functools
import math

import jax
import jax.numpy as jnp
from jax import lax
from jax.experimental import pallas as pl
from jax.experimental.pallas import tpu as pltpu

F32 = jnp.float32
BF16 = jnp.bfloat16

D_MODEL = 1024
DEPTH = 2
N_A_LAYERS = DEPTH // 2
RWKV_HEAD = 64
RWKV_HEADS = D_MODEL // RWKV_HEAD
RWKV_GN_EPS = 64e-5
DIFF_QK_DIM = 64
DIFF_V_DIM = 2 * DIFF_QK_DIM
DIFF_HEADS = D_MODEL // DIFF_V_DIM
SUBLN_EPS = 1e-5
N_GROUPS = 4
EXPERTS_PER_GROUP = 8
N_EXPERTS = N_GROUPS * EXPERTS_PER_GROUP
EXPERT_FF = 512
RMS_EPS = 1e-6

LANES = 128
SUBLANES = 8
VMEM_LIMIT_BYTES = 56 * 1024 * 1024

SCAN_CHUNK = 64
SCAN_BASE_BLOCK = 8
SCAN_CHUNKS_PER_STEP = 2
PAIR = 2 * RWKV_HEAD
PROJ_TM = 512
DENSE_TM = 512
ATT_TQ = 512
ATT_HB = 2
MOE_TM = 256
RANK_TB = 1024
ROW_TB = 256
assert SCAN_CHUNK == RWKV_HEAD and PAIR == LANES


def _cparams(*sem):
    return pltpu.CompilerParams(dimension_semantics=sem, vmem_limit_bytes=VMEM_LIMIT_BYTES)


def _dot(a, b):
    return jnp.dot(a.astype(BF16), b.astype(BF16), preferred_element_type=F32)


def _split(x):
    hi = x.astype(BF16)
    lo = (x - hi.astype(F32)).astype(BF16)
    return hi, lo


def _dot3(a, b):
    ah, al = _split(a)
    bh, bl = _split(b)
    d = functools.partial(jnp.dot, preferred_element_type=F32)
    return d(ah, bh) + d(ah, bl) + d(al, bh)


def _dot_hl(a, b_exact):
    ah, al = _split(a)
    d = functools.partial(jnp.dot, preferred_element_type=F32)
    return d(ah, b_exact) + d(al, b_exact)


def _rms_mod(x, g, shift, scale):
    ms = jnp.mean(x * x, axis=-1, keepdims=True)
    return (x * lax.rsqrt(ms + RMS_EPS) * g) * (1.0 + scale) + shift


def _ada_kernel(c_ref, w_ref, b_ref, o_ref):
    c = c_ref[...]
    ca = c * jax.nn.sigmoid(c)
    o_ref[...] = _dot3(ca, w_ref[...]) + b_ref[...]


def _ada(c_pad, w, b, tn):
    L, D, N = w.shape
    return pl.pallas_call(
        _ada_kernel,
        out_shape=jax.ShapeDtypeStruct((L, SUBLANES, N), F32),
        grid=(L, N // tn),
        in_specs=[
            pl.BlockSpec((SUBLANES, D), lambda l, j: (0, 0)),
            pl.BlockSpec((None, D, tn), lambda l, j: (l, 0, j)),
            pl.BlockSpec((None, 1, tn), lambda l, j: (l, 0, j)),
        ],
        out_specs=pl.BlockSpec((None, SUBLANES, tn), lambda l, j: (l, 0, j)),
        compiler_params=_cparams("arbitrary", "arbitrary"),
        name="ada_mod",
    )(c_pad, w, b.reshape(L, 1, N))


def _rwkv_proj_kernel(x_ref, xp_ref, mod_ref, g_ref, mu_ref, wrkv_ref, w1_ref, w2_ref, a1_ref, a2_ref,
                      g1_ref, g2_ref, vec_ref, r_ref, k_ref, v_ref, lw_ref, kk_ref, al_ref, gate_ref):
    i = pl.program_id(1)
    g = g_ref[...]
    shift, scale = mod_ref[0, 0:1, :], mod_ref[0, 1:2, :]
    h = _rms_mod(x_ref[0], g, shift, scale)
    hp = _rms_mod(xp_ref[0, SUBLANES - 1:SUBLANES, :], g, shift, scale)
    hp = jnp.where(i == 0, 0.0, hp)
    row = lax.broadcasted_iota(jnp.int32, h.shape, 0)
    h_prev = jnp.where(row == 0, hp, pltpu.roll(h, 1, axis=0))
    xx = h_prev - h
    mu = mu_ref[...]
    xs = [(h + xx * mu[j:j + 1, :]).astype(BF16) for j in range(6)]
    w0, a0, k_k, k_a = (vec_ref[j:j + 1, :] for j in range(4))
    d = functools.partial(jnp.dot, preferred_element_type=F32)
    r = d(xs[0], wrkv_ref[0])
    k = d(xs[1], wrkv_ref[1])
    v = d(xs[2], wrkv_ref[2])
    z = w0 + _dot(jnp.tanh(d(xs[3], w1_ref[...])), w2_ref[...])
    lw = (-math.exp(-0.5)) * jax.nn.sigmoid(z)
    a = jax.nn.sigmoid(a0 + _dot(d(xs[4], a1_ref[...]), a2_ref[...]))
    gate = _dot(jax.nn.sigmoid(d(xs[5], g1_ref[...])), g2_ref[...])
    r_ref[0] = r.astype(BF16)
    k_ref[0] = (k * (1.0 + (a - 1.0) * k_a)).astype(BF16)
    v_ref[0] = v.astype(BF16)
    lw_ref[0] = lw
    kk_ref[0] = (k * k_k).astype(BF16)
    al_ref[0] = a.astype(BF16)
    gate_ref[0] = gate.astype(BF16)


def _rwkv_proj(x, mod2, g, mu, wrkv, w1, w2, a1, a2, g1, g2, vec):
    B, T, D = x.shape
    tm = PROJ_TM
    const2 = lambda b, i: (0, 0)
    const3 = lambda b, i: (0, 0, 0)
    act = pl.BlockSpec((1, tm, D), lambda b, i: (b, i, 0))
    n_sub = tm // SUBLANES
    return pl.pallas_call(
        _rwkv_proj_kernel,
        out_shape=[jax.ShapeDtypeStruct((B, T, D), F32 if n == 3 else BF16) for n in range(7)],
        grid=(B, T // tm),
        in_specs=[
            act,
            pl.BlockSpec((1, SUBLANES, D), lambda b, i: (b, jnp.maximum(i * n_sub - 1, 0), 0)),
            pl.BlockSpec((1, 2, D), lambda b, i: (b, 0, 0)),
            pl.BlockSpec((1, D), const2),
            pl.BlockSpec((6, D), const2),
            pl.BlockSpec((3, D, D), const3),
            pl.BlockSpec(w1.shape, const2), pl.BlockSpec(w2.shape, const2),
            pl.BlockSpec(a1.shape, const2), pl.BlockSpec(a2.shape, const2),
            pl.BlockSpec(g1.shape, const2), pl.BlockSpec(g2.shape, const2),
            pl.BlockSpec((4, D), const2),
        ],
        out_specs=[act] * 7,
        compiler_params=_cparams("arbitrary", "arbitrary"),
        name="rwkv_proj",
    )(x, x, mod2, g, mu, wrkv, w1, w2, a1, a2, g1, g2, vec)


def _rwkv_scan_kernel(r_ref, k_ref, v_ref, lw_ref, kk_ref, al_ref, pv_ref, y_ref, h_ref):
    C = SCAN_CHUNK
    P2 = 2 * C

    @pl.when(pl.program_id(1) == 0)
    def _():
        h_ref[...] = jnp.zeros_like(h_ref)

    lane = lax.broadcasted_iota(jnp.int32, (1, PAIR), 1)
    m_left = (lane < RWKV_HEAD).astype(F32)
    m_right = 1.0 - m_left
    ri = lax.broadcasted_iota(jnp.int32, (P2, P2), 0)
    ci = lax.broadcasted_iota(jnp.int32, (P2, P2), 1)
    same = (ri >= C) == (ci >= C)
    strict = same & (ri > ci)
    incl = same & (ri >= ci)
    eye = ri == ci
    tri = (lax.broadcasted_iota(jnp.int32, (C, C), 0) >= lax.broadcasted_iota(jnp.int32, (C, C), 1)).astype(BF16)

    def stack(x):
        return jnp.concatenate([x * m_left, x * m_right], axis=0)

    def head_sums(x):
        s_left = jnp.sum(x * m_left, axis=-1, keepdims=True)
        s_right = jnp.sum(x * m_right, axis=-1, keepdims=True)
        return jnp.where(lane < RWKV_HEAD, s_left, s_right)

    inv_n = 1.0 / RWKV_HEAD
    dd = functools.partial(jnp.dot, preferred_element_type=F32)
    n_pairs = RWKV_HEADS // 2
    units = [(ch, p) for ch in range(SCAN_CHUNKS_PER_STEP) for p in range(n_pairs)]
    idx = [(slice(ch * C, (ch + 1) * C), slice(p * PAIR, (p + 1) * PAIR)) for ch, p in units]
    U = range(len(units))
    ld = lambda ref, rs, sl: ref[0, rs, sl].astype(F32)
    kkr = [ld(kk_ref, rs, sl) for rs, sl in idx]
    ss = [head_sums(x * x) for x in kkr]
    lws = [lw_ref[0, rs, sl] for rs, sl in idx]
    Ls = []
    for lw in lws:
        l_hi, l_lo = _split(lw)
        cs = dd(tri, jnp.concatenate([l_hi, l_lo], axis=1))
        Ls.append(cs[:, :PAIR] + cs[:, PAIR:])
    lhs_g, rhs_g, bk_hat, vs, at32, rt32, dec_end = [], [], [], [], [], [], []
    for u in U:
        rs, sl = idx[u]
        L, lw = Ls[u], lws[u]
        kk = kkr[u] * lax.rsqrt(jnp.maximum(ss[u], 1e-24))
        b_vec = kk * ld(al_ref, rs, sl)
        k = ld(k_ref, rs, sl)
        LC = L[C - 1:C, :]
        e_neg = jnp.exp(-L)
        e_end = jnp.exp(LC - L)
        At = stack(-kk * jnp.exp(L - lw))
        Rt = stack(ld(r_ref, rs, sl) * jnp.exp(L))
        at32.append(At)
        rt32.append(Rt)
        lhs_g.append(jnp.concatenate([At, Rt], axis=0).astype(BF16))
        rhs_g.append(jnp.concatenate([stack(b_vec * e_neg), stack(k * e_neg)], axis=0).astype(BF16))
        bk_hat.append(jnp.concatenate([stack(b_vec * e_end), stack(k * e_end)], axis=0))
        vs.append(stack(ld(v_ref, rs, sl)).astype(BF16))
        dec_end.append(jnp.exp(LC))
    G = [lax.dot_general(lhs_g[u], rhs_g[u], (((1,), (1,)), ((), ())), preferred_element_type=F32) for u in U]
    A_ak = [jnp.where(strict, G[u][:P2, P2:], 0.0).astype(BF16) for u in U]
    A_r = [jnp.concatenate([jnp.where(incl, G[u][P2:, :P2], 0.0), jnp.where(incl, G[u][P2:, P2:], 0.0)],
                           axis=1).astype(BF16) for u in U]
    W = [dd(A_ak[u], vs[u]) for u in U]
    bsz = lambda b: (ri >> int(math.log2(b))) == (ci >> int(math.log2(b)))
    b8 = bsz(SCAN_BASE_BLOCK)
    D1 = [jnp.where(strict & b8, G[u][:P2, :P2], 0.0).astype(BF16) for u in U]
    D2 = [dd(D1[u], D1[u]).astype(BF16) for u in U]
    D4 = [dd(D2[u], D2[u]).astype(BF16) for u in U]
    eye_f = eye.astype(F32)
    P1 = [eye_f + D1[u].astype(F32) + D2[u].astype(F32) + dd(D1[u], D2[u]) for u in U]
    Tm = [P1[u] + dd(P1[u].astype(BF16), D4[u]) for u in U]
    blk = SCAN_BASE_BLOCK
    while blk < C:
        off = strict & bsz(2 * blk) & ~bsz(blk)
        Mo = [jnp.where(off, G[u][:P2, :P2], 0.0).astype(BF16) for u in U]
        Tb = [Tm[u].astype(BF16) for u in U]
        TM = [dd(Tb[u], Mo[u]).astype(BF16) for u in U]
        Tm = [Tm[u] + dd(TM[u], Tb[u]) for u in U]
        blk *= 2
    Z = [dd(Tm[u].astype(BF16), jnp.concatenate([at32[u], W[u]], axis=1).astype(BF16)) for u in U]
    rhs = [jnp.concatenate([Z[u].astype(BF16), jnp.concatenate([jnp.zeros_like(vs[u]), vs[u]], axis=1)], axis=0)
           for u in U]
    o6 = [dd(A_r[u], rhs[u]) for u in U]
    o7 = [dd(bk_hat[u].T.astype(BF16), rhs[u]) for u in U]
    H = [h_ref[p] for p in range(n_pairs)]
    Y = [None] * len(units)
    for u in U:
        p = units[u][1]
        Hb = H[p].astype(BF16)
        Y[u] = dd((rt32[u] + o6[u][:, :PAIR]).astype(BF16), Hb) + o6[u][:, PAIR:]
        Mbd = o7[u][:, :PAIR] + jnp.where(eye, dec_end[u], 0.0)
        H[p] = dd(Mbd.astype(BF16), Hb) + o7[u][:, PAIR:]
    for p in range(n_pairs):
        h_ref[p] = H[p]
    ys = [Y[u][:C] + Y[u][C:] for u in U]
    rk = [ld(r_ref, rs, sl) * ld(k_ref, rs, sl) * pv_ref[0:1, sl] for rs, sl in idx]
    st1 = [head_sums(jnp.concatenate([ys[u], rk[u]], axis=0)) for u in U]
    yc = [ys[u] - st1[u][:C] * inv_n for u in U]
    var = [head_sums(yc[u] * yc[u]) * inv_n for u in U]
    for u in U:
        rs, sl = idx[u]
        bonus = st1[u][C:] * ld(v_ref, rs, sl)
        y_ref[0, rs, sl] = (yc[u] * lax.rsqrt(var[u] + RWKV_GN_EPS) * pv_ref[1:2, sl] + pv_ref[2:3, sl]
                            + bonus).astype(y_ref.dtype)


def _rwkv_scan(r, k, v, lw, kk, al, pvec):
    B, T, D = r.shape
    rows = SCAN_CHUNK * SCAN_CHUNKS_PER_STEP
    act = pl.BlockSpec((1, rows, D), lambda b, c: (b, c, 0))
    return pl.pallas_call(
        _rwkv_scan_kernel,
        out_shape=jax.ShapeDtypeStruct((B, T, D), BF16),
        grid=(B, T // rows),
        in_specs=[act] * 6 + [pl.BlockSpec((3, D), lambda b, c: (0, 0))],
        out_specs=act,
        scratch_shapes=[pltpu.VMEM((RWKV_HEADS // 2, PAIR, PAIR), F32)],
        compiler_params=_cparams("arbitrary", "arbitrary"),
        name="rwkv_scan",
    )(r, k, v, lw, kk, al, pvec)


def _proj_res_route_kernel(*refs, has_gate):
    if has_gate:
        y_ref, g_ref, x_ref, gm_ref, w_ref, modf_ref, gf_ref, wr_ref, o_ref, info_ref, cnt_ref = refs
        y = y_ref[0].astype(F32) * g_ref[0].astype(F32)
    else:
        y_ref, x_ref, gm_ref, w_ref, modf_ref, gf_ref, wr_ref, o_ref, info_ref, cnt_ref = refs
        y = y_ref[0]
    x_new =x_ref[0] + gm_ref[0] * jnp.dot(y.astype(BF16), w_ref[...], preferred_element_type=F32)
    o_ref[0] = x_new
    h = _rms_mod(x_new, gf_ref[...], modf_ref[0, 0:1, :], modf_ref[0, 1:2, :])
    first = (pl.program_id(0) == 0) & (pl.program_id(1) == 0)
    _route(h, wr_ref[...], info_ref, cnt_ref, first)


def _proj_res_route(y, g, x, gm, w, mod_f, norm_f, w_cat):
    B, T, D = x.shape
    tm = DENSE_TM
    per_b = T // tm
    act =pl.BlockSpec((1, tm, D), lambda b, i: (b, i, 0))
    const = lambda b, i: (0, 0)
    flat = lambda b, i: (b * per_b + i, 0)
    ins = [y] + ([g] if g is not None else []) + [x, gm, w, mod_f, norm_f, w_cat]
    specs = [act] * (len(ins) - 5) + [pl.BlockSpec((1, 1, D), lambda b, i: (b, 0, 0)), pl.BlockSpec((D, D), const),
                                     pl.BlockSpec((1, 2, D), lambda b, i: (b, 0, 0)), pl.BlockSpec((1, D), const),
                                     pl.BlockSpec((D, 2 * LANES), const)]
    return pl.pallas_call(
        functools.partial(_proj_res_route_kernel, has_gate=g is not None),
        out_shape=[jax.ShapeDtypeStruct((B, T, D), F32), jax.ShapeDtypeStruct((B * T, LANES), F32),
                   jax.ShapeDtypeStruct((SUBLANES, LANES), F32)],
        grid=(B, per_b),
        in_specs=specs,
        out_specs=[act, pl.BlockSpec((tm, LANES), flat), pl.BlockSpec((SUBLANES, LANES), const)],
        compiler_params=_cparams("arbitrary", "arbitrary"),
        name="proj_res_route",
    )(*ins)


def _norm_mm_kernel(*refs, n_proj):
    x_ref = refs[0]
    ins, outs = refs[1:1 + 3 * n_proj], refs[1 + 3 * n_proj:]
    x = x_ref[0]
    xn = x * lax.rsqrt(jnp.mean(x * x, axis=-1, keepdims=True) + RMS_EPS)
    for p in range(n_proj):
        mod_ref, g_ref, w_ref = ins[3 * p:3 * p + 3]
        h = (xn * g_ref[...]) * (1.0 + mod_ref[0, 1:2, :]) + mod_ref[0, 0:1, :]
        outs[p][0] = jnp.dot(h.astype(BF16), w_ref[...], preferred_element_type=F32).astype(outs[p].dtype)


def _norm_mm(x, projs, out_dtype):
    B, T, D = x.shape
    tm = DENSE_TM
    in_specs = [pl.BlockSpec((1, tm, D), lambda b, i: (b, i, 0))]
    args = [x]
    for mod2, g, w in projs:
        in_specs += [pl.BlockSpec((1, 2, D), lambda b, i: (b, 0, 0)), pl.BlockSpec((1, D), lambda b, i: (0, 0)),
                     pl.BlockSpec(w.shape, lambda b, i: (0, 0))]
        args += [mod2, g, w]
    return pl.pallas_call(
        functools.partial(_norm_mm_kernel, n_proj=len(projs)),
        out_shape=[jax.ShapeDtypeStruct((B, T, w.shape[1]), out_dtype) for _, _, w in projs],
        grid=(B, T // tm),
        in_specs=in_specs,
        out_specs=[pl.BlockSpec((1, tm, w.shape[1]), lambda b, i: (b, i, 0)) for _, _, w in projs],
        compiler_params=_cparams("arbitrary", "arbitrary"),
        name="norm_mm",
    )(*args)


def _diff_attn_kernel(q_ref, k_ref, v_ref, lam_ref, sg_ref, o_ref, m_ref, acc_ref, s_ref, *, lambda_init):
    tq, HB, dv = ATT_TQ, ATT_HB, DIFF_V_DIM
    qi = pl.program_id(2)
    heads = range(HB)
    hs = [slice(h * dv, (h + 1) * dv) for h in heads]
    lane = lax.broadcasted_iota(jnp.int32, (1, dv), 1)
    m_left = (lane < DIFF_QK_DIM).astype(F32)
    qs = []
    for h in heads:
        q = q_ref[0, :, hs[h]].astype(F32) * (DIFF_QK_DIM ** -0.5 * math.log2(math.e))
        qs.append(jnp.concatenate([q * m_left, q * (1.0 - m_left)], axis=0).astype(BF16))
    ones_col = jnp.ones((tq, dv), BF16)
    causal = (lax.broadcasted_iota(jnp.int32, (2 * tq, tq), 1)
              <= lax.broadcasted_iota(jnp.int32, (2 * tq, tq), 0) % tq)

    def key_rows(j):
        return pl.ds(pl.multiple_of(j * tq, tq), tq)

    def scores_into(slot, j):
        for h in heads:
            s_ref[slot, h] = lax.dot_general(qs[h], k_ref[0, key_rows(j), hs[h]], (((1,), (1,)), ((), ())),
                                             preferred_element_type=F32)

    def block(slot, j, first, prefetch):
        if prefetch is not None:
            scores_into(1 - slot, prefetch)
        for h in heads:
            s = s_ref[slot, h]
            if first:
                s = jnp.where(causal, s, -jnp.inf)
                m_new = jnp.broadcast_to(jnp.max(s, axis=-1, keepdims=True), (2 * tq, LANES))
            else:
                m_old = m_ref[h]
                m_new = jnp.maximum(m_old, jnp.max(s, axis=-1, keepdims=True))
            m_ref[h] = m_new
            p = jnp.exp2(s - jnp.concatenate([m_new] * (tq // LANES), axis=1)).astype(BF16)
            pv = jnp.dot(p, jnp.concatenate([v_ref[0, key_rows(j), hs[h]], ones_col], axis=1),
                         preferred_element_type=F32)
            if first:
                acc_ref[h] = pv
            else:
                alpha = jnp.exp2(m_old - m_new)
                acc_ref[h] = acc_ref[h] * jnp.concatenate([alpha, alpha], axis=1) + pv

    scores_into(0, qi)
    block(0, qi, True, 0)

    def pair(u, c):
        last = jnp.maximum(qi - 1, 0)
        block(1, 2 * u, False, jnp.minimum(2 * u + 1, last))
        block(0, 2 * u + 1, False, jnp.minimum(2 * u + 2, last))
        return c

    lax.fori_loop(0, qi // 2, pair, 0)

    @pl.when(qi % 2 == 1)
    def _():
        block(1, qi - 1, False, None)

    lv = lam_ref[...]
    lam = (jnp.exp(jnp.sum(lv[0:1] * lv[1:2], axis=-1, keepdims=True))
           - jnp.exp(jnp.sum(lv[2:3] * lv[3:4], axis=-1, keepdims=True)) + lambda_init)
    for h in heads:
        acc = acc_ref[h]
        o = acc[:, :dv] / acc[:, dv:]
        o = o[:tq] - lam * o[tq:]
        ms = jnp.mean(o * o, axis=-1, keepdims=True)
        o_ref[0, :, hs[h]] = (o * lax.rsqrt(ms + SUBLN_EPS) * sg_ref[...] * (1.0 - lambda_init)).astype(o_ref.dtype)


def _diff_attn(q, kv, lam_vecs, subln_g, lambda_init):
    B, T, D = q.shape
    tq, HB = ATT_TQ, ATT_HB
    n_hb = DIFF_HEADS // HB
    w = HB * DIFF_V_DIM
    return pl.pallas_call(
        functools.partial(_diff_attn_kernel, lambda_init=lambda_init),
        out_shape=jax.ShapeDtypeStruct((B, T, D), BF16),
        grid=(B, n_hb, T // tq),
        in_specs=[
            pl.BlockSpec((1, tq, w), lambda b, h, i: (b, i, h)),
            pl.BlockSpec((1, T, w), lambda b, h, i: (b, 0, h)),
            pl.BlockSpec((1, T, w), lambda b, h, i: (b, 0, n_hb + h)),
            pl.BlockSpec((4, DIFF_QK_DIM), lambda b, h, i: (0, 0)),
            pl.BlockSpec((1, DIFF_V_DIM), lambda b, h, i: (0, 0)),
        ],
        out_specs=pl.BlockSpec((1, tq, w), lambda b, h, i: (b, i, h)),
        scratch_shapes=[pltpu.VMEM((HB, 2 * tq, LANES), F32), pltpu.VMEM((HB, 2 * tq, 2 * DIFF_V_DIM), F32),
                        pltpu.VMEM((2, HB, 2 * tq, tq), F32)],
        compiler_params=_cparams("arbitrary", "arbitrary", "arbitrary"),
        name="diff_attn",
    )(q, kv, kv, lam_vecs, subln_g)


def _route(h, w_cat, info_ref, cnt_ref, first_step):
    h_hi, h_lo = _split(h)
    two = jnp.dot(h_hi, w_cat, preferred_element_type=F32)
    logit = (two[:, :LANES] + two[:, LANES:]
             + jnp.dot(h_lo, w_cat[:, :LANES], preferred_element_type=F32))
    lane_i = lax.broadcasted_iota(jnp.int32, logit.shape, 1)
    lane = lane_i.astype(F32)
    neg = -jnp.inf
    big = float(LANES)
    is_grp = lane_i < N_GROUPS
    gl = jnp.where(is_grp, logit, neg)
    gmax = jnp.max(gl, axis=-1, keepdims=True)
    gidx = jnp.min(jnp.where(gl == gmax, lane, big), axis=-1, keepdims=True)
    grp_gate = 1.0 / jnp.sum(jnp.where(is_grp, jnp.exp(logit - gmax), 0.0), axis=-1, keepdims=True)
    lo = N_GROUPS + gidx * EXPERTS_PER_GROUP
    in_grp = (lane >= lo) & (lane < lo + EXPERTS_PER_GROUP)
    el = jnp.where(in_grp, logit, neg)
    t1 = jnp.max(el, axis=-1, keepdims=True)
    i1 = jnp.min(jnp.where(el == t1, lane, big), axis=-1, keepdims=True)
    el2 = jnp.where(lane == i1, neg, el)
    t2 = jnp.max(el2, axis=-1, keepdims=True)
    i2 = jnp.min(jnp.where(el2 == t2, lane, big), axis=-1, keepdims=True)
    e21 = jnp.exp(t2 - t1)
    p1 = 1.0 / (1.0 + e21)
    w1 = grp_gate * p1
    w2 = grp_gate * (e21 * p1)
    e1 = i1 - N_GROUPS
    e2 = i2 - N_GROUPS
    info_ref[...] = jnp.where(lane_i == 0, e1, jnp.where(lane_i == 1, e2, jnp.where(lane_i == 2, w1, jnp.where(lane_i == 3, w2, 0.0))))

    @pl.when(first_step)
    def _():
        cnt_ref[...] = jnp.zeros_like(cnt_ref)

    picked = ((lane == e1) | (lane == e2)).astype(F32)
    cnt_ref[...] += jnp.broadcast_to(jnp.sum(picked, axis=0, keepdims=True), cnt_ref.shape)


def _rank_kernel(info_ref, cnt_ref, dest_ref, meta_ref, start_ref):
    i = pl.program_id(0)
    tb = info_ref.shape[0]
    lane = lax.broadcasted_iota(jnp.int32, (tb, LANES), 1)
    info = info_ref[...]
    e0 = info[:, 0:1].astype(jnp.int32)
    e1 = info[:, 1:2].astype(jnp.int32)
    o0 = (lane == e0).astype(F32)
    o1 = (lane == e1).astype(F32)
    both = o0 + o1

    @pl.when(i == 0)
    def _():
        cnt = cnt_ref[0:1, :]
        padded = jnp.floor((cnt + (MOE_TM - 1)) * (1.0 / MOE_TM)) * MOE_TM
        r = lax.broadcasted_iota(jnp.int32, (LANES, LANES), 0)
        c = lax.broadcasted_iota(jnp.int32, (LANES, LANES), 1)
        upper_strict = (r < c).astype(BF16)
        start = _dot_hl(jnp.broadcast_to(padded, (SUBLANES, LANES)), upper_strict)[0:1]
        start_ref[...] = start
        row = lax.broadcasted_iota(jnp.int32, (SUBLANES, LANES), 0)
        meta_ref[...] = jnp.where(row == 0, start + padded, start + cnt)

    r = lax.broadcasted_iota(jnp.int32, (tb, tb), 0)
    c = lax.broadcasted_iota(jnp.int32, (tb, tb), 1)
    lower_strict = (r > c).astype(BF16)
    before = jnp.dot(lower_strict, both.astype(BF16), preferred_element_type=F32) + start_ref[...]
    d0 = jnp.sum(o0 * before, axis=-1, keepdims=True)
    d1 = jnp.sum(o1 * before, axis=-1, keepdims=True)
    dest_ref[...] = jnp.where(lane == 0, d0, jnp.where(lane == 1, d1, 0.0)).astype(jnp.int32)
    start_ref[...] += jnp.sum(both, axis=0, keepdims=True)


def _rank(info, cnt):
    n_tok = info.shape[0]
    tb = RANK_TB
    n_blk = n_tok // tb
    return pl.pallas_call(
        _rank_kernel,
        out_shape=[jax.ShapeDtypeStruct((n_tok, LANES), jnp.int32), jax.ShapeDtypeStruct((SUBLANES, LANES), F32)],
        grid=(n_blk,),
        in_specs=[pl.BlockSpec((tb, LANES), lambda i: (i, 0)), pl.BlockSpec((SUBLANES, LANES), lambda i: (0, 0))],
        out_specs=[pl.BlockSpec((tb, LANES), lambda i: (i, 0)), pl.BlockSpec((SUBLANES, LANES), lambda i: (0, 0))],
        scratch_shapes=[pltpu.VMEM((1, LANES), F32)],
        compiler_params=_cparams("arbitrary"),
        name="moe_rank",
    )(info, cnt)


assert D_MODEL == SUBLANES * LANES


def _tile_rows_store(ref, x):
    n = x.shape[0]
    for s in range(SUBLANES):
        ref[pl.ds(s, n, stride=SUBLANES), :] = x[:, s * LANES:(s + 1) * LANES]


def _tile_rows_load(ref, n):
    return jnp.concatenate([ref[pl.ds(s, n, stride=SUBLANES), :] for s in range(SUBLANES)], axis=1)


def _row_copy(src_ref, s, dst_ref, d, sem):
    rows = lambda r: pl.ds(pl.multiple_of(r * SUBLANES, SUBLANES), SUBLANES)
    return pltpu.make_async_copy(src_ref.at[rows(s)], dst_ref.at[rows(d)], sem)


def _zero_padding_rows(seg_ref, xs_ref, zero_ref, zsem, *, start):
    def copy(first_row, n_rows):
        cp = pltpu.make_async_copy(zero_ref.at[pl.ds(0, n_rows * SUBLANES)],
                                   xs_ref.at[pl.ds(pl.multiple_of(first_row * SUBLANES, SUBLANES), n_rows * SUBLANES)],
                                   zsem)
        cp.start() if start else cp.wait()

    for e in range(N_EXPERTS):
        lo, n = seg_ref[1, e], seg_ref[0, e] - seg_ref[1, e]
        bit = MOE_TM // 2
        while bit >= 1:
            @pl.when((n & bit) != 0)
            def _(lo=lo, n=n, bit=bit):
                copy(lo + (n & ~(2 * bit - 1)), bit)
            bit //= 2
    n_total = xs_ref.shape[0] // SUBLANES

    def tail(j, c):
        copy(seg_ref[0, N_EXPERTS - 1] + j * MOE_TM, MOE_TM)
        return c

    lax.fori_loop(0, (n_total - seg_ref[0, N_EXPERTS - 1]) // MOE_TM, tail, 0)


def _dispatch_kernel(dest_ref, seg_ref, x_ref, mod_ref, g_ref, xs_ref, h_ref, zero_ref, sem, zsem):
    tb = x_ref.shape[0]
    i = pl.program_id(0)
    last = pl.num_programs(0) - 1
    h = _rms_mod(x_ref[...], g_ref[...], mod_ref[0, 0:1, :], mod_ref[0, 1:2, :])

    @pl.when(i == 0)
    def _():
        zero_ref[...] = jnp.zeros_like(zero_ref)
        _zero_padding_rows(seg_ref, xs_ref, zero_ref, zsem, start=True)

    @pl.when(i == last)
    def _():
        _zero_padding_rows(seg_ref, xs_ref, zero_ref, zsem, start=False)

    def drain(slot):
        def wait(j, c):
            _row_copy(h_ref.at[slot], 0, xs_ref, 0, sem.at[slot]).wait()
            return c
        lax.fori_loop(0, 2 * tb, wait, 0, unroll=8)

    for slot in range(2):
        @pl.when(i % 2 == slot)
        def _(slot=slot):
            @pl.when(i >= 2)
            def _():
                drain(slot)

            _tile_rows_store(h_ref.at[slot], h)

            def start(j, c):
                _row_copy(h_ref.at[slot], j, xs_ref, dest_ref[0, 0, 2 * j], sem.at[slot]).start(priority=0)
                _row_copy(h_ref.at[slot], j, xs_ref, dest_ref[0, 0, 2 * j + 1], sem.at[slot]).start(priority=1)
                return c

            lax.fori_loop(0, tb, start, 0, unroll=8)

            @pl.when(i == last)
            def _():
                drain(slot)

                @pl.when(i >= 1)
                def _():
                    drain(1 - slot)


def _dispatch(dest3, seg, x2, mod2, g, n_rows, T):
    n_tok, D = x2.shape
    tb = ROW_TB
    per_b = T // tb
    return pl.pallas_call(
        _dispatch_kernel,
        out_shape=jax.ShapeDtypeStruct((n_rows * SUBLANES, LANES), F32),
        grid=(n_tok // tb,),
        in_specs=[
            pl.BlockSpec((1, 1, 2 * tb), lambda i: (i, 0, 0), memory_space=pltpu.SMEM),
            pl.BlockSpec(memory_space=pltpu.SMEM),
            pl.BlockSpec((tb, D), lambda i: (i, 0)),
            pl.BlockSpec((1, 2, D), lambda i: (i // per_b, 0, 0)),
            pl.BlockSpec((1, D), lambda i: (0, 0)),
        ],
        out_specs=pl.BlockSpec(memory_space=pl.ANY),
        scratch_shapes=[pltpu.VMEM((2, tb * SUBLANES, LANES), F32), pltpu.VMEM((MOE_TM * SUBLANES, LANES), F32),
                        pltpu.SemaphoreType.DMA((2,)), pltpu.SemaphoreType.DMA(())],
        compiler_params=_cparams("arbitrary"),
        name="moe_dispatch",
    )(dest3, seg, x2, mod2, g)


def _expert_kernel(be_ref, nb_ref, nxt_ref, xs_ref, wg_hbm, wu_hbm, wd_hbm, ys_ref, stage_g, stage_u, stage_d,
                   wgb, wub, wdb, sem, *, layer):
    i = pl.program_id(0)
    e = be_ref[i]
    changed = (i == 0) | (e != be_ref[jnp.maximum(i - 1, 0)])

    def fetch(ex):
        return (pltpu.make_async_copy(wg_hbm.at[layer, ex], stage_g, sem.at[0]),
                pltpu.make_async_copy(wu_hbm.at[layer, ex], stage_u, sem.at[1]),
                pltpu.make_async_copy(wd_hbm.at[layer, ex], stage_d, sem.at[2]))

    @pl.when(i == 0)
    def _():
        for cp in fetch(e):
            cp.start()

    @pl.when(changed)
    def _():
        for cp in fetch(e):
            cp.wait()
        wgb[...] = stage_g[...].astype(BF16)
        wub[...] = stage_u[...].astype(BF16)
        wdb[...] = stage_d[...].astype(BF16)

        @pl.when(nxt_ref[i] >= 0)
        def _():
            for cp in fetch(nxt_ref[i]):
                cp.start()

    @pl.when(i < nb_ref[0])
    def _():
        x = _tile_rows_load(xs_ref, MOE_TM).astype(BF16)
        a = jnp.dot(x, wgb[...], preferred_element_type=F32)
        u = jnp.dot(x, wub[...], preferred_element_type=F32)
        hdn = (a * jax.nn.sigmoid(a)) * u
        _tile_rows_store(ys_ref, jnp.dot(hdn.astype(BF16), wdb[...], preferred_element_type=F32))

    @pl.when(i >= nb_ref[0])
    def _():
        ys_ref[...] = jnp.zeros_like(ys_ref)


def _experts(blk_e, n_used, next_e, xs, w_gate, w_up, w_down, layer):
    D = D_MODEL
    n_rows = xs.shape[0] // SUBLANES
    tm = MOE_TM
    FF = EXPERT_FF
    row_block = pl.BlockSpec((tm * SUBLANES, LANES), lambda i, be, nb, nx: (i, 0))
    hbm = pl.BlockSpec(memory_space=pl.ANY)
    grid_spec = pltpu.PrefetchScalarGridSpec(
        num_scalar_prefetch=3,
        grid=(n_rows // tm,),
        in_specs=[row_block, hbm, hbm, hbm],
        out_specs=row_block,
        scratch_shapes=[pltpu.VMEM((D, FF), F32), pltpu.VMEM((D, FF), F32), pltpu.VMEM((FF, D), F32),
                        pltpu.VMEM((D, FF), BF16), pltpu.VMEM((D, FF), BF16), pltpu.VMEM((FF, D), BF16),
                        pltpu.SemaphoreType.DMA((3,))],
    )
    return pl.pallas_call(
        functools.partial(_expert_kernel, layer=layer),
        out_shape=jax.ShapeDtypeStruct(xs.shape, F32),
        grid_spec=grid_spec,
        compiler_params=_cparams("arbitrary"),
        name="moe_experts",
    )(blk_e, n_used, next_e, xs, w_gate, w_up, w_down)


def _combine_kernel(dest_ref, dest_next_ref, info_ref, x_ref, gf_ref, fg_ref, ys_ref, o_ref, y_ref, sem, *,
                    final_norm):
    tb = x_ref.shape[0]
    i = pl.program_id(0)
    last = pl.num_programs(0) - 1

    def gather(d_ref, slot):
        def start(j, c):
            _row_copy(ys_ref, d_ref[0, 0, 2 * j], y_ref.at[slot, 0], j, sem.at[slot]).start(priority=0)
            _row_copy(ys_ref, d_ref[0, 0, 2 * j + 1], y_ref.at[slot, 1], j, sem.at[slot]).start(priority=1)
            return c
        lax.fori_loop(0, tb, start, 0, unroll=8)

    @pl.when(i == 0)
    def _():
        gather(dest_ref, 0)

    for slot in range(2):
        @pl.when(i % 2 == slot)
        def _(slot=slot):
            @pl.when(i < last)
            def _():
                gather(dest_next_ref, 1 - slot)

            def wait(j, c):
                _row_copy(ys_ref, 0, y_ref.at[slot, 0], 0, sem.at[slot]).wait()
                return c

            lax.fori_loop(0, 2 * tb, wait, 0, unroll=8)
            info = info_ref[...]
            moe = (info[:, 2:3] * _tile_rows_load(y_ref.at[slot, 0], tb)
                   + info[:, 3:4] * _tile_rows_load(y_ref.at[slot, 1], tb))
            out = x_ref[...] + gf_ref[0] * moe
            if final_norm:
                ms = jnp.mean(out * out, axis=-1, keepdims=True)
                out = out * lax.rsqrt(ms + RMS_EPS) * fg_ref[...]
            o_ref[...] = out


def _combine(dest3, info, x2, gf, final_g, ys, T, final_norm):
    n_tok, D = x2.shape
    tb = ROW_TB
    per_b = T // tb
    n_steps = n_tok // tb
    return pl.pallas_call(
        functools.partial(_combine_kernel, final_norm=final_norm),
        out_shape=jax.ShapeDtypeStruct((n_tok, D), F32),
        grid=(n_steps,),
        in_specs=[
            pl.BlockSpec((1, 1, 2 * tb), lambda i: (i, 0, 0), memory_space=pltpu.SMEM),
            pl.BlockSpec((1, 1, 2 * tb), lambda i: (jnp.minimum(i + 1, n_steps - 1), 0, 0), memory_space=pltpu.SMEM),
            pl.BlockSpec((tb, LANES), lambda i: (i, 0)),
            pl.BlockSpec((tb, D), lambda i: (i, 0)),
            pl.BlockSpec((1, 1, D), lambda i: (i // per_b, 0, 0)),
            pl.BlockSpec((1, D), lambda i: (0, 0)),
            pl.BlockSpec(memory_space=pl.ANY),
        ],
        out_specs=pl.BlockSpec((tb, D), lambda i: (i, 0)),
        scratch_shapes=[pltpu.VMEM((2, 2, tb * SUBLANES, LANES), F32), pltpu.SemaphoreType.DMA((2,))],
        compiler_params=_cparams("arbitrary"),
        name="moe_combine",
    )(dest3, dest3, info, x2, gf, final_g, ys)


def _moe_rows(n_tok):
    return -(-(2 * n_tok + N_EXPERTS * MOE_TM) // MOE_TM) * MOE_TM


def _moe_layer(x, info, cnt, mod_f, gf, norm_g, w_gate, w_up, w_down, layer, final_g, final_norm):
    B, T, D = x.shape
    n_tok = B * T
    x2 = x.reshape(n_tok, D)
    n_rows = _moe_rows(n_tok)
    dest, meta = _rank(info, cnt)
    seg = meta[0:2, :N_EXPERTS].astype(jnp.int32)
    pad_end = seg[0]
    n_blocks = n_rows // MOE_TM
    blk_start = jnp.arange(n_blocks, dtype=jnp.int32) * MOE_TM
    blk_e = jnp.minimum(jnp.sum(pad_end[None, :] <= blk_start[:, None], axis=1), N_EXPERTS - 1).astype(jnp.int32)
    n_used = (pad_end[N_EXPERTS - 1:] // MOE_TM).astype(jnp.int32)
    dest3 = dest[:, :2].reshape(n_tok // ROW_TB, 1, 2 * ROW_TB)
    xs = _dispatch(dest3, seg, x2, mod_f, norm_g, n_rows, T)
    seg_end = jnp.sum(blk_e[None, :] <= blk_e[:, None], axis=1)
    next_e = jnp.where(seg_end < n_blocks, blk_e[jnp.minimum(seg_end, n_blocks - 1)], -1).astype(jnp.int32)
    ys = _experts(blk_e, n_used, next_e, xs, w_gate, w_up, w_down, layer)
    out = _combine(dest3, info, x2, gf, final_g, ys, T, final_norm)
    return out.reshape(B, T, D)


def kernel(x, c, ada_w, ada_b, norm_mix_g, norm_ffn_g, rw_mu, rw_w_rkv, rw_w0, rw_w1, rw_w2, rw_a0, rw_a1, rw_a2,
           rw_g1, rw_g2, rw_k_k, rw_k_a, rw_r_k, rw_gn_g, rw_gn_b, rw_w_o, ada_kv_w, ada_kv_b, norm_kv_g, w_kv,
           df_w_q, df_lq1, df_lk1, df_lq2, df_lk2, df_subln_g, df_w_o, moe_w_rg, moe_w_re, moe_w_gate, moe_w_up,
           moe_w_down, final_g):
    B, T, D = x.shape
    c_pad = jnp.zeros((SUBLANES, D), F32).at[:B].set(c)
    mod = _ada(c_pad, ada_w, ada_b, 6 * D // 4)[:, :B]
    mod_kv = _ada(c_pad, ada_kv_w[None], ada_kv_b[None], D)[0, :B]
    bf = lambda w: w.astype(BF16)
    row = lambda v: v.reshape(1, -1)

    for l in range(DEPTH):
        sh_m, sc_m, g_m, sh_f, sc_f, g_f = jnp.split(mod[l], 6, axis=-1)
        mod_m = jnp.stack([sh_m, sc_m], axis=1)
        mod_f = jnp.stack([sh_f, sc_f], axis=1)
        if l < N_A_LAYERS:
            i = l
            vec = jnp.stack([rw_w0[i], rw_a0[i], rw_k_k[i], rw_k_a[i]], axis=0)
            r, k, v, lw, kk, al, gate = _rwkv_proj(
                x, mod_m, row(norm_mix_g[l]), rw_mu[i], bf(rw_w_rkv[i]), bf(rw_w1[i]), bf(rw_w2[i]),
                bf(rw_a1[i]), bf(rw_a2[i]), bf(rw_g1[i]), bf(rw_g2[i]), vec)
            pvec = jnp.stack([rw_r_k[i].reshape(-1), rw_gn_g[i], rw_gn_b[i]], axis=0)
            y = _rwkv_scan(r, k, v, lw, kk, al, pvec)
            w_o = bf(rw_w_o[i])
        else:
            j = l - N_A_LAYERS
            q_proj = (mod_m, row(norm_mix_g[l]), bf(df_w_q[j]))
            if l == N_A_LAYERS:
                sh_kv, sc_kv = jnp.split(mod_kv, 2, axis=-1)
                q, kv = _norm_mm(x, [q_proj, (jnp.stack([sh_kv, sc_kv], axis=1), row(norm_kv_g), bf(w_kv))], BF16)
            else:
                q, = _norm_mm(x, [q_proj], BF16)
            lambda_init = 0.8 - 0.6 * math.exp(-0.3 * l)
            lam_vecs = jnp.stack([df_lq1[j], df_lk1[j], df_lq2[j], df_lk2[j]], axis=0)
            y, gate = _diff_attn(q, kv, lam_vecs, row(df_subln_g[j]), lambda_init), None
            w_o = bf(df_w_o[j])
        w_r = jnp.concatenate([moe_w_rg[l], moe_w_re[l], jnp.zeros((D, LANES - N_GROUPS - N_EXPERTS), F32)], axis=1)
        w_r_hi = bf(w_r)
        w_cat = jnp.concatenate([w_r_hi, bf(w_r - w_r_hi.astype(F32))], axis=1)
        x, info, cnt = _proj_res_route(y, gate, x, g_m[:, None, :], w_o, mod_f, row(norm_ffn_g[l]), w_cat)
        x = _moe_layer(x, info, cnt, mod_f, g_f[:, None, :], row(norm_ffn_g[l]), moe_w_gate, moe_w_up,
                       moe_w_down, l, row(final_g), final_norm=(l == DEPTH - 1))
    return x
```

```python
import functools
import math

import jax
import jax.numpy as jnp
from jax import lax
from jax.experimental import pallas as pl
from jax.experimental.pallas import tpu as pltpu

F32 = jnp.float32
BF16 = jnp.bfloat16

D_MODEL = 1024
DEPTH = 2
N_A_LAYERS = DEPTH // 2
RWKV_HEAD = 64
RWKV_HEADS = D_MODEL // RWKV_HEAD
RWKV_GN_EPS = 64e-5
DIFF_QK_DIM = 64
DIFF_V_DIM = 2 * DIFF_QK_DIM
DIFF_HEADS = D_MODEL // DIFF_V_DIM
SUBLN_EPS = 1e-5
N_GROUPS = 4
EXPERTS_PER_GROUP = 8
N_EXPERTS = N_GROUPS * EXPERTS_PER_GROUP
EXPERT_FF = 512
RMS_EPS = 1e-6

LANES = 128
SUBLANES = 8
VMEM_LIMIT_BYTES = 56 * 1024 * 1024

SCAN_CHUNK = 64
SCAN_BASE_BLOCK = 8
SCAN_CHUNKS_PER_STEP = 2
PAIR = 2 * RWKV_HEAD
PROJ_TM = 512
DENSE_TM = 512
ATT_TQ = 512
ATT_HB = 2
MOE_TM = 256
RANK_TB = 1024
ROW_TB = 256
assert SCAN_CHUNK == RWKV_HEAD and PAIR == LANES


def _cparams(*sem):
    return pltpu.CompilerParams(dimension_semantics=sem, vmem_limit_bytes=VMEM_LIMIT_BYTES)


def _dot(a, b):
    return jnp.dot(a.astype(BF16), b.astype(BF16), preferred_element_type=F32)


def _split(x):
    hi = x.astype(BF16)
    lo = (x - hi.astype(F32)).astype(BF16)
    return hi, lo


def _dot3(a, b):
    ah, al = _split(a)
    bh, bl = _split(b)
    d = functools.partial(jnp.dot, preferred_element_type=F32)
    return d(ah, bh) + d(ah, bl) + d(al, bh)


def _dot_hl(a, b_exact):
    ah, al = _split(a)
    d = functools.partial(jnp.dot, preferred_element_type=F32)
    return d(ah, b_exact) + d(al, b_exact)


def _rms_mod(x, g, shift, scale):
    ms = jnp.mean(x * x, axis=-1, keepdims=True)
    return (x * lax.rsqrt(ms + RMS_EPS) * g) * (1.0 + scale) + shift


def _ada_kernel(c_ref, w_ref, b_ref, o_ref):
    c = c_ref[...]
    ca = c * jax.nn.sigmoid(c)
    o_ref[...] = _dot3(ca, w_ref[...]) + b_ref[...]


def _ada(c_pad, w, b, tn):
    L, D, N = w.shape
    return pl.pallas_call(
        _ada_kernel,
        out_shape=jax.ShapeDtypeStruct((L, SUBLANES, N), F32),
        grid=(L, N // tn),
        in_specs=[
            pl.BlockSpec((SUBLANES, D), lambda l, j: (0, 0)),
            pl.BlockSpec((None, D, tn), lambda l, j: (l, 0, j)),
            pl.BlockSpec((None, 1, tn), lambda l, j: (l, 0, j)),
        ],
        out_specs=pl.BlockSpec((None, SUBLANES, tn), lambda l, j: (l, 0, j)),
        compiler_params=_cparams("arbitrary", "arbitrary"),
        name="ada_mod",
    )(c_pad, w, b.reshape(L, 1, N))


def _rwkv_proj_kernel(x_ref, xp_ref, mod_ref, g_ref, mu_ref, wrkv_ref, w1_ref, w2_ref, a1_ref, a2_ref,
                      g1_ref, g2_ref, vec_ref, r_ref, k_ref, v_ref, lw_ref, kk_ref, al_ref, gate_ref):
    i = pl.program_id(1)
    g = g_ref[...]
    shift, scale = mod_ref[0, 0:1, :], mod_ref[0, 1:2, :]
    h = _rms_mod(x_ref[0], g, shift, scale)
    hp = _rms_mod(xp_ref[0, SUBLANES - 1:SUBLANES, :], g, shift, scale)
    hp = jnp.where(i == 0, 0.0, hp)
    row = lax.broadcasted_iota(jnp.int32, h.shape, 0)
    h_prev = jnp.where(row == 0, hp, pltpu.roll(h, 1, axis=0))
    xx = h_prev - h
    mu = mu_ref[...]
    xs = [(h + xx * mu[j:j + 1, :]).astype(BF16) for j in range(6)]
    w0, a0, k_k, k_a = (vec_ref[j:j + 1, :] for j in range(4))
    d = functools.partial(jnp.dot, preferred_element_type=F32)
    r = d(xs[0], wrkv_ref[0])
    k = d(xs[1], wrkv_ref[1])
    v = d(xs[2], wrkv_ref[2])
    z = w0 + _dot(jnp.tanh(d(xs[3], w1_ref[...])), w2_ref[...])
    lw = (-math.exp(-0.5)) * jax.nn.sigmoid(z)
    a = jax.nn.sigmoid(a0 + _dot(d(xs[4], a1_ref[...]), a2_ref[...]))
    gate = _dot(jax.nn.sigmoid(d(xs[5], g1_ref[...])), g2_ref[...])
    r_ref[0] = r.astype(BF16)
    k_ref[0] = (k * (1.0 + (a - 1.0) * k_a)).astype(BF16)
    v_ref[0] = v.astype(BF16)
    lw_ref[0] = lw
    kk_ref[0] = (k * k_k).astype(BF16)
    al_ref[0] = a.astype(BF16)
    gate_ref[0] = gate.astype(BF16)


def _rwkv_proj(x, mod2, g, mu, wrkv, w1, w2, a1, a2, g1, g2, vec):
    B, T, D = x.shape
    tm = PROJ_TM
    const2 = lambda b, i: (0, 0)
    const3 = lambda b, i: (0, 0, 0)
    act = pl.BlockSpec((1, tm, D), lambda b, i: (b, i, 0))
    n_sub = tm // SUBLANES
    return pl.pallas_call(
        _rwkv_proj_kernel,
        out_shape=[jax.ShapeDtypeStruct((B, T, D), F32 if n == 3 else BF16) for n in range(7)],
        grid=(B, T // tm),
        in_specs=[
            act,
            pl.BlockSpec((1, SUBLANES, D), lambda b, i: (b, jnp.maximum(i * n_sub - 1, 0), 0)),
            pl.BlockSpec((1, 2, D), lambda b, i: (b, 0, 0)),
            pl.BlockSpec((1, D), const2),
            pl.BlockSpec((6, D), const2),
            pl.BlockSpec((3, D, D), const3),
            pl.BlockSpec(w1.shape, const2), pl.BlockSpec(w2.shape, const2),
            pl.BlockSpec(a1.shape, const2), pl.BlockSpec(a2.shape, const2),
            pl.BlockSpec(g1.shape, const2), pl.BlockSpec(g2.shape, const2),
            pl.BlockSpec((4, D), const2),
        ],
        out_specs=[act] * 7,
        compiler_params=_cparams("arbitrary", "arbitrary"),
        name="rwkv_proj",
    )(x, x, mod2, g, mu, wrkv, w1, w2, a1, a2, g1, g2, vec)


def _rwkv_scan_kernel(r_ref, k_ref, v_ref, lw_ref, kk_ref, al_ref, pv_ref, y_ref, h_ref):
    C = SCAN_CHUNK
    P2 = 2 * C

    @pl.when(pl.program_id(1) == 0)
    def _():
        h_ref[...] = jnp.zeros_like(h_ref)

    lane = lax.broadcasted_iota(jnp.int32, (1, PAIR), 1)
    m_left = (lane < RWKV_HEAD).astype(F32)
    m_right = 1.0 - m_left
    ri = lax.broadcasted_iota(jnp.int32, (P2, P2), 0)
    ci = lax.broadcasted_iota(jnp.int32, (P2, P2), 1)
    same = (ri >= C) == (ci >= C)
    strict = same & (ri > ci)
    incl = same & (ri >= ci)
    eye = ri == ci
    block_ones = same.astype(BF16)
    tri = (lax.broadcasted_iota(jnp.int32, (C, C), 0) >= lax.broadcasted_iota(jnp.int32, (C, C), 1)).astype(BF16)

    def stack(x):
        return jnp.concatenate([x * m_left, x * m_right], axis=0)

    def head_sums(x):
        s_left = jnp.sum(x * m_left, axis=-1, keepdims=True)
        s_right = jnp.sum(x * m_right, axis=-1, keepdims=True)
        return jnp.where(lane < RWKV_HEAD, s_left, s_right)

    inv_n = 1.0 / RWKV_HEAD
    dd = functools.partial(jnp.dot, preferred_element_type=F32)
    n_pairs = RWKV_HEADS // 2
    units = [(ch, p) for ch in range(SCAN_CHUNKS_PER_STEP) for p in range(n_pairs)]
    idx = [(slice(ch * C, (ch + 1) * C), slice(p * PAIR, (p + 1) * PAIR)) for ch, p in units]
    U = range(len(units))
    ld = lambda ref, rs, sl: ref[0, rs, sl].astype(F32)
    kkr = [ld(kk_ref, rs, sl) for rs, sl in idx]
    ss = [head_sums(x * x) for x in kkr]
    lws = [lw_ref[0, rs, sl] for rs, sl in idx]
    Ls = []
    for lw in lws:
        l_hi, l_lo = _split(lw)
        cs = dd(tri, jnp.concatenate([l_hi, l_lo], axis=1))
        Ls.append(cs[:, :PAIR] + cs[:, PAIR:])
    lhs_g, rhs_g, bk_hat, vs, at32, rt32, dec_end = [], [], [], [], [], [], []
    for u in U:
        rs, sl = idx[u]
        L, lw = Ls[u], lws[u]
        kk = kkr[u] * lax.rsqrt(jnp.maximum(ss[u], 1e-24))
        b_vec = kk * ld(al_ref, rs, sl)
        k = ld(k_ref, rs, sl)
        LC = L[C - 1:C, :]
        e_neg = jnp.exp(-L)
        e_end = jnp.exp(LC - L)
        At = stack(-kk * jnp.exp(L - lw))
        Rt = stack(ld(r_ref, rs, sl) * jnp.exp(L))
        at32.append(At)
        rt32.append(Rt)
        lhs_g.append(jnp.concatenate([At, Rt], axis=0).astype(BF16))
        rhs_g.append(jnp.concatenate([stack(b_vec * e_neg), stack(k * e_neg)], axis=0).astype(BF16))
        bk_hat.append(jnp.concatenate([stack(b_vec * e_end), stack(k * e_end)], axis=0))
        vs.append(stack(ld(v_ref, rs, sl)).astype(BF16))
        dec_end.append(jnp.exp(LC))
    G = [lax.dot_general(lhs_g[u], rhs_g[u], (((1,), (1,)), ((), ())), preferred_element_type=F32) for u in U]
    A_ak = [jnp.where(strict, G[u][:P2, P2:], 0.0).astype(BF16) for u in U]
    A_r = [jnp.concatenate([jnp.where(incl, G[u][P2:, :P2], 0.0), jnp.where(incl, G[u][P2:, P2:], 0.0)],
                           axis=1).astype(BF16) for u in U]
    W = [dd(A_ak[u], vs[u]) for u in U]
    bsz = lambda b: (ri >> int(math.log2(b))) == (ci >> int(math.log2(b)))
    b8 = bsz(SCAN_BASE_BLOCK)
    D1 = [jnp.where(strict & b8, G[u][:P2, :P2], 0.0).astype(BF16) for u in U]
    D2 = [dd(D1[u], D1[u]).astype(BF16) for u in U]
    D4 = [dd(D2[u], D2[u]).astype(BF16) for u in U]
    eye_f = eye.astype(F32)
    P1 = [eye_f + D1[u].astype(F32) + D2[u].astype(F32) + dd(D1[u], D2[u]) for u in U]
    Tm = [P1[u] + dd(P1[u].astype(BF16), D4[u]) for u in U]
    blk = SCAN_BASE_BLOCK
    while blk < C:
        off = strict & bsz(2 * blk) & ~bsz(blk)
        Mo = [jnp.where(off, G[u][:P2, :P2], 0.0).astype(BF16) for u in U]
        Tb = [Tm[u].astype(BF16) for u in U]
        TM = [dd(Tb[u], Mo[u]).astype(BF16) for u in U]
        Tm = [Tm[u] + dd(TM[u], Tb[u]) for u in U]
        blk *= 2
    Z = [dd(Tm[u].astype(BF16), jnp.concatenate([at32[u], W[u]], axis=1).astype(BF16)) for u in U]
    rhs = [jnp.concatenate([Z[u].astype(BF16), jnp.concatenate([jnp.zeros_like(vs[u]), vs[u]], axis=1)], axis=0)
           for u in U]
    o6 = [dd(A_r[u], rhs[u]) for u in U]
    o7 = [dd(bk_hat[u].T.astype(BF16), rhs[u]) for u in U]
    H = [h_ref[p] for p in range(n_pairs)]
    Y = [None] * len(units)
    for u in U:
        p = units[u][1]
        Hb = H[p].astype(BF16)
        Y[u] = dd((rt32[u] + o6[u][:, :PAIR]).astype(BF16), Hb) + o6[u][:, PAIR:]
        Mbd = o7[u][:, :PAIR] + jnp.where(eye, dec_end[u], 0.0)
        H[p] = dd(Mbd.astype(BF16), Hb) + o7[u][:, PAIR:]
    for p in range(n_pairs):
        h_ref[p] = H[p]
    ys = [Y[u][:C] + Y[u][C:] for u in U]
    rk = [ld(r_ref, rs, sl) * ld(k_ref, rs, sl) * pv_ref[0:1, sl] for rs, sl in idx]
    st1 = [head_sums(jnp.concatenate([ys[u], rk[u]], axis=0)) for u in U]
    yc = [ys[u] - st1[u][:C] * inv_n for u in U]
    var = [head_sums(yc[u] * yc[u]) * inv_n for u in U]
    for u in U:
        rs, sl = idx[u]
        bonus = st1[u][C:] * ld(v_ref, rs, sl)
        y_ref[0, rs, sl] = (yc[u] * lax.rsqrt(var[u] + RWKV_GN_EPS) * pv_ref[1:2, sl] + pv_ref[2:3, sl]
                            + bonus).astype(y_ref.dtype)


def _rwkv_scan(r, k, v, lw, kk, al, pvec):
    B, T, D = r.shape
    rows = SCAN_CHUNK * SCAN_CHUNKS_PER_STEP
    act = pl.BlockSpec((1, rows, D), lambda b, c: (b, c, 0))
    return pl.pallas_call(
        _rwkv_scan_kernel,
        out_shape=jax.ShapeDtypeStruct((B, T, D), BF16),
        grid=(B, T // rows),
        in_specs=[act] * 6 + [pl.BlockSpec((3, D), lambda b, c: (0, 0))],
        out_specs=act,
        scratch_shapes=[pltpu.VMEM((RWKV_HEADS // 2, PAIR, PAIR), F32)],
        compiler_params=_cparams("arbitrary", "arbitrary"),
        name="rwkv_scan",
    )(r, k, v, lw, kk, al, pvec)


def _proj_res_route_kernel(*refs, has_gate):
    if has_gate:
        y_ref, g_ref, x_ref, gm_ref, w_ref, modf_ref, gf_ref, wr_ref, o_ref, info_ref, cnt_ref = refs
        y = y_ref[0].astype(F32) * g_ref[0].astype(F32)
    else:
        y_ref, x_ref, gm_ref, w_ref, modf_ref, gf_ref, wr_ref, o_ref, info_ref, cnt_ref = refs
        y = y_ref[0]
    x_new =x_ref[0] + gm_ref[0] * jnp.dot(y.astype(BF16), w_ref[...], preferred_element_type=F32)
    o_ref[0] = x_new
    h = _rms_mod(x_new, gf_ref[...], modf_ref[0, 0:1, :], modf_ref[0, 1:2, :])
    first = (pl.program_id(0) == 0) & (pl.program_id(1) == 0)
    _route(h, wr_ref[...], info_ref, cnt_ref, first)


def _proj_res_route(y, g, x, gm, w, mod_f, norm_f, w_cat):
    B, T, D = x.shape
    tm = DENSE_TM
    per_b = T // tm
    act =pl.BlockSpec((1, tm, D), lambda b, i: (b, i, 0))
    const = lambda b, i: (0, 0)
    flat = lambda b, i: (b * per_b + i, 0)
    ins = [y] + ([g] if g is not None else []) + [x, gm, w, mod_f, norm_f, w_cat]
    specs = [act] * (len(ins) - 5) + [pl.BlockSpec((1, 1, D), lambda b, i: (b, 0, 0)), pl.BlockSpec((D, D), const),
                                     pl.BlockSpec((1, 2, D), lambda b, i: (b, 0, 0)), pl.BlockSpec((1, D), const),
                                     pl.BlockSpec((D, 2 * LANES), const)]
    return pl.pallas_call(
        functools.partial(_proj_res_route_kernel, has_gate=g is not None),
        out_shape=[jax.ShapeDtypeStruct((B, T, D), F32), jax.ShapeDtypeStruct((B * T, LANES), F32),
                   jax.ShapeDtypeStruct((SUBLANES, LANES), F32)],
        grid=(B, per_b),
        in_specs=specs,
        out_specs=[act, pl.BlockSpec((tm, LANES), flat), pl.BlockSpec((SUBLANES, LANES), const)],
        compiler_params=_cparams("arbitrary", "arbitrary"),
        name="proj_res_route",
    )(*ins)


def _norm_mm_kernel(*refs, n_proj):
    x_ref = refs[0]
    ins, outs = refs[1:1 + 3 * n_proj], refs[1 + 3 * n_proj:]
    x = x_ref[0]
    xn = x * lax.rsqrt(jnp.mean(x * x, axis=-1, keepdims=True) + RMS_EPS)
    for p in range(n_proj):
        mod_ref, g_ref, w_ref = ins[3 * p:3 * p + 3]
        h = (xn * g_ref[...]) * (1.0 + mod_ref[0, 1:2, :]) + mod_ref[0, 0:1, :]
        outs[p][0] = jnp.dot(h.astype(BF16), w_ref[...], preferred_element_type=F32).astype(outs[p].dtype)


def _norm_mm(x, projs, out_dtype):
    B, T, D = x.shape
    tm = DENSE_TM
    in_specs = [pl.BlockSpec((1, tm, D), lambda b, i: (b, i, 0))]
    args = [x]
    for mod2, g, w in projs:
        in_specs += [pl.BlockSpec((1, 2, D), lambda b, i: (b, 0, 0)), pl.BlockSpec((1, D), lambda b, i: (0, 0)),
                     pl.BlockSpec(w.shape, lambda b, i: (0, 0))]
        args += [mod2, g, w]
    return pl.pallas_call(
        functools.partial(_norm_mm_kernel, n_proj=len(projs)),
        out_shape=[jax.ShapeDtypeStruct((B, T, w.shape[1]), out_dtype) for _, _, w in projs],
        grid=(B, T // tm),
        in_specs=in_specs,
        out_specs=[pl.BlockSpec((1, tm, w.shape[1]), lambda b, i: (b, i, 0)) for _, _, w in projs],
        compiler_params=_cparams("arbitrary", "arbitrary"),
        name="norm_mm",
    )(*args)


def _diff_attn_kernel(q_ref, k_ref, v_ref, lam_ref, sg_ref, o_ref, m_ref, acc_ref, s_ref, *, lambda_init):
    tq, HB, dv = ATT_TQ, ATT_HB, DIFF_V_DIM
    qi = pl.program_id(2)
    heads = range(HB)
    hs = [slice(h * dv, (h + 1) * dv) for h in heads]
    lane = lax.broadcasted_iota(jnp.int32, (1, dv), 1)
    m_left = (lane < DIFF_QK_DIM).astype(F32)
    qs = []
    for h in heads:
        q = q_ref[0, :, hs[h]].astype(F32) * (DIFF_QK_DIM ** -0.5 * math.log2(math.e))
        qs.append(jnp.concatenate([q * m_left, q * (1.0 - m_left)], axis=0).astype(BF16))
    ones_col = jnp.ones((tq, dv), BF16)
    causal = (lax.broadcasted_iota(jnp.int32, (2 * tq, tq), 1)
              <= lax.broadcasted_iota(jnp.int32, (2 * tq, tq), 0) % tq)

    def key_rows(j):
        return pl.ds(pl.multiple_of(j * tq, tq), tq)

    def scores_into(slot, j):
        for h in heads:
            s_ref[slot, h] = lax.dot_general(qs[h], k_ref[0, key_rows(j), hs[h]], (((1,), (1,)), ((), ())),
                                             preferred_element_type=F32)

    def block(slot, j, first, prefetch):
        if prefetch is not None:
            scores_into(1 - slot, prefetch)
        for h in heads:
            s = s_ref[slot, h]
            if first:
                s = jnp.where(causal, s, -jnp.inf)
                m_new = jnp.broadcast_to(jnp.max(s, axis=-1, keepdims=True), (2 * tq, LANES))
            else:
                m_old = m_ref[h]
                m_new = jnp.maximum(m_old, jnp.max(s, axis=-1, keepdims=True))
            m_ref[h] = m_new
            p = jnp.exp2(s - jnp.concatenate([m_new] * (tq // LANES), axis=1)).astype(BF16)
            pv = jnp.dot(p, jnp.concatenate([v_ref[0, key_rows(j), hs[h]], ones_col], axis=1),
                         preferred_element_type=F32)
            if first:
                acc_ref[h] = pv
            else:
                alpha = jnp.exp2(m_old - m_new)
                acc_ref[h] = acc_ref[h] * jnp.concatenate([alpha, alpha], axis=1) + pv

    scores_into(0, qi)
    block(0, qi, True, 0)

    def pair(u, c):
        last = jnp.maximum(qi - 1, 0)
        block(1, 2 * u, False, jnp.minimum(2 * u + 1, last))
        block(0, 2 * u + 1, False, jnp.minimum(2 * u + 2, last))
        return c

    lax.fori_loop(0, qi // 2, pair, 0)

    @pl.when(qi % 2 == 1)
    def _():
        block(1, qi - 1, False, None)

    lv = lam_ref[...]
    lam = (jnp.exp(jnp.sum(lv[0:1] * lv[1:2], axis=-1, keepdims=True))
           - jnp.exp(jnp.sum(lv[2:3] * lv[3:4], axis=-1, keepdims=True)) + lambda_init)
    for h in heads:
        acc = acc_ref[h]
        o = acc[:, :dv] / acc[:, dv:]
        o = o[:tq] - lam * o[tq:]
        ms = jnp.mean(o * o, axis=-1, keepdims=True)
        o_ref[0, :, hs[h]] = (o * lax.rsqrt(ms + SUBLN_EPS) * sg_ref[...] * (1.0 - lambda_init)).astype(o_ref.dtype)


def _diff_attn(q, kv, lam_vecs, subln_g, lambda_init):
    B, T, D = q.shape
    tq, HB = ATT_TQ, ATT_HB
    n_hb = DIFF_HEADS // HB
    w = HB * DIFF_V_DIM
    return pl.pallas_call(
        functools.partial(_diff_attn_kernel, lambda_init=lambda_init),
        out_shape=jax.ShapeDtypeStruct((B, T, D), BF16),
        grid=(B, n_hb, T // tq),
        in_specs=[
            pl.BlockSpec((1, tq, w), lambda b, h, i: (b, i, h)),
            pl.BlockSpec((1, T, w), lambda b, h, i: (b, 0, h)),
            pl.BlockSpec((1, T, w), lambda b, h, i: (b, 0, n_hb + h)),
            pl.BlockSpec((4, DIFF_QK_DIM), lambda b, h, i: (0, 0)),
            pl.BlockSpec((1, DIFF_V_DIM), lambda b, h, i: (0, 0)),
        ],
        out_specs=pl.BlockSpec((1, tq, w), lambda b, h, i: (b, i, h)),
        scratch_shapes=[pltpu.VMEM((HB, 2 * tq, LANES), F32), pltpu.VMEM((HB, 2 * tq, 2 * DIFF_V_DIM), F32),
                        pltpu.VMEM((2, HB, 2 * tq, tq), F32)],
        compiler_params=_cparams("arbitrary", "arbitrary", "arbitrary"),
        name="diff_attn",
    )(q, kv, kv, lam_vecs, subln_g)


def _route(h, w_cat, info_ref, cnt_ref, first_step):
    h_hi, h_lo = _split(h)
    two = jnp.dot(h_hi, w_cat, preferred_element_type=F32)
    logit = (two[:, :LANES] + two[:, LANES:]
             + jnp.dot(h_lo, w_cat[:, :LANES], preferred_element_type=F32))
    lane_i = lax.broadcasted_iota(jnp.int32, logit.shape, 1)
    lane = lane_i.astype(F32)
    neg = -jnp.inf
    big = float(LANES)
    is_grp = lane_i < N_GROUPS
    gl = jnp.where(is_grp, logit, neg)
    gmax = jnp.max(gl, axis=-1, keepdims=True)
    gidx = jnp.min(jnp.where(gl == gmax, lane, big), axis=-1, keepdims=True)
    grp_gate = 1.0 / jnp.sum(jnp.where(is_grp, jnp.exp(logit - gmax), 0.0), axis=-1, keepdims=True)
    lo = N_GROUPS + gidx * EXPERTS_PER_GROUP
    in_grp = (lane >= lo) & (lane < lo + EXPERTS_PER_GROUP)
    el = jnp.where(in_grp, logit, neg)
    t1 = jnp.max(el, axis=-1, keepdims=True)
    i1 = jnp.min(jnp.where(el == t1, lane, big), axis=-1, keepdims=True)
    el2 = jnp.where(lane == i1, neg, el)
    t2 = jnp.max(el2, axis=-1, keepdims=True)
    i2 = jnp.min(jnp.where(el2 == t2, lane, big), axis=-1, keepdims=True)
    e21 = jnp.exp(t2 - t1)
    p1 = 1.0 / (1.0 + e21)
    w1 = grp_gate * p1
    w2 = grp_gate * (e21 * p1)
    e1 = i1 - N_GROUPS
    e2 = i2 - N_GROUPS
    info_ref[...] = jnp.where(lane_i == 0, e1, jnp.where(lane_i == 1, e2, jnp.where(lane_i == 2, w1, jnp.where(lane_i == 3, w2, 0.0))))

    @pl.when(first_step)
    def _():
        cnt_ref[...] = jnp.zeros_like(cnt_ref)

    picked = ((lane == e1) | (lane == e2)).astype(F32)
    cnt_ref[...] += jnp.broadcast_to(jnp.sum(picked, axis=0, keepdims=True), cnt_ref.shape)


def _rank_kernel(info_ref, cnt_ref, dest_ref, meta_ref, start_ref):
    i = pl.program_id(0)
    tb = info_ref.shape[0]
    lane = lax.broadcasted_iota(jnp.int32, (tb, LANES), 1)
    info = info_ref[...]
    e0 = info[:, 0:1].astype(jnp.int32)
    e1 = info[:, 1:2].astype(jnp.int32)
    o0 = (lane == e0).astype(F32)
    o1 = (lane == e1).astype(F32)
    both = o0 + o1

    @pl.when(i == 0)
    def _():
        cnt = cnt_ref[0:1, :]
        padded = jnp.floor((cnt + (MOE_TM - 1)) * (1.0 / MOE_TM)) * MOE_TM
        r = lax.broadcasted_iota(jnp.int32, (LANES, LANES), 0)
        c = lax.broadcasted_iota(jnp.int32, (LANES, LANES), 1)
        upper_strict = (r < c).astype(BF16)
        start = _dot_hl(jnp.broadcast_to(padded, (SUBLANES, LANES)), upper_strict)[0:1]
        start_ref[...] = start
        row = lax.broadcasted_iota(jnp.int32, (SUBLANES, LANES), 0)
        meta_ref[...] = jnp.where(row == 0, start + padded, start + cnt)

    r = lax.broadcasted_iota(jnp.int32, (tb, tb), 0)
    c = lax.broadcasted_iota(jnp.int32, (tb, tb), 1)
    lower_strict = (r > c).astype(BF16)
    before = jnp.dot(lower_strict, both.astype(BF16), preferred_element_type=F32) + start_ref[...]
    d0 = jnp.sum(o0 * before, axis=-1, keepdims=True)
    d1 = jnp.sum(o1 * before, axis=-1, keepdims=True)
    dest_rows = jnp.transpose(jnp.where(lane == 0, d0, jnp.where(lane == 1, d1, 0.0)))[:SUBLANES]
    dest_ref[0] = dest_rows.astype(jnp.int32)
    start_ref[...] += jnp.sum(both, axis=0, keepdims=True)


def _rank(info, cnt):
    n_tok = info.shape[0]
    tb = RANK_TB
    n_blk = n_tok // tb
    return pl.pallas_call(
        _rank_kernel,
        out_shape=[jax.ShapeDtypeStruct((n_blk, SUBLANES, tb), jnp.int32), jax.ShapeDtypeStruct((SUBLANES, LANES), F32)],
        grid=(n_blk,),
        in_specs=[pl.BlockSpec((tb, LANES), lambda i: (i, 0)), pl.BlockSpec((SUBLANES, LANES), lambda i: (0, 0))],
        out_specs=[pl.BlockSpec((1, SUBLANES, tb), lambda i: (i, 0, 0)), pl.BlockSpec((SUBLANES, LANES), lambda i: (0, 0))],
        scratch_shapes=[pltpu.VMEM((1, LANES), F32)],
        compiler_params=_cparams("arbitrary"),
        name="moe_rank",
    )(info, cnt)


assert D_MODEL == SUBLANES * LANES


def _tile_rows_store(ref, x):
    n = x.shape[0]
    for s in range(SUBLANES):
        ref[pl.ds(s, n, stride=SUBLANES), :] = x[:, s * LANES:(s + 1) * LANES]


def _tile_rows_load(ref, n):
    return jnp.concatenate([ref[pl.ds(s, n, stride=SUBLANES), :] for s in range(SUBLANES)], axis=1)


def _dest_spec(step_of):
    per = RANK_TB // ROW_TB
    return pl.BlockSpec((1, SUBLANES, ROW_TB), lambda i: (step_of(i) // per, 0, step_of(i) % per),
                        memory_space=pltpu.SMEM)


def _row_copy(src_ref, s, dst_ref, d, sem):
    rows = lambda r: pl.ds(pl.multiple_of(r * SUBLANES, SUBLANES), SUBLANES)
    return pltpu.make_async_copy(src_ref.at[rows(s)], dst_ref.at[rows(d)], sem)


def _zero_padding_rows(seg_ref, xs_ref, zero_ref, zsem, *, start):
    def copy(first_row, n_rows):
        cp = pltpu.make_async_copy(zero_ref.at[pl.ds(0, n_rows * SUBLANES)],
                                   xs_ref.at[pl.ds(pl.multiple_of(first_row * SUBLANES, SUBLANES), n_rows * SUBLANES)],
                                   zsem)
        cp.start() if start else cp.wait()

    for e in range(N_EXPERTS):
        lo, n = seg_ref[1, e], seg_ref[0, e] - seg_ref[1, e]
        bit = MOE_TM // 2
        while bit >= 1:
            @pl.when((n & bit) != 0)
            def _(lo=lo, n=n, bit=bit):
                copy(lo + (n & ~(2 * bit - 1)), bit)
            bit //= 2
    n_total = xs_ref.shape[0] // SUBLANES

    def tail(j, c):
        copy(seg_ref[0, N_EXPERTS - 1] + j * MOE_TM, MOE_TM)
        return c

    lax.fori_loop(0, (n_total - seg_ref[0, N_EXPERTS - 1]) // MOE_TM, tail, 0)


def _dispatch_kernel(dest_ref, seg_ref, x_ref, mod_ref, g_ref, xs_ref, h_ref, zero_ref, sem, zsem):
    tb = x_ref.shape[0]
    i = pl.program_id(0)
    last = pl.num_programs(0) - 1
    h = _rms_mod(x_ref[...], g_ref[...], mod_ref[0, 0:1, :], mod_ref[0, 1:2, :])

    @pl.when(i == 0)
    def _():
        zero_ref[...] = jnp.zeros_like(zero_ref)
        _zero_padding_rows(seg_ref, xs_ref, zero_ref, zsem, start=True)

    @pl.when(i == last)
    def _():
        _zero_padding_rows(seg_ref, xs_ref, zero_ref, zsem, start=False)

    def drain(slot):
        def wait(j, c):
            _row_copy(h_ref.at[slot], 0, xs_ref, 0, sem.at[slot]).wait()
            return c
        lax.fori_loop(0, 2 * tb, wait, 0, unroll=8)

    for slot in range(2):
        @pl.when(i % 2 == slot)
        def _(slot=slot):
            @pl.when(i >= 2)
            def _():
                drain(slot)

            _tile_rows_store(h_ref.at[slot], h)

            def start(j, c):
                _row_copy(h_ref.at[slot], j, xs_ref, dest_ref[0, 0, j], sem.at[slot]).start(priority=0)
                _row_copy(h_ref.at[slot], j, xs_ref, dest_ref[0, 1, j], sem.at[slot]).start(priority=1)
                return c

            lax.fori_loop(0, tb, start, 0, unroll=8)

            @pl.when(i == last)
            def _():
                drain(slot)

                @pl.when(i >= 1)
                def _():
                    drain(1 - slot)


def _dispatch(dest3, seg, x2, mod2, g, n_rows, T):
    n_tok, D = x2.shape
    tb = ROW_TB
    per_b = T // tb
    return pl.pallas_call(
        _dispatch_kernel,
        out_shape=jax.ShapeDtypeStruct((n_rows * SUBLANES, LANES), F32),
        grid=(n_tok // tb,),
        in_specs=[
            _dest_spec(lambda i: i),
            pl.BlockSpec(memory_space=pltpu.SMEM),
            pl.BlockSpec((tb, D), lambda i: (i, 0)),
            pl.BlockSpec((1, 2, D), lambda i: (i // per_b, 0, 0)),
            pl.BlockSpec((1, D), lambda i: (0, 0)),
        ],
        out_specs=pl.BlockSpec(memory_space=pl.ANY),
        scratch_shapes=[pltpu.VMEM((2, tb * SUBLANES, LANES), F32), pltpu.VMEM((MOE_TM * SUBLANES, LANES), F32),
                        pltpu.SemaphoreType.DMA((2,)), pltpu.SemaphoreType.DMA(())],
        compiler_params=_cparams("arbitrary"),
        name="moe_dispatch",
    )(dest3, seg, x2, mod2, g)


def _expert_kernel(be_ref, nb_ref, nxt_ref, xs_ref, wg_hbm, wu_hbm, wd_hbm, ys_ref, stage_g, stage_u, stage_d,
                   wgb, wub, wdb, sem, *, layer):
    i = pl.program_id(0)
    e = be_ref[i]
    changed = (i == 0) | (e != be_ref[jnp.maximum(i - 1, 0)])

    def fetch(ex):
        return (pltpu.make_async_copy(wg_hbm.at[layer, ex], stage_g, sem.at[0]),
                pltpu.make_async_copy(wu_hbm.at[layer, ex], stage_u, sem.at[1]),
                pltpu.make_async_copy(wd_hbm.at[layer, ex], stage_d, sem.at[2]))

    @pl.when(i == 0)
    def _():
        for cp in fetch(e):
            cp.start()

    @pl.when(changed)
    def _():
        for cp in fetch(e):
            cp.wait()
        wgb[...] = stage_g[...].astype(BF16)
        wub[...] = stage_u[...].astype(BF16)
        wdb[...] = stage_d[...].astype(BF16)

        @pl.when(nxt_ref[i] >= 0)
        def _():
            for cp in fetch(nxt_ref[i]):
                cp.start()

    @pl.when(i < nb_ref[0])
    def _():
        x = _tile_rows_load(xs_ref, MOE_TM).astype(BF16)
        a = jnp.dot(x, wgb[...], preferred_element_type=F32)
        u = jnp.dot(x, wub[...], preferred_element_type=F32)
        hdn = (a * jax.nn.sigmoid(a)) * u
        _tile_rows_store(ys_ref, jnp.dot(hdn.astype(BF16), wdb[...], preferred_element_type=F32))

    @pl.when(i >= nb_ref[0])
    def _():
        ys_ref[...] = jnp.zeros_like(ys_ref)


def _experts(blk_e, n_used, next_e, xs, w_gate, w_up, w_down, layer):
    D = D_MODEL
    n_rows = xs.shape[0] // SUBLANES
    tm = MOE_TM
    FF = EXPERT_FF
    row_block = pl.BlockSpec((tm * SUBLANES, LANES), lambda i, be, nb, nx: (i, 0))
    hbm = pl.BlockSpec(memory_space=pl.ANY)
    grid_spec = pltpu.PrefetchScalarGridSpec(
        num_scalar_prefetch=3,
        grid=(n_rows // tm,),
        in_specs=[row_block, hbm, hbm, hbm],
        out_specs=row_block,
        scratch_shapes=[pltpu.VMEM((D, FF), F32), pltpu.VMEM((D, FF), F32), pltpu.VMEM((FF, D), F32),
                        pltpu.VMEM((D, FF), BF16), pltpu.VMEM((D, FF), BF16), pltpu.VMEM((FF, D), BF16),
                        pltpu.SemaphoreType.DMA((3,))],
    )
    return pl.pallas_call(
        functools.partial(_expert_kernel, layer=layer),
        out_shape=jax.ShapeDtypeStruct(xs.shape, F32),
        grid_spec=grid_spec,
        compiler_params=_cparams("arbitrary"),
        name="moe_experts",
    )(blk_e, n_used, next_e, xs, w_gate, w_up, w_down)


def _combine_kernel(dest_ref, dest_next_ref, info_ref, x_ref, gf_ref, fg_ref, ys_ref, o_ref, y_ref, sem, *,
                    final_norm):
    tb = x_ref.shape[0]
    i = pl.program_id(0)
    last = pl.num_programs(0) - 1

    def gather(d_ref, slot):
        def start(j, c):
            _row_copy(ys_ref, d_ref[0, 0, j], y_ref.at[slot, 0], j, sem.at[slot]).start(priority=0)
            _row_copy(ys_ref, d_ref[0, 1, j], y_ref.at[slot, 1], j, sem.at[slot]).start(priority=1)
            return c
        lax.fori_loop(0, tb, start, 0, unroll=8)

    @pl.when(i == 0)
    def _():
        gather(dest_ref, 0)

    for slot in range(2):
        @pl.when(i % 2 == slot)
        def _(slot=slot):
            @pl.when(i < last)
            def _():
                gather(dest_next_ref, 1 - slot)

            def wait(j, c):
                _row_copy(ys_ref, 0, y_ref.at[slot, 0], 0, sem.at[slot]).wait()
                return c

            lax.fori_loop(0, 2 * tb, wait, 0, unroll=8)
            info = info_ref[...]
            moe = (info[:, 2:3] * _tile_rows_load(y_ref.at[slot, 0], tb)
                   + info[:, 3:4] * _tile_rows_load(y_ref.at[slot, 1], tb))
            out = x_ref[...] + gf_ref[0] * moe
            if final_norm:
                ms = jnp.mean(out * out, axis=-1, keepdims=True)
                out = out * lax.rsqrt(ms + RMS_EPS) * fg_ref[...]
            o_ref[...] = out


def _combine(dest3, info, x2, gf, final_g, ys, T, final_norm):
    n_tok, D = x2.shape
    tb = ROW_TB
    per_b = T // tb
    n_steps = n_tok // tb
    return pl.pallas_call(
        functools.partial(_combine_kernel, final_norm=final_norm),
        out_shape=jax.ShapeDtypeStruct((n_tok, D), F32),
        grid=(n_steps,),
        in_specs=[
            _dest_spec(lambda i: i),
            _dest_spec(lambda i: jnp.minimum(i + 1, n_steps - 1)),
            pl.BlockSpec((tb, LANES), lambda i: (i, 0)),
            pl.BlockSpec((tb, D), lambda i: (i, 0)),
            pl.BlockSpec((1, 1, D), lambda i: (i // per_b, 0, 0)),
            pl.BlockSpec((1, D), lambda i: (0, 0)),
            pl.BlockSpec(memory_space=pl.ANY),
        ],
        out_specs=pl.BlockSpec((tb, D), lambda i: (i, 0)),
        scratch_shapes=[pltpu.VMEM((2, 2, tb * SUBLANES, LANES), F32), pltpu.SemaphoreType.DMA((2,))],
        compiler_params=_cparams("arbitrary"),
        name="moe_combine",
    )(dest3, dest3, info, x2, gf, final_g, ys)


def _moe_rows(n_tok):
    return -(-(2 * n_tok + N_EXPERTS * MOE_TM) // MOE_TM) * MOE_TM


def _moe_layer(x, info, cnt, mod_f, gf, norm_g, w_gate, w_up, w_down, layer, final_g, final_norm):
    B, T, D = x.shape
    n_tok = B * T
    x2 = x.reshape(n_tok, D)
    n_rows = _moe_rows(n_tok)
    dest, meta = _rank(info, cnt)
    seg = meta[0:2, :N_EXPERTS].astype(jnp.int32)
    pad_end = seg[0]
    n_blocks = n_rows // MOE_TM
    blk_start = jnp.arange(n_blocks, dtype=jnp.int32) * MOE_TM
    blk_e = jnp.minimum(jnp.sum(pad_end[None, :] <= blk_start[:, None], axis=1), N_EXPERTS - 1).astype(jnp.int32)
    n_used = (pad_end[N_EXPERTS - 1:] // MOE_TM).astype(jnp.int32)
    dest3 = dest
    xs = _dispatch(dest3, seg, x2, mod_f, norm_g, n_rows, T)
    seg_end = jnp.sum(blk_e[None, :] <= blk_e[:, None], axis=1)
    next_e = jnp.where(seg_end < n_blocks, blk_e[jnp.minimum(seg_end, n_blocks - 1)], -1).astype(jnp.int32)
    ys = _experts(blk_e, n_used, next_e, xs, w_gate, w_up, w_down, layer)
    out = _combine(dest3, info, x2, gf, final_g, ys, T, final_norm)
    return out.reshape(B, T, D)


def kernel(x, c, ada_w, ada_b, norm_mix_g, norm_ffn_g, rw_mu, rw_w_rkv, rw_w0, rw_w1, rw_w2, rw_a0, rw_a1, rw_a2,
           rw_g1, rw_g2, rw_k_k, rw_k_a, rw_r_k, rw_gn_g, rw_gn_b, rw_w_o, ada_kv_w, ada_kv_b, norm_kv_g, w_kv,
           df_w_q, df_lq1, df_lk1, df_lq2, df_lk2, df_subln_g, df_w_o, moe_w_rg, moe_w_re, moe_w_gate, moe_w_up,
           moe_w_down, final_g):
    B, T, D = x.shape
    c_pad = jnp.zeros((SUBLANES, D), F32).at[:B].set(c)
    mod = _ada(c_pad, ada_w, ada_b, 6 * D // 4)[:, :B]
    mod_kv = _ada(c_pad, ada_kv_w[None], ada_kv_b[None], D)[0, :B]
    bf = lambda w: w.astype(BF16)
    row = lambda v: v.reshape(1, -1)

    for l in range(DEPTH):
        sh_m, sc_m, g_m, sh_f, sc_f, g_f = jnp.split(mod[l], 6, axis=-1)
        mod_m = jnp.stack([sh_m, sc_m], axis=1)
        mod_f = jnp.stack([sh_f, sc_f], axis=1)
        if l < N_A_LAYERS:
            i = l
            vec = jnp.stack([rw_w0[i], rw_a0[i], rw_k_k[i], rw_k_a[i]], axis=0)
            r, k, v, lw, kk, al, gate = _rwkv_proj(
                x, mod_m, row(norm_mix_g[l]), rw_mu[i], bf(rw_w_rkv[i]), bf(rw_w1[i]), bf(rw_w2[i]),
                bf(rw_a1[i]), bf(rw_a2[i]), bf(rw_g1[i]), bf(rw_g2[i]), vec)
            pvec = jnp.stack([rw_r_k[i].reshape(-1), rw_gn_g[i], rw_gn_b[i]], axis=0)
            y = _rwkv_scan(r, k, v, lw, kk, al, pvec)
            w_o = bf(rw_w_o[i])
        else:
            j = l - N_A_LAYERS
            q_proj = (mod_m, row(norm_mix_g[l]), bf(df_w_q[j]))
            if l == N_A_LAYERS:
                sh_kv, sc_kv = jnp.split(mod_kv, 2, axis=-1)
                q, kv = _norm_mm(x, [q_proj, (jnp.stack([sh_kv, sc_kv], axis=1), row(norm_kv_g), bf(w_kv))], BF16)
            else:
                q, = _norm_mm(x, [q_proj], BF16)
            lambda_init = 0.8 - 0.6 * math.exp(-0.3 * l)
            lam_vecs = jnp.stack([df_lq1[j], df_lk1[j], df_lq2[j], df_lk2[j]], axis=0)
            y, gate = _diff_attn(q, kv, lam_vecs, row(df_subln_g[j]), lambda_init), None
            w_o = bf(df_w_o[j])
        w_r = jnp.concatenate([moe_w_rg[l], moe_w_re[l], jnp.zeros((D, LANES - N_GROUPS - N_EXPERTS), F32)], axis=1)
        w_r_hi = bf(w_r)
        w_cat = jnp.concatenate([w_r_hi, bf(w_r - w_r_hi.astype(F32))], axis=1)
        x, info, cnt = _proj_res_route(y, gate, x, g_m[:, None, :], w_o, mod_f, row(norm_ffn_g[l]), w_cat)
        x = _moe_layer(x, info, cnt, mod_f, g_f[:, None, :], row(norm_ffn_g[l]), moe_w_gate, moe_w_up,
                       moe_w_down, l, row(final_g), final_norm=(l == DEPTH - 1))
    return x
```

```python
import functools
import math

import jax
import jax.numpy as jnp
from jax import lax
from jax.experimental import pallas as pl
from jax.experimental.pallas import tpu as pltpu

F32 = jnp.float32
BF16 = jnp.bfloat16

D_MODEL = 1024
DEPTH = 2
N_A_LAYERS = DEPTH // 2
RWKV_HEAD = 64
RWKV_HEADS = D_MODEL // RWKV_HEAD
RWKV_GN_EPS = 64e-5
DIFF_QK_DIM = 64
DIFF_V_DIM = 2 * DIFF_QK_DIM
DIFF_HEADS = D_MODEL // DIFF_V_DIM
SUBLN_EPS = 1e-5
N_GROUPS = 4
EXPERTS_PER_GROUP = 8
N_EXPERTS = N_GROUPS * EXPERTS_PER_GROUP
EXPERT_FF = 512
RMS_EPS = 1e-6

LANES = 128
SUBLANES = 8
VMEM_LIMIT_BYTES = 56 * 1024 * 1024

SCAN_CHUNK = 64
SCAN_BASE_BLOCK = 8
SCAN_CHUNKS_PER_STEP = 2
PAIR = 2 * RWKV_HEAD
PROJ_TM = 512
DENSE_TM = 512
ATT_TQ = 512
ATT_HB = 2
MOE_TM = 256
RANK_TB = 1024
ROW_TB = 256
assert SCAN_CHUNK == RWKV_HEAD and PAIR == LANES


def _cparams(*sem):
    return pltpu.CompilerParams(dimension_semantics=sem, vmem_limit_bytes=VMEM_LIMIT_BYTES)


def _dot(a, b):
    return jnp.dot(a.astype(BF16), b.astype(BF16), preferred_element_type=F32)


def _split(x):
    hi = x.astype(BF16)
    lo = (x - hi.astype(F32)).astype(BF16)
    return hi, lo


def _dot3(a, b):
    ah, al = _split(a)
    bh, bl = _split(b)
    d = functools.partial(jnp.dot, preferred_element_type=F32)
    return d(ah, bh) + d(ah, bl) + d(al, bh)


def _dot_hl(a, b_exact):
    ah, al = _split(a)
    d = functools.partial(jnp.dot, preferred_element_type=F32)
    return d(ah, b_exact) + d(al, b_exact)


def _rms_mod(x, g, shift, scale):
    ms = jnp.mean(x * x, axis=-1, keepdims=True)
    return (x * lax.rsqrt(ms + RMS_EPS) * g) * (1.0 + scale) + shift


def _ada_kernel(c_ref, w_ref, b_ref, o_ref):
    c = c_ref[...]
    ca = c * jax.nn.sigmoid(c)
    o_ref[...] = _dot3(ca, w_ref[...]) + b_ref[...]


def _ada(c_pad, w, b, tn):
    L, D, N = w.shape
    return pl.pallas_call(
        _ada_kernel,
        out_shape=jax.ShapeDtypeStruct((L, SUBLANES, N), F32),
        grid=(L, N // tn),
        in_specs=[
            pl.BlockSpec((SUBLANES, D), lambda l, j: (0, 0)),
            pl.BlockSpec((None, D, tn), lambda l, j: (l, 0, j)),
            pl.BlockSpec((None, 1, tn), lambda l, j: (l, 0, j)),
        ],
        out_specs=pl.BlockSpec((None, SUBLANES, tn), lambda l, j: (l, 0, j)),
        compiler_params=_cparams("arbitrary", "arbitrary"),
        name="ada_mod",
    )(c_pad, w, b.reshape(L, 1, N))


def _rwkv_proj_kernel(x_ref, xp_ref, mod_ref, g_ref, mu_ref, wrkv_ref, w1_ref, w2_ref, a1_ref, a2_ref,
                      g1_ref, g2_ref, vec_ref, r_ref, k_ref, v_ref, lw_ref, kk_ref, al_ref, gate_ref):
    i = pl.program_id(1)
    g = g_ref[...]
    shift, scale = mod_ref[0, 0:1, :], mod_ref[0, 1:2, :]
    h = _rms_mod(x_ref[0], g, shift, scale)
    hp = _rms_mod(xp_ref[0, SUBLANES - 1:SUBLANES, :], g, shift, scale)
    hp = jnp.where(i == 0, 0.0, hp)
    row = lax.broadcasted_iota(jnp.int32, h.shape, 0)
    h_prev = jnp.where(row == 0, hp, pltpu.roll(h, 1, axis=0))
    xx = h_prev - h
    mu = mu_ref[...]
    xs = [(h + xx * mu[j:j + 1, :]).astype(BF16) for j in range(6)]
    w0, a0, k_k, k_a = (vec_ref[j:j + 1, :] for j in range(4))
    d = functools.partial(jnp.dot, preferred_element_type=F32)
    r = d(xs[0], wrkv_ref[0])
    k = d(xs[1], wrkv_ref[1])
    v = d(xs[2], wrkv_ref[2])
    z = w0 + _dot(jnp.tanh(d(xs[3], w1_ref[...])), w2_ref[...])
    lw = (-math.exp(-0.5)) * jax.nn.sigmoid(z)
    a = jax.nn.sigmoid(a0 + _dot(d(xs[4], a1_ref[...]), a2_ref[...]))
    gate = _dot(jax.nn.sigmoid(d(xs[5], g1_ref[...])), g2_ref[...])
    r_ref[0] = r.astype(BF16)
    k_ref[0] = (k * (1.0 + (a - 1.0) * k_a)).astype(BF16)
    v_ref[0] = v.astype(BF16)
    lw_ref[0] = lw
    kk_ref[0] = (k * k_k).astype(BF16)
    al_ref[0] = a.astype(BF16)
    gate_ref[0] = gate.astype(BF16)


def _rwkv_proj(x, mod2, g, mu, wrkv, w1, w2, a1, a2, g1, g2, vec):
    B, T, D = x.shape
    tm = PROJ_TM
    const2 = lambda b, i: (0, 0)
    const3 = lambda b, i: (0, 0, 0)
    act = pl.BlockSpec((1, tm, D), lambda b, i: (b, i, 0))
    n_sub = tm // SUBLANES
    return pl.pallas_call(
        _rwkv_proj_kernel,
        out_shape=[jax.ShapeDtypeStruct((B, T, D), F32 if n == 3 else BF16) for n in range(7)],
        grid=(B, T // tm),
        in_specs=[
            act,
            pl.BlockSpec((1, SUBLANES, D), lambda b, i: (b, jnp.maximum(i * n_sub - 1, 0), 0)),
            pl.BlockSpec((1, 2, D), lambda b, i: (b, 0, 0)),
            pl.BlockSpec((1, D), const2),
            pl.BlockSpec((6, D), const2),
            pl.BlockSpec((3, D, D), const3),
            pl.BlockSpec(w1.shape, const2), pl.BlockSpec(w2.shape, const2),
            pl.BlockSpec(a1.shape, const2), pl.BlockSpec(a2.shape, const2),
            pl.BlockSpec(g1.shape, const2), pl.BlockSpec(g2.shape, const2),
            pl.BlockSpec((4, D), const2),
        ],
        out_specs=[act] * 7,
        compiler_params=_cparams("arbitrary", "arbitrary"),
        name="rwkv_proj",
    )(x, x, mod2, g, mu, wrkv, w1, w2, a1, a2, g1, g2, vec)


def _rwkv_scan_kernel(r_ref, k_ref, v_ref, lw_ref, kk_ref, al_ref, pv_ref, y_ref, h_ref):
    C = SCAN_CHUNK
    P2 = 2 * C

    @pl.when(pl.program_id(1) == 0)
    def _():
        h_ref[...] = jnp.zeros_like(h_ref)

    lane = lax.broadcasted_iota(jnp.int32, (1, PAIR), 1)
    m_left = (lane < RWKV_HEAD).astype(F32)
    m_right = 1.0 - m_left
    ri = lax.broadcasted_iota(jnp.int32, (P2, P2), 0)
    ci = lax.broadcasted_iota(jnp.int32, (P2, P2), 1)
    same = (ri >= C) == (ci >= C)
    strict = same & (ri > ci)
    incl = same & (ri >= ci)
    eye = ri == ci
    block_ones = same.astype(BF16)
    tri = (lax.broadcasted_iota(jnp.int32, (C, C), 0) >= lax.broadcasted_iota(jnp.int32, (C, C), 1)).astype(BF16)

    def stack(x):
        return jnp.concatenate([x * m_left, x * m_right], axis=0)

    def head_sums(x):
        s_left = jnp.sum(x * m_left, axis=-1, keepdims=True)
        s_right = jnp.sum(x * m_right, axis=-1, keepdims=True)
        return jnp.where(lane < RWKV_HEAD, s_left, s_right)

    inv_n = 1.0 / RWKV_HEAD
    dd = functools.partial(jnp.dot, preferred_element_type=F32)
    n_pairs = RWKV_HEADS // 2
    units = [(ch, p) for ch in range(SCAN_CHUNKS_PER_STEP) for p in range(n_pairs)]
    idx = [(slice(ch * C, (ch + 1) * C), slice(p * PAIR, (p + 1) * PAIR)) for ch, p in units]
    U = range(len(units))
    ld = lambda ref, rs, sl: ref[0, rs, sl].astype(F32)
    kkr = [ld(kk_ref, rs, sl) for rs, sl in idx]
    ss = [head_sums(x * x) for x in kkr]
    lws = [lw_ref[0, rs, sl] for rs, sl in idx]
    Ls = []
    for lw in lws:
        l_hi, l_lo = _split(lw)
        cs = dd(tri, jnp.concatenate([l_hi, l_lo], axis=1))
        Ls.append(cs[:, :PAIR] + cs[:, PAIR:])
    lhs_g, rhs_g, bk_hat, vs, at32, rt32, dec_end = [], [], [], [], [], [], []
    for u in U:
        rs, sl = idx[u]
        L, lw = Ls[u], lws[u]
        kk = kkr[u] * lax.rsqrt(jnp.maximum(ss[u], 1e-24))
        b_vec = kk * ld(al_ref, rs, sl)
        k = ld(k_ref, rs, sl)
        LC = L[C - 1:C, :]
        e_neg = jnp.exp(-L)
        e_end = jnp.exp(LC - L)
        At = stack(-kk * jnp.exp(L - lw))
        Rt = stack(ld(r_ref, rs, sl) * jnp.exp(L))
        at32.append(At)
        rt32.append(Rt)
        lhs_g.append(jnp.concatenate([At, Rt], axis=0).astype(BF16))
        rhs_g.append(jnp.concatenate([stack(b_vec * e_neg), stack(k * e_neg)], axis=0).astype(BF16))
        bk_hat.append(jnp.concatenate([stack(b_vec * e_end), stack(k * e_end)], axis=0))
        vs.append(stack(ld(v_ref, rs, sl)).astype(BF16))
        dec_end.append(jnp.exp(LC))
    G = [lax.dot_general(lhs_g[u], rhs_g[u], (((1,), (1,)), ((), ())), preferred_element_type=F32) for u in U]
    A_ak = [jnp.where(strict, G[u][:P2, P2:], 0.0).astype(BF16) for u in U]
    A_r = [jnp.concatenate([jnp.where(incl, G[u][P2:, :P2], 0.0), jnp.where(incl, G[u][P2:, P2:], 0.0)],
                           axis=1).astype(BF16) for u in U]
    W = [dd(A_ak[u], vs[u]) for u in U]
    bsz = lambda b: (ri >> int(math.log2(b))) == (ci >> int(math.log2(b)))
    b8 = bsz(SCAN_BASE_BLOCK)
    D1 = [jnp.where(strict & b8, G[u][:P2, :P2], 0.0).astype(BF16) for u in U]
    D2 = [dd(D1[u], D1[u]).astype(BF16) for u in U]
    D4 = [dd(D2[u], D2[u]).astype(BF16) for u in U]
    eye_f = eye.astype(F32)
    P1 = [eye_f + D1[u].astype(F32) + D2[u].astype(F32) + dd(D1[u], D2[u]) for u in U]
    Tm = [P1[u] + dd(P1[u].astype(BF16), D4[u]) for u in U]
    blk = SCAN_BASE_BLOCK
    while blk < C:
        off = strict & bsz(2 * blk) & ~bsz(blk)
        Mo = [jnp.where(off, G[u][:P2, :P2], 0.0).astype(BF16) for u in U]
        Tb = [Tm[u].astype(BF16) for u in U]
        TM = [dd(Tb[u], Mo[u]).astype(BF16) for u in U]
        Tm = [Tm[u] + dd(TM[u], Tb[u]) for u in U]
        blk *= 2
    Z = [dd(Tm[u].astype(BF16), jnp.concatenate([at32[u], W[u]], axis=1).astype(BF16)) for u in U]
    rhs = [jnp.concatenate([Z[u].astype(BF16), jnp.concatenate([jnp.zeros_like(vs[u]), vs[u]], axis=1)], axis=0)
           for u in U]
    o6 = [dd(A_r[u], rhs[u]) for u in U]
    o7 = [dd(bk_hat[u].T.astype(BF16), rhs[u]) for u in U]
    H = [h_ref[p] for p in range(n_pairs)]
    Y = [None] * len(units)
    for u in U:
        p = units[u][1]
        Hb = H[p].astype(BF16)
        Y[u] = dd((rt32[u] + o6[u][:, :PAIR]).astype(BF16), Hb) + o6[u][:, PAIR:]
        Mbd = o7[u][:, :PAIR] + jnp.where(eye, dec_end[u], 0.0)
        H[p] = dd(Mbd.astype(BF16), Hb) + o7[u][:, PAIR:]
    for p in range(n_pairs):
        h_ref[p] = H[p]
    ys = [Y[u][:C] + Y[u][C:] for u in U]
    rk = [ld(r_ref, rs, sl) * ld(k_ref, rs, sl) * pv_ref[0:1, sl] for rs, sl in idx]
    st1 = [head_sums(jnp.concatenate([ys[u], rk[u]], axis=0)) for u in U]
    yc = [ys[u] - st1[u][:C] * inv_n for u in U]
    var = [head_sums(yc[u] * yc[u]) * inv_n for u in U]
    for u in U:
        rs, sl = idx[u]
        bonus = st1[u][C:] * ld(v_ref, rs, sl)
        y_ref[0, rs, sl] = (yc[u] * lax.rsqrt(var[u] + RWKV_GN_EPS) * pv_ref[1:2, sl] + pv_ref[2:3, sl]
                            + bonus).astype(y_ref.dtype)


def _rwkv_scan(r, k, v, lw, kk, al, pvec):
    B, T, D = r.shape
    rows = SCAN_CHUNK * SCAN_CHUNKS_PER_STEP
    act = pl.BlockSpec((1, rows, D), lambda b, c: (b, c, 0))
    return pl.pallas_call(
        _rwkv_scan_kernel,
        out_shape=jax.ShapeDtypeStruct((B, T, D), BF16),
        grid=(B, T // rows),
        in_specs=[act] * 6 + [pl.BlockSpec((3, D), lambda b, c: (0, 0))],
        out_specs=act,
        scratch_shapes=[pltpu.VMEM((RWKV_HEADS // 2, PAIR, PAIR), F32)],
        compiler_params=_cparams("arbitrary", "arbitrary"),
        name="rwkv_scan",
    )(r, k, v, lw, kk, al, pvec)


def _proj_res_route_kernel(*refs, has_gate):
    if has_gate:
        y_ref, g_ref, x_ref, gm_ref, w_ref, modf_ref, gf_ref, wr_ref, o_ref, info_ref, cnt_ref = refs
        y = y_ref[0].astype(F32) * g_ref[0].astype(F32)
    else:
        y_ref, x_ref, gm_ref, w_ref, modf_ref, gf_ref, wr_ref, o_ref, info_ref, cnt_ref = refs
        y = y_ref[0]
    x_new =x_ref[0] + gm_ref[0] * jnp.dot(y.astype(BF16), w_ref[...], preferred_element_type=F32)
    o_ref[0] = x_new
    h = _rms_mod(x_new, gf_ref[...], modf_ref[0, 0:1, :], modf_ref[0, 1:2, :])
    first = (pl.program_id(0) == 0) & (pl.program_id(1) == 0)
    _route(h, wr_ref[...], info_ref, cnt_ref, first)


def _proj_res_route(y, g, x, gm, w, mod_f, norm_f, w_cat):
    B, T, D = x.shape
    tm = DENSE_TM
    per_b = T // tm
    act =pl.BlockSpec((1, tm, D), lambda b, i: (b, i, 0))
    const = lambda b, i: (0, 0)
    flat = lambda b, i: (b * per_b + i, 0)
    ins = [y] + ([g] if g is not None else []) + [x, gm, w, mod_f, norm_f, w_cat]
    specs = [act] * (len(ins) - 5) + [pl.BlockSpec((1, 1, D), lambda b, i: (b, 0, 0)), pl.BlockSpec((D, D), const),
                                     pl.BlockSpec((1, 2, D), lambda b, i: (b, 0, 0)), pl.BlockSpec((1, D), const),
                                     pl.BlockSpec((D, 2 * LANES), const)]
    return pl.pallas_call(
        functools.partial(_proj_res_route_kernel, has_gate=g is not None),
        out_shape=[jax.ShapeDtypeStruct((B, T, D), F32), jax.ShapeDtypeStruct((B * T, LANES), F32),
                   jax.ShapeDtypeStruct((SUBLANES, LANES), F32)],
        grid=(B, per_b),
        in_specs=specs,
        out_specs=[act, pl.BlockSpec((tm, LANES), flat), pl.BlockSpec((SUBLANES, LANES), const)],
        compiler_params=_cparams("arbitrary", "arbitrary"),
        name="proj_res_route",
    )(*ins)


def _norm_mm_kernel(*refs, n_proj):
    x_ref = refs[0]
    ins, outs = refs[1:1 + 3 * n_proj], refs[1 + 3 * n_proj:]
    x = x_ref[0]
    xn = x * lax.rsqrt(jnp.mean(x * x, axis=-1, keepdims=True) + RMS_EPS)
    for p in range(n_proj):
        mod_ref, g_ref, w_ref = ins[3 * p:3 * p + 3]
        h = (xn * g_ref[...]) * (1.0 + mod_ref[0, 1:2, :]) + mod_ref[0, 0:1, :]
        outs[p][0] = jnp.dot(h.astype(BF16), w_ref[...], preferred_element_type=F32).astype(outs[p].dtype)


def _norm_mm(x, projs, out_dtype):
    B, T, D = x.shape
    tm = DENSE_TM
    in_specs = [pl.BlockSpec((1, tm, D), lambda b, i: (b, i, 0))]
    args = [x]
    for mod2, g, w in projs:
        in_specs += [pl.BlockSpec((1, 2, D), lambda b, i: (b, 0, 0)), pl.BlockSpec((1, D), lambda b, i: (0, 0)),
                     pl.BlockSpec(w.shape, lambda b, i: (0, 0))]
        args += [mod2, g, w]
    return pl.pallas_call(
        functools.partial(_norm_mm_kernel, n_proj=len(projs)),
        out_shape=[jax.ShapeDtypeStruct((B, T, w.shape[1]), out_dtype) for _, _, w in projs],
        grid=(B, T // tm),
        in_specs=in_specs,
        out_specs=[pl.BlockSpec((1, tm, w.shape[1]), lambda b, i: (b, i, 0)) for _, _, w in projs],
        compiler_params=_cparams("arbitrary", "arbitrary"),
        name="norm_mm",
    )(*args)


def _diff_attn_kernel(q_ref, k_ref, v_ref, lam_ref, sg_ref, o_ref, m_ref, acc_ref, s_ref, *, lambda_init):
    tq, HB, dv = ATT_TQ, ATT_HB, DIFF_V_DIM
    qi = pl.program_id(2)
    heads = range(HB)
    hs = [slice(h * dv, (h + 1) * dv) for h in heads]
    lane = lax.broadcasted_iota(jnp.int32, (1, dv), 1)
    m_left = (lane < DIFF_QK_DIM).astype(F32)
    qs = []
    for h in heads:
        q = q_ref[0, :, hs[h]].astype(F32) * (DIFF_QK_DIM ** -0.5 * math.log2(math.e))
        qs.append(jnp.concatenate([q * m_left, q * (1.0 - m_left)], axis=0).astype(BF16))
    ones_col = jnp.ones((tq, dv), BF16)
    causal = (lax.broadcasted_iota(jnp.int32, (2 * tq, tq), 1)
              <= lax.broadcasted_iota(jnp.int32, (2 * tq, tq), 0) % tq)

    def key_rows(j):
        return pl.ds(pl.multiple_of(j * tq, tq), tq)

    def scores_into(slot, j):
        for h in heads:
            s_ref[slot, h] = lax.dot_general(qs[h], k_ref[0, key_rows(j), hs[h]], (((1,), (1,)), ((), ())),
                                             preferred_element_type=F32)

    def block(slot, j, first, prefetch):
        if prefetch is not None:
            scores_into(1 - slot, prefetch)
        for h in heads:
            s = s_ref[slot, h]
            if first:
                s = jnp.where(causal, s, -jnp.inf)
                m_new = jnp.broadcast_to(jnp.max(s, axis=-1, keepdims=True), (2 * tq, LANES))
            else:
                m_old = m_ref[h]
                m_new = jnp.maximum(m_old, jnp.max(s, axis=-1, keepdims=True))
            m_ref[h] = m_new
            p = jnp.exp2(s - jnp.concatenate([m_new] * (tq // LANES), axis=1)).astype(BF16)
            pv = jnp.dot(p, jnp.concatenate([v_ref[0, key_rows(j), hs[h]], ones_col], axis=1),
                         preferred_element_type=F32)
            if first:
                acc_ref[h] = pv
            else:
                alpha = jnp.exp2(m_old - m_new)
                acc_ref[h] = acc_ref[h] * jnp.concatenate([alpha, alpha], axis=1) + pv

    scores_into(0, qi)
    block(0, qi, True, 0)

    def pair(u, c):
        last = jnp.maximum(qi - 1, 0)
        block(1, 2 * u, False, jnp.minimum(2 * u + 1, last))
        block(0, 2 * u + 1, False, jnp.minimum(2 * u + 2, last))
        return c

    lax.fori_loop(0, qi // 2, pair, 0)

    @pl.when(qi % 2 == 1)
    def _():
        block(1, qi - 1, False, None)

    lv = lam_ref[...]
    lam = (jnp.exp(jnp.sum(lv[0:1] * lv[1:2], axis=-1, keepdims=True))
           - jnp.exp(jnp.sum(lv[2:3] * lv[3:4], axis=-1, keepdims=True)) + lambda_init)
    for h in heads:
        acc = acc_ref[h]
        o = acc[:, :dv] / acc[:, dv:]
        o = o[:tq] - lam * o[tq:]
        ms = jnp.mean(o * o, axis=-1, keepdims=True)
        o_ref[0, :, hs[h]] = (o * lax.rsqrt(ms + SUBLN_EPS) * sg_ref[...] * (1.0 - lambda_init)).astype(o_ref.dtype)


def _diff_attn(q, kv, lam_vecs, subln_g, lambda_init):
    B, T, D = q.shape
    tq, HB = ATT_TQ, ATT_HB
    n_hb = DIFF_HEADS // HB
    w = HB * DIFF_V_DIM
    return pl.pallas_call(
        functools.partial(_diff_attn_kernel, lambda_init=lambda_init),
        out_shape=jax.ShapeDtypeStruct((B, T, D), BF16),
        grid=(B, n_hb, T // tq),
        in_specs=[
            pl.BlockSpec((1, tq, w), lambda b, h, i: (b, i, h)),
            pl.BlockSpec((1, T, w), lambda b, h, i: (b, 0, h)),
            pl.BlockSpec((1, T, w), lambda b, h, i: (b, 0, n_hb + h)),
            pl.BlockSpec((4, DIFF_QK_DIM), lambda b, h, i: (0, 0)),
            pl.BlockSpec((1, DIFF_V_DIM), lambda b, h, i: (0, 0)),
        ],
        out_specs=pl.BlockSpec((1, tq, w), lambda b, h, i: (b, i, h)),
        scratch_shapes=[pltpu.VMEM((HB, 2 * tq, LANES), F32), pltpu.VMEM((HB, 2 * tq, 2 * DIFF_V_DIM), F32),
                        pltpu.VMEM((2, HB, 2 * tq, tq), F32)],
        compiler_params=_cparams("arbitrary", "arbitrary", "arbitrary"),
        name="diff_attn",
    )(q, kv, kv, lam_vecs, subln_g)


def _route(h, w_cat, info_ref, cnt_ref, first_step):
    h_hi, h_lo = _split(h)
    two = jnp.dot(h_hi, w_cat, preferred_element_type=F32)
    logit = (two[:, :LANES] + two[:, LANES:]
             + jnp.dot(h_lo, w_cat[:, :LANES], preferred_element_type=F32))
    lane_i = lax.broadcasted_iota(jnp.int32, logit.shape, 1)
    lane = lane_i.astype(F32)
    neg = -jnp.inf
    big = float(LANES)
    is_grp = lane_i < N_GROUPS
    gl = jnp.where(is_grp, logit, neg)
    gmax = jnp.max(gl, axis=-1, keepdims=True)
    gidx = jnp.min(jnp.where(gl == gmax, lane, big), axis=-1, keepdims=True)
    grp_gate = 1.0 / jnp.sum(jnp.where(is_grp, jnp.exp(logit - gmax), 0.0), axis=-1, keepdims=True)
    lo = N_GROUPS + gidx * EXPERTS_PER_GROUP
    in_grp = (lane >= lo) & (lane < lo + EXPERTS_PER_GROUP)
    el = jnp.where(in_grp, logit, neg)
    t1 = jnp.max(el, axis=-1, keepdims=True)
    i1 = jnp.min(jnp.where(el == t1, lane, big), axis=-1, keepdims=True)
    el2 = jnp.where(lane == i1, neg, el)
    t2 = jnp.max(el2, axis=-1, keepdims=True)
    i2 = jnp.min(jnp.where(el2 == t2, lane, big), axis=-1, keepdims=True)
    e21 = jnp.exp(t2 - t1)
    p1 = 1.0 / (1.0 + e21)
    w1 = grp_gate * p1
    w2 = grp_gate * (e21 * p1)
    e1 = i1 - N_GROUPS
    e2 = i2 - N_GROUPS
    info_ref[...] = jnp.where(lane_i == 0, e1, jnp.where(lane_i == 1, e2, jnp.where(lane_i == 2, w1, jnp.where(lane_i == 3, w2, 0.0))))

    @pl.when(first_step)
    def _():
        cnt_ref[...] = jnp.zeros_like(cnt_ref)

    picked = ((lane == e1) | (lane == e2)).astype(F32)
    cnt_ref[...] += jnp.broadcast_to(jnp.sum(picked, axis=0, keepdims=True), cnt_ref.shape)


def _rank_kernel(info_ref, cnt_ref, dest_ref, meta_ref, start_ref):
    i = pl.program_id(0)
    tb = info_ref.shape[0]
    lane = lax.broadcasted_iota(jnp.int32, (tb, LANES), 1)
    info = info_ref[...]
    e0 = info[:, 0:1].astype(jnp.int32)
    e1 = info[:, 1:2].astype(jnp.int32)
    o0 = (lane == e0).astype(F32)
    o1 = (lane == e1).astype(F32)
    both = o0 + o1

    @pl.when(i == 0)
    def _():
        cnt = cnt_ref[0:1, :]
        padded = jnp.floor((cnt + (MOE_TM - 1)) * (1.0 / MOE_TM)) * MOE_TM
        r = lax.broadcasted_iota(jnp.int32, (LANES, LANES), 0)
        c = lax.broadcasted_iota(jnp.int32, (LANES, LANES), 1)
        upper_strict = (r < c).astype(BF16)
        start = _dot_hl(jnp.broadcast_to(padded, (SUBLANES, LANES)), upper_strict)[0:1]
        start_ref[...] = start
        row = lax.broadcasted_iota(jnp.int32, (SUBLANES, LANES), 0)
        meta_ref[...] = jnp.where(row == 0, start + padded, start + cnt)

    r = lax.broadcasted_iota(jnp.int32, (tb, tb), 0)
    c = lax.broadcasted_iota(jnp.int32, (tb, tb), 1)
    lower_strict = (r > c).astype(BF16)
    before = jnp.dot(lower_strict, both.astype(BF16), preferred_element_type=F32) + start_ref[...]
    d0 = jnp.sum(o0 * before, axis=-1, keepdims=True)
    d1 = jnp.sum(o1 * before, axis=-1, keepdims=True)
    dest_ref[...] = jnp.where(lane == 0, d0, jnp.where(lane == 1, d1, 0.0)).astype(jnp.int32)
    start_ref[...] += jnp.sum(both, axis=0, keepdims=True)


def _rank(info, cnt):
    n_tok = info.shape[0]
    tb = RANK_TB
    n_blk = n_tok // tb
    return pl.pallas_call(
        _rank_kernel,
        out_shape=[jax.ShapeDtypeStruct((n_tok, LANES), jnp.int32), jax.ShapeDtypeStruct((SUBLANES, LANES), F32)],
        grid=(n_blk,),
        in_specs=[pl.BlockSpec((tb, LANES), lambda i: (i, 0)), pl.BlockSpec((SUBLANES, LANES), lambda i: (0, 0))],
        out_specs=[pl.BlockSpec((tb, LANES), lambda i: (i, 0)), pl.BlockSpec((SUBLANES, LANES), lambda i: (0, 0))],
        scratch_shapes=[pltpu.VMEM((1, LANES), F32)],
        compiler_params=_cparams("arbitrary"),
        name="moe_rank",
    )(info, cnt)


assert D_MODEL == SUBLANES * LANES


def _tile_rows_store(ref, x):
    n = x.shape[0]
    for s in range(SUBLANES):
        ref[pl.ds(s, n, stride=SUBLANES), :] = x[:, s * LANES:(s + 1) * LANES]


def _tile_rows_load(ref, n):
    return jnp.concatenate([ref[pl.ds(s, n, stride=SUBLANES), :] for s in range(SUBLANES)], axis=1)


def _row_copy(src_ref, s, dst_ref, d, sem):
    rows = lambda r: pl.ds(pl.multiple_of(r * SUBLANES, SUBLANES), SUBLANES)
    return pltpu.make_async_copy(src_ref.at[rows(s)], dst_ref.at[rows(d)], sem)


def _zero_padding_rows(seg_ref, xs_ref, zero_ref, zsem, *, start):
    def copy(first_row, n_rows):
        cp = pltpu.make_async_copy(zero_ref.at[pl.ds(0, n_rows * SUBLANES)],
                                   xs_ref.at[pl.ds(pl.multiple_of(first_row * SUBLANES, SUBLANES), n_rows * SUBLANES)],
                                   zsem)
        cp.start(priority=1) if start else cp.wait()

    for e in range(N_EXPERTS):
        lo, n = seg_ref[1, e], seg_ref[0, e] - seg_ref[1, e]
        bit = MOE_TM // 2
        while bit >= 1:
            @pl.when((n & bit) != 0)
            def _(lo=lo, n=n, bit=bit):
                copy(lo + (n & ~(2 * bit - 1)), bit)
            bit //= 2
    n_total = xs_ref.shape[0] // SUBLANES

    def tail(j, c):
        copy(seg_ref[0, N_EXPERTS - 1] + j * MOE_TM, MOE_TM)
        return c

    lax.fori_loop(0, (n_total - seg_ref[0, N_EXPERTS - 1]) // MOE_TM, tail, 0)


def _dispatch_kernel(dest_ref, seg_ref, x_ref, mod_ref, g_ref, xs_ref, h_ref, zero_ref, sem, zsem):
    tb = x_ref.shape[0]
    i = pl.program_id(0)
    last = pl.num_programs(0) - 1
    h = _rms_mod(x_ref[...], g_ref[...], mod_ref[0, 0:1, :], mod_ref[0, 1:2, :])

    @pl.when(i == 0)
    def _():
        zero_ref[...] = jnp.zeros_like(zero_ref)
        _zero_padding_rows(seg_ref, xs_ref, zero_ref, zsem, start=True)

    @pl.when(i == last)
    def _():
        _zero_padding_rows(seg_ref, xs_ref, zero_ref, zsem, start=False)

    def drain(slot):
        def wait(j, c):
            _row_copy(h_ref.at[slot], 0, xs_ref, 0, sem.at[slot]).wait()
            return c
        lax.fori_loop(0, 2 * tb, wait, 0, unroll=8)

    for slot in range(2):
        @pl.when(i % 2 == slot)
        def _(slot=slot):
            @pl.when(i >= 2)
            def _():
                drain(slot)

            _tile_rows_store(h_ref.at[slot], h)

            def start(j, c):
                _row_copy(h_ref.at[slot], j, xs_ref, dest_ref[0, 0, 2 * j], sem.at[slot]).start(priority=0)
                _row_copy(h_ref.at[slot], j, xs_ref, dest_ref[0, 0, 2 * j + 1], sem.at[slot]).start(priority=1)
                return c

            lax.fori_loop(0, tb, start, 0, unroll=8)

            @pl.when(i == last)
            def _():
                drain(slot)

                @pl.when(i >= 1)
                def _():
                    drain(1 - slot)


def _dispatch(dest3, seg, x2, mod2, g, n_rows, T):
    n_tok, D = x2.shape
    tb = ROW_TB
    per_b = T // tb
    return pl.pallas_call(
        _dispatch_kernel,
        out_shape=jax.ShapeDtypeStruct((n_rows * SUBLANES, LANES), F32),
        grid=(n_tok // tb,),
        in_specs=[
            pl.BlockSpec((1, 1, 2 * tb), lambda i: (i, 0, 0), memory_space=pltpu.SMEM),
            pl.BlockSpec(memory_space=pltpu.SMEM),
            pl.BlockSpec((tb, D), lambda i: (i, 0)),
            pl.BlockSpec((1, 2, D), lambda i: (i // per_b, 0, 0)),
            pl.BlockSpec((1, D), lambda i: (0, 0)),
        ],
        out_specs=pl.BlockSpec(memory_space=pl.ANY),
        scratch_shapes=[pltpu.VMEM((2, tb * SUBLANES, LANES), F32), pltpu.VMEM((MOE_TM * SUBLANES, LANES), F32),
                        pltpu.SemaphoreType.DMA((2,)), pltpu.SemaphoreType.DMA(())],
        compiler_params=_cparams("arbitrary"),
        name="moe_dispatch",
    )(dest3, seg, x2, mod2, g)


def _expert_kernel(be_ref, nb_ref, nxt_ref, xs_ref, wg_hbm, wu_hbm, wd_hbm, ys_ref, stage_g, stage_u, stage_d,
                   wgb, wub, wdb, sem, *, layer):
    i = pl.program_id(0)
    e = be_ref[i]
    changed = (i == 0) | (e != be_ref[jnp.maximum(i - 1, 0)])

    def fetch(ex):
        return (pltpu.make_async_copy(wg_hbm.at[layer, ex], stage_g, sem.at[0]),
                pltpu.make_async_copy(wu_hbm.at[layer, ex], stage_u, sem.at[1]),
                pltpu.make_async_copy(wd_hbm.at[layer, ex], stage_d, sem.at[2]))

    @pl.when(i == 0)
    def _():
        for cp in fetch(e):
            cp.start()

    @pl.when(changed)
    def _():
        for cp in fetch(e):
            cp.wait()
        wgb[...] = stage_g[...].astype(BF16)
        wub[...] = stage_u[...].astype(BF16)
        wdb[...] = stage_d[...].astype(BF16)

        @pl.when(nxt_ref[i] >= 0)
        def _():
            for cp in fetch(nxt_ref[i]):
                cp.start()

    @pl.when(i < nb_ref[0])
    def _():
        x = _tile_rows_load(xs_ref, MOE_TM).astype(BF16)
        a = jnp.dot(x, wgb[...], preferred_element_type=F32)
        u = jnp.dot(x, wub[...], preferred_element_type=F32)
        hdn = (a * jax.nn.sigmoid(a)) * u
        _tile_rows_store(ys_ref, jnp.dot(hdn.astype(BF16), wdb[...], preferred_element_type=F32))

    @pl.when(i >= nb_ref[0])
    def _():
        ys_ref[...] = jnp.zeros_like(ys_ref)


def _experts(blk_e, n_used, next_e, xs, w_gate, w_up, w_down, layer):
    D = D_MODEL
    n_rows = xs.shape[0] // SUBLANES
    tm = MOE_TM
    FF = EXPERT_FF
    row_block = pl.BlockSpec((tm * SUBLANES, LANES), lambda i, be, nb, nx: (i, 0))
    hbm = pl.BlockSpec(memory_space=pl.ANY)
    grid_spec = pltpu.PrefetchScalarGridSpec(
        num_scalar_prefetch=3,
        grid=(n_rows // tm,),
        in_specs=[row_block, hbm, hbm, hbm],
        out_specs=row_block,
        scratch_shapes=[pltpu.VMEM((D, FF), F32), pltpu.VMEM((D, FF), F32), pltpu.VMEM((FF, D), F32),
                        pltpu.VMEM((D, FF), BF16), pltpu.VMEM((D, FF), BF16), pltpu.VMEM((FF, D), BF16),
                        pltpu.SemaphoreType.DMA((3,))],
    )
    return pl.pallas_call(
        functools.partial(_expert_kernel, layer=layer),
        out_shape=jax.ShapeDtypeStruct(xs.shape, F32),
        grid_spec=grid_spec,
        compiler_params=_cparams("arbitrary"),
        name="moe_experts",
    )(blk_e, n_used, next_e, xs, w_gate, w_up, w_down)


def _combine_kernel(dest_ref, dest_next_ref, info_ref, x_ref, gf_ref, fg_ref, ys_ref, o_ref, y_ref, sem, *,
                    final_norm):
    tb = x_ref.shape[0]
    i = pl.program_id(0)
    last = pl.num_programs(0) - 1

    def gather(d_ref, slot):
        def start(j, c):
            _row_copy(ys_ref, d_ref[0, 0, 2 * j], y_ref.at[slot, 0], j, sem.at[slot]).start(priority=0)
            _row_copy(ys_ref, d_ref[0, 0, 2 * j + 1], y_ref.at[slot, 1], j, sem.at[slot]).start(priority=1)
            return c
        lax.fori_loop(0, tb, start, 0, unroll=8)

    @pl.when(i == 0)
    def _():
        gather(dest_ref, 0)

    for slot in range(2):
        @pl.when(i % 2 == slot)
        def _(slot=slot):
            @pl.when(i < last)
            def _():
                gather(dest_next_ref, 1 - slot)

            def wait(j, c):
                _row_copy(ys_ref, 0, y_ref.at[slot, 0], 0, sem.at[slot]).wait()
                return c

            lax.fori_loop(0, 2 * tb, wait, 0, unroll=8)
            info = info_ref[...]
            moe = (info[:, 2:3] * _tile_rows_load(y_ref.at[slot, 0], tb)
                   + info[:, 3:4] * _tile_rows_load(y_ref.at[slot, 1], tb))
            out = x_ref[...] + gf_ref[0] * moe
            if final_norm:
                ms = jnp.mean(out * out, axis=-1, keepdims=True)
                out = out * lax.rsqrt(ms + RMS_EPS) * fg_ref[...]
            o_ref[...] = out


def _combine(dest3, info, x2, gf, final_g, ys, T, final_norm):
    n_tok, D = x2.shape
    tb = ROW_TB
    per_b = T // tb
    n_steps = n_tok // tb
    return pl.pallas_call(
        functools.partial(_combine_kernel, final_norm=final_norm),
        out_shape=jax.ShapeDtypeStruct((n_tok, D), F32),
        grid=(n_steps,),
        in_specs=[
            pl.BlockSpec((1, 1, 2 * tb), lambda i: (i, 0, 0), memory_space=pltpu.SMEM),
            pl.BlockSpec((1, 1, 2 * tb), lambda i: (jnp.minimum(i + 1, n_steps - 1), 0, 0), memory_space=pltpu.SMEM),
            pl.BlockSpec((tb, LANES), lambda i: (i, 0)),
            pl.BlockSpec((tb, D), lambda i: (i, 0)),
            pl.BlockSpec((1, 1, D), lambda i: (i // per_b, 0, 0)),
            pl.BlockSpec((1, D), lambda i: (0, 0)),
            pl.BlockSpec(memory_space=pl.ANY),
        ],
        out_specs=pl.BlockSpec((tb, D), lambda i: (i, 0)),
        scratch_shapes=[pltpu.VMEM((2, 2, tb * SUBLANES, LANES), F32), pltpu.SemaphoreType.DMA((2,))],
        compiler_params=_cparams("arbitrary"),
        name="moe_combine",
    )(dest3, dest3, info, x2, gf, final_g, ys)


def _moe_rows(n_tok):
    return -(-(2 * n_tok + N_EXPERTS * MOE_TM) // MOE_TM) * MOE_TM


def _moe_layer(x, info, cnt, mod_f, gf, norm_g, w_gate, w_up, w_down, layer, final_g, final_norm):
    B, T, D = x.shape
    n_tok = B * T
    x2 = x.reshape(n_tok, D)
    n_rows = _moe_rows(n_tok)
    dest, meta = _rank(info, cnt)
    seg = meta[0:2, :N_EXPERTS].astype(jnp.int32)
    pad_end = seg[0]
    n_blocks = n_rows // MOE_TM
    blk_start = jnp.arange(n_blocks, dtype=jnp.int32) * MOE_TM
    blk_e = jnp.minimum(jnp.sum(pad_end[None, :] <= blk_start[:, None], axis=1), N_EXPERTS - 1).astype(jnp.int32)
    n_used = (pad_end[N_EXPERTS - 1:] // MOE_TM).astype(jnp.int32)
    dest3 = dest[:, :2].reshape(n_tok // ROW_TB, 1, 2 * ROW_TB)
    xs = _dispatch(dest3, seg, x2, mod_f, norm_g, n_rows, T)
    seg_end = jnp.sum(blk_e[None, :] <= blk_e[:, None], axis=1)
    next_e = jnp.where(seg_end < n_blocks, blk_e[jnp.minimum(seg_end, n_blocks - 1)], -1).astype(jnp.int32)
    ys = _experts(blk_e, n_used, next_e, xs, w_gate, w_up, w_down, layer)
    out = _combine(dest3, info, x2, gf, final_g, ys, T, final_norm)
    return out.reshape(B, T, D)


def kernel(x, c, ada_w, ada_b, norm_mix_g, norm_ffn_g, rw_mu, rw_w_rkv, rw_w0, rw_w1, rw_w2, rw_a0, rw_a1, rw_a2,
           rw_g1, rw_g2, rw_k_k, rw_k_a, rw_r_k, rw_gn_g, rw_gn_b, rw_w_o, ada_kv_w, ada_kv_b, norm_kv_g, w_kv,
           df_w_q, df_lq1, df_lk1, df_lq2, df_lk2, df_subln_g, df_w_o, moe_w_rg, moe_w_re, moe_w_gate, moe_w_up,
           moe_w_down, final_g):
    B, T, D = x.shape
    c_pad = jnp.zeros((SUBLANES, D), F32).at[:B].set(c)
    mod = _ada(c_pad, ada_w, ada_b, 6 * D // 4)[:, :B]
    mod_kv = _ada(c_pad, ada_kv_w[None], ada_kv_b[None], D)[0, :B]
    bf = lambda w: w.astype(BF16)
    row = lambda v: v.reshape(1, -1)

    for l in range(DEPTH):
        sh_m, sc_m, g_m, sh_f, sc_f, g_f = jnp.split(mod[l], 6, axis=-1)
        mod_m = jnp.stack([sh_m, sc_m], axis=1)
        mod_f = jnp.stack([sh_f, sc_f], axis=1)
        if l < N_A_LAYERS:
            i = l
            vec = jnp.stack([rw_w0[i], rw_a0[i], rw_k_k[i], rw_k_a[i]], axis=0)
            r, k, v, lw, kk, al, gate = _rwkv_proj(
                x, mod_m, row(norm_mix_g[l]), rw_mu[i], bf(rw_w_rkv[i]), bf(rw_w1[i]), bf(rw_w2[i]),
                bf(rw_a1[i]), bf(rw_a2[i]), bf(rw_g1[i]), bf(rw_g2[i]), vec)
            pvec = jnp.stack([rw_r_k[i].reshape(-1), rw_gn_g[i], rw_gn_b[i]], axis=0)
            y = _rwkv_scan(r, k, v, lw, kk, al, pvec)
            w_o = bf(rw_w_o[i])
        else:
            j = l - N_A_LAYERS
            q_proj = (mod_m, row(norm_mix_g[l]), bf(df_w_q[j]))
            if l == N_A_LAYERS:
                sh_kv, sc_kv = jnp.split(mod_kv, 2, axis=-1)
                q, kv = _norm_mm(x, [q_proj, (jnp.stack([sh_kv, sc_kv], axis=1), row(norm_kv_g), bf(w_kv))], BF16)
            else:
                q, = _norm_mm(x, [q_proj], BF16)
            lambda_init = 0.8 - 0.6 * math.exp(-0.3 * l)
            lam_vecs = jnp.stack([df_lq1[j], df_lk1[j], df_lq2[j], df_lk2[j]], axis=0)
            y, gate = _diff_attn(q, kv, lam_vecs, row(df_subln_g[j]), lambda_init), None
            w_o = bf(df_w_o[j])
        w_r = jnp.concatenate([moe_w_rg[l], moe_w_re[l], jnp.zeros((D, LANES - N_GROUPS - N_EXPERTS), F32)], axis=1)
        w_r_hi = bf(w_r)
        w_cat = jnp.concatenate([w_r_hi, bf(w_r - w_r_hi.astype(F32))], axis=1)
        x, info, cnt = _proj_res_route(y, gate, x, g_m[:, None, :], w_o, mod_f, row(norm_ffn_g[l]), w_cat)
        x = _moe_layer(x, info, cnt, mod_f, g_f[:, None, :], row(norm_ffn_g[l]), moe_w_gate, moe_w_up,
                       moe_w_down, l, row(final_g), final_norm=(l == DEPTH - 1))
    return x
```

```python
import functools
import math

import jax
import jax.numpy as jnp
from jax import lax
from jax.experimental import pallas as pl
from jax.experimental.pallas import tpu as pltpu

F32 = jnp.float32
BF16 = jnp.bfloat16

D_MODEL = 1024
DEPTH = 2
N_A_LAYERS = DEPTH // 2
RWKV_HEAD = 64
RWKV_HEADS = D_MODEL // RWKV_HEAD
RWKV_GN_EPS = 64e-5
DIFF_QK_DIM = 64
DIFF_V_DIM = 2 * DIFF_QK_DIM
DIFF_HEADS = D_MODEL // DIFF_V_DIM
SUBLN_EPS = 1e-5
N_GROUPS = 4
EXPERTS_PER_GROUP = 8
N_EXPERTS = N_GROUPS * EXPERTS_PER_GROUP
EXPERT_FF = 512
RMS_EPS = 1e-6

LANES = 128
SUBLANES = 8
VMEM_LIMIT_BYTES = 56 * 1024 * 1024

SCAN_CHUNK = 64
SCAN_BASE_BLOCK = 8
SCAN_CHUNKS_PER_STEP = 2
PAIR = 2 * RWKV_HEAD
PROJ_TM = 512
DENSE_TM = 512
ATT_TQ = 512
ATT_HB = 4
MOE_TM = 256
RANK_TB = 1024
ROW_TB = 256
assert SCAN_CHUNK == RWKV_HEAD and PAIR == LANES


def _cparams(*sem):
    return pltpu.CompilerParams(dimension_semantics=sem, vmem_limit_bytes=VMEM_LIMIT_BYTES)


def _dot(a, b):
    return jnp.dot(a.astype(BF16), b.astype(BF16), preferred_element_type=F32)


def _split(x):
    hi = x.astype(BF16)
    lo = (x - hi.astype(F32)).astype(BF16)
    return hi, lo


def _dot3(a, b):
    ah, al = _split(a)
    bh, bl = _split(b)
    d = functools.partial(jnp.dot, preferred_element_type=F32)
    return d(ah, bh) + d(ah, bl) + d(al, bh)


def _dot_hl(a, b_exact):
    ah, al = _split(a)
    d = functools.partial(jnp.dot, preferred_element_type=F32)
    return d(ah, b_exact) + d(al, b_exact)


def _rms_mod(x, g, shift, scale):
    ms = jnp.mean(x * x, axis=-1, keepdims=True)
    return (x * lax.rsqrt(ms + RMS_EPS) * g) * (1.0 + scale) + shift


def _ada_kernel(c_ref, w_ref, b_ref, o_ref):
    c = c_ref[...]
    ca = c * jax.nn.sigmoid(c)
    o_ref[...] = _dot3(ca, w_ref[...]) + b_ref[...]


def _ada(c_pad, w, b, tn):
    L, D, N = w.shape
    return pl.pallas_call(
        _ada_kernel,
        out_shape=jax.ShapeDtypeStruct((L, SUBLANES, N), F32),
        grid=(L, N // tn),
        in_specs=[
            pl.BlockSpec((SUBLANES, D), lambda l, j: (0, 0)),
            pl.BlockSpec((None, D, tn), lambda l, j: (l, 0, j)),
            pl.BlockSpec((None, 1, tn), lambda l, j: (l, 0, j)),
        ],
        out_specs=pl.BlockSpec((None, SUBLANES, tn), lambda l, j: (l, 0, j)),
        compiler_params=_cparams("arbitrary", "arbitrary"),
        name="ada_mod",
    )(c_pad, w, b.reshape(L, 1, N))


def _rwkv_proj_kernel(x_ref, xp_ref, mod_ref, g_ref, mu_ref, wrkv_ref, w1_ref, w2_ref, a1_ref, a2_ref,
                      g1_ref, g2_ref, vec_ref, r_ref, k_ref, v_ref, lw_ref, kk_ref, al_ref, gate_ref):
    i = pl.program_id(1)
    g = g_ref[...]
    shift, scale = mod_ref[0, 0:1, :], mod_ref[0, 1:2, :]
    h = _rms_mod(x_ref[0], g, shift, scale)
    hp = _rms_mod(xp_ref[0, SUBLANES - 1:SUBLANES, :], g, shift, scale)
    hp = jnp.where(i == 0, 0.0, hp)
    row = lax.broadcasted_iota(jnp.int32, h.shape, 0)
    h_prev = jnp.where(row == 0, hp, pltpu.roll(h, 1, axis=0))
    xx = h_prev - h
    mu = mu_ref[...]
    xs = [(h + xx * mu[j:j + 1, :]).astype(BF16) for j in range(6)]
    w0, a0, k_k, k_a = (vec_ref[j:j + 1, :] for j in range(4))
    d = functools.partial(jnp.dot, preferred_element_type=F32)
    r = d(xs[0], wrkv_ref[0])
    k = d(xs[1], wrkv_ref[1])
    v = d(xs[2], wrkv_ref[2])
    z = w0 + _dot(jnp.tanh(d(xs[3], w1_ref[...])), w2_ref[...])
    lw = (-math.exp(-0.5)) * jax.nn.sigmoid(z)
    a = jax.nn.sigmoid(a0 + _dot(d(xs[4], a1_ref[...]), a2_ref[...]))
    gate = _dot(jax.nn.sigmoid(d(xs[5], g1_ref[...])), g2_ref[...])
    r_ref[0] = r.astype(BF16)
    k_ref[0] = (k * (1.0 + (a - 1.0) * k_a)).astype(BF16)
    v_ref[0] = v.astype(BF16)
    lw_ref[0] = lw
    kk_ref[0] = (k * k_k).astype(BF16)
    al_ref[0] = a.astype(BF16)
    gate_ref[0] = gate.astype(BF16)


def _rwkv_proj(x, mod2, g, mu, wrkv, w1, w2, a1, a2, g1, g2, vec):
    B, T, D = x.shape
    tm = PROJ_TM
    const2 = lambda b, i: (0, 0)
    const3 = lambda b, i: (0, 0, 0)
    act = pl.BlockSpec((1, tm, D), lambda b, i: (b, i, 0))
    n_sub = tm // SUBLANES
    return pl.pallas_call(
        _rwkv_proj_kernel,
        out_shape=[jax.ShapeDtypeStruct((B, T, D), F32 if n == 3 else BF16) for n in range(7)],
        grid=(B, T // tm),
        in_specs=[
            act,
            pl.BlockSpec((1, SUBLANES, D), lambda b, i: (b, jnp.maximum(i * n_sub - 1, 0), 0)),
            pl.BlockSpec((1, 2, D), lambda b, i: (b, 0, 0)),
            pl.BlockSpec((1, D), const2),
            pl.BlockSpec((6, D), const2),
            pl.BlockSpec((3, D, D), const3),
            pl.BlockSpec(w1.shape, const2), pl.BlockSpec(w2.shape, const2),
            pl.BlockSpec(a1.shape, const2), pl.BlockSpec(a2.shape, const2),
            pl.BlockSpec(g1.shape, const2), pl.BlockSpec(g2.shape, const2),
            pl.BlockSpec((4, D), const2),
        ],
        out_specs=[act] * 7,
        compiler_params=_cparams("arbitrary", "arbitrary"),
        name="rwkv_proj",
    )(x, x, mod2, g, mu, wrkv, w1, w2, a1, a2, g1, g2, vec)


def _rwkv_scan_kernel(r_ref, k_ref, v_ref, lw_ref, kk_ref, al_ref, pv_ref, y_ref, h_ref):
    C = SCAN_CHUNK
    P2 = 2 * C

    @pl.when(pl.program_id(1) == 0)
    def _():
        h_ref[...] = jnp.zeros_like(h_ref)

    lane = lax.broadcasted_iota(jnp.int32, (1, PAIR), 1)
    m_left = (lane < RWKV_HEAD).astype(F32)
    m_right = 1.0 - m_left
    ri = lax.broadcasted_iota(jnp.int32, (P2, P2), 0)
    ci = lax.broadcasted_iota(jnp.int32, (P2, P2), 1)
    same = (ri >= C) == (ci >= C)
    strict = same & (ri > ci)
    incl = same & (ri >= ci)
    eye = ri == ci
    block_ones = same.astype(BF16)
    tri = (lax.broadcasted_iota(jnp.int32, (C, C), 0) >= lax.broadcasted_iota(jnp.int32, (C, C), 1)).astype(BF16)

    def stack(x):
        return jnp.concatenate([x * m_left, x * m_right], axis=0)

    def head_sums(x):
        s_left = jnp.sum(x * m_left, axis=-1, keepdims=True)
        s_right = jnp.sum(x * m_right, axis=-1, keepdims=True)
        return jnp.where(lane < RWKV_HEAD, s_left, s_right)

    inv_n = 1.0 / RWKV_HEAD
    dd = functools.partial(jnp.dot, preferred_element_type=F32)
    n_pairs = RWKV_HEADS // 2
    units = [(ch, p) for ch in range(SCAN_CHUNKS_PER_STEP) for p in range(n_pairs)]
    idx = [(slice(ch * C, (ch + 1) * C), slice(p * PAIR, (p + 1) * PAIR)) for ch, p in units]
    U = range(len(units))
    ld = lambda ref, rs, sl: ref[0, rs, sl].astype(F32)
    kkr = [ld(kk_ref, rs, sl) for rs, sl in idx]
    ss = [head_sums(x * x) for x in kkr]
    lws = [lw_ref[0, rs, sl] for rs, sl in idx]
    Ls = []
    for lw in lws:
        l_hi, l_lo = _split(lw)
        cs = dd(tri, jnp.concatenate([l_hi, l_lo], axis=1))
        Ls.append(cs[:, :PAIR] + cs[:, PAIR:])
    lhs_g, rhs_g, bk_hat, vs, at32, rt32, dec_end = [], [], [], [], [], [], []
    for u in U:
        rs, sl = idx[u]
        L, lw = Ls[u], lws[u]
        kk = kkr[u] * lax.rsqrt(jnp.maximum(ss[u], 1e-24))
        b_vec = kk * ld(al_ref, rs, sl)
        k = ld(k_ref, rs, sl)
        LC = L[C - 1:C, :]
        e_neg = jnp.exp(-L)
        e_end = jnp.exp(LC - L)
        At = stack(-kk * jnp.exp(L - lw))
        Rt = stack(ld(r_ref, rs, sl) * jnp.exp(L))
        at32.append(At)
        rt32.append(Rt)
        lhs_g.append(jnp.concatenate([At, Rt], axis=0).astype(BF16))
        rhs_g.append(jnp.concatenate([stack(b_vec * e_neg), stack(k * e_neg)], axis=0).astype(BF16))
        bk_hat.append(jnp.concatenate([stack(b_vec * e_end), stack(k * e_end)], axis=0))
        vs.append(stack(ld(v_ref, rs, sl)).astype(BF16))
        dec_end.append(jnp.exp(LC))
    G = [lax.dot_general(lhs_g[u], rhs_g[u], (((1,), (1,)), ((), ())), preferred_element_type=F32) for u in U]
    A_ak = [jnp.where(strict, G[u][:P2, P2:], 0.0).astype(BF16) for u in U]
    A_r = [jnp.concatenate([jnp.where(incl, G[u][P2:, :P2], 0.0), jnp.where(incl, G[u][P2:, P2:], 0.0)],
                           axis=1).astype(BF16) for u in U]
    W = [dd(A_ak[u], vs[u]) for u in U]
    bsz = lambda b: (ri >> int(math.log2(b))) == (ci >> int(math.log2(b)))
    b8 = bsz(SCAN_BASE_BLOCK)
    D1 = [jnp.where(strict & b8, G[u][:P2, :P2], 0.0).astype(BF16) for u in U]
    D2 = [dd(D1[u], D1[u]).astype(BF16) for u in U]
    D4 = [dd(D2[u], D2[u]).astype(BF16) for u in U]
    eye_f = eye.astype(F32)
    P1 = [eye_f + D1[u].astype(F32) + D2[u].astype(F32) + dd(D1[u], D2[u]) for u in U]
    Tm = [P1[u] + dd(P1[u].astype(BF16), D4[u]) for u in U]
    blk = SCAN_BASE_BLOCK
    while blk < C:
        off = strict & bsz(2 * blk) & ~bsz(blk)
        Mo = [jnp.where(off, G[u][:P2, :P2], 0.0).astype(BF16) for u in U]
        Tb = [Tm[u].astype(BF16) for u in U]
        TM = [dd(Tb[u], Mo[u]).astype(BF16) for u in U]
        Tm = [Tm[u] + dd(TM[u], Tb[u]) for u in U]
        blk *= 2
    Z = [dd(Tm[u].astype(BF16), jnp.concatenate([at32[u], W[u]], axis=1).astype(BF16)) for u in U]
    rhs = [jnp.concatenate([Z[u].astype(BF16), jnp.concatenate([jnp.zeros_like(vs[u]), vs[u]], axis=1)], axis=0)
           for u in U]
    o6 = [dd(A_r[u], rhs[u]) for u in U]
    o7 = [dd(bk_hat[u].T.astype(BF16), rhs[u]) for u in U]
    H = [h_ref[p] for p in range(n_pairs)]
    Y = [None] * len(units)
    for u in U:
        p = units[u][1]
        Hb = H[p].astype(BF16)
        Y[u] = dd((rt32[u] + o6[u][:, :PAIR]).astype(BF16), Hb) + o6[u][:, PAIR:]
        Mbd = o7[u][:, :PAIR] + jnp.where(eye, dec_end[u], 0.0)
        H[p] = dd(Mbd.astype(BF16), Hb) + o7[u][:, PAIR:]
    for p in range(n_pairs):
        h_ref[p] = H[p]
    ys = [Y[u][:C] + Y[u][C:] for u in U]
    rk = [ld(r_ref, rs, sl) * ld(k_ref, rs, sl) * pv_ref[0:1, sl] for rs, sl in idx]
    st1 = [head_sums(jnp.concatenate([ys[u], rk[u]], axis=0)) for u in U]
    yc = [ys[u] - st1[u][:C] * inv_n for u in U]
    var = [head_sums(yc[u] * yc[u]) * inv_n for u in U]
    for u in U:
        rs, sl = idx[u]
        bonus = st1[u][C:] * ld(v_ref, rs, sl)
        y_ref[0, rs, sl] = (yc[u] * lax.rsqrt(var[u] + RWKV_GN_EPS) * pv_ref[1:2, sl] + pv_ref[2:3, sl]
                            + bonus).astype(y_ref.dtype)


def _rwkv_scan(r, k, v, lw, kk, al, pvec):
    B, T, D = r.shape
    rows = SCAN_CHUNK * SCAN_CHUNKS_PER_STEP
    act = pl.BlockSpec((1, rows, D), lambda b, c: (b, c, 0))
    return pl.pallas_call(
        _rwkv_scan_kernel,
        out_shape=jax.ShapeDtypeStruct((B, T, D), BF16),
        grid=(B, T // rows),
        in_specs=[act] * 6 + [pl.BlockSpec((3, D), lambda b, c: (0, 0))],
        out_specs=act,
        scratch_shapes=[pltpu.VMEM((RWKV_HEADS // 2, PAIR, PAIR), F32)],
        compiler_params=_cparams("arbitrary", "arbitrary"),
        name="rwkv_scan",
    )(r, k, v, lw, kk, al, pvec)


def _proj_res_route_kernel(*refs, has_gate):
    if has_gate:
        y_ref, g_ref, x_ref, gm_ref, w_ref, modf_ref, gf_ref, wr_ref, o_ref, info_ref, cnt_ref = refs
        y = y_ref[0].astype(F32) * g_ref[0].astype(F32)
    else:
        y_ref, x_ref, gm_ref, w_ref, modf_ref, gf_ref, wr_ref, o_ref, info_ref, cnt_ref = refs
        y = y_ref[0]
    x_new =x_ref[0] + gm_ref[0] * jnp.dot(y.astype(BF16), w_ref[...], preferred_element_type=F32)
    o_ref[0] = x_new
    h = _rms_mod(x_new, gf_ref[...], modf_ref[0, 0:1, :], modf_ref[0, 1:2, :])
    first = (pl.program_id(0) == 0) & (pl.program_id(1) == 0)
    _route(h, wr_ref[...], info_ref, cnt_ref, first)


def _proj_res_route(y, g, x, gm, w, mod_f, norm_f, w_cat):
    B, T, D = x.shape
    tm = DENSE_TM
    per_b = T // tm
    act =pl.BlockSpec((1, tm, D), lambda b, i: (b, i, 0))
    const = lambda b, i: (0, 0)
    flat = lambda b, i: (b * per_b + i, 0)
    ins = [y] + ([g] if g is not None else []) + [x, gm, w, mod_f, norm_f, w_cat]
    specs = [act] * (len(ins) - 5) + [pl.BlockSpec((1, 1, D), lambda b, i: (b, 0, 0)), pl.BlockSpec((D, D), const),
                                     pl.BlockSpec((1, 2, D), lambda b, i: (b, 0, 0)), pl.BlockSpec((1, D), const),
                                     pl.BlockSpec((D, 2 * LANES), const)]
    return pl.pallas_call(
        functools.partial(_proj_res_route_kernel, has_gate=g is not None),
        out_shape=[jax.ShapeDtypeStruct((B, T, D), F32), jax.ShapeDtypeStruct((B * T, LANES), F32),
                   jax.ShapeDtypeStruct((SUBLANES, LANES), F32)],
        grid=(B, per_b),
        in_specs=specs,
        out_specs=[act, pl.BlockSpec((tm, LANES), flat), pl.BlockSpec((SUBLANES, LANES), const)],
        compiler_params=_cparams("arbitrary", "arbitrary"),
        name="proj_res_route",
    )(*ins)


def _norm_mm_kernel(*refs, n_proj):
    x_ref = refs[0]
    ins, outs = refs[1:1 + 3 * n_proj], refs[1 + 3 * n_proj:]
    x = x_ref[0]
    xn = x * lax.rsqrt(jnp.mean(x * x, axis=-1, keepdims=True) + RMS_EPS)
    for p in range(n_proj):
        mod_ref, g_ref, w_ref = ins[3 * p:3 * p + 3]
        h = (xn * g_ref[...]) * (1.0 + mod_ref[0, 1:2, :]) + mod_ref[0, 0:1, :]
        outs[p][0] = jnp.dot(h.astype(BF16), w_ref[...], preferred_element_type=F32).astype(outs[p].dtype)


def _norm_mm(x, projs, out_dtype):
    B, T, D = x.shape
    tm = DENSE_TM
    in_specs = [pl.BlockSpec((1, tm, D), lambda b, i: (b, i, 0))]
    args = [x]
    for mod2, g, w in projs:
        in_specs += [pl.BlockSpec((1, 2, D), lambda b, i: (b, 0, 0)), pl.BlockSpec((1, D), lambda b, i: (0, 0)),
                     pl.BlockSpec(w.shape, lambda b, i: (0, 0))]
        args += [mod2, g, w]
    return pl.pallas_call(
        functools.partial(_norm_mm_kernel, n_proj=len(projs)),
        out_shape=[jax.ShapeDtypeStruct((B, T, w.shape[1]), out_dtype) for _, _, w in projs],
        grid=(B, T // tm),
        in_specs=in_specs,
        out_specs=[pl.BlockSpec((1, tm, w.shape[1]), lambda b, i: (b, i, 0)) for _, _, w in projs],
        compiler_params=_cparams("arbitrary", "arbitrary"),
        name="norm_mm",
    )(*args)


def _diff_attn_kernel(q_ref, k_ref, v_ref, lam_ref, sg_ref, o_ref, m_ref, acc_ref, s_ref, *, lambda_init):
    tq, HB, dv = ATT_TQ, ATT_HB, DIFF_V_DIM
    qi = pl.program_id(2)
    heads = range(HB)
    hs = [slice(h * dv, (h + 1) * dv) for h in heads]
    lane = lax.broadcasted_iota(jnp.int32, (1, dv), 1)
    m_left = (lane < DIFF_QK_DIM).astype(F32)
    qs = []
    for h in heads:
        q = q_ref[0, :, hs[h]].astype(F32) * (DIFF_QK_DIM ** -0.5 * math.log2(math.e))
        qs.append(jnp.concatenate([q * m_left, q * (1.0 - m_left)], axis=0).astype(BF16))
    ones_col = jnp.ones((tq, dv), BF16)
    causal = (lax.broadcasted_iota(jnp.int32, (2 * tq, tq), 1)
              <= lax.broadcasted_iota(jnp.int32, (2 * tq, tq), 0) % tq)

    def key_rows(j):
        return pl.ds(pl.multiple_of(j * tq, tq), tq)

    def scores_into(slot, j):
        for h in heads:
            s_ref[slot, h] = lax.dot_general(qs[h], k_ref[0, key_rows(j), hs[h]], (((1,), (1,)), ((), ())),
                                             preferred_element_type=F32)

    def block(slot, j, first, prefetch):
        if prefetch is not None:
            scores_into(1 - slot, prefetch)
        for h in heads:
            s = s_ref[slot, h]
            if first:
                s = jnp.where(causal, s, -jnp.inf)
                m_new = jnp.broadcast_to(jnp.max(s, axis=-1, keepdims=True), (2 * tq, LANES))
            else:
                m_old = m_ref[h]
                m_new = jnp.maximum(m_old, jnp.max(s, axis=-1, keepdims=True))
            m_ref[h] = m_new
            p = jnp.exp2(s - jnp.concatenate([m_new] * (tq // LANES), axis=1)).astype(BF16)
            pv = jnp.dot(p, jnp.concatenate([v_ref[0, key_rows(j), hs[h]], ones_col], axis=1),
                         preferred_element_type=F32)
            if first:
                acc_ref[h] = pv
            else:
                alpha = jnp.exp2(m_old - m_new)
                acc_ref[h] = acc_ref[h] * jnp.concatenate([alpha, alpha], axis=1) + pv

    scores_into(0, qi)
    block(0, qi, True, 0)

    def pair(u, c):
        last = jnp.maximum(qi - 1, 0)
        block(1, 2 * u, False, jnp.minimum(2 * u + 1, last))
        block(0, 2 * u + 1, False, jnp.minimum(2 * u + 2, last))
        return c

    lax.fori_loop(0, qi // 2, pair, 0)

    @pl.when(qi % 2 == 1)
    def _():
        block(1, qi - 1, False, None)

    lv = lam_ref[...]
    lam = (jnp.exp(jnp.sum(lv[0:1] * lv[1:2], axis=-1, keepdims=True))
           - jnp.exp(jnp.sum(lv[2:3] * lv[3:4], axis=-1, keepdims=True)) + lambda_init)
    for h in heads:
        acc = acc_ref[h]
        o = acc[:, :dv] / acc[:, dv:]
        o = o[:tq] - lam * o[tq:]
        ms = jnp.mean(o * o, axis=-1, keepdims=True)
        o_ref[0, :, hs[h]] = (o * lax.rsqrt(ms + SUBLN_EPS) * sg_ref[...] * (1.0 - lambda_init)).astype(o_ref.dtype)


def _diff_attn(q, kv, lam_vecs, subln_g, lambda_init):
    B, T, D = q.shape
    tq, HB = ATT_TQ, ATT_HB
    n_hb = DIFF_HEADS // HB
    w = HB * DIFF_V_DIM
    return pl.pallas_call(
        functools.partial(_diff_attn_kernel, lambda_init=lambda_init),
        out_shape=jax.ShapeDtypeStruct((B, T, D), BF16),
        grid=(B, n_hb, T // tq),
        in_specs=[
            pl.BlockSpec((1, tq, w), lambda b, h, i: (b, i, h)),
            pl.BlockSpec((1, T, w), lambda b, h, i: (b, 0, h)),
            pl.BlockSpec((1, T, w), lambda b, h, i: (b, 0, n_hb + h)),
            pl.BlockSpec((4, DIFF_QK_DIM), lambda b, h, i: (0, 0)),
            pl.BlockSpec((1, DIFF_V_DIM), lambda b, h, i: (0, 0)),
        ],
        out_specs=pl.BlockSpec((1, tq, w), lambda b, h, i: (b, i, h)),
        scratch_shapes=[pltpu.VMEM((HB, 2 * tq, LANES), F32), pltpu.VMEM((HB, 2 * tq, 2 * DIFF_V_DIM), F32),
                        pltpu.VMEM((2, HB, 2 * tq, tq), F32)],
        compiler_params=_cparams("arbitrary", "arbitrary", "arbitrary"),
        name="diff_attn",
    )(q, kv, kv, lam_vecs, subln_g)


def _route(h, w_cat, info_ref, cnt_ref, first_step):
    h_hi, h_lo = _split(h)
    two = jnp.dot(h_hi, w_cat, preferred_element_type=F32)
    logit = (two[:, :LANES] + two[:, LANES:]
             + jnp.dot(h_lo, w_cat[:, :LANES], preferred_element_type=F32))
    lane_i = lax.broadcasted_iota(jnp.int32, logit.shape, 1)
    lane = lane_i.astype(F32)
    neg = -jnp.inf
    big = float(LANES)
    is_grp = lane_i < N_GROUPS
    gl = jnp.where(is_grp, logit, neg)
    gmax = jnp.max(gl, axis=-1, keepdims=True)
    gidx = jnp.min(jnp.where(gl == gmax, lane, big), axis=-1, keepdims=True)
    grp_gate = 1.0 / jnp.sum(jnp.where(is_grp, jnp.exp(logit - gmax), 0.0), axis=-1, keepdims=True)
    lo = N_GROUPS + gidx * EXPERTS_PER_GROUP
    in_grp = (lane >= lo) & (lane < lo + EXPERTS_PER_GROUP)
    el = jnp.where(in_grp, logit, neg)
    t1 = jnp.max(el, axis=-1, keepdims=True)
    i1 = jnp.min(jnp.where(el == t1, lane, big), axis=-1, keepdims=True)
    el2 = jnp.where(lane == i1, neg, el)
    t2 = jnp.max(el2, axis=-1, keepdims=True)
    i2 = jnp.min(jnp.where(el2 == t2, lane, big), axis=-1, keepdims=True)
    e21 = jnp.exp(t2 - t1)
    p1 = 1.0 / (1.0 + e21)
    w1 = grp_gate * p1
    w2 = grp_gate * (e21 * p1)
    e1 = i1 - N_GROUPS
    e2 = i2 - N_GROUPS
    info_ref[...] = jnp.where(lane_i == 0, e1, jnp.where(lane_i == 1, e2, jnp.where(lane_i == 2, w1, jnp.where(lane_i == 3, w2, 0.0))))

    @pl.when(first_step)
    def _():
        cnt_ref[...] = jnp.zeros_like(cnt_ref)

    picked = ((lane == e1) | (lane == e2)).astype(F32)
    cnt_ref[...] += jnp.broadcast_to(jnp.sum(picked, axis=0, keepdims=True), cnt_ref.shape)


def _rank_kernel(info_ref, cnt_ref, dest_ref, meta_ref, start_ref):
    i = pl.program_id(0)
    tb = info_ref.shape[0]
    lane = lax.broadcasted_iota(jnp.int32, (tb, LANES), 1)
    info = info_ref[...]
    e0 = info[:, 0:1].astype(jnp.int32)
    e1 = info[:, 1:2].astype(jnp.int32)
    o0 = (lane == e0).astype(F32)
    o1 = (lane == e1).astype(F32)
    both = o0 + o1

    @pl.when(i == 0)
    def _():
        cnt = cnt_ref[0:1, :]
        padded = jnp.floor((cnt + (MOE_TM - 1)) * (1.0 / MOE_TM)) * MOE_TM
        r = lax.broadcasted_iota(jnp.int32, (LANES, LANES), 0)
        c = lax.broadcasted_iota(jnp.int32, (LANES, LANES), 1)
        upper_strict = (r < c).astype(BF16)
        start = _dot_hl(jnp.broadcast_to(padded, (SUBLANES, LANES)), upper_strict)[0:1]
        start_ref[...] = start
        row = lax.broadcasted_iota(jnp.int32, (SUBLANES, LANES), 0)
        meta_ref[...] = jnp.where(row == 0, start + padded, start + cnt)

    r = lax.broadcasted_iota(jnp.int32, (tb, tb), 0)
    c = lax.broadcasted_iota(jnp.int32, (tb, tb), 1)
    lower_strict = (r > c).astype(BF16)
    before = jnp.dot(lower_strict, both.astype(BF16), preferred_element_type=F32) + start_ref[...]
    d0 = jnp.sum(o0 * before, axis=-1, keepdims=True)
    d1 = jnp.sum(o1 * before, axis=-1, keepdims=True)
    dest_ref[...] = jnp.where(lane == 0, d0, jnp.where(lane == 1, d1, 0.0)).astype(jnp.int32)
    start_ref[...] += jnp.sum(both, axis=0, keepdims=True)


def _rank(info, cnt):
    n_tok = info.shape[0]
    tb = RANK_TB
    n_blk = n_tok // tb
    return pl.pallas_call(
        _rank_kernel,
        out_shape=[jax.ShapeDtypeStruct((n_tok, LANES), jnp.int32), jax.ShapeDtypeStruct((SUBLANES, LANES), F32)],
        grid=(n_blk,),
        in_specs=[pl.BlockSpec((tb, LANES), lambda i: (i, 0)), pl.BlockSpec((SUBLANES, LANES), lambda i: (0, 0))],
        out_specs=[pl.BlockSpec((tb, LANES), lambda i: (i, 0)), pl.BlockSpec((SUBLANES, LANES), lambda i: (0, 0))],
        scratch_shapes=[pltpu.VMEM((1, LANES), F32)],
        compiler_params=_cparams("arbitrary"),
        name="moe_rank",
    )(info, cnt)


assert D_MODEL == SUBLANES * LANES


def _tile_rows_store(ref, x):
    n = x.shape[0]
    for s in range(SUBLANES):
        ref[pl.ds(s, n, stride=SUBLANES), :] = x[:, s * LANES:(s + 1) * LANES]


def _tile_rows_load(ref, n):
    return jnp.concatenate([ref[pl.ds(s, n, stride=SUBLANES), :] for s in range(SUBLANES)], axis=1)


def _row_copy(src_ref, s, dst_ref, d, sem):
    rows = lambda r: pl.ds(pl.multiple_of(r * SUBLANES, SUBLANES), SUBLANES)
    return pltpu.make_async_copy(src_ref.at[rows(s)], dst_ref.at[rows(d)], sem)


def _zero_padding_rows(seg_ref, xs_ref, zero_ref, zsem, *, start):
    def copy(first_row, n_rows):
        cp = pltpu.make_async_copy(zero_ref.at[pl.ds(0, n_rows * SUBLANES)],
                                   xs_ref.at[pl.ds(pl.multiple_of(first_row * SUBLANES, SUBLANES), n_rows * SUBLANES)],
                                   zsem)
        cp.start() if start else cp.wait()

    for e in range(N_EXPERTS):
        lo, n = seg_ref[1, e], seg_ref[0, e] - seg_ref[1, e]
        bit = MOE_TM // 2
        while bit >= 1:
            @pl.when((n & bit) != 0)
            def _(lo=lo, n=n, bit=bit):
                copy(lo + (n & ~(2 * bit - 1)), bit)
            bit //= 2
    n_total = xs_ref.shape[0] // SUBLANES

    def tail(j, c):
        copy(seg_ref[0, N_EXPERTS - 1] + j * MOE_TM, MOE_TM)
        return c

    lax.fori_loop(0, (n_total - seg_ref[0, N_EXPERTS - 1]) // MOE_TM, tail, 0)


def _dispatch_kernel(dest_ref, seg_ref, x_ref, mod_ref, g_ref, xs_ref, h_ref, zero_ref, sem, zsem):
    tb = x_ref.shape[0]
    i = pl.program_id(0)
    last = pl.num_programs(0) - 1
    h = _rms_mod(x_ref[...], g_ref[...], mod_ref[0, 0:1, :], mod_ref[0, 1:2, :])

    @pl.when(i == 0)
    def _():
        zero_ref[...] = jnp.zeros_like(zero_ref)
        _zero_padding_rows(seg_ref, xs_ref, zero_ref, zsem, start=True)

    @pl.when(i == last)
    def _():
        _zero_padding_rows(seg_ref, xs_ref, zero_ref, zsem, start=False)

    def drain(slot):
        def wait(j, c):
            _row_copy(h_ref.at[slot], 0, xs_ref, 0, sem.at[slot]).wait()
            return c
        lax.fori_loop(0, 2 * tb, wait, 0, unroll=8)

    for slot in range(2):
        @pl.when(i % 2 == slot)
        def _(slot=slot):
            @pl.when(i >= 2)
            def _():
                drain(slot)

            _tile_rows_store(h_ref.at[slot], h)

            def start(j, c):
                _row_copy(h_ref.at[slot], j, xs_ref, dest_ref[0, 0, 2 * j], sem.at[slot]).start(priority=0)
                _row_copy(h_ref.at[slot], j, xs_ref, dest_ref[0, 0, 2 * j + 1], sem.at[slot]).start(priority=1)
                return c

            lax.fori_loop(0, tb, start, 0, unroll=8)

            @pl.when(i == last)
            def _():
                drain(slot)

                @pl.when(i >= 1)
                def _():
                    drain(1 - slot)


def _dispatch(dest3, seg, x2, mod2, g, n_rows, T):
    n_tok, D = x2.shape
    tb = ROW_TB
    per_b = T // tb
    return pl.pallas_call(
        _dispatch_kernel,
        out_shape=jax.ShapeDtypeStruct((n_rows * SUBLANES, LANES), F32),
        grid=(n_tok // tb,),
        in_specs=[
            pl.BlockSpec((1, 1, 2 * tb), lambda i: (i, 0, 0), memory_space=pltpu.SMEM),
            pl.BlockSpec(memory_space=pltpu.SMEM),
            pl.BlockSpec((tb, D), lambda i: (i, 0)),
            pl.BlockSpec((1, 2, D), lambda i: (i // per_b, 0, 0)),
            pl.BlockSpec((1, D), lambda i: (0, 0)),
        ],
        out_specs=pl.BlockSpec(memory_space=pl.ANY),
        scratch_shapes=[pltpu.VMEM((2, tb * SUBLANES, LANES), F32), pltpu.VMEM((MOE_TM * SUBLANES, LANES), F32),
                        pltpu.SemaphoreType.DMA((2,)), pltpu.SemaphoreType.DMA(())],
        compiler_params=_cparams("arbitrary"),
        name="moe_dispatch",
    )(dest3, seg, x2, mod2, g)


def _expert_kernel(be_ref, nb_ref, nxt_ref, xs_ref, wg_hbm, wu_hbm, wd_hbm, ys_ref, stage_g, stage_u, stage_d,
                   wgb, wub, wdb, sem, *, layer):
    i = pl.program_id(0)
    e = be_ref[i]
    changed = (i == 0) | (e != be_ref[jnp.maximum(i - 1, 0)])

    def fetch(ex):
        return (pltpu.make_async_copy(wg_hbm.at[layer, ex], stage_g, sem.at[0]),
                pltpu.make_async_copy(wu_hbm.at[layer, ex], stage_u, sem.at[1]),
                pltpu.make_async_copy(wd_hbm.at[layer, ex], stage_d, sem.at[2]))

    @pl.when(i == 0)
    def _():
        for cp in fetch(e):
            cp.start()

    @pl.when(changed)
    def _():
        for cp in fetch(e):
            cp.wait()
        wgb[...] = stage_g[...].astype(BF16)
        wub[...] = stage_u[...].astype(BF16)
        wdb[...] = stage_d[...].astype(BF16)

        @pl.when(nxt_ref[i] >= 0)
        def _():
            for cp in fetch(nxt_ref[i]):
                cp.start()

    @pl.when(i < nb_ref[0])
    def _():
        x = _tile_rows_load(xs_ref, MOE_TM).astype(BF16)
        a = jnp.dot(x, wgb[...], preferred_element_type=F32)
        u = jnp.dot(x, wub[...], preferred_element_type=F32)
        hdn = (a * jax.nn.sigmoid(a)) * u
        _tile_rows_store(ys_ref, jnp.dot(hdn.astype(BF16), wdb[...], preferred_element_type=F32))

    @pl.when(i >= nb_ref[0])
    def _():
        ys_ref[...] = jnp.zeros_like(ys_ref)


def _experts(blk_e, n_used, next_e, xs, w_gate, w_up, w_down, layer):
    D = D_MODEL
    n_rows = xs.shape[0] // SUBLANES
    tm = MOE_TM
    FF = EXPERT_FF
    row_block = pl.BlockSpec((tm * SUBLANES, LANES), lambda i, be, nb, nx: (i, 0))
    hbm = pl.BlockSpec(memory_space=pl.ANY)
    grid_spec = pltpu.PrefetchScalarGridSpec(
        num_scalar_prefetch=3,
        grid=(n_rows // tm,),
        in_specs=[row_block, hbm, hbm, hbm],
        out_specs=row_block,
        scratch_shapes=[pltpu.VMEM((D, FF), F32), pltpu.VMEM((D, FF), F32), pltpu.VMEM((FF, D), F32),
                        pltpu.VMEM((D, FF), BF16), pltpu.VMEM((D, FF), BF16), pltpu.VMEM((FF, D), BF16),
                        pltpu.SemaphoreType.DMA((3,))],
    )
    return pl.pallas_call(
        functools.partial(_expert_kernel, layer=layer),
        out_shape=jax.ShapeDtypeStruct(xs.shape, F32),
        grid_spec=grid_spec,
        compiler_params=_cparams("arbitrary"),
        name="moe_experts",
    )(blk_e, n_used, next_e, xs, w_gate, w_up, w_down)


def _combine_kernel(dest_ref, dest_next_ref, info_ref, x_ref, gf_ref, fg_ref, ys_ref, o_ref, y_ref, sem, *,
                    final_norm):
    tb = x_ref.shape[0]
    i = pl.program_id(0)
    last = pl.num_programs(0) - 1

    def gather(d_ref, slot):
        def start(j, c):
            _row_copy(ys_ref, d_ref[0, 0, 2 * j], y_ref.at[slot, 0], j, sem.at[slot]).start(priority=0)
            _row_copy(ys_ref, d_ref[0, 0, 2 * j + 1], y_ref.at[slot, 1], j, sem.at[slot]).start(priority=1)
            return c
        lax.fori_loop(0, tb, start, 0, unroll=8)

    @pl.when(i == 0)
    def _():
        gather(dest_ref, 0)

    for slot in range(2):
        @pl.when(i % 2 == slot)
        def _(slot=slot):
            @pl.when(i < last)
            def _():
                gather(dest_next_ref, 1 - slot)

            def wait(j, c):
                _row_copy(ys_ref, 0, y_ref.at[slot, 0], 0, sem.at[slot]).wait()
                return c

            lax.fori_loop(0, 2 * tb, wait, 0, unroll=8)
            info = info_ref[...]
            moe = (info[:, 2:3] * _tile_rows_load(y_ref.at[slot, 0], tb)
                   + info[:, 3:4] * _tile_rows_load(y_ref.at[slot, 1], tb))
            out = x_ref[...] + gf_ref[0] * moe
            if final_norm:
                ms = jnp.mean(out * out, axis=-1, keepdims=True)
                out = out * lax.rsqrt(ms + RMS_EPS) * fg_ref[...]
            o_ref[...] = out


def _combine(dest3, info, x2, gf, final_g, ys, T, final_norm):
    n_tok, D = x2.shape
    tb = ROW_TB
    per_b = T // tb
    n_steps = n_tok // tb
    return pl.pallas_call(
        functools.partial(_combine_kernel, final_norm=final_norm),
        out_shape=jax.ShapeDtypeStruct((n_tok, D), F32),
        grid=(n_steps,),
        in_specs=[
            pl.BlockSpec((1, 1, 2 * tb), lambda i: (i, 0, 0), memory_space=pltpu.SMEM),
            pl.BlockSpec((1, 1, 2 * tb), lambda i: (jnp.minimum(i + 1, n_steps - 1), 0, 0), memory_space=pltpu.SMEM),
            pl.BlockSpec((tb, LANES), lambda i: (i, 0)),
            pl.BlockSpec((tb, D), lambda i: (i, 0)),
            pl.BlockSpec((1, 1, D), lambda i: (i // per_b, 0, 0)),
            pl.BlockSpec((1, D), lambda i: (0, 0)),
            pl.BlockSpec(memory_space=pl.ANY),
        ],
        out_specs=pl.BlockSpec((tb, D), lambda i: (i, 0)),
        scratch_shapes=[pltpu.VMEM((2, 2, tb * SUBLANES, LANES), F32), pltpu.SemaphoreType.DMA((2,))],
        compiler_params=_cparams("arbitrary"),
        name="moe_combine",
    )(dest3, dest3, info, x2, gf, final_g, ys)


def _moe_rows(n_tok):
    return -(-(2 * n_tok + N_EXPERTS * MOE_TM) // MOE_TM) * MOE_TM


def _moe_layer(x, info, cnt, mod_f, gf, norm_g, w_gate, w_up, w_down, layer, final_g, final_norm):
    B, T, D = x.shape
    n_tok = B * T
    x2 = x.reshape(n_tok, D)
    n_rows = _moe_rows(n_tok)
    dest, meta = _rank(info, cnt)
    seg = meta[0:2, :N_EXPERTS].astype(jnp.int32)
    pad_end = seg[0]
    n_blocks = n_rows // MOE_TM
    blk_start = jnp.arange(n_blocks, dtype=jnp.int32) * MOE_TM
    blk_e = jnp.minimum(jnp.sum(pad_end[None, :] <= blk_start[:, None], axis=1), N_EXPERTS - 1).astype(jnp.int32)
    n_used = (pad_end[N_EXPERTS - 1:] // MOE_TM).astype(jnp.int32)
    dest3 = dest[:, :2].reshape(n_tok // ROW_TB, 1, 2 * ROW_TB)
    xs = _dispatch(dest3, seg, x2, mod_f, norm_g, n_rows, T)
    seg_end = jnp.sum(blk_e[None, :] <= blk_e[:, None], axis=1)
    next_e = jnp.where(seg_end < n_blocks, blk_e[jnp.minimum(seg_end, n_blocks - 1)], -1).astype(jnp.int32)
    ys = _experts(blk_e, n_used, next_e, xs, w_gate, w_up, w_down, layer)
    out = _combine(dest3, info, x2, gf, final_g, ys, T, final_norm)
    return out.reshape(B, T, D)


def kernel(x, c, ada_w, ada_b, norm_mix_g, norm_ffn_g, rw_mu, rw_w_rkv, rw_w0, rw_w1, rw_w2, rw_a0, rw_a1, rw_a2,
           rw_g1, rw_g2, rw_k_k, rw_k_a, rw_r_k, rw_gn_g, rw_gn_b, rw_w_o, ada_kv_w, ada_kv_b, norm_kv_g, w_kv,
           df_w_q, df_lq1, df_lk1, df_lq2, df_lk2, df_subln_g, df_w_o, moe_w_rg, moe_w_re, moe_w_gate, moe_w_up,
           moe_w_down, final_g):
    B, T, D = x.shape
    c_pad = jnp.zeros((SUBLANES, D), F32).at[:B].set(c)
    mod = _ada(c_pad, ada_w, ada_b, 6 * D // 4)[:, :B]
    mod_kv = _ada(c_pad, ada_kv_w[None], ada_kv_b[None], D)[0, :B]
    bf = lambda w: w.astype(BF16)
    row = lambda v: v.reshape(1, -1)

    for l in range(DEPTH):
        sh_m, sc_m, g_m, sh_f, sc_f, g_f = jnp.split(mod[l], 6, axis=-1)
        mod_m = jnp.stack([sh_m, sc_m], axis=1)
        mod_f = jnp.stack([sh_f, sc_f], axis=1)
        if l < N_A_LAYERS:
            i = l
            vec = jnp.stack([rw_w0[i], rw_a0[i], rw_k_k[i], rw_k_a[i]], axis=0)
            r, k, v, lw, kk, al, gate = _rwkv_proj(
                x, mod_m, row(norm_mix_g[l]), rw_mu[i], bf(rw_w_rkv[i]), bf(rw_w1[i]), bf(rw_w2[i]),
                bf(rw_a1[i]), bf(rw_a2[i]), bf(rw_g1[i]), bf(rw_g2[i]), vec)
            pvec = jnp.stack([rw_r_k[i].reshape(-1), rw_gn_g[i], rw_gn_b[i]], axis=0)
            y = _rwkv_scan(r, k, v, lw, kk, al, pvec)
            w_o = bf(rw_w_o[i])
        else:
            j = l - N_A_LAYERS
            q_proj = (mod_m, row(norm_mix_g[l]), bf(df_w_q[j]))
            if l == N_A_LAYERS:
                sh_kv, sc_kv = jnp.split(mod_kv, 2, axis=-1)
                q, kv = _norm_mm(x, [q_proj, (jnp.stack([sh_kv, sc_kv], axis=1), row(norm_kv_g), bf(w_kv))], BF16)
            else:
                q, = _norm_mm(x, [q_proj], BF16)
            lambda_init = 0.8 - 0.6 * math.exp(-0.3 * l)
            lam_vecs = jnp.stack([df_lq1[j], df_lk1[j], df_lq2[j], df_lk2[j]], axis=0)
            y, gate = _diff_attn(q, kv, lam_vecs, row(df_subln_g[j]), lambda_init), None
            w_o = bf(df_w_o[j])
        w_r = jnp.concatenate([moe_w_rg[l], moe_w_re[l], jnp.zeros((D, LANES - N_GROUPS - N_EXPERTS), F32)], axis=1)
        w_r_hi = bf(w_r)
        w_cat = jnp.concatenate([w_r_hi, bf(w_r - w_r_hi.astype(F32))], axis=1)
        x, info, cnt = _proj_res_route(y, gate, x, g_m[:, None, :], w_o, mod_f, row(norm_ffn_g[l]), w_cat)
        x = _moe_layer(x, info, cnt, mod_f, g_f[:, None, :], row(norm_ffn_g[l]), moe_w_gate, moe_w_up,
                       moe_w_down, l, row(final_g), final_norm=(l == DEPTH - 1))
    return x
```
